```python
import math
import jax, jax.numpy as jnp
from jax import lax
import numpy as np

D_MODEL = 2048
BATCH = 1
SEQ = 8192
DEPTH = 2

CHUNK = 64
Q_BLOCK = 128
ROPE_THETA = 500000.0
NORM_EPS = 1e-6
LN_EPS = 1e-5
NEG_INF = -1e30
N_BRANCH = 4

MLA_HEADS = 8
MLA_Q_LORA = 512
MLA_KV_LORA = 256
MLA_NOPE = 64
MLA_ROPE = 32
MLA_V = 64
MLA_SCALE = (MLA_NOPE + MLA_ROPE) ** -0.5
MLA_WIDTH = MLA_HEADS * MLA_V

SC_WIDTH = 512
SC_KERNEL = 3

DIFF_HEADS = 4
DIFF_HEAD_DIM = 64
DIFF_ROT = DIFF_HEAD_DIM // 4
DIFF_SCALE = DIFF_HEAD_DIM ** -0.5
DIFF_WIDTH = DIFF_HEADS * 2 * DIFF_HEAD_DIM

CONF_WIDTH = 512
CONF_KERNEL = 31

PEER_HEADS = 8
PEER_N_KEYS = 128
PEER_N_EXPERTS = PEER_N_KEYS * PEER_N_KEYS
PEER_KEY_DIM = 128
PEER_HALF = PEER_KEY_DIM // 2
PEER_TOPK = 16
PEER_BLOCK = 128

IN_SIZES = (MLA_Q_LORA, MLA_KV_LORA, MLA_ROPE, 3 * SC_WIDTH, DIFF_WIDTH, DIFF_WIDTH, DIFF_WIDTH, 2 * CONF_WIDTH, N_BRANCH * D_MODEL)
IN_COLS = sum(IN_SIZES)
IN_OFFSETS = tuple(int(v) for v in np.cumsum(IN_SIZES)[:-1])

kernel_name = 'hybrid_gated_mla_conv_diff_peer'


def rms_norm(x, g, eps=NORM_EPS):
    xf = x.astype(jnp.float32)
    y = xf * lax.rsqrt(jnp.mean(xf * xf, axis=-1, keepdims=True) + eps)
    return (y * g.astype(jnp.float32)).astype(x.dtype)


def layer_norm(x, g, b, eps=LN_EPS):
    xf = x.astype(jnp.float32)
    mu = jnp.mean(xf, axis=-1, keepdims=True)
    var = jnp.mean(jnp.square(xf - mu), axis=-1, keepdims=True)
    y = (xf - mu) * lax.rsqrt(var + eps)
    return (y * g.astype(jnp.float32) + b.astype(jnp.float32)).astype(x.dtype)


def rope_tables(positions, dim):
    inv_freq = ROPE_THETA ** (-jnp.arange(0, dim, 2, dtype=jnp.float32) / dim)
    ang = positions.astype(jnp.float32)[..., None] * inv_freq
    return jnp.cos(ang), jnp.sin(ang)


def apply_rope(x, cos, sin):
    xf = x.astype(jnp.float32)
    x1, x2 = jnp.split(xf, 2, axis=-1)
    return jnp.concatenate([x1 * cos - x2 * sin, x2 * cos + x1 * sin], axis=-1).astype(x.dtype)


def partial_rope(x, cos, sin):
    return jnp.concatenate([apply_rope(x[..., :DIFF_ROT], cos, sin), x[..., DIFF_ROT:]], axis=-1)


def causal_depthwise_conv(x, w):
    k = w.shape[0]
    return lax.conv_general_dilated(x, w[:, None, :], window_strides=(1,), padding=[(k - 1, 0)],
                                    dimension_numbers=('NWC', 'WIO', 'NWC'),
                                    feature_group_count=x.shape[-1])


def block_causal_mask(blk, seq):
    q_pos = blk * Q_BLOCK + jnp.arange(Q_BLOCK)
    k_pos = jnp.arange(seq)
    return k_pos[None, :] < ((q_pos // CHUNK) + 1)[:, None] * CHUNK


def to_query_blocks(t):
    b, s = t.shape[0], t.shape[1]
    return jnp.moveaxis(t.reshape((b, s // Q_BLOCK, Q_BLOCK) + t.shape[2:]), 1, 0)


def from_query_blocks(t):
    t = jnp.moveaxis(t, 0, 1)
    return t.reshape((t.shape[0], t.shape[1] * t.shape[2]) + t.shape[3:])


def mla_attention(q_nope, q_rope, k_nope, k_rope, v):
    seq = k_nope.shape[1]

    def one(args):
        qn, qr, blk = args
        s = jnp.einsum('bqhd,bkhd->bhqk', qn, k_nope) + jnp.einsum('bqhr,bkr->bhqk', qr, k_rope)
        s = s.astype(jnp.float32) * MLA_SCALE
        p = jax.nn.softmax(jnp.where(block_causal_mask(blk, seq), s, NEG_INF), axis=-1)
        return jnp.einsum('bhqk,bkhd->bqhd', p.astype(v.dtype), v)

    out = lax.map(one, (to_query_blocks(q_nope), to_query_blocks(q_rope), jnp.arange(seq // Q_BLOCK)))
    return from_query_blocks(out)


def diff_attention(q1, q2, k1, k2, v, lam):
    seq = k1.shape[1]

    def one(args):
        qa, qb, blk = args
        mask = block_causal_mask(blk, seq)
        s1 = jnp.einsum('bqhd,bkhd->bhqk', qa, k1).astype(jnp.float32) * DIFF_SCALE
        s2 = jnp.einsum('bqhd,bkhd->bhqk', qb, k2).astype(jnp.float32) * DIFF_SCALE
        p = jax.nn.softmax(jnp.where(mask, s1, NEG_INF), axis=-1) - lam * jax.nn.softmax(jnp.where(mask, s2, NEG_INF), axis=-1)
        return jnp.einsum('bhqk,bkhd->bqhd', p.astype(v.dtype), v)

    out = lax.map(one, (to_query_blocks(q1), to_query_blocks(q2), jnp.arange(seq // Q_BLOCK)))
    return from_query_blocks(out)


def mla_branch(cq, ckv, kr, q_norm_g, w_uq, kv_norm_g, w_ukv, cos, sin):
    b, s = cq.shape[0], cq.shape[1]
    q = (rms_norm(cq, q_norm_g) @ w_uq).reshape(b, s, MLA_HEADS, MLA_NOPE + MLA_ROPE)
    q_nope = q[..., :MLA_NOPE]
    q_rope = apply_rope(q[..., MLA_NOPE:], cos[:, :, None, :], sin[:, :, None, :])
    kv = (rms_norm(ckv, kv_norm_g) @ w_ukv).reshape(b, s, MLA_HEADS, MLA_NOPE + MLA_V)
    k_nope, v = kv[..., :MLA_NOPE], kv[..., MLA_NOPE:]
    k_rope = apply_rope(kr, cos, sin)
    return mla_attention(q_nope, q_rope, k_nope, k_rope, v).reshape(b, s, MLA_WIDTH)


def short_conv_branch(sc_in, conv_w):
    bg, cg, xv = jnp.split(sc_in, 3, axis=-1)
    return bg * causal_depthwise_conv(cg * xv, conv_w)


def diff_branch(dq, dk, dv, lam_vecs, norm_g, lam_init, cos, sin):
    b, s = dq.shape[0], dq.shape[1]
    c5, s5 = cos[:, :, None, None, :], sin[:, :, None, None, :]
    q = partial_rope(dq.reshape(b, s, DIFF_HEADS, 2, DIFF_HEAD_DIM), c5, s5)
    k = partial_rope(dk.reshape(b, s, DIFF_HEADS, 2, DIFF_HEAD_DIM), c5, s5)
    v = dv.reshape(b, s, DIFF_HEADS, 2 * DIFF_HEAD_DIM)
    lv = lam_vecs.astype(jnp.float32)
    lam = jnp.exp(jnp.sum(lv[0] * lv[1])) - jnp.exp(jnp.sum(lv[2] * lv[3])) + lam_init
    o = diff_attention(q[..., 0, :], q[..., 1, :], k[..., 0, :], k[..., 1, :], v, lam)
    o = rms_norm(o, norm_g) * (1.0 - lam_init)
    return o.reshape(b, s, DIFF_WIDTH)


def conformer_branch(conf_in, dw_w, dw_b, ln_g, ln_b):
    a, gt = jnp.split(conf_in, 2, axis=-1)
    u = a * jax.nn.sigmoid(gt)
    u = causal_depthwise_conv(u, dw_w) + dw_b
    return jax.nn.silu(layer_norm(u, ln_g, ln_b))


def peer_ffn(h, w_q, sub_keys, u_tab, v_tab):
    b, s, d = h.shape
    q = (h @ w_q).reshape(b, s, PEER_HEADS, 2, PEER_HALF)
    s1 = jnp.einsum('bshd,nd->bshn', q[..., 0, :], sub_keys[0]).astype(jnp.float32)
    s2 = jnp.einsum('bshd,nd->bshn', q[..., 1, :], sub_keys[1]).astype(jnp.float32)
    t1, i1 = lax.top_k(s1, PEER_TOPK)
    t2, i2 = lax.top_k(s2, PEER_TOPK)
    cand = (t1[..., :, None] + t2[..., None, :]).reshape(b, s, PEER_HEADS, PEER_TOPK * PEER_TOPK)
    cand_idx = (i1[..., :, None] * PEER_N_KEYS + i2[..., None, :]).reshape(b, s, PEER_HEADS, PEER_TOPK * PEER_TOPK)
    best, pos = lax.top_k(cand, PEER_TOPK)
    idx = jnp.take_along_axis(cand_idx, pos, axis=-1)
    gate = jax.nn.softmax(best, axis=-1)
    n_blk = (b * s) // PEER_BLOCK
    hb = h.reshape(n_blk, PEER_BLOCK, d)
    ib = idx.reshape(n_blk, PEER_BLOCK, PEER_HEADS, PEER_TOPK)
    gb = gate.reshape(n_blk, PEER_BLOCK, PEER_HEADS, PEER_TOPK)

    def one(args):
        ht, it, gt = args
        a = jnp.einsum('thkd,td->thk', u_tab[it], ht)
        wgt = (jax.nn.gelu(a.astype(jnp.float32), approximate=False) * gt).astype(ht.dtype)
        return jnp.einsum('thk,thkd->td', wgt, v_tab[it])

    return lax.map(one, (hb, ib, gb)).reshape(b, s, d)


def setup_inputs(seed: int = 0) -> dict:
    key = jax.random.key(seed)
    ks = iter(jax.random.split(key, 40))

    def w(shape, fan_in):
        return jax.random.normal(next(ks), shape, jnp.float32) * (fan_in ** -0.5)

    def gain(shape):
        return 1.0 + 0.02 * jax.random.normal(next(ks), shape, jnp.float32)

    def small(shape, scale):
        return scale * jax.random.normal(next(ks), shape, jnp.float32)

    x = jax.random.normal(next(ks), (BATCH, SEQ, D_MODEL), jnp.float32)
    offset = jax.random.randint(next(ks), (BATCH, 1), 0, 4096, dtype=jnp.int32)
    positions = (offset + jnp.arange(SEQ, dtype=jnp.int32)[None, :]).astype(jnp.int32)
    L = DEPTH
    return {
        'x': x,
        'positions': positions,
        'mix_norm_g': gain((L, D_MODEL)),
        'w_in': w((L, D_MODEL, IN_COLS), D_MODEL),
        'gate_b': small((L, N_BRANCH, D_MODEL), 0.1),
        'mla_q_norm_g': gain((L, MLA_Q_LORA)),
        'mla_w_uq': w((L, MLA_Q_LORA, MLA_HEADS * (MLA_NOPE + MLA_ROPE)), MLA_Q_LORA),
        'mla_kv_norm_g': gain((L, MLA_KV_LORA)),
        'mla_w_ukv': w((L, MLA_KV_LORA, MLA_HEADS * (MLA_NOPE + MLA_V)), MLA_KV_LORA),
        'mla_w_out': w((L, MLA_WIDTH, D_MODEL), MLA_WIDTH),
        'sc_conv_w': w((L, SC_KERNEL, SC_WIDTH), SC_KERNEL),
        'sc_w_out': w((L, SC_WIDTH, D_MODEL), SC_WIDTH),
        'diff_lambda': small((L, 4, DIFF_HEAD_DIM), 0.1),
        'diff_norm_g': gain((L, 2 * DIFF_HEAD_DIM)),
        'diff_w_out': w((L, DIFF_WIDTH, D_MODEL), DIFF_WIDTH),
        'conf_dw_w': w((L, CONF_KERNEL, CONF_WIDTH), CONF_KERNEL),
        'conf_dw_b': small((L, CONF_WIDTH), 0.02),
        'conf_ln_g': gain((L, CONF_WIDTH)),
        'conf_ln_b': small((L, CONF_WIDTH), 0.02),
        'conf_w_out': w((L, CONF_WIDTH, D_MODEL), CONF_WIDTH),
        'w_o': w((L, D_MODEL, D_MODEL), D_MODEL),
        'ffn_norm_g': gain((L, D_MODEL)),
        'peer_w_q': w((L, D_MODEL, PEER_HEADS * PEER_KEY_DIM), D_MODEL),
        'peer_sub_keys': w((L, 2, PEER_N_KEYS, PEER_HALF), PEER_HALF),
        'peer_u': w((L, PEER_N_EXPERTS, D_MODEL), D_MODEL),
        'peer_v': w((L, PEER_N_EXPERTS, D_MODEL), PEER_HEADS),
        'final_norm_g': gain((D_MODEL,)),
    }


def reference(x, positions, mix_norm_g, w_in, gate_b, mla_q_norm_g, mla_w_uq, mla_kv_norm_g, mla_w_ukv,
              mla_w_out, sc_conv_w, sc_w_out, diff_lambda, diff_norm_g, diff_w_out, conf_dw_w, conf_dw_b,
              conf_ln_g, conf_ln_b, conf_w_out, w_o, ffn_norm_g, peer_w_q, peer_sub_keys, peer_u, peer_v,
              final_norm_g):
    b, s, d = x.shape
    cos_mla, sin_mla = rope_tables(positions, MLA_ROPE)
    cos_diff, sin_diff = rope_tables(positions, DIFF_ROT)
    for i in range(DEPTH):
        lam_init = 0.8 - 0.6 * math.exp(-0.3 * i)
        h = rms_norm(x, mix_norm_g[i])
        z = h @ w_in[i]
        cq, ckv, kr, sc_in, dq, dk, dv, conf_in, gate_logits = jnp.split(z, IN_OFFSETS, axis=-1)
        y_mla = mla_branch(cq, ckv, kr, mla_q_norm_g[i], mla_w_uq[i], mla_kv_norm_g[i], mla_w_ukv[i],
                           cos_mla, sin_mla) @ mla_w_out[i]
        y_sc = short_conv_branch(sc_in, sc_conv_w[i]) @ sc_w_out[i]
        y_diff = diff_branch(dq, dk, dv, diff_lambda[i], diff_norm_g[i], lam_init, cos_diff, sin_diff) @ diff_w_out[i]
        y_conf = conformer_branch(conf_in, conf_dw_w[i], conf_dw_b[i], conf_ln_g[i], conf_ln_b[i]) @ conf_w_out[i]
        g = jax.nn.sigmoid(gate_logits.reshape(b, s, N_BRANCH, d) + gate_b[i])
        merged = g[..., 0, :] * y_mla + g[..., 1, :] * y_sc + g[..., 2, :] * y_diff + g[..., 3, :] * y_conf
        x = x + merged @ w_o[i]
        x = x + peer_ffn(rms_norm(x, ffn_norm_g[i]), peer_w_q[i], peer_sub_keys[i], peer_u[i], peer_v[i])
    return rms_norm(x, final_norm_g)
```

```python
import functools
import math

import jax
import jax.numpy as jnp
import numpy as np
from jax import lax
from jax.experimental import pallas as pl
from jax.experimental.pallas import tpu as pltpu

F32 = jnp.float32
BF16 = jnp.bfloat16

D_MODEL = 2048
DEPTH = 2
CHUNK = 64
ROPE_THETA = 500000.0
NORM_EPS = 1e-6
LN_EPS = 1e-5
NEG_INF = -1e30
N_BRANCH = 4

MLA_HEADS = 8
MLA_Q_LORA = 512
MLA_KV_LORA = 256
MLA_NOPE = 64
MLA_ROPE = 32
MLA_V = 64
MLA_SCALE = (MLA_NOPE + MLA_ROPE) ** -0.5

SC_WIDTH = 512
SC_KERNEL = 3

DIFF_HEADS = 4
DIFF_HEAD_DIM = 64
DIFF_ROT = DIFF_HEAD_DIM // 4
DIFF_SCALE = DIFF_HEAD_DIM ** -0.5
DIFF_WIDTH = DIFF_HEADS * 2 * DIFF_HEAD_DIM

CONF_WIDTH = 512
CONF_KERNEL = 31

PEER_HEADS = 8
PEER_N_KEYS = 128
PEER_N_EXPERTS = PEER_N_KEYS * PEER_N_KEYS
PEER_KEY_DIM = 128
PEER_HALF = PEER_KEY_DIM // 2
PEER_TOPK = 16

IN_SIZES = (MLA_Q_LORA, MLA_KV_LORA, MLA_ROPE, 3 * SC_WIDTH, DIFF_WIDTH, DIFF_WIDTH, DIFF_WIDTH,
            2 * CONF_WIDTH, N_BRANCH * D_MODEL)
IN_OFFSETS = tuple(int(v) for v in np.cumsum((0,) + IN_SIZES)[:-1])

LANES = 128
HALO = 32
VMEM_LIMIT = 56 * 1024 * 1024

Z_CQ = 0
Z_SC = 512
Z_DQ = 2048
Z_DK = 2560
Z_DV = 3072
Z_CONF = 3584
Z_GATE = 4608
Z_CKV = 12800
Z_KR = 13056
Z_COLS = 13312


def _cparams(sem):
    return pltpu.CompilerParams(dimension_semantics=sem, vmem_limit_bytes=VMEM_LIMIT)


def _rmsnorm_body(x_ref, g_ref, o_ref):
    x = x_ref[...].astype(F32)
    y = x * lax.rsqrt(jnp.mean(x * x, axis=-1, keepdims=True) + NORM_EPS)
    o_ref[...] = (y * g_ref[...]).astype(o_ref.dtype)


def _rmsnorm(x, g, out_dtype, tm=512):
    s, d = x.shape
    tm = min(tm, s)
    return pl.pallas_call(
        _rmsnorm_body,
        grid=(s // tm,),
        in_specs=[pl.BlockSpec((tm, d), lambda i: (i, 0)), pl.BlockSpec((1, d), lambda i: (0, 0))],
        out_specs=pl.BlockSpec((tm, d), lambda i: (i, 0)),
        out_shape=jax.ShapeDtypeStruct((s, d), out_dtype),
        compiler_params=_cparams(("parallel",)),
        name="rmsnorm",
    )(x, g.reshape(1, d))


def _matmul_body(a_ref, b_ref, o_ref):
    o_ref[...] = jnp.dot(a_ref[...], b_ref[...], preferred_element_type=F32).astype(o_ref.dtype)


def _matmul_res_body(a_ref, b_ref, r_ref, o_ref):
    o_ref[...] = (r_ref[...] + jnp.dot(a_ref[...], b_ref[...], preferred_element_type=F32)).astype(o_ref.dtype)


def _matmul(a, b, out_dtype, residual=None, tm=1024, tn=1024, name="matmul"):
    m, k = a.shape
    n = b.shape[1]
    tm, tn = min(tm, m), min(tn, n)
    in_specs = [pl.BlockSpec((tm, k), lambda i, j: (i, 0)), pl.BlockSpec((k, tn), lambda i, j: (0, j))]
    args = [a, b]
    body = _matmul_body
    if residual is not None:
        in_specs.append(pl.BlockSpec((tm, tn), lambda i, j: (i, j)))
        args.append(residual)
        body = _matmul_res_body
    return pl.pallas_call(
        body,
        grid=(m // tm, n // tn),
        in_specs=in_specs,
        out_specs=pl.BlockSpec((tm, tn), lambda i, j: (i, j)),
        out_shape=jax.ShapeDtypeStruct((m, n), out_dtype),
        compiler_params=_cparams(("parallel", "arbitrary")),
        name=name,
    )(*args)


def _rope128(x, c, s_up, s_dn, half):
    return x * c + pltpu.roll(x, LANES - half, 1) * s_up + pltpu.roll(x, half, 1) * s_dn


def _mla_prep_body(cq_ref, ckv_ref, kr_ref, c_ref, su_ref, sd_ref, qg_ref, kvg_ref, wq_ref, wk_ref, wv_ref,
                   q_ref, k_ref, v_ref):
    half = MLA_ROPE // 2
    c, su, sd = c_ref[...], su_ref[...], sd_ref[...]

    def norm(x_ref, g_ref):
        x = x_ref[...].astype(F32)
        y = x * lax.rsqrt(jnp.mean(x * x, axis=-1, keepdims=True) + NORM_EPS)
        return (y * g_ref[...]).astype(BF16)

    qf = jnp.dot(norm(cq_ref, qg_ref), wq_ref[...], preferred_element_type=F32)
    ckvn = norm(ckv_ref, kvg_ref)
    kf = jnp.dot(ckvn, wk_ref[...], preferred_element_type=F32)
    v_ref[...] = jnp.dot(ckvn, wv_ref[...], preferred_element_type=F32).astype(v_ref.dtype)
    kr = _rope128(kr_ref[...].astype(F32), c, su, sd, half)
    for h in range(MLA_HEADS):
        sl = slice(h * LANES, (h + 1) * LANES)
        q_ref[:, sl] = (_rope128(qf[:, sl], c, su, sd, half) * MLA_SCALE).astype(q_ref.dtype)
        k_ref[:, sl] = (kf[:, sl] + kr).astype(k_ref.dtype)


def _mla_prep(z, tabs, qg, kvg, wq, wk, wv, tm=512):
    s = z.shape[0]
    tm = min(tm, s)
    row = lambda w, idx: pl.BlockSpec((tm, w), lambda i: (i, idx))
    full = lambda a: pl.BlockSpec(a.shape, lambda i: (0,) * a.ndim)
    hw = MLA_HEADS * LANES
    return pl.pallas_call(
        _mla_prep_body,
        grid=(s // tm,),
        in_specs=[row(MLA_Q_LORA, Z_CQ // MLA_Q_LORA), row(MLA_KV_LORA, Z_CKV // MLA_KV_LORA),
                  row(LANES, Z_KR // LANES),
                  pl.BlockSpec((tm, LANES), lambda i: (i, 0)), pl.BlockSpec((tm, LANES), lambda i: (i, 0)),
                  pl.BlockSpec((tm, LANES), lambda i: (i, 0)),
                  full(qg), full(kvg), full(wq), full(wk), full(wv)],
        out_specs=[pl.BlockSpec((tm, hw), lambda i: (i, 0)), pl.BlockSpec((tm, hw), lambda i: (i, 0)),
                   pl.BlockSpec((tm, MLA_HEADS * MLA_V), lambda i: (i, 0))],
        out_shape=[jax.ShapeDtypeStruct((s, hw), BF16), jax.ShapeDtypeStruct((s, hw), BF16),
                   jax.ShapeDtypeStruct((s, MLA_HEADS * MLA_V), BF16)],
        compiler_params=_cparams(("parallel",)),
        name="mla_prep",
    )(z, z, z, *tabs, qg, kvg, wq, wk, wv)


def _chunk_mask(tq, tk):
    rows = lax.broadcasted_iota(jnp.int32, (tq, tk), 0) // CHUNK
    cols = lax.broadcasted_iota(jnp.int32, (tq, tk), 1) // CHUNK
    return cols <= rows


def _flash_stream(q, k_at, v_at, m_ref, l_ref, acc_ref, slot, n_full, tq, tk):
    m_ref[slot] = jnp.full((tq, 1), NEG_INF, F32)
    l_ref[slot] = jnp.zeros((tq, 1), F32)
    acc_ref[slot] = jnp.zeros((tq, LANES), F32)

    def step(j, masked):
        s = lax.dot_general(q, k_at(j), (((1,), (1,)), ((), ())), preferred_element_type=F32)
        if masked:
            s = jnp.where(_chunk_mask(tq, tk), s, NEG_INF)
        m_prev = m_ref[slot]
        m_new = jnp.maximum(m_prev, jnp.max(s, axis=1, keepdims=True))
        alpha = jnp.exp(m_prev - m_new)
        p = jnp.exp(s - m_new)
        l_ref[slot] = alpha * l_ref[slot] + jnp.sum(p, axis=1, keepdims=True)
        acc_ref[slot] = alpha * acc_ref[slot] + jnp.dot(p.astype(BF16), v_at(j), preferred_element_type=F32)
        m_ref[slot] = m_new

    def body(j, carry):
        step(j, False)
        return carry

    lax.fori_loop(0, n_full, body, 0)
    step(n_full, True)


def _mla_attn_body(q_ref, k_ref, v_ref, o_ref, m_ref, l_ref, acc_ref, *, tq):
    i = pl.program_id(1)
    for hh in range(2):
        sl = slice(hh * LANES, (hh + 1) * LANES)
        k_at = lambda j, sl=sl: k_ref[pl.ds(pl.multiple_of(j * tq, tq), tq), sl]
        v_at = lambda j: v_ref[pl.ds(pl.multiple_of(j * tq, tq), tq), :]
        _flash_stream(q_ref[:, sl], k_at, v_at, m_ref, l_ref, acc_ref, hh, i, tq, tq)
    lane = lax.broadcasted_iota(jnp.int32, (tq, LANES), 1)
    o0 = acc_ref[0] / l_ref[0]
    o1 = acc_ref[1] / l_ref[1]
    o_ref[...] = jnp.where(lane < MLA_V, o0, o1).astype(o_ref.dtype)


def _mla_attn(q, k, v, tq=512):
    s = q.shape[0]
    tq = min(tq, s)
    pairs = MLA_HEADS // 2
    return pl.pallas_call(
        functools.partial(_mla_attn_body, tq=tq),
        grid=(pairs, s // tq),
        in_specs=[pl.BlockSpec((tq, 2 * LANES), lambda p, i: (i, p)),
                  pl.BlockSpec((s, 2 * LANES), lambda p, i: (0, p)),
                  pl.BlockSpec((s, LANES), lambda p, i: (0, p))],
        out_specs=pl.BlockSpec((tq, LANES), lambda p, i: (i, p)),
        out_shape=jax.ShapeDtypeStruct((s, MLA_HEADS * MLA_V), BF16),
        scratch_shapes=[pltpu.VMEM((2, tq, 1), F32), pltpu.VMEM((2, tq, 1), F32),
                        pltpu.VMEM((2, tq, LANES), F32)],
        compiler_params=_cparams(("parallel", "arbitrary")),
        name="mla_attn",
    )(q, k, v)


def _diff_prep_body(dq_ref, dk_ref, c_ref, su_ref, sd_ref, q_ref, k_ref):
    half = DIFF_ROT // 2
    c, su, sd = c_ref[...], su_ref[...], sd_ref[...]
    tm = dq_ref.shape[0]
    lane = lax.broadcasted_iota(jnp.int32, (tm, LANES), 1)
    for h in range(DIFF_HEADS):
        sl = slice(h * LANES, (h + 1) * LANES)
        q = _rope128(dq_ref[:, sl].astype(F32), c, su, sd, half) * DIFF_SCALE
        q_ref[:, (2 * h) * LANES:(2 * h + 1) * LANES] = jnp.where(lane < DIFF_HEAD_DIM, q, 0.0).astype(q_ref.dtype)
        q_ref[:, (2 * h + 1) * LANES:(2 * h + 2) * LANES] = jnp.where(lane >= DIFF_HEAD_DIM, q, 0.0).astype(q_ref.dtype)
        k_ref[:, sl] = _rope128(dk_ref[:, sl].astype(F32), c, su, sd, half).astype(k_ref.dtype)


def _diff_prep(z, tabs, tm=512):
    s = z.shape[0]
    tm = min(tm, s)
    tab = pl.BlockSpec((tm, LANES), lambda i: (i, 0))
    return pl.pallas_call(
        _diff_prep_body,
        grid=(s // tm,),
        in_specs=[pl.BlockSpec((tm, DIFF_WIDTH), lambda i: (i, Z_DQ // DIFF_WIDTH)),
                  pl.BlockSpec((tm, DIFF_WIDTH), lambda i: (i, Z_DK // DIFF_WIDTH)), tab, tab, tab],
        out_specs=[pl.BlockSpec((tm, 2 * DIFF_WIDTH), lambda i: (i, 0)),
                   pl.BlockSpec((tm, DIFF_WIDTH), lambda i: (i, 0))],
        out_shape=[jax.ShapeDtypeStruct((s, 2 * DIFF_WIDTH), BF16), jax.ShapeDtypeStruct((s, DIFF_WIDTH), BF16)],
        compiler_params=_cparams(("parallel",)),
        name="diff_prep",
    )(z, z, *tabs)


def _diff_attn_body(q_ref, k_ref, v_ref, lam_ref, g_ref, o_ref, m_ref, l_ref, acc_ref, *, tq, lam_init):
    i = pl.program_id(1)
    k_at = lambda j: k_ref[pl.ds(pl.multiple_of(j * tq, tq), tq), :]
    v_at = lambda j: v_ref[pl.ds(pl.multiple_of(j * tq, tq), tq), :]
    for c in range(2):
        _flash_stream(q_ref[:, c * LANES:(c + 1) * LANES], k_at, v_at, m_ref, l_ref, acc_ref, c, i, tq, tq)
    lv = lam_ref[...]
    lam = (jnp.exp(jnp.sum(lv[0:1] * lv[1:2], axis=1, keepdims=True))
           - jnp.exp(jnp.sum(lv[2:3] * lv[3:4], axis=1, keepdims=True)) + lam_init)
    o = acc_ref[0] / l_ref[0] - lam * (acc_ref[1] / l_ref[1])
    o = o * lax.rsqrt(jnp.mean(o * o, axis=-1, keepdims=True) + NORM_EPS)
    o_ref[...] = (o * g_ref[...] * (1.0 - lam_init)).astype(o_ref.dtype)


def _diff_attn(qd, kd, z, lam_vecs, norm_g, lam_init, tq=512):
    s = qd.shape[0]
    tq = min(tq, s)
    return pl.pallas_call(
        functools.partial(_diff_attn_body, tq=tq, lam_init=lam_init),
        grid=(DIFF_HEADS, s // tq),
        in_specs=[pl.BlockSpec((tq, 2 * LANES), lambda h, i: (i, h)),
                  pl.BlockSpec((s, LANES), lambda h, i: (0, h)),
                  pl.BlockSpec((s, LANES), lambda h, i: (0, Z_DV // LANES + h)),
                  pl.BlockSpec(lam_vecs.shape, lambda h, i: (0, 0)),
                  pl.BlockSpec((1, LANES), lambda h, i: (0, 0))],
        out_specs=pl.BlockSpec((tq, LANES), lambda h, i: (i, h)),
        out_shape=jax.ShapeDtypeStruct((s, DIFF_WIDTH), BF16),
        scratch_shapes=[pltpu.VMEM((2, tq, 1), F32), pltpu.VMEM((2, tq, 1), F32),
                        pltpu.VMEM((2, tq, LANES), F32)],
        compiler_params=_cparams(("parallel", "arbitrary")),
        name="diff_attn",
    )(qd, kd, z, lam_vecs, norm_g.reshape(1, LANES))


def _conv_body(bg_ref, cg_ref, xv_ref, cgh_ref, xvh_ref, a_ref, gt_ref, ah_ref, gth_ref,
               scw_ref, dww_ref, dwb_ref, lng_ref, lnb_ref, sc_ref, cf_ref, ext_ref):
    tm = bg_ref.shape[0]
    first = pl.program_id(0) == 0

    def fill(cur, halo):
        ext_ref[0:HALO, :] = jnp.where(first, 0.0, halo)
        ext_ref[HALO:HALO + tm, :] = cur

    def conv(w_ref, taps):
        acc = jnp.zeros((tm, ext_ref.shape[1]), F32)
        for j in range(taps):
            off = HALO - (taps - 1) + j
            acc = acc + w_ref[j:j + 1, :] * ext_ref[off:off + tm, :]
        return acc

    fill(cg_ref[...].astype(F32) * xv_ref[...].astype(F32), cgh_ref[...].astype(F32) * xvh_ref[...].astype(F32))
    sc_ref[...] = (bg_ref[...].astype(F32) * conv(scw_ref, SC_KERNEL)).astype(sc_ref.dtype)

    fill(a_ref[...].astype(F32) * jax.nn.sigmoid(gt_ref[...].astype(F32)),
         ah_ref[...].astype(F32) * jax.nn.sigmoid(gth_ref[...].astype(F32)))
    u = conv(dww_ref, CONF_KERNEL) + dwb_ref[...]
    mu = jnp.mean(u, axis=-1, keepdims=True)
    var = jnp.mean(jnp.square(u - mu), axis=-1, keepdims=True)
    y = (u - mu) * lax.rsqrt(var + LN_EPS) * lng_ref[...] + lnb_ref[...]
    cf_ref[...] = (y * jax.nn.sigmoid(y)).astype(cf_ref.dtype)


def _conv_branches(z, sc_w, dw_w, dw_b, ln_g, ln_b, tm=512):
    s = z.shape[0]
    tm = min(tm, s)
    w = SC_WIDTH
    cur = lambda col: pl.BlockSpec((tm, w), lambda i: (i, col // w))
    halo = lambda col: pl.BlockSpec((HALO, w), lambda i: (jnp.maximum(i * (tm // HALO) - 1, 0), col // w))
    full = lambda a: pl.BlockSpec(a.shape, lambda i: (0,) * a.ndim)
    vec = lambda a: a.reshape(1, -1)
    args = [sc_w, dw_w, vec(dw_b), vec(ln_g), vec(ln_b)]
    return pl.pallas_call(
        _conv_body,
        grid=(s // tm,),
        in_specs=[cur(Z_SC), cur(Z_SC + w), cur(Z_SC + 2 * w), halo(Z_SC + w), halo(Z_SC + 2 * w),
                  cur(Z_CONF), cur(Z_CONF + w), halo(Z_CONF), halo(Z_CONF + w)] + [full(a) for a in args],
        out_specs=[pl.BlockSpec((tm, w), lambda i: (i, 0)), pl.BlockSpec((tm, w), lambda i: (i, 0))],
        out_shape=[jax.ShapeDtypeStruct((s, w), BF16), jax.ShapeDtypeStruct((s, w), BF16)],
        scratch_shapes=[pltpu.VMEM((HALO + tm, w), F32)],
        compiler_params=_cparams(("parallel",)),
        name="conv_branches",
    )(*([z] * 9), *args)


def _merge_body(b0, b1, b2, b3, w0, w1, w2, w3, g0, g1, g2, g3, gb_ref, o_ref):
    acc = None
    for n, (b_ref, w_ref, g_ref) in enumerate(((b0, w0, g0), (b1, w1, g1), (b2, w2, g2), (b3, w3, g3))):
        y = jnp.dot(b_ref[...], w_ref[...], preferred_element_type=F32)
        t = jax.nn.sigmoid(g_ref[...].astype(F32) + gb_ref[n:n + 1, :]) * y
        acc = t if acc is None else acc + t
    o_ref[...] = acc.astype(o_ref.dtype)


def _merge(branches, w_outs, z, gate_b, tm=512, tn=512):
    s = z.shape[0]
    tm = min(tm, s)
    kw = branches[0].shape[1]
    return pl.pallas_call(
        _merge_body,
        grid=(s // tm, D_MODEL // tn),
        in_specs=([pl.BlockSpec((tm, kw), lambda i, j: (i, 0))] * 4
                  + [pl.BlockSpec((kw, tn), lambda i, j: (0, j))] * 4
                  + [pl.BlockSpec((tm, tn), lambda i, j, b=b: (i, (Z_GATE + b * D_MODEL) // tn + j))
                     for b in range(N_BRANCH)]
                  + [pl.BlockSpec((N_BRANCH, tn), lambda i, j: (0, j))]),
        out_specs=pl.BlockSpec((tm, tn), lambda i, j: (i, j)),
        out_shape=jax.ShapeDtypeStruct((s, D_MODEL), BF16),
        compiler_params=_cparams(("parallel", "arbitrary")),
        name="merge",
    )(*branches, *w_outs, z, z, z, z, gate_b)


def _peer_scores_body(q_ref, k1_ref, k2_ref, s1_ref, s2_ref, e1_ref, e2n_ref, tau_ref, t1_ref, t2_ref):
    tt = q_ref.shape[0]
    key_iota = lax.broadcasted_iota(jnp.int32, (PEER_N_KEYS, tt), 0).astype(F32)

    def top_values(s_t, t_ref):
        def body(r, s):
            m = jnp.max(s, axis=0, keepdims=True)
            idx = jnp.min(jnp.where(s == m, key_iota, float(PEER_N_KEYS)), axis=0, keepdims=True)
            t_ref[pl.ds(r, 1), :] = m
            return jnp.where(key_iota == idx, -jnp.inf, s)

        lax.fori_loop(0, PEER_TOPK, body, s_t)

    def head(h, carry):
        qh = q_ref[:, pl.ds(pl.multiple_of(h * PEER_KEY_DIM, PEER_KEY_DIM), PEER_KEY_DIM)]
        dims = (((1,), (1,)), ((), ()))
        s1 = lax.dot_general(k1_ref[...], qh, dims, precision=lax.Precision.HIGHEST, preferred_element_type=F32)
        s2 = lax.dot_general(k2_ref[...], qh, dims, precision=lax.Precision.HIGHEST, preferred_element_type=F32)
        top_values(s1, t1_ref)
        top_values(s2, t2_ref)
        t1, t2 = t1_ref[...], t2_ref[...]
        cand = jnp.concatenate([t1[a:a + 1] + t2 for a in range(PEER_TOPK)], axis=0)

        def tau_body(r, carry):
            c, cnt, tau = carry
            m = jnp.max(c, axis=0, keepdims=True)
            eq = c == m
            tau = jnp.where(cnt < float(PEER_TOPK), m, tau)
            cnt = cnt + jnp.sum(jnp.where(eq, 1.0, 0.0), axis=0, keepdims=True)
            return jnp.where(eq, -jnp.inf, c), cnt, tau

        zero = jnp.zeros((1, tt), F32)
        _, _, tau = lax.fori_loop(0, PEER_TOPK, tau_body, (cand, zero, zero))

        m1, m2 = t1[0:1], t2[0:1]
        c1, c2 = jnp.exp(t1 - m1), jnp.exp(t2 - m2)
        ec = jnp.concatenate([c1[a:a + 1] * c2 for a in range(PEER_TOPK)], axis=0)
        zden = jnp.sum(jnp.where(cand >= tau, ec, 0.0), axis=0, keepdims=True)
        s1_ref[h] = s1
        s2_ref[h] = s2
        e1_ref[h] = jnp.exp(s1 - m1)
        e2n_ref[h] = jnp.exp(s2 - m2) / zden
        tau_ref[pl.ds(h, 1), :] = tau
        return carry

    lax.fori_loop(0, PEER_HEADS, head, 0)


def _peer_scores(q, k1, k2, tt=256):
    s = q.shape[0]
    tt = min(tt, s)
    tab = pl.BlockSpec((PEER_HEADS, PEER_N_KEYS, tt), lambda i: (0, 0, i))
    tab_shape = jax.ShapeDtypeStruct((PEER_HEADS, PEER_N_KEYS, s), F32)
    return pl.pallas_call(
        _peer_scores_body,
        grid=(s // tt,),
        in_specs=[pl.BlockSpec((tt, PEER_HEADS * PEER_KEY_DIM), lambda i: (i, 0)),
                  pl.BlockSpec(k1.shape, lambda i: (0, 0)), pl.BlockSpec(k2.shape, lambda i: (0, 0))],
        out_specs=[tab, tab, tab, tab, pl.BlockSpec((PEER_HEADS, tt), lambda i: (0, i))],
        out_shape=[tab_shape, tab_shape, tab_shape, tab_shape, jax.ShapeDtypeStruct((PEER_HEADS, s), F32)],
        scratch_shapes=[pltpu.VMEM((PEER_TOPK, tt), F32), pltpu.VMEM((PEER_TOPK, tt), F32)],
        compiler_params=_cparams(("parallel",)),
        name="peer_scores",
    )(q, k1, k2)


def _peer_dense_body(h_ref, u_ref, vt_ref, s1_ref, e1_ref, s2_ref, e2n_ref, tau_ref, x_ref, o_ref,
                     acc_ref, w_ref, *, eb):
    e = pl.program_id(1)
    tt = h_ref.shape[0]
    nsub = eb // PEER_N_KEYS
    ncol = tt // LANES

    @pl.when(e == 0)
    def _():
        acc_ref[...] = jnp.zeros_like(acc_ref)

    a_t = lax.dot_general(u_ref[...], h_ref[...], (((1,), (1,)), ((), ())), preferred_element_type=F32)
    gelu_t = 0.5 * a_t * (1.0 + lax.erf(a_t * (2.0 ** -0.5)))

    for il in range(nsub):
        for c in range(ncol):
            cols = slice(c * LANES, (c + 1) * LANES)
            g = jnp.zeros((PEER_N_KEYS, LANES), F32)
            for h in range(PEER_HEADS):
                r = h * nsub + il
                ssum = s2_ref[h, :, cols] + s1_ref[0, r:r + 1, cols]
                val = e2n_ref[h, :, cols] * e1_ref[0, r:r + 1, cols]
                g = g + jnp.where(ssum >= tau_ref[h:h + 1, cols], val, 0.0)
            rows = slice(il * PEER_N_KEYS, (il + 1) * PEER_N_KEYS)
            w_ref[rows, cols] = (gelu_t[rows, cols] * g).astype(w_ref.dtype)

    acc_ref[...] += jnp.dot(vt_ref[...], w_ref[...], preferred_element_type=F32)

    @pl.when(e == pl.num_programs(1) - 1)
    def _():
        o_ref[...] = x_ref[...] + acc_ref[...].T


def _peer_dense(hn, u, vt, s1e, e1e, s2, e2n, tau, x, tt=512, eb=512):
    s, d = hn.shape
    tt = min(tt, s)
    ne = u.shape[0]
    rows = PEER_HEADS * (eb // PEER_N_KEYS)
    sub = pl.BlockSpec((1, rows, tt), lambda t, e: (e, 0, t))
    tab = pl.BlockSpec((PEER_HEADS, PEER_N_KEYS, tt), lambda t, e: (0, 0, t))
    return pl.pallas_call(
        functools.partial(_peer_dense_body, eb=eb),
        grid=(s // tt, ne // eb),
        in_specs=[pl.BlockSpec((tt, d), lambda t, e: (t, 0)),
                  pl.BlockSpec((eb, d), lambda t, e: (e, 0)),
                  pl.BlockSpec((d, eb), lambda t, e: (0, e)),
                  sub, sub, tab, tab,
                  pl.BlockSpec((PEER_HEADS, tt), lambda t, e: (0, t)),
                  pl.BlockSpec((tt, d), lambda t, e: (t, 0))],
        out_specs=pl.BlockSpec((tt, d), lambda t, e: (t, 0)),
        out_shape=jax.ShapeDtypeStruct((s, d), F32),
        scratch_shapes=[pltpu.VMEM((d, tt), F32), pltpu.VMEM((eb, tt), BF16)],
        compiler_params=_cparams(("parallel", "arbitrary")),
        name="peer_dense",
    )(hn, u, vt, s1e, e1e, s2, e2n, tau, x)


def _rope_tables(positions, dim, lanes_x1):
    half = dim // 2
    inv_freq = ROPE_THETA ** (-jnp.arange(0, dim, 2, dtype=F32) / dim)
    ang = positions.astype(F32)[:, None] * inv_freq
    cos, sin = jnp.cos(ang), jnp.sin(ang)
    s = positions.shape[0]
    c = jnp.ones((s, LANES), F32)
    su = jnp.zeros((s, LANES), F32)
    sd = jnp.zeros((s, LANES), F32)
    for off in lanes_x1:
        c = c.at[:, off:off + half].set(cos).at[:, off + half:off + dim].set(cos)
        su = su.at[:, off:off + half].set(-sin)
        sd = sd.at[:, off + half:off + dim].set(sin)
    return c, su, sd


def _layout_w_in(w):
    d = w.shape[0]
    seg = [w[:, o:o + n] for o, n in zip(IN_OFFSETS, IN_SIZES)]
    cq, ckv, kr, sc, dq, dk, dv, conf, gate = seg
    kr_blk = jnp.concatenate([jnp.zeros((d, MLA_NOPE), w.dtype), kr,
                              jnp.zeros((d, LANES - MLA_NOPE - MLA_ROPE), w.dtype)], axis=1)
    pad = jnp.zeros((d, Z_COLS - (Z_KR + LANES)), w.dtype)
    return jnp.concatenate([cq, sc, dq, dk, dv, conf, gate, ckv, kr_blk, pad], axis=1).astype(BF16)


def _layout_mla_weights(w_uq, w_ukv):
    lq = w_uq.shape[0]
    wq = w_uq.reshape(lq, MLA_HEADS, MLA_NOPE + MLA_ROPE)
    wq = jnp.pad(wq, ((0, 0), (0, 0), (0, LANES - MLA_NOPE - MLA_ROPE))).reshape(lq, MLA_HEADS * LANES)
    lk = w_ukv.shape[0]
    wkv = w_ukv.reshape(lk, MLA_HEADS, MLA_NOPE + MLA_V)
    wk = jnp.pad(wkv[:, :, :MLA_NOPE], ((0, 0), (0, 0), (0, LANES - MLA_NOPE))).reshape(lk, MLA_HEADS * LANES)
    wv = wkv[:, :, MLA_NOPE:].reshape(lk, MLA_HEADS * MLA_V)
    return wq.astype(BF16), wk.astype(BF16), wv.astype(BF16)


def kernel(x, positions, mix_norm_g, w_in, gate_b, mla_q_norm_g, mla_w_uq, mla_kv_norm_g, mla_w_ukv, mla_w_out,
           sc_conv_w, sc_w_out, diff_lambda, diff_norm_g, diff_w_out, conf_dw_w, conf_dw_b, conf_ln_g, conf_ln_b,
           conf_w_out, w_o, ffn_norm_g, peer_w_q, peer_sub_keys, peer_u, peer_v, final_norm_g):
    b, s, d = x.shape
    assert b == 1 and d == D_MODEL
    xs = x.reshape(s, d)
    pos = positions.reshape(s)
    mla_tabs = _rope_tables(pos, MLA_ROPE, (MLA_NOPE,))
    diff_tabs = _rope_tables(pos, DIFF_ROT, (0, DIFF_HEAD_DIM))
    eb = 512
    nsub = eb // PEER_N_KEYS

    for i in range(DEPTH):
        lam_init = 0.8 - 0.6 * math.exp(-0.3 * i)
        hn = _rmsnorm(xs, mix_norm_g[i], BF16)
        z = _matmul(hn, _layout_w_in(w_in[i]), BF16, name="in_proj")
        wq, wk, wv = _layout_mla_weights(mla_w_uq[i], mla_w_ukv[i])
        q, k, v = _mla_prep(z, mla_tabs, mla_q_norm_g[i].reshape(1, -1), mla_kv_norm_g[i].reshape(1, -1), wq, wk, wv)
        o_mla = _mla_attn(q, k, v)
        o_sc, o_conf = _conv_branches(z, sc_conv_w[i], conf_dw_w[i], conf_dw_b[i], conf_ln_g[i], conf_ln_b[i])
        qd, kd = _diff_prep(z, diff_tabs)
        o_diff = _diff_attn(qd, kd, z, diff_lambda[i], diff_norm_g[i], lam_init)
        merged = _merge((o_mla, o_sc, o_diff, o_conf),
                        tuple(w.astype(BF16) for w in (mla_w_out[i], sc_w_out[i], diff_w_out[i], conf_w_out[i])),
                        z, gate_b[i])
        xs = _matmul(merged, w_o[i].astype(BF16), F32, residual=xs, name="out_proj")
        hf = _rmsnorm(xs, ffn_norm_g[i], BF16)
        pq = _matmul(hf, peer_w_q[i].astype(BF16), F32, name="peer_q")
        zk = jnp.zeros((PEER_N_KEYS, PEER_HALF), F32)
        k1 = jnp.concatenate([peer_sub_keys[i, 0], zk], axis=1)
        k2 = jnp.concatenate([zk, peer_sub_keys[i, 1]], axis=1)
        s1, s2, e1, e2n, tau = _peer_scores(pq, k1, k2)

        def by_block(t):
            t = t.reshape(PEER_HEADS, PEER_N_KEYS // nsub, nsub, s)
            return t.transpose(1, 0, 2, 3).reshape(PEER_N_KEYS // nsub, PEER_HEADS * nsub, s)

        xs = _peer_dense(hf, peer_u[i].astype(BF16), peer_v[i].T.astype(BF16), by_block(s1), by_block(e1),
                         s2, e2n, tau, xs, eb=eb)
    out = _rmsnorm(xs, final_norm_g, F32)
    return out.reshape(b, s, d)
```

```python
import functools
import math

import jax
import jax.numpy as jnp
import numpy as np
from jax import lax
from jax.experimental import pallas as pl
from jax.experimental.pallas import tpu as pltpu

F32 = jnp.float32
BF16 = jnp.bfloat16

D_MODEL = 2048
DEPTH = 2
CHUNK = 64
ROPE_THETA = 500000.0
NORM_EPS = 1e-6
LN_EPS = 1e-5
NEG_INF = -1e30
N_BRANCH = 4

MLA_HEADS = 8
MLA_Q_LORA = 512
MLA_KV_LORA = 256
MLA_NOPE = 64
MLA_ROPE = 32
MLA_V = 64
MLA_SCALE = (MLA_NOPE + MLA_ROPE) ** -0.5

SC_WIDTH = 512
SC_KERNEL = 3

DIFF_HEADS = 4
DIFF_HEAD_DIM = 64
DIFF_ROT = DIFF_HEAD_DIM // 4
DIFF_SCALE = DIFF_HEAD_DIM ** -0.5
DIFF_WIDTH = DIFF_HEADS * 2 * DIFF_HEAD_DIM

CONF_WIDTH = 512
CONF_KERNEL = 31

PEER_HEADS = 8
PEER_N_KEYS = 128
PEER_N_EXPERTS = PEER_N_KEYS * PEER_N_KEYS
PEER_KEY_DIM = 128
PEER_HALF = PEER_KEY_DIM // 2
PEER_TOPK = 16

IN_SIZES = (MLA_Q_LORA, MLA_KV_LORA, MLA_ROPE, 3 * SC_WIDTH, DIFF_WIDTH, DIFF_WIDTH, DIFF_WIDTH,
            2 * CONF_WIDTH, N_BRANCH * D_MODEL)
IN_OFFSETS = tuple(int(v) for v in np.cumsum((0,) + IN_SIZES)[:-1])

LOG2E = math.log2(math.e)

LANES = 128
SUBLANES = 8
HALO = 32
VMEM_LIMIT = 56 * 1024 * 1024

ROW_TILE = 512
MM_TILE = 1024
ATTN_TILE = 512
SCORE_TILE = 512
PEER_TOKENS = 512
PEER_EXPERTS = 1024
PEER_CHUNK = 256

Z_CQ = 0
Z_SC = 512
Z_DQ = 2048
Z_DK = 2560
Z_DV = 3072
Z_CONF = 3584
Z_GATE = 4608
Z_CKV = 12800
Z_KR = 13056
Z_COLS = 13312


def _cparams(sem):
    return pltpu.CompilerParams(dimension_semantics=sem, vmem_limit_bytes=VMEM_LIMIT)


def _rmsnorm_body(x_ref, g_ref, o_ref):
    x = x_ref[...].astype(F32)
    y = x * lax.rsqrt(jnp.mean(x * x, axis=-1, keepdims=True) + NORM_EPS)
    o_ref[...] = (y * g_ref[...]).astype(o_ref.dtype)


def _rmsnorm(x, g, out_dtype):
    s, d = x.shape
    tm = min(ROW_TILE, s)
    return pl.pallas_call(
        _rmsnorm_body,
        grid=(s // tm,),
        in_specs=[pl.BlockSpec((tm, d), lambda i: (i, 0)), pl.BlockSpec((1, d), lambda i: (0, 0))],
        out_specs=pl.BlockSpec((tm, d), lambda i: (i, 0)),
        out_shape=jax.ShapeDtypeStruct((s, d), out_dtype),
        compiler_params=_cparams(("parallel",)),
        name="rmsnorm",
    )(x, g.reshape(1, d))


def _matmul_body(a_ref, b_ref, o_ref):
    o_ref[...] = jnp.dot(a_ref[...], b_ref[...], preferred_element_type=F32).astype(o_ref.dtype)


def _matmul_res_body(a_ref, b_ref, r_ref, o_ref):
    o_ref[...] = (r_ref[...] + jnp.dot(a_ref[...], b_ref[...], preferred_element_type=F32)).astype(o_ref.dtype)


def _matmul(a, b, out_dtype, residual=None, name="matmul"):
    m, k = a.shape
    n = b.shape[1]
    tm, tn = min(MM_TILE, m), min(MM_TILE, n)
    in_specs = [pl.BlockSpec((tm, k), lambda i, j: (i, 0)), pl.BlockSpec((k, tn), lambda i, j: (0, j))]
    args = [a, b]
    body = _matmul_body
    if residual is not None:
        in_specs.append(pl.BlockSpec((tm, tn), lambda i, j: (i, j)))
        args.append(residual)
        body = _matmul_res_body
    return pl.pallas_call(
        body,
        grid=(m // tm, n // tn),
        in_specs=in_specs,
        out_specs=pl.BlockSpec((tm, tn), lambda i, j: (i, j)),
        out_shape=jax.ShapeDtypeStruct((m, n), out_dtype),
        compiler_params=_cparams(("parallel", "arbitrary")),
        name=name,
    )(*args)


def _rope128(x, c, s_up, s_dn, half):
    return x * c + pltpu.roll(x, LANES - half, 1) * s_up + pltpu.roll(x, half, 1) * s_dn


def _mla_prep_body(cq_ref, ckv_ref, kr_ref, c_ref, su_ref, sd_ref, ct_ref, sut_ref, sdt_ref,
                   qg_ref, kvg_ref, wqt_ref, wk_ref, wvt_ref, qt_ref, k_ref, vt_ref):
    half = MLA_ROPE // 2
    nt = (((1,), (1,)), ((), ()))

    def norm(x_ref, g_ref):
        x = x_ref[...].astype(F32)
        y = x * lax.rsqrt(jnp.mean(x * x, axis=-1, keepdims=True) + NORM_EPS)
        return (y * g_ref[...]).astype(BF16)

    qf_t = lax.dot_general(wqt_ref[...], norm(cq_ref, qg_ref), nt, preferred_element_type=F32)
    ckvn = norm(ckv_ref, kvg_ref)
    kf = jnp.dot(ckvn, wk_ref[...], preferred_element_type=F32)
    vt_ref[...] = lax.dot_general(wvt_ref[...], ckvn, nt, preferred_element_type=F32).astype(vt_ref.dtype)
    kr = _rope128(kr_ref[...].astype(F32), c_ref[...], su_ref[...], sd_ref[...], half)
    ct, sut, sdt = ct_ref[...], sut_ref[...], sdt_ref[...]
    for h in range(MLA_HEADS):
        sl = slice(h * LANES, (h + 1) * LANES)
        x = qf_t[sl, :]
        q = x * ct + pltpu.roll(x, LANES - half, 0) * sut + pltpu.roll(x, half, 0) * sdt
        qt_ref[sl, :] = (q * (MLA_SCALE * LOG2E)).astype(qt_ref.dtype)
        k_ref[:, sl] = (kf[:, sl] + kr).astype(k_ref.dtype)


def _mla_prep(z, tabs, tabs_t, qg, kvg, wqt, wk, wvt):
    s = z.shape[0]
    tm = min(ROW_TILE, s)
    row = lambda w, idx: pl.BlockSpec((tm, w), lambda i: (i, idx))
    full = lambda a: pl.BlockSpec(a.shape, lambda i: (0,) * a.ndim)
    tab = pl.BlockSpec((tm, LANES), lambda i: (i, 0))
    tab_t = pl.BlockSpec((LANES, tm), lambda i: (0, i))
    hw = MLA_HEADS * LANES
    vw = MLA_HEADS * MLA_V
    return pl.pallas_call(
        _mla_prep_body,
        grid=(s // tm,),
        in_specs=[row(MLA_Q_LORA, Z_CQ // MLA_Q_LORA), row(MLA_KV_LORA, Z_CKV // MLA_KV_LORA),
                  row(LANES, Z_KR // LANES), tab, tab, tab, tab_t, tab_t, tab_t,
                  full(qg), full(kvg), full(wqt), full(wk), full(wvt)],
        out_specs=[pl.BlockSpec((hw, tm), lambda i: (0, i)), pl.BlockSpec((tm, hw), lambda i: (i, 0)),
                   pl.BlockSpec((vw, tm), lambda i: (0, i))],
        out_shape=[jax.ShapeDtypeStruct((hw, s), BF16), jax.ShapeDtypeStruct((s, hw), BF16),
                   jax.ShapeDtypeStruct((vw, s), BF16)],
        compiler_params=_cparams(("parallel",)),
        name="mla_prep",
    )(z, z, z, *tabs, *tabs_t, qg, kvg, wqt, wk, wvt)


def _flash_stream_t(q_t, k_at, vt_at, m_ref, l_ref, acc_ref, slot, n_full, t):
    m_ref[slot] = jnp.full((1, t), NEG_INF, F32)
    l_ref[slot] = jnp.zeros((1, t), F32)
    acc_ref[slot] = jnp.zeros(acc_ref.shape[1:], F32)

    def step(j, masked):
        s = jnp.dot(k_at(j), q_t, preferred_element_type=F32)
        if masked:
            key_chunk = lax.broadcasted_iota(jnp.int32, (t, t), 0) // CHUNK
            qry_chunk = lax.broadcasted_iota(jnp.int32, (t, t), 1) // CHUNK
            s = jnp.where(key_chunk <= qry_chunk, s, NEG_INF)
        m_prev = m_ref[slot]
        m_new = jnp.maximum(m_prev, jnp.max(s, axis=0, keepdims=True))
        alpha = jnp.exp2(m_prev - m_new)
        p = jnp.exp2(s - m_new)
        l_ref[slot] = alpha * l_ref[slot] + jnp.sum(p, axis=0, keepdims=True)
        acc_ref[slot] = alpha * acc_ref[slot] + jnp.dot(vt_at(j), p.astype(BF16), preferred_element_type=F32)
        m_ref[slot] = m_new

    def body(j, carry):
        step(j, False)
        return carry

    lax.fori_loop(0, n_full, body, 0)
    step(n_full, True)


def _mla_attn_body(qt_ref, k_ref, vt_ref, o_ref, m_ref, l_ref, acc_ref, *, t):
    i = pl.program_id(1)
    rows = lambda j: pl.ds(pl.multiple_of(j * t, t), t)
    for hh in range(2):
        sl = slice(hh * LANES, (hh + 1) * LANES)
        k_at = lambda j, sl=sl: k_ref[rows(j), sl]
        vt_at = lambda j, hh=hh: vt_ref[hh * MLA_V:(hh + 1) * MLA_V, rows(j)]
        _flash_stream_t(qt_ref[sl, :], k_at, vt_at, m_ref, l_ref, acc_ref, hh, i, t)
    o_t = jnp.concatenate([acc_ref[0] / l_ref[0], acc_ref[1] / l_ref[1]], axis=0)
    o_ref[...] = o_t.T.astype(o_ref.dtype)


def _mla_attn(qt, k, vt):
    s = k.shape[0]
    t = min(ATTN_TILE, s)
    pairs = MLA_HEADS // 2
    return pl.pallas_call(
        functools.partial(_mla_attn_body, t=t),
        grid=(pairs, s // t),
        in_specs=[pl.BlockSpec((2 * LANES, t), lambda p, i: (p, i)),
                  pl.BlockSpec((s, 2 * LANES), lambda p, i: (0, p)),
                  pl.BlockSpec((2 * MLA_V, s), lambda p, i: (p, 0))],
        out_specs=pl.BlockSpec((t, 2 * MLA_V), lambda p, i: (i, p)),
        out_shape=jax.ShapeDtypeStruct((s, MLA_HEADS * MLA_V), BF16),
        scratch_shapes=[pltpu.VMEM((2, 1, t), F32), pltpu.VMEM((2, 1, t), F32),
                        pltpu.VMEM((2, MLA_V, t), F32)],
        compiler_params=_cparams(("parallel", "arbitrary")),
        name="mla_attn",
    )(qt, k, vt)


def _diff_prep_body(dq_ref, dk_ref, dv_ref, c_ref, su_ref, sd_ref, qt_ref, k_ref, vt_ref):
    half = DIFF_ROT // 2
    c, su, sd = c_ref[...], su_ref[...], sd_ref[...]
    tm = dq_ref.shape[0]
    lane = lax.broadcasted_iota(jnp.int32, (tm, LANES), 1)
    vt_ref[...] = dv_ref[...].astype(F32).T.astype(vt_ref.dtype)
    for h in range(DIFF_HEADS):
        sl = slice(h * LANES, (h + 1) * LANES)
        q = _rope128(dq_ref[:, sl].astype(F32), c, su, sd, half) * (DIFF_SCALE * LOG2E)
        qt_ref[(2 * h) * LANES:(2 * h + 1) * LANES, :] = jnp.where(lane < DIFF_HEAD_DIM, q, 0.0).T.astype(qt_ref.dtype)
        qt_ref[(2 * h + 1) * LANES:(2 * h + 2) * LANES, :] = jnp.where(lane >= DIFF_HEAD_DIM, q, 0.0).T.astype(qt_ref.dtype)
        k_ref[:, sl] = _rope128(dk_ref[:, sl].astype(F32), c, su, sd, half).astype(k_ref.dtype)


def _diff_prep(z, tabs):
    s = z.shape[0]
    tm = min(ROW_TILE, s)
    tab = pl.BlockSpec((tm, LANES), lambda i: (i, 0))
    col = lambda start: pl.BlockSpec((tm, DIFF_WIDTH), lambda i: (i, start // DIFF_WIDTH))
    return pl.pallas_call(
        _diff_prep_body,
        grid=(s // tm,),
        in_specs=[col(Z_DQ), col(Z_DK), col(Z_DV), tab, tab, tab],
        out_specs=[pl.BlockSpec((2 * DIFF_WIDTH, tm), lambda i: (0, i)),
                   pl.BlockSpec((tm, DIFF_WIDTH), lambda i: (i, 0)),
                   pl.BlockSpec((DIFF_WIDTH, tm), lambda i: (0, i))],
        out_shape=[jax.ShapeDtypeStruct((2 * DIFF_WIDTH, s), BF16), jax.ShapeDtypeStruct((s, DIFF_WIDTH), BF16),
                   jax.ShapeDtypeStruct((DIFF_WIDTH, s), BF16)],
        compiler_params=_cparams(("parallel",)),
        name="diff_prep",
    )(z, z, z, *tabs)


def _diff_attn_body(qt_ref, k_ref, vt_ref, lam_ref, g_ref, o_ref, m_ref, l_ref, acc_ref, *, t, lam_init):
    i = pl.program_id(1)
    rows = lambda j: pl.ds(pl.multiple_of(j * t, t), t)
    k_at = lambda j: k_ref[rows(j), :]
    vt_at = lambda j: vt_ref[:, rows(j)]
    for c in range(2):
        _flash_stream_t(qt_ref[c * LANES:(c + 1) * LANES, :], k_at, vt_at, m_ref, l_ref, acc_ref, c, i, t)
    lv = lam_ref[...]
    lam = (jnp.exp(jnp.sum(lv[0:1] * lv[1:2], axis=1, keepdims=True))
           - jnp.exp(jnp.sum(lv[2:3] * lv[3:4], axis=1, keepdims=True)) + lam_init)
    o = (acc_ref[0] / l_ref[0] - lam * (acc_ref[1] / l_ref[1])).T
    o = o * lax.rsqrt(jnp.mean(o * o, axis=-1, keepdims=True) + NORM_EPS)
    o_ref[...] = (o * g_ref[...] * (1.0 - lam_init)).astype(o_ref.dtype)


def _diff_attn(qt, k, vt, lam_vecs, norm_g, lam_init):
    s = k.shape[0]
    t = min(ATTN_TILE, s)
    return pl.pallas_call(
        functools.partial(_diff_attn_body, t=t, lam_init=lam_init),
        grid=(DIFF_HEADS, s // t),
        in_specs=[pl.BlockSpec((2 * LANES, t), lambda h, i: (h, i)),
                  pl.BlockSpec((s, LANES), lambda h, i: (0, h)),
                  pl.BlockSpec((LANES, s), lambda h, i: (h, 0)),
                  pl.BlockSpec(lam_vecs.shape, lambda h, i: (0, 0)),
                  pl.BlockSpec((1, LANES), lambda h, i: (0, 0))],
        out_specs=pl.BlockSpec((t, LANES), lambda h, i: (i, h)),
        out_shape=jax.ShapeDtypeStruct((s, DIFF_WIDTH), BF16),
        scratch_shapes=[pltpu.VMEM((2, 1, t), F32), pltpu.VMEM((2, 1, t), F32),
                        pltpu.VMEM((2, LANES, t), F32)],
        compiler_params=_cparams(("parallel", "arbitrary")),
        name="diff_attn",
    )(qt, k, vt, lam_vecs, norm_g.reshape(1, LANES))


def _conv_body(bg_ref, cg_ref, xv_ref, cgh_ref, xvh_ref, a_ref, gt_ref, ah_ref, gth_ref,
               scw_ref, dww_ref, dwb_ref, lng_ref, lnb_ref, sc_ref, cf_ref, ext_ref):
    tm = bg_ref.shape[0]
    first = pl.program_id(0) == 0

    def fill(cur, halo):
        ext_ref[0:HALO, :] = jnp.where(first, 0.0, halo)
        ext_ref[HALO:HALO + tm, :] = cur

    def conv(w_ref, taps):
        acc = jnp.zeros((tm, ext_ref.shape[1]), F32)
        for j in range(taps):
            off = HALO - (taps - 1) + j
            acc = acc + w_ref[j:j + 1, :] * ext_ref[off:off + tm, :]
        return acc

    fill(cg_ref[...].astype(F32) * xv_ref[...].astype(F32), cgh_ref[...].astype(F32) * xvh_ref[...].astype(F32))
    sc_ref[...] = (bg_ref[...].astype(F32) * conv(scw_ref, SC_KERNEL)).astype(sc_ref.dtype)

    fill(a_ref[...].astype(F32) * jax.nn.sigmoid(gt_ref[...].astype(F32)),
         ah_ref[...].astype(F32) * jax.nn.sigmoid(gth_ref[...].astype(F32)))
    u = conv(dww_ref, CONF_KERNEL) + dwb_ref[...]
    mu = jnp.mean(u, axis=-1, keepdims=True)
    var = jnp.mean(jnp.square(u - mu), axis=-1, keepdims=True)
    y = (u - mu) * lax.rsqrt(var + LN_EPS) * lng_ref[...] + lnb_ref[...]
    cf_ref[...] = (y * jax.nn.sigmoid(y)).astype(cf_ref.dtype)


def _conv_branches(z, sc_w, dw_w, dw_b, ln_g, ln_b):
    s = z.shape[0]
    tm = min(ROW_TILE, s)
    w = SC_WIDTH
    cur = lambda col: pl.BlockSpec((tm, w), lambda i: (i, col // w))
    halo = lambda col: pl.BlockSpec((HALO, w), lambda i: (jnp.maximum(i * (tm // HALO) - 1, 0), col // w))
    full = lambda a: pl.BlockSpec(a.shape, lambda i: (0,) * a.ndim)
    vec = lambda a: a.reshape(1, -1)
    args = [sc_w, dw_w, vec(dw_b), vec(ln_g), vec(ln_b)]
    return pl.pallas_call(
        _conv_body,
        grid=(s // tm,),
        in_specs=[cur(Z_SC), cur(Z_SC + w), cur(Z_SC + 2 * w), halo(Z_SC + w), halo(Z_SC + 2 * w),
                  cur(Z_CONF), cur(Z_CONF + w), halo(Z_CONF), halo(Z_CONF + w)] + [full(a) for a in args],
        out_specs=[pl.BlockSpec((tm, w), lambda i: (i, 0)), pl.BlockSpec((tm, w), lambda i: (i, 0))],
        out_shape=[jax.ShapeDtypeStruct((s, w), BF16), jax.ShapeDtypeStruct((s, w), BF16)],
        scratch_shapes=[pltpu.VMEM((HALO + tm, w), F32)],
        compiler_params=_cparams(("parallel",)),
        name="conv_branches",
    )(*([z] * 9), *args)


def _merge_body(b0, b1, b2, b3, w0, w1, w2, w3, g0, g1, g2, g3, gb_ref, o_ref):
    acc = None
    for n, (b_ref, w_ref, g_ref) in enumerate(((b0, w0, g0), (b1, w1, g1), (b2, w2, g2), (b3, w3, g3))):
        y = jnp.dot(b_ref[...], w_ref[...], preferred_element_type=F32)
        t = jax.nn.sigmoid(g_ref[...].astype(F32) + gb_ref[n:n + 1, :]) * y
        acc = t if acc is None else acc + t
    o_ref[...] = acc.astype(o_ref.dtype)


def _merge(branches, w_outs, z, gate_b):
    s = z.shape[0]
    tm = min(ROW_TILE, s)
    tn = ROW_TILE
    kw = branches[0].shape[1]
    return pl.pallas_call(
        _merge_body,
        grid=(s // tm, D_MODEL // tn),
        in_specs=([pl.BlockSpec((tm, kw), lambda i, j: (i, 0))] * 4
                  + [pl.BlockSpec((kw, tn), lambda i, j: (0, j))] * 4
                  + [pl.BlockSpec((tm, tn), lambda i, j, b=b: (i, (Z_GATE + b * D_MODEL) // tn + j))
                     for b in range(N_BRANCH)]
                  + [pl.BlockSpec((N_BRANCH, tn), lambda i, j: (0, j))]),
        out_specs=pl.BlockSpec((tm, tn), lambda i, j: (i, j)),
        out_shape=jax.ShapeDtypeStruct((s, D_MODEL), BF16),
        compiler_params=_cparams(("parallel", "arbitrary")),
        name="merge",
    )(*branches, *w_outs, z, z, z, z, gate_b)


def _peer_candidates(a, b, op):
    parts = [op(a[0:1], b)]
    parts += [op(a[i:i + 1], b[0:SUBLANES]) for i in range(1, SUBLANES)]
    parts.append(op(a[SUBLANES:], b[0:1]))
    return jnp.concatenate(parts, axis=0)


def _peer_scores_body(q_ref, k1_ref, k2_ref, s1_ref, s2_ref, e1_ref, e2n_ref, tau_ref, t1_ref, t2_ref):
    tt = q_ref.shape[0]
    key_iota = lax.broadcasted_iota(jnp.int32, (PEER_N_KEYS, tt), 0).astype(F32)

    def extract(s, t_ref, r):
        m = jnp.max(s, axis=0, keepdims=True)
        idx = jnp.min(jnp.where(s == m, key_iota, float(PEER_N_KEYS)), axis=0, keepdims=True)
        t_ref[pl.ds(r, 1), :] = m
        return jnp.where(key_iota == idx, -jnp.inf, s)

    def head(h, carry):
        qh = q_ref[:, pl.ds(pl.multiple_of(h * PEER_KEY_DIM, PEER_KEY_DIM), PEER_KEY_DIM)]
        dims = (((1,), (1,)), ((), ()))
        s1 = lax.dot_general(k1_ref[...], qh, dims, precision=lax.Precision.HIGHEST, preferred_element_type=F32)
        s2 = lax.dot_general(k2_ref[...], qh, dims, precision=lax.Precision.HIGHEST, preferred_element_type=F32)

        def top_body(r, ss):
            return extract(ss[0], t1_ref, r), extract(ss[1], t2_ref, r)

        lax.fori_loop(0, PEER_TOPK, top_body, (s1, s2))
        t1, t2 = t1_ref[...], t2_ref[...]
        cand = _peer_candidates(t1, t2, jnp.add)

        def tau_body(r, carry):
            c, cnt, tau = carry
            m = jnp.max(c, axis=0, keepdims=True)
            eq = c == m
            tau = jnp.where(cnt < float(PEER_TOPK), m, tau)
            cnt = cnt + jnp.sum(jnp.where(eq, 1.0, 0.0), axis=0, keepdims=True)
            return jnp.where(eq, -jnp.inf, c), cnt, tau

        zero = jnp.zeros((1, tt), F32)
        _, _, tau = lax.fori_loop(0, PEER_TOPK, tau_body, (cand, zero, zero))

        m1, m2 = t1[0:1], t2[0:1]
        ec = _peer_candidates(jnp.exp(t1 - m1), jnp.exp(t2 - m2), jnp.multiply)
        zden = jnp.sum(jnp.where(cand >= tau, ec, 0.0), axis=0, keepdims=True)
        s1_ref[h] = s1
        s2_ref[h] = s2
        e1_ref[h] = jnp.exp(s1 - m1)
        e2n_ref[h] = jnp.exp(s2 - m2) / zden
        tau_ref[pl.ds(h, 1), :] = tau
        return carry

    lax.fori_loop(0, PEER_HEADS, head, 0)


def _peer_scores(q, k1, k2):
    s = q.shape[0]
    tt = min(SCORE_TILE, s)
    tab = pl.BlockSpec((PEER_HEADS, PEER_N_KEYS, tt), lambda i: (0, 0, i))
    tab_shape = jax.ShapeDtypeStruct((PEER_HEADS, PEER_N_KEYS, s), F32)
    return pl.pallas_call(
        _peer_scores_body,
        grid=(s // tt,),
        in_specs=[pl.BlockSpec((tt, PEER_HEADS * PEER_KEY_DIM), lambda i: (i, 0)),
                  pl.BlockSpec(k1.shape, lambda i: (0, 0)), pl.BlockSpec(k2.shape, lambda i: (0, 0))],
        out_specs=[tab, tab, tab, tab, pl.BlockSpec((PEER_HEADS, tt), lambda i: (0, i))],
        out_shape=[tab_shape, tab_shape, tab_shape, tab_shape, jax.ShapeDtypeStruct((PEER_HEADS, s), F32)],
        scratch_shapes=[pltpu.VMEM((PEER_TOPK, tt), F32), pltpu.VMEM((PEER_TOPK, tt), F32)],
        compiler_params=_cparams(("parallel",)),
        name="peer_scores",
    )(q, k1, k2)


def _peer_dense_body(h_ref, u_ref, v_ref, s1_ref, e1_ref, s2_ref, e2n_ref, tau_ref, x_ref, o_ref, w_ref):
    e = pl.program_id(1)
    tt = h_ref.shape[0]
    eb = u_ref.shape[0]
    ncol = tt // LANES
    sub_per_chunk = PEER_CHUNK // PEER_N_KEYS

    @pl.when(e == 0)
    def _():
        o_ref[...] = x_ref[...]

    def gates(il, c):
        cols = slice(c * LANES, (c + 1) * LANES)
        g = jnp.zeros((PEER_N_KEYS, LANES), F32)
        for h in range(PEER_HEADS):
            ssum = s2_ref[h, :, cols] + s1_ref[h, il:il + 1, cols]
            val = e2n_ref[h, :, cols] * e1_ref[h, il:il + 1, cols]
            g = g + jnp.where(ssum >= tau_ref[h:h + 1, cols], val, 0.0)
        return g

    for k in range(eb // PEER_CHUNK):
        rows = slice(k * PEER_CHUNK, (k + 1) * PEER_CHUNK)
        a_t = lax.dot_general(u_ref[rows, :], h_ref[...], (((1,), (1,)), ((), ())),
                              preferred_element_type=F32)
        gelu_t = 0.5 * a_t * (1.0 + lax.erf(a_t * (2.0 ** -0.5)))
        for sub in range(sub_per_chunk):
            il = k * sub_per_chunk + sub
            for c in range(ncol):
                cols = slice(c * LANES, (c + 1) * LANES)
                r0 = sub * PEER_N_KEYS
                w_t = gelu_t[r0:r0 + PEER_N_KEYS, cols] * gates(il, c)
                w_ref[cols, il * PEER_N_KEYS:(il + 1) * PEER_N_KEYS] = w_t.T.astype(w_ref.dtype)

    o_ref[...] += jnp.dot(w_ref[...], v_ref[...], preferred_element_type=F32)


def _peer_dense(hn, u, v, s1, e1, s2, e2n, tau, x):
    s, d = hn.shape
    tt = min(PEER_TOKENS, s)
    eb = PEER_EXPERTS
    ne = u.shape[0]
    sub = pl.BlockSpec((PEER_HEADS, eb // PEER_N_KEYS, tt), lambda t, e: (0, e, t))
    tab = pl.BlockSpec((PEER_HEADS, PEER_N_KEYS, tt), lambda t, e: (0, 0, t))
    return pl.pallas_call(
        _peer_dense_body,
        grid=(s // tt, ne // eb),
        in_specs=[pl.BlockSpec((tt, d), lambda t, e: (t, 0)),
                  pl.BlockSpec((eb, d), lambda t, e: (e, 0)),
                  pl.BlockSpec((eb, d), lambda t, e: (e, 0)),
                  sub, sub, tab, tab,
                  pl.BlockSpec((PEER_HEADS, tt), lambda t, e: (0, t)),
                  pl.BlockSpec((tt, d), lambda t, e: (t, 0))],
        out_specs=pl.BlockSpec((tt, d), lambda t, e: (t, 0)),
        out_shape=jax.ShapeDtypeStruct((s, d), F32),
        scratch_shapes=[pltpu.VMEM((tt, eb), BF16)],
        compiler_params=_cparams(("parallel", "arbitrary")),
        name="peer_dense",
    )(hn, u, v, s1, e1, s2, e2n, tau, x)


def _rope_tables(positions, dim, lanes_x1):
    half = dim // 2
    inv_freq = ROPE_THETA ** (-jnp.arange(0, dim, 2, dtype=F32) / dim)
    ang = positions.astype(F32)[:, None] * inv_freq
    cos, sin = jnp.cos(ang), jnp.sin(ang)
    s = positions.shape[0]
    c_parts, su_parts, sd_parts = [], [], []
    pos = 0
    for off in lanes_x1:
        gap = off - pos
        c_parts += [jnp.ones((s, gap), F32), cos, cos]
        su_parts += [jnp.zeros((s, gap), F32), -sin, jnp.zeros((s, half), F32)]
        sd_parts += [jnp.zeros((s, gap + half), F32), sin]
        pos = off + dim
    tail = LANES - pos
    c_parts.append(jnp.ones((s, tail), F32))
    su_parts.append(jnp.zeros((s, tail), F32))
    sd_parts.append(jnp.zeros((s, tail), F32))
    return tuple(jnp.concatenate(p, axis=1) for p in (c_parts, su_parts, sd_parts))


def _layout_w_in(w):
    d = w.shape[0]
    seg = [w[:, o:o + n] for o, n in zip(IN_OFFSETS, IN_SIZES)]
    cq, ckv, kr, sc, dq, dk, dv, conf, gate = seg
    kr_blk = jnp.concatenate([jnp.zeros((d, MLA_NOPE), w.dtype), kr,
                              jnp.zeros((d, LANES - MLA_NOPE - MLA_ROPE), w.dtype)], axis=1)
    pad = jnp.zeros((d, Z_COLS - (Z_KR + LANES)), w.dtype)
    return jnp.concatenate([cq, sc, dq, dk, dv, conf, gate, ckv, kr_blk, pad], axis=1).astype(BF16)


def _layout_mla_weights(w_uq, w_ukv):
    lq = w_uq.shape[0]
    wq = w_uq.reshape(lq, MLA_HEADS, MLA_NOPE + MLA_ROPE)
    wq = jnp.pad(wq, ((0, 0), (0, 0), (0, LANES - MLA_NOPE - MLA_ROPE))).reshape(lq, MLA_HEADS * LANES)
    lk = w_ukv.shape[0]
    wkv = w_ukv.reshape(lk, MLA_HEADS, MLA_NOPE + MLA_V)
    wk = jnp.pad(wkv[:, :, :MLA_NOPE], ((0, 0), (0, 0), (0, LANES - MLA_NOPE))).reshape(lk, MLA_HEADS * LANES)
    wv = wkv[:, :, MLA_NOPE:].reshape(lk, MLA_HEADS * MLA_V)
    return wq.T.astype(BF16), wk.astype(BF16), wv.T.astype(BF16)


def kernel(x, positions, mix_norm_g, w_in, gate_b, mla_q_norm_g, mla_w_uq, mla_kv_norm_g, mla_w_ukv, mla_w_out,
           sc_conv_w, sc_w_out, diff_lambda, diff_norm_g, diff_w_out, conf_dw_w, conf_dw_b, conf_ln_g, conf_ln_b,
           conf_w_out, w_o, ffn_norm_g, peer_w_q, peer_sub_keys, peer_u, peer_v, final_norm_g):
    b, s, d = x.shape
    assert b == 1 and d == D_MODEL
    xs = x.reshape(s, d)
    pos = positions.reshape(s)
    mla_tabs = _rope_tables(pos, MLA_ROPE, (MLA_NOPE,))
    mla_tabs_t = tuple(t.T for t in mla_tabs)
    diff_tabs = _rope_tables(pos, DIFF_ROT, (0, DIFF_HEAD_DIM))

    for i in range(DEPTH):
        lam_init = 0.8 - 0.6 * math.exp(-0.3 * i)
        hn = _rmsnorm(xs, mix_norm_g[i], BF16)
        z = _matmul(hn, _layout_w_in(w_in[i]), BF16, name="in_proj")
        wqt, wk, wvt = _layout_mla_weights(mla_w_uq[i], mla_w_ukv[i])
        qt, k, vt = _mla_prep(z, mla_tabs, mla_tabs_t, mla_q_norm_g[i].reshape(1, -1),
                              mla_kv_norm_g[i].reshape(1, -1), wqt, wk, wvt)
        o_mla = _mla_attn(qt, k, vt)
        o_sc, o_conf = _conv_branches(z, sc_conv_w[i], conf_dw_w[i], conf_dw_b[i], conf_ln_g[i], conf_ln_b[i])
        qdt, kd, vdt = _diff_prep(z, diff_tabs)
        o_diff = _diff_attn(qdt, kd, vdt, diff_lambda[i], diff_norm_g[i], lam_init)
        merged = _merge((o_mla, o_sc, o_diff, o_conf),
                        tuple(w.astype(BF16) for w in (mla_w_out[i], sc_w_out[i], diff_w_out[i], conf_w_out[i])),
                        z, gate_b[i])
        xs = _matmul(merged, w_o[i].astype(BF16), F32, residual=xs, name="out_proj")
        hf = _rmsnorm(xs, ffn_norm_g[i], BF16)
        pq = _matmul(hf, peer_w_q[i].astype(BF16), F32, name="peer_q")
        zk = jnp.zeros((PEER_N_KEYS, PEER_HALF), F32)
        k1 = jnp.concatenate([peer_sub_keys[i, 0], zk], axis=1)
        k2 = jnp.concatenate([zk, peer_sub_keys[i, 1]], axis=1)
        s1, s2, e1, e2n, tau = _peer_scores(pq, k1, k2)
        xs = _peer_dense(hf, peer_u[i].astype(BF16), peer_v[i].astype(BF16), s1, e1, s2, e2n, tau, xs)
    out = _rmsnorm(xs, final_norm_g, F32)
    return out.reshape(b, s, d)
```

```python
import functools
import math

import jax
import jax.numpy as jnp
import numpy as np
from jax import lax
from jax.experimental import pallas as pl
from jax.experimental.pallas import tpu as pltpu

F32 = jnp.float32
BF16 = jnp.bfloat16

D_MODEL = 2048
DEPTH = 2
CHUNK = 64
ROPE_THETA = 500000.0
NORM_EPS = 1e-6
LN_EPS = 1e-5
NEG_INF = -1e30
N_BRANCH = 4

MLA_HEADS = 8
MLA_Q_LORA = 512
MLA_KV_LORA = 256
MLA_NOPE = 64
MLA_ROPE = 32
MLA_V = 64
MLA_SCALE = (MLA_NOPE + MLA_ROPE) ** -0.5

SC_WIDTH = 512
SC_KERNEL = 3

DIFF_HEADS = 4
DIFF_HEAD_DIM = 64
DIFF_ROT = DIFF_HEAD_DIM // 4
DIFF_SCALE = DIFF_HEAD_DIM ** -0.5
DIFF_WIDTH = DIFF_HEADS * 2 * DIFF_HEAD_DIM

CONF_WIDTH = 512
CONF_KERNEL = 31

PEER_HEADS = 8
PEER_N_KEYS = 128
PEER_N_EXPERTS = PEER_N_KEYS * PEER_N_KEYS
PEER_KEY_DIM = 128
PEER_HALF = PEER_KEY_DIM // 2
PEER_TOPK = 16

IN_SIZES = (MLA_Q_LORA, MLA_KV_LORA, MLA_ROPE, 3 * SC_WIDTH, DIFF_WIDTH, DIFF_WIDTH, DIFF_WIDTH,
            2 * CONF_WIDTH, N_BRANCH * D_MODEL)
IN_OFFSETS = tuple(int(v) for v in np.cumsum((0,) + IN_SIZES)[:-1])

LOG2E = math.log2(math.e)

LANES = 128
SUBLANES = 8
HALO = 32
VMEM_LIMIT = 56 * 1024 * 1024

ROW_TILE = 512
MM_TILE = 1024
ATTN_TILE = 512
SCORE_TILE = 512
PEER_TOKENS = 512
PEER_EXPERTS = 1024
PEER_CHUNK = 256

Z_CQ = 0
Z_SC = 512
Z_DQ = 2048
Z_DK = 2560
Z_DV = 3072
Z_CONF = 3584
Z_GATE = 4608
Z_CKV = 12800
Z_KR = 13056
Z_COLS = 13312


def _cparams(sem):
    return pltpu.CompilerParams(dimension_semantics=sem, vmem_limit_bytes=VMEM_LIMIT)


def _rmsnorm_body(x_ref, g_ref, o_ref, *maybe_ot_ref):
    x = x_ref[...].astype(F32)
    y = x * lax.rsqrt(jnp.mean(x * x, axis=-1, keepdims=True) + NORM_EPS) * g_ref[...]
    o_ref[...] = y.astype(o_ref.dtype)
    for ot_ref in maybe_ot_ref:
        ot_ref[...] = y.T.astype(ot_ref.dtype)


def _rmsnorm(x, g, out_dtype, with_transposed=False):
    s, d = x.shape
    tm = min(ROW_TILE, s)
    out_specs = [pl.BlockSpec((tm, d), lambda i: (i, 0))]
    out_shape = [jax.ShapeDtypeStruct((s, d), out_dtype)]
    if with_transposed:
        out_specs.append(pl.BlockSpec((d, tm), lambda i: (0, i)))
        out_shape.append(jax.ShapeDtypeStruct((d, s), out_dtype))
    out = pl.pallas_call(
        _rmsnorm_body,
        grid=(s // tm,),
        in_specs=[pl.BlockSpec((tm, d), lambda i: (i, 0)), pl.BlockSpec((1, d), lambda i: (0, 0))],
        out_specs=out_specs,
        out_shape=out_shape,
        compiler_params=_cparams(("parallel",)),
        name="rmsnorm",
    )(x, g.reshape(1, d))
    return out if with_transposed else out[0]


def _matmul_body(a_ref, b_ref, o_ref):
    o_ref[...] = jnp.dot(a_ref[...], b_ref[...], preferred_element_type=F32).astype(o_ref.dtype)


def _matmul_res_body(a_ref, b_ref, r_ref, o_ref):
    o_ref[...] = (r_ref[...] + jnp.dot(a_ref[...], b_ref[...], preferred_element_type=F32)).astype(o_ref.dtype)


def _matmul(a, b, out_dtype, residual=None, name="matmul"):
    m, k = a.shape
    n = b.shape[1]
    tm, tn = min(MM_TILE, m), min(MM_TILE, n)
    in_specs = [pl.BlockSpec((tm, k), lambda i, j: (i, 0)), pl.BlockSpec((k, tn), lambda i, j: (0, j))]
    args = [a, b]
    body = _matmul_body
    if residual is not None:
        in_specs.append(pl.BlockSpec((tm, tn), lambda i, j: (i, j)))
        args.append(residual)
        body = _matmul_res_body
    return pl.pallas_call(
        body,
        grid=(m // tm, n // tn),
        in_specs=in_specs,
        out_specs=pl.BlockSpec((tm, tn), lambda i, j: (i, j)),
        out_shape=jax.ShapeDtypeStruct((m, n), out_dtype),
        compiler_params=_cparams(("parallel", "arbitrary")),
        name=name,
    )(*args)


def _rope128(x, c, s_up, s_dn, half):
    return x * c + pltpu.roll(x, LANES - half, 1) * s_up + pltpu.roll(x, half, 1) * s_dn


def _mla_prep_body(cq_ref, ckv_ref, kr_ref, c_ref, su_ref, sd_ref, ct_ref, sut_ref, sdt_ref,
                   qg_ref, kvg_ref, wqt_ref, wk_ref, wvt_ref, qt_ref, k_ref, vt_ref):
    half = MLA_ROPE // 2
    nt = (((1,), (1,)), ((), ()))

    def norm(x_ref, g_ref):
        x = x_ref[...].astype(F32)
        y = x * lax.rsqrt(jnp.mean(x * x, axis=-1, keepdims=True) + NORM_EPS)
        return (y * g_ref[...]).astype(BF16)

    qf_t = lax.dot_general(wqt_ref[...], norm(cq_ref, qg_ref), nt, preferred_element_type=F32)
    ckvn = norm(ckv_ref, kvg_ref)
    kf = jnp.dot(ckvn, wk_ref[...], preferred_element_type=F32)
    vt_ref[...] = lax.dot_general(wvt_ref[...], ckvn, nt, preferred_element_type=F32).astype(vt_ref.dtype)
    kr = _rope128(kr_ref[...].astype(F32), c_ref[...], su_ref[...], sd_ref[...], half)
    ct, sut, sdt = ct_ref[...], sut_ref[...], sdt_ref[...]
    for h in range(MLA_HEADS):
        sl = slice(h * LANES, (h + 1) * LANES)
        x = qf_t[sl, :]
        q = x * ct + pltpu.roll(x, LANES - half, 0) * sut + pltpu.roll(x, half, 0) * sdt
        qt_ref[sl, :] = (q * (MLA_SCALE * LOG2E)).astype(qt_ref.dtype)
        k_ref[:, sl] = (kf[:, sl] + kr).astype(k_ref.dtype)


def _mla_prep(z, tabs, tabs_t, qg, kvg, wqt, wk, wvt):
    s = z.shape[0]
    tm = min(ROW_TILE, s)
    row = lambda w, idx: pl.BlockSpec((tm, w), lambda i: (i, idx))
    full = lambda a: pl.BlockSpec(a.shape, lambda i: (0,) * a.ndim)
    tab = pl.BlockSpec((tm, LANES), lambda i: (i, 0))
    tab_t = pl.BlockSpec((LANES, tm), lambda i: (0, i))
    hw = MLA_HEADS * LANES
    vw = MLA_HEADS * MLA_V
    return pl.pallas_call(
        _mla_prep_body,
        grid=(s // tm,),
        in_specs=[row(MLA_Q_LORA, Z_CQ // MLA_Q_LORA), row(MLA_KV_LORA, Z_CKV // MLA_KV_LORA),
                  row(LANES, Z_KR // LANES), tab, tab, tab, tab_t, tab_t, tab_t,
                  full(qg), full(kvg), full(wqt), full(wk), full(wvt)],
        out_specs=[pl.BlockSpec((hw, tm), lambda i: (0, i)), pl.BlockSpec((tm, hw), lambda i: (i, 0)),
                   pl.BlockSpec((vw, tm), lambda i: (0, i))],
        out_shape=[jax.ShapeDtypeStruct((hw, s), BF16), jax.ShapeDtypeStruct((s, hw), BF16),
                   jax.ShapeDtypeStruct((vw, s), BF16)],
        compiler_params=_cparams(("parallel",)),
        name="mla_prep",
    )(z, z, z, *tabs, *tabs_t, qg, kvg, wqt, wk, wvt)


def _flash_streams_t(q_ts, k_ats, vt_ats, m_ref, l_ref, acc_ref, n_full, t):
    streams = range(len(q_ts))
    for n in streams:
        m_ref[n] = jnp.full((1, t), NEG_INF, F32)
        l_ref[n] = jnp.zeros((1, t), F32)
        acc_ref[n] = jnp.zeros(acc_ref.shape[1:], F32)

    def step(j, masked):
        scores = [jnp.dot(k_ats[n](j), q_ts[n], preferred_element_type=F32) for n in streams]
        probs, alphas = [], []
        for n in streams:
            s = scores[n]
            if masked:
                key_chunk = lax.broadcasted_iota(jnp.int32, (t, t), 0) // CHUNK
                qry_chunk = lax.broadcasted_iota(jnp.int32, (t, t), 1) // CHUNK
                s = jnp.where(key_chunk <= qry_chunk, s, NEG_INF)
            m_prev = m_ref[n]
            m_new = jnp.maximum(m_prev, jnp.max(s, axis=0, keepdims=True))
            alpha = jnp.exp2(m_prev - m_new)
            p = jnp.exp2(s - m_new)
            l_ref[n] = alpha * l_ref[n] + jnp.sum(p, axis=0, keepdims=True)
            m_ref[n] = m_new
            probs.append(p.astype(BF16))
            alphas.append(alpha)
        for n in streams:
            acc_ref[n] = alphas[n] * acc_ref[n] + jnp.dot(vt_ats[n](j), probs[n], preferred_element_type=F32)

    def body(j, carry):
        step(j, False)
        return carry

    lax.fori_loop(0, n_full, body, 0)
    step(n_full, True)


def _mla_attn_body(qt_ref, k_ref, vt_ref, o_ref, m_ref, l_ref, acc_ref, *, t):
    i = pl.program_id(1)
    rows = lambda j: pl.ds(pl.multiple_of(j * t, t), t)
    lanes = [slice(hh * LANES, (hh + 1) * LANES) for hh in range(2)]
    k_ats = [lambda j, sl=sl: k_ref[rows(j), sl] for sl in lanes]
    vt_ats = [lambda j, hh=hh: vt_ref[hh * MLA_V:(hh + 1) * MLA_V, rows(j)] for hh in range(2)]
    _flash_streams_t([qt_ref[sl, :] for sl in lanes], k_ats, vt_ats, m_ref, l_ref, acc_ref, i, t)
    o_t = jnp.concatenate([acc_ref[0] / l_ref[0], acc_ref[1] / l_ref[1]], axis=0)
    o_ref[...] = o_t.T.astype(o_ref.dtype)


def _mla_attn(qt, k, vt):
    s = k.shape[0]
    t = min(ATTN_TILE, s)
    pairs = MLA_HEADS // 2
    return pl.pallas_call(
        functools.partial(_mla_attn_body, t=t),
        grid=(pairs, s // t),
        in_specs=[pl.BlockSpec((2 * LANES, t), lambda p, i: (p, i)),
                  pl.BlockSpec((s, 2 * LANES), lambda p, i: (0, p)),
                  pl.BlockSpec((2 * MLA_V, s), lambda p, i: (p, 0))],
        out_specs=pl.BlockSpec((t, 2 * MLA_V), lambda p, i: (i, p)),
        out_shape=jax.ShapeDtypeStruct((s, MLA_HEADS * MLA_V), BF16),
        scratch_shapes=[pltpu.VMEM((2, 1, t), F32), pltpu.VMEM((2, 1, t), F32),
                        pltpu.VMEM((2, MLA_V, t), F32)],
        compiler_params=_cparams(("parallel", "arbitrary")),
        name="mla_attn",
    )(qt, k, vt)


def _diff_prep_body(dq_ref, dk_ref, dv_ref, c_ref, su_ref, sd_ref, qt_ref, k_ref, vt_ref):
    half = DIFF_ROT // 2
    c, su, sd = c_ref[...], su_ref[...], sd_ref[...]
    tm = dq_ref.shape[0]
    lane = lax.broadcasted_iota(jnp.int32, (tm, LANES), 1)
    vt_ref[...] = dv_ref[...].astype(F32).T.astype(vt_ref.dtype)
    for h in range(DIFF_HEADS):
        sl = slice(h * LANES, (h + 1) * LANES)
        q = _rope128(dq_ref[:, sl].astype(F32), c, su, sd, half) * (DIFF_SCALE * LOG2E)
        qt_ref[(2 * h) * LANES:(2 * h + 1) * LANES, :] = jnp.where(lane < DIFF_HEAD_DIM, q, 0.0).T.astype(qt_ref.dtype)
        qt_ref[(2 * h + 1) * LANES:(2 * h + 2) * LANES, :] = jnp.where(lane >= DIFF_HEAD_DIM, q, 0.0).T.astype(qt_ref.dtype)
        k_ref[:, sl] = _rope128(dk_ref[:, sl].astype(F32), c, su, sd, half).astype(k_ref.dtype)


def _diff_prep(z, tabs):
    s = z.shape[0]
    tm = min(ROW_TILE, s)
    tab = pl.BlockSpec((tm, LANES), lambda i: (i, 0))
    col = lambda start: pl.BlockSpec((tm, DIFF_WIDTH), lambda i: (i, start // DIFF_WIDTH))
    return pl.pallas_call(
        _diff_prep_body,
        grid=(s // tm,),
        in_specs=[col(Z_DQ), col(Z_DK), col(Z_DV), tab, tab, tab],
        out_specs=[pl.BlockSpec((2 * DIFF_WIDTH, tm), lambda i: (0, i)),
                   pl.BlockSpec((tm, DIFF_WIDTH), lambda i: (i, 0)),
                   pl.BlockSpec((DIFF_WIDTH, tm), lambda i: (0, i))],
        out_shape=[jax.ShapeDtypeStruct((2 * DIFF_WIDTH, s), BF16), jax.ShapeDtypeStruct((s, DIFF_WIDTH), BF16),
                   jax.ShapeDtypeStruct((DIFF_WIDTH, s), BF16)],
        compiler_params=_cparams(("parallel",)),
        name="diff_prep",
    )(z, z, z, *tabs)


def _diff_attn_body(qt_ref, k_ref, vt_ref, lam_ref, g_ref, o_ref, m_ref, l_ref, acc_ref, *, t, lam_init):
    i = pl.program_id(1)
    rows = lambda j: pl.ds(pl.multiple_of(j * t, t), t)
    k_at = lambda j: k_ref[rows(j), :]
    vt_at = lambda j: vt_ref[:, rows(j)]
    q_ts = [qt_ref[c * LANES:(c + 1) * LANES, :] for c in range(2)]
    _flash_streams_t(q_ts, [k_at, k_at], [vt_at, vt_at], m_ref, l_ref, acc_ref, i, t)
    lv = lam_ref[...]
    lam = (jnp.exp(jnp.sum(lv[0:1] * lv[1:2], axis=1, keepdims=True))
           - jnp.exp(jnp.sum(lv[2:3] * lv[3:4], axis=1, keepdims=True)) + lam_init)
    o = (acc_ref[0] / l_ref[0] - lam * (acc_ref[1] / l_ref[1])).T
    o = o * lax.rsqrt(jnp.mean(o * o, axis=-1, keepdims=True) + NORM_EPS)
    o_ref[...] = (o * g_ref[...] * (1.0 - lam_init)).astype(o_ref.dtype)


def _diff_attn(qt, k, vt, lam_vecs, norm_g, lam_init):
    s = k.shape[0]
    t = min(ATTN_TILE, s)
    return pl.pallas_call(
        functools.partial(_diff_attn_body, t=t, lam_init=lam_init),
        grid=(DIFF_HEADS, s // t),
        in_specs=[pl.BlockSpec((2 * LANES, t), lambda h, i: (h, i)),
                  pl.BlockSpec((s, LANES), lambda h, i: (0, h)),
                  pl.BlockSpec((LANES, s), lambda h, i: (h, 0)),
                  pl.BlockSpec(lam_vecs.shape, lambda h, i: (0, 0)),
                  pl.BlockSpec((1, LANES), lambda h, i: (0, 0))],
        out_specs=pl.BlockSpec((t, LANES), lambda h, i: (i, h)),
        out_shape=jax.ShapeDtypeStruct((s, DIFF_WIDTH), BF16),
        scratch_shapes=[pltpu.VMEM((2, 1, t), F32), pltpu.VMEM((2, 1, t), F32),
                        pltpu.VMEM((2, LANES, t), F32)],
        compiler_params=_cparams(("parallel", "arbitrary")),
        name="diff_attn",
    )(qt, k, vt, lam_vecs, norm_g.reshape(1, LANES))


def _conv_body(bg_ref, cg_ref, xv_ref, cgh_ref, xvh_ref, a_ref, gt_ref, ah_ref, gth_ref,
               scw_ref, dww_ref, dwb_ref, lng_ref, lnb_ref, sc_ref, cf_ref, ext_ref):
    tm = bg_ref.shape[0]
    first = pl.program_id(0) == 0

    def fill(cur, halo):
        ext_ref[0:HALO, :] = jnp.where(first, 0.0, halo)
        ext_ref[HALO:HALO + tm, :] = cur

    def conv(w_ref, taps):
        acc = jnp.zeros((tm, ext_ref.shape[1]), F32)
        for j in range(taps):
            off = HALO - (taps - 1) + j
            acc = acc + w_ref[j:j + 1, :] * ext_ref[off:off + tm, :]
        return acc

    fill(cg_ref[...].astype(F32) * xv_ref[...].astype(F32), cgh_ref[...].astype(F32) * xvh_ref[...].astype(F32))
    sc_ref[...] = (bg_ref[...].astype(F32) * conv(scw_ref, SC_KERNEL)).astype(sc_ref.dtype)

    fill(a_ref[...].astype(F32) * jax.nn.sigmoid(gt_ref[...].astype(F32)),
         ah_ref[...].astype(F32) * jax.nn.sigmoid(gth_ref[...].astype(F32)))
    u = conv(dww_ref, CONF_KERNEL) + dwb_ref[...]
    mu = jnp.mean(u, axis=-1, keepdims=True)
    var = jnp.mean(jnp.square(u - mu), axis=-1, keepdims=True)
    y = (u - mu) * lax.rsqrt(var + LN_EPS) * lng_ref[...] + lnb_ref[...]
    cf_ref[...] = (y * jax.nn.sigmoid(y)).astype(cf_ref.dtype)


def _conv_branches(z, sc_w, dw_w, dw_b, ln_g, ln_b):
    s = z.shape[0]
    tm = min(ROW_TILE, s)
    w = SC_WIDTH
    cur = lambda col: pl.BlockSpec((tm, w), lambda i: (i, col // w))
    halo = lambda col: pl.BlockSpec((HALO, w), lambda i: (jnp.maximum(i * (tm // HALO) - 1, 0), col // w))
    full = lambda a: pl.BlockSpec(a.shape, lambda i: (0,) * a.ndim)
    vec = lambda a: a.reshape(1, -1)
    args = [sc_w, dw_w, vec(dw_b), vec(ln_g), vec(ln_b)]
    return pl.pallas_call(
        _conv_body,
        grid=(s // tm,),
        in_specs=[cur(Z_SC), cur(Z_SC + w), cur(Z_SC + 2 * w), halo(Z_SC + w), halo(Z_SC + 2 * w),
                  cur(Z_CONF), cur(Z_CONF + w), halo(Z_CONF), halo(Z_CONF + w)] + [full(a) for a in args],
        out_specs=[pl.BlockSpec((tm, w), lambda i: (i, 0)), pl.BlockSpec((tm, w), lambda i: (i, 0))],
        out_shape=[jax.ShapeDtypeStruct((s, w), BF16), jax.ShapeDtypeStruct((s, w), BF16)],
        scratch_shapes=[pltpu.VMEM((HALO + tm, w), F32)],
        compiler_params=_cparams(("parallel",)),
        name="conv_branches",
    )(*([z] * 9), *args)


def _merge_body(b0, b1, b2, b3, w0, w1, w2, w3, g0, g1, g2, g3, gb_ref, o_ref):
    acc = None
    for n, (b_ref, w_ref, g_ref) in enumerate(((b0, w0, g0), (b1, w1, g1), (b2, w2, g2), (b3, w3, g3))):
        y = jnp.dot(b_ref[...], w_ref[...], preferred_element_type=F32)
        t = jax.nn.sigmoid(g_ref[...].astype(F32) + gb_ref[n:n + 1, :]) * y
        acc = t if acc is None else acc + t
    o_ref[...] = acc.astype(o_ref.dtype)


def _merge(branches, w_outs, z, gate_b):
    s = z.shape[0]
    tm = min(ROW_TILE, s)
    tn = ROW_TILE
    kw = branches[0].shape[1]
    return pl.pallas_call(
        _merge_body,
        grid=(s // tm, D_MODEL // tn),
        in_specs=([pl.BlockSpec((tm, kw), lambda i, j: (i, 0))] * 4
                  + [pl.BlockSpec((kw, tn), lambda i, j: (0, j))] * 4
                  + [pl.BlockSpec((tm, tn), lambda i, j, b=b: (i, (Z_GATE + b * D_MODEL) // tn + j))
                     for b in range(N_BRANCH)]
                  + [pl.BlockSpec((N_BRANCH, tn), lambda i, j: (0, j))]),
        out_specs=pl.BlockSpec((tm, tn), lambda i, j: (i, j)),
        out_shape=jax.ShapeDtypeStruct((s, D_MODEL), BF16),
        compiler_params=_cparams(("parallel", "arbitrary")),
        name="merge",
    )(*branches, *w_outs, z, z, z, z, gate_b)


def _order(v, i, j):
    v[i], v[j] = jnp.maximum(v[i], v[j]), jnp.minimum(v[i], v[j])


def _bitonic_sort_desc(v):
    n = len(v)
    k = 2
    while k <= n:
        j = k // 2
        while j >= 1:
            for i in range(n):
                partner = i ^ j
                if partner > i:
                    if i & k == 0:
                        _order(v, i, partner)
                    else:
                        _order(v, partner, i)
            j //= 2
        k *= 2


def _bitonic_merge_desc(v):
    n = len(v)
    j = n // 2
    while j >= 1:
        for i in range(n):
            if i ^ j > i:
                _order(v, i, i ^ j)
        j //= 2


def _top16_desc(groups):
    v = list(groups)
    _bitonic_sort_desc(v)
    for shift in (SUBLANES // 2, SUBLANES // 4, SUBLANES // 8):
        other = [pltpu.roll(x, shift, 0) for x in v]
        v = [jnp.maximum(v[k], other[PEER_TOPK - 1 - k]) for k in range(PEER_TOPK)]
        _bitonic_merge_desc(v)
    return v


def _by_sublane(v, start):
    sub = lax.broadcasted_iota(jnp.int32, v[0].shape, 0)
    out = v[start]
    for r in range(1, SUBLANES):
        out = jnp.where(sub == r, v[start + r], out)
    return out


def _peer_candidates(a, a_hi, b_lo, b_hi, b0, op):
    groups = [op(a[0], b_lo), op(a[0], b_hi)]
    groups += [op(a[i], b_lo) for i in range(1, SUBLANES)]
    groups.append(op(a_hi, b0))
    return groups


def _peer_scores_body(q_ref, k1_ref, k2_ref, th_ref, s2_ref, e1_ref, e2n_ref):
    tt = q_ref.shape[0]
    n_groups = PEER_N_KEYS // SUBLANES

    def head(h, carry):
        qh = q_ref[:, pl.ds(pl.multiple_of(h * PEER_KEY_DIM, PEER_KEY_DIM), PEER_KEY_DIM)]
        dims = (((1,), (1,)), ((), ()))
        s1 = lax.dot_general(k1_ref[...], qh, dims, precision=lax.Precision.HIGHEST, preferred_element_type=F32)
        s2 = lax.dot_general(k2_ref[...], qh, dims, precision=lax.Precision.HIGHEST, preferred_element_type=F32)
        split = lambda s: [s[g * SUBLANES:(g + 1) * SUBLANES, :] for g in range(n_groups)]
        t1 = _top16_desc(split(s1))
        t2 = _top16_desc(split(s2))
        t1_hi, t2_lo, t2_hi = _by_sublane(t1, SUBLANES), _by_sublane(t2, 0), _by_sublane(t2, SUBLANES)
        cand = _peer_candidates(t1, t1_hi, t2_lo, t2_hi, t2[0], jnp.add)
        pad = [jnp.full((SUBLANES, tt), -jnp.inf, F32)] * (PEER_TOPK - len(cand))
        tau8 = _top16_desc(cand + pad)[PEER_TOPK - 1]

        m1, m2 = t1[0], t2[0]
        c1 = [jnp.exp(x - m1) for x in t1[:SUBLANES]]
        ec = _peer_candidates(c1, jnp.exp(t1_hi - m1), jnp.exp(t2_lo - m2), jnp.exp(t2_hi - m2),
                              jnp.ones((SUBLANES, tt), F32), jnp.multiply)
        zsum = jnp.zeros((SUBLANES, tt), F32)
        for cg, eg in zip(cand, ec):
            zsum = zsum + jnp.where(cg >= tau8, eg, 0.0)
        zden = jnp.sum(zsum, axis=0, keepdims=True)
        tau = tau8[0:1]
        th = jnp.full(s1.shape, jnp.inf, F32)
        for b in range(PEER_TOPK):
            t2b = t2[b][0:1]
            th = jnp.minimum(th, jnp.where(s1 + t2b >= tau, t2b, jnp.inf))
        th_ref[h] = th
        s2_ref[h] = s2
        e1_ref[h] = jnp.exp(s1 - m1[0:1])
        e2n_ref[h] = jnp.exp(s2 - m2[0:1]) / zden
        return carry

    lax.fori_loop(0, PEER_HEADS, head, 0)


def _peer_scores(q, k1, k2):
    s = q.shape[0]
    tt = min(SCORE_TILE, s)
    tab = pl.BlockSpec((PEER_HEADS, PEER_N_KEYS, tt), lambda i: (0, 0, i))
    tab_shape = jax.ShapeDtypeStruct((PEER_HEADS, PEER_N_KEYS, s), F32)
    return pl.pallas_call(
        _peer_scores_body,
        grid=(s // tt,),
        in_specs=[pl.BlockSpec((tt, PEER_HEADS * PEER_KEY_DIM), lambda i: (i, 0)),
                  pl.BlockSpec(k1.shape, lambda i: (0, 0)), pl.BlockSpec(k2.shape, lambda i: (0, 0))],
        out_specs=[tab, tab, tab, tab],
        out_shape=[tab_shape, tab_shape, tab_shape, tab_shape],
        compiler_params=_cparams(("parallel",)),
        name="peer_scores",
    )(q, k1, k2)


def _peer_items(s, n_items, nb):
    split = lambda it: (it // nb, it % nb)
    return (split(jnp.minimum(s, n_items - 1)), split(jnp.clip(s - 1, 0, n_items - 1)),
            split(jnp.clip(s - 2, 0, n_items - 1)))


def _peer_dense_body(ht_ref, u_ref, vt_ref, th_ref, e1_ref, s2_ref, e2n_ref, o_ref, a0, a1, w0, w1,
                     *, n_items, nb):
    s = pl.program_id(0)
    tt = ht_ref.shape[1]
    eb = u_ref.shape[0]
    d = vt_ref.shape[0]
    n_sub = eb // PEER_N_KEYS
    jrows = 2 * SUBLANES

    @pl.when(s == 0)
    def _():
        for ref in (a0, a1, w0, w1):
            ref[...] = jnp.zeros_like(ref)

    _, _, (_, block_c) = _peer_items(s, n_items, nb)

    @pl.when(block_c == 0)
    def _():
        o_ref[...] = jnp.zeros_like(o_ref)

    def run(a_new, a_old, w_new, w_old):
        def stage_a(k):
            rows = pl.ds(pl.multiple_of(k * PEER_CHUNK, PEER_CHUNK), PEER_CHUNK)
            a_t = jnp.dot(u_ref[rows, :], ht_ref[...], preferred_element_type=F32)
            a_new[rows, :] = 0.5 * a_t * (1.0 + lax.erf(a_t * (2.0 ** -0.5)))

        def stage_b(c, jc):
            cols = pl.ds(pl.multiple_of(c * LANES, LANES), LANES)
            jr = slice(jc * jrows, (jc + 1) * jrows)
            g = [jnp.zeros((jrows, LANES), F32) for _ in range(n_sub)]
            for h in range(PEER_HEADS):
                s2c = s2_ref[h, jr, cols]
                e2c = e2n_ref[h, jr, cols]
                for il in range(n_sub):
                    gate = e2c * e1_ref[h, il:il + 1, cols]
                    g[il] = g[il] + jnp.where(s2c >= th_ref[h, il:il + 1, cols], gate, 0.0)
            for il in range(n_sub):
                rows = slice(il * PEER_N_KEYS + jc * jrows, il * PEER_N_KEYS + (jc + 1) * jrows)
                w_new[rows, cols] = (a_old[rows, cols] * g[il]).astype(w_new.dtype)

        def stage_c(n):
            rows = pl.ds(pl.multiple_of(n * PEER_CHUNK, PEER_CHUNK), PEER_CHUNK)
            o_ref[rows, :] += jnp.dot(vt_ref[rows, :], w_old[...], preferred_element_type=F32)

        n_trips = tt // LANES
        assert eb // PEER_CHUNK == n_trips and d // PEER_CHUNK == 2 * n_trips

        def trip(k, carry):
            stage_a(k)
            stage_c(2 * k)
            for jc in range(PEER_N_KEYS // jrows):
                stage_b(k, jc)
            stage_c(2 * k + 1)
            return carry

        lax.fori_loop(0, n_trips, trip, 0)

    parity = lax.rem(s, 2)

    @pl.when(parity == 0)
    def _():
        run(a0, a1, w1, w0)

    @pl.when(parity == 1)
    def _():
        run(a1, a0, w0, w1)


def _peer_dense(hn_t, u, vt, th, e1, s2, e2n):
    d, s = hn_t.shape
    tt = min(PEER_TOKENS, s)
    eb = PEER_EXPERTS
    nb = u.shape[0] // eb
    n_items = (s // tt) * nb
    items = functools.partial(_peer_items, n_items=n_items, nb=nb)
    sub = pl.BlockSpec((PEER_HEADS, eb // PEER_N_KEYS, tt), lambda i: (0, items(i)[1][1], items(i)[1][0]))
    tab = pl.BlockSpec((PEER_HEADS, PEER_N_KEYS, tt), lambda i: (0, 0, items(i)[1][0]))
    return pl.pallas_call(
        functools.partial(_peer_dense_body, n_items=n_items, nb=nb),
        grid=(n_items + 2,),
        in_specs=[pl.BlockSpec((d, tt), lambda i: (0, items(i)[0][0])),
                  pl.BlockSpec((eb, d), lambda i: (items(i)[0][1], 0)),
                  pl.BlockSpec((d, eb), lambda i: (0, items(i)[2][1])),
                  sub, sub, tab, tab],
        out_specs=pl.BlockSpec((d, tt), lambda i: (0, items(i)[2][0])),
        out_shape=jax.ShapeDtypeStruct((d, s), F32),
        scratch_shapes=[pltpu.VMEM((eb, tt), F32), pltpu.VMEM((eb, tt), F32),
                        pltpu.VMEM((eb, tt), BF16), pltpu.VMEM((eb, tt), BF16)],
        compiler_params=_cparams(("arbitrary",)),
        name="peer_dense",
    )(hn_t, u, vt, th, e1, s2, e2n)


def _add_norm_body(x_ref, dt_ref, g_ref, *out_refs):
    x = x_ref[...] + dt_ref[...].T
    y = x * lax.rsqrt(jnp.mean(x * x, axis=-1, keepdims=True) + NORM_EPS)
    out_refs[-1][...] = (y * g_ref[...]).astype(out_refs[-1].dtype)
    if len(out_refs) == 2:
        out_refs[0][...] = x


def _add_norm(x, delta_t, g, norm_dtype, with_sum):
    s, d = x.shape
    tm = min(ROW_TILE, s)
    row = pl.BlockSpec((tm, d), lambda i: (i, 0))
    out_specs = [row, row] if with_sum else [row]
    out_shape = [jax.ShapeDtypeStruct((s, d), norm_dtype)]
    if with_sum:
        out_shape.insert(0, jax.ShapeDtypeStruct((s, d), F32))
    return pl.pallas_call(
        _add_norm_body,
        grid=(s // tm,),
        in_specs=[row, pl.BlockSpec((d, tm), lambda i: (0, i)), pl.BlockSpec((1, d), lambda i: (0, 0))],
        out_specs=out_specs,
        out_shape=out_shape,
        compiler_params=_cparams(("parallel",)),
        name="add_norm",
    )(x, delta_t, g.reshape(1, d))


def _rope_tables(positions, dim, lanes_x1):
    half = dim // 2
    inv_freq = ROPE_THETA ** (-jnp.arange(0, dim, 2, dtype=F32) / dim)
    ang = positions.astype(F32)[:, None] * inv_freq
    cos, sin = jnp.cos(ang), jnp.sin(ang)
    s = positions.shape[0]
    c_parts, su_parts, sd_parts = [], [], []
    pos = 0
    for off in lanes_x1:
        gap = off - pos
        c_parts += [jnp.ones((s, gap), F32), cos, cos]
        su_parts += [jnp.zeros((s, gap), F32), -sin, jnp.zeros((s, half), F32)]
        sd_parts += [jnp.zeros((s, gap + half), F32), sin]
        pos = off + dim
    tail = LANES - pos
    c_parts.append(jnp.ones((s, tail), F32))
    su_parts.append(jnp.zeros((s, tail), F32))
    sd_parts.append(jnp.zeros((s, tail), F32))
    return tuple(jnp.concatenate(p, axis=1) for p in (c_parts, su_parts, sd_parts))


def _layout_w_in(w):
    d = w.shape[0]
    seg = [w[:, o:o + n] for o, n in zip(IN_OFFSETS, IN_SIZES)]
    cq, ckv, kr, sc, dq, dk, dv, conf, gate = seg
    kr_blk = jnp.concatenate([jnp.zeros((d, MLA_NOPE), w.dtype), kr,
                              jnp.zeros((d, LANES - MLA_NOPE - MLA_ROPE), w.dtype)], axis=1)
    pad = jnp.zeros((d, Z_COLS - (Z_KR + LANES)), w.dtype)
    return jnp.concatenate([cq, sc, dq, dk, dv, conf, gate, ckv, kr_blk, pad], axis=1).astype(BF16)


def _layout_mla_weights(w_uq, w_ukv):
    lq = w_uq.shape[0]
    wq = w_uq.reshape(lq, MLA_HEADS, MLA_NOPE + MLA_ROPE)
    wq = jnp.pad(wq, ((0, 0), (0, 0), (0, LANES - MLA_NOPE - MLA_ROPE))).reshape(lq, MLA_HEADS * LANES)
    lk = w_ukv.shape[0]
    wkv = w_ukv.reshape(lk, MLA_HEADS, MLA_NOPE + MLA_V)
    wk = jnp.pad(wkv[:, :, :MLA_NOPE], ((0, 0), (0, 0), (0, LANES - MLA_NOPE))).reshape(lk, MLA_HEADS * LANES)
    wv = wkv[:, :, MLA_NOPE:].reshape(lk, MLA_HEADS * MLA_V)
    return wq.T.astype(BF16), wk.astype(BF16), wv.T.astype(BF16)


def kernel(x, positions, mix_norm_g, w_in, gate_b, mla_q_norm_g, mla_w_uq, mla_kv_norm_g, mla_w_ukv, mla_w_out,
           sc_conv_w, sc_w_out, diff_lambda, diff_norm_g, diff_w_out, conf_dw_w, conf_dw_b, conf_ln_g, conf_ln_b,
           conf_w_out, w_o, ffn_norm_g, peer_w_q, peer_sub_keys, peer_u, peer_v, final_norm_g):
    b, s, d = x.shape
    assert b == 1 and d == D_MODEL
    xs = x.reshape(s, d)
    pos = positions.reshape(s)
    mla_tabs = _rope_tables(pos, MLA_ROPE, (MLA_NOPE,))
    mla_tabs_t = tuple(t.T for t in mla_tabs)
    diff_tabs = _rope_tables(pos, DIFF_ROT, (0, DIFF_HEAD_DIM))

    for i in range(DEPTH):
        lam_init = 0.8 - 0.6 * math.exp(-0.3 * i)
        if i == 0:
            hn = _rmsnorm(xs, mix_norm_g[i], BF16)
        else:
            xs, hn = _add_norm(xs, delta_t, mix_norm_g[i], BF16, with_sum=True)
        z = _matmul(hn, _layout_w_in(w_in[i]), BF16, name="in_proj")
        wqt, wk, wvt = _layout_mla_weights(mla_w_uq[i], mla_w_ukv[i])
        qt, k, vt = _mla_prep(z, mla_tabs, mla_tabs_t, mla_q_norm_g[i].reshape(1, -1),
                              mla_kv_norm_g[i].reshape(1, -1), wqt, wk, wvt)
        o_mla = _mla_attn(qt, k, vt)
        o_sc, o_conf = _conv_branches(z, sc_conv_w[i], conf_dw_w[i], conf_dw_b[i], conf_ln_g[i], conf_ln_b[i])
        qdt, kd, vdt = _diff_prep(z, diff_tabs)
        o_diff = _diff_attn(qdt, kd, vdt, diff_lambda[i], diff_norm_g[i], lam_init)
        merged = _merge((o_mla, o_sc, o_diff, o_conf),
                        tuple(w.astype(BF16) for w in (mla_w_out[i], sc_w_out[i], diff_w_out[i], conf_w_out[i])),
                        z, gate_b[i])
        xs = _matmul(merged, w_o[i].astype(BF16), F32, residual=xs, name="out_proj")
        hf, hf_t = _rmsnorm(xs, ffn_norm_g[i], BF16, with_transposed=True)
        pq = _matmul(hf, peer_w_q[i].astype(BF16), F32, name="peer_q")
        zk = jnp.zeros((PEER_N_KEYS, PEER_HALF), F32)
        k1 = jnp.concatenate([peer_sub_keys[i, 0], zk], axis=1)
        k2 = jnp.concatenate([zk, peer_sub_keys[i, 1]], axis=1)
        th, s2, e1, e2n = _peer_scores(pq, k1, k2)
        delta_t = _peer_dense(hf_t, peer_u[i].astype(BF16), peer_v[i].T.astype(BF16), th, e1, s2, e2n)
    (out,) = _add_norm(xs, delta_t, final_norm_g, F32, with_sum=False)
    return out.reshape(b, s, d)
```

```python
import functools
import math

import jax
import jax.numpy as jnp
import numpy as np
from jax import lax
from jax.experimental import pallas as pl
from jax.experimental.pallas import tpu as pltpu

F32 = jnp.float32
BF16 = jnp.bfloat16

D_MODEL = 2048
DEPTH = 2
CHUNK = 64
ROPE_THETA = 500000.0
NORM_EPS = 1e-6
LN_EPS = 1e-5
NEG_INF = -1e30
N_BRANCH = 4

MLA_HEADS = 8
MLA_Q_LORA = 512
MLA_KV_LORA = 256
MLA_NOPE = 64
MLA_ROPE = 32
MLA_V = 64
MLA_SCALE = (MLA_NOPE + MLA_ROPE) ** -0.5

SC_WIDTH = 512
SC_KERNEL = 3

DIFF_HEADS = 4
DIFF_HEAD_DIM = 64
DIFF_ROT = DIFF_HEAD_DIM // 4
DIFF_SCALE = DIFF_HEAD_DIM ** -0.5
DIFF_WIDTH = DIFF_HEADS * 2 * DIFF_HEAD_DIM

CONF_WIDTH = 512
CONF_KERNEL = 31

PEER_HEADS = 8
PEER_N_KEYS = 128
PEER_N_EXPERTS = PEER_N_KEYS * PEER_N_KEYS
PEER_KEY_DIM = 128
PEER_HALF = PEER_KEY_DIM // 2
PEER_TOPK = 16

IN_SIZES = (MLA_Q_LORA, MLA_KV_LORA, MLA_ROPE, 3 * SC_WIDTH, DIFF_WIDTH, DIFF_WIDTH, DIFF_WIDTH,
            2 * CONF_WIDTH, N_BRANCH * D_MODEL)
IN_OFFSETS = tuple(int(v) for v in np.cumsum((0,) + IN_SIZES)[:-1])

LOG2E = math.log2(math.e)

LANES = 128
SUBLANES = 8
HALO = 32
VMEM_LIMIT = 56 * 1024 * 1024

ROW_TILE = 512
MM_TILE = 1024
ATTN_TILE = 512
SCORE_TILE = 512
PEER_TOKENS = 512
PEER_EXPERTS = 1024
PEER_CHUNK = 256

Z_SC = 0
Z_DQ = 1536
Z_DK = 2048
Z_DV = 2560
Z_CONF = 3072
Z_GATE = 4096
Z_CQ = 12288
Z_CKV = 12800
Z_KR = 13056
Z_COLS = 13312


def _cparams(sem):
    return pltpu.CompilerParams(dimension_semantics=sem, vmem_limit_bytes=VMEM_LIMIT)


def _rmsnorm_body(x_ref, g_ref, o_ref, *maybe_ot_ref):
    x = x_ref[...].astype(F32)
    y = x * lax.rsqrt(jnp.mean(x * x, axis=-1, keepdims=True) + NORM_EPS) * g_ref[...]
    o_ref[...] = y.astype(o_ref.dtype)
    for ot_ref in maybe_ot_ref:
        ot_ref[...] = y.T.astype(ot_ref.dtype)


def _rmsnorm(x, g, out_dtype, with_transposed=False):
    s, d = x.shape
    tm = min(ROW_TILE, s)
    out_specs = [pl.BlockSpec((tm, d), lambda i: (i, 0))]
    out_shape = [jax.ShapeDtypeStruct((s, d), out_dtype)]
    if with_transposed:
        out_specs.append(pl.BlockSpec((d, tm), lambda i: (0, i)))
        out_shape.append(jax.ShapeDtypeStruct((d, s), out_dtype))
    out = pl.pallas_call(
        _rmsnorm_body,
        grid=(s // tm,),
        in_specs=[pl.BlockSpec((tm, d), lambda i: (i, 0)), pl.BlockSpec((1, d), lambda i: (0, 0))],
        out_specs=out_specs,
        out_shape=out_shape,
        compiler_params=_cparams(("parallel",)),
        name="rmsnorm",
    )(x, g.reshape(1, d))
    return out if with_transposed else out[0]


def _matmul_body(a_ref, b_ref, o_ref):
    o_ref[...] = jnp.dot(a_ref[...], b_ref[...], preferred_element_type=F32).astype(o_ref.dtype)


def _matmul_res_body(a_ref, b_ref, r_ref, o_ref):
    o_ref[...] = (r_ref[...] + jnp.dot(a_ref[...], b_ref[...], preferred_element_type=F32)).astype(o_ref.dtype)


def _matmul(a, b, out_dtype, residual=None, name="matmul"):
    m, k = a.shape
    n = b.shape[1]
    tm, tn = min(MM_TILE, m), min(MM_TILE, n)
    in_specs = [pl.BlockSpec((tm, k), lambda i, j: (i, 0)), pl.BlockSpec((k, tn), lambda i, j: (0, j))]
    args = [a, b]
    body = _matmul_body
    if residual is not None:
        in_specs.append(pl.BlockSpec((tm, tn), lambda i, j: (i, j)))
        args.append(residual)
        body = _matmul_res_body
    return pl.pallas_call(
        body,
        grid=(m // tm, n // tn),
        in_specs=in_specs,
        out_specs=pl.BlockSpec((tm, tn), lambda i, j: (i, j)),
        out_shape=jax.ShapeDtypeStruct((m, n), out_dtype),
        compiler_params=_cparams(("parallel", "arbitrary")),
        name=name,
    )(*args)


def _sigmoid(x):
    return 0.5 + 0.5 * jnp.tanh(0.5 * x)


def _rope128(x, c, s_up, s_dn, half):
    return x * c + pltpu.roll(x, LANES - half, 1) * s_up + pltpu.roll(x, half, 1) * s_dn


def _mla_prep_body(cq_ref, ckv_ref, kr_ref, c_ref, su_ref, sd_ref, ct_ref, sut_ref, sdt_ref,
                   qg_ref, kvg_ref, wqt_ref, wk_ref, wvt_ref, qt_ref, k_ref, vt_ref):
    half = MLA_ROPE // 2
    nt = (((1,), (1,)), ((), ()))

    def norm(x_ref, g_ref):
        x = x_ref[...].astype(F32)
        y = x * lax.rsqrt(jnp.mean(x * x, axis=-1, keepdims=True) + NORM_EPS)
        return (y * g_ref[...]).astype(BF16)

    qf_t = lax.dot_general(wqt_ref[...], norm(cq_ref, qg_ref), nt, preferred_element_type=F32)
    ckvn = norm(ckv_ref, kvg_ref)
    kf = jnp.dot(ckvn, wk_ref[...], preferred_element_type=F32)
    vt_ref[...] = lax.dot_general(wvt_ref[...], ckvn, nt, preferred_element_type=F32).astype(vt_ref.dtype)
    kr = _rope128(kr_ref[...].astype(F32), c_ref[...], su_ref[...], sd_ref[...], half)
    ct, sut, sdt = ct_ref[...], sut_ref[...], sdt_ref[...]
    for h in range(MLA_HEADS):
        sl = slice(h * LANES, (h + 1) * LANES)
        x = qf_t[sl, :]
        q = x * ct + pltpu.roll(x, LANES - half, 0) * sut + pltpu.roll(x, half, 0) * sdt
        qt_ref[sl, :] = (q * (MLA_SCALE * LOG2E)).astype(qt_ref.dtype)
        k_ref[:, sl] = (kf[:, sl] + kr).astype(k_ref.dtype)


def _mla_prep(z, tabs, tabs_t, qg, kvg, wqt, wk, wvt):
    s = z.shape[0]
    tm = min(ROW_TILE, s)
    row = lambda w, idx: pl.BlockSpec((tm, w), lambda i: (i, idx))
    full = lambda a: pl.BlockSpec(a.shape, lambda i: (0,) * a.ndim)
    tab = pl.BlockSpec((tm, LANES), lambda i: (i, 0))
    tab_t = pl.BlockSpec((LANES, tm), lambda i: (0, i))
    hw = MLA_HEADS * LANES
    vw = MLA_HEADS * MLA_V
    return pl.pallas_call(
        _mla_prep_body,
        grid=(s // tm,),
        in_specs=[row(MLA_Q_LORA, Z_CQ // MLA_Q_LORA), row(MLA_KV_LORA, Z_CKV // MLA_KV_LORA),
                  row(LANES, Z_KR // LANES), tab, tab, tab, tab_t, tab_t, tab_t,
                  full(qg), full(kvg), full(wqt), full(wk), full(wvt)],
        out_specs=[pl.BlockSpec((hw, tm), lambda i: (0, i)), pl.BlockSpec((tm, hw), lambda i: (i, 0)),
                   pl.BlockSpec((vw, tm), lambda i: (0, i))],
        out_shape=[jax.ShapeDtypeStruct((hw, s), BF16), jax.ShapeDtypeStruct((s, hw), BF16),
                   jax.ShapeDtypeStruct((vw, s), BF16)],
        compiler_params=_cparams(("parallel",)),
        name="mla_prep",
    )(z, z, z, *tabs, *tabs_t, qg, kvg, wqt, wk, wvt)


def _flash_streams_t(q_ts, k_ats, vt_ats, m_ref, l_ref, acc_ref, n_full, t):
    streams = range(len(q_ts))
    for n in streams:
        m_ref[n] = jnp.full((1, t), NEG_INF, F32)
        l_ref[n] = jnp.zeros((1, t), F32)
        acc_ref[n] = jnp.zeros(acc_ref.shape[1:], F32)

    def step(j, masked):
        scores = [jnp.dot(k_ats[n](j), q_ts[n], preferred_element_type=F32) for n in streams]
        probs, alphas = [], []
        for n in streams:
            s = scores[n]
            if masked:
                key_chunk = lax.broadcasted_iota(jnp.int32, (t, t), 0) // CHUNK
                qry_chunk = lax.broadcasted_iota(jnp.int32, (t, t), 1) // CHUNK
                s = jnp.where(key_chunk <= qry_chunk, s, NEG_INF)
            m_prev = m_ref[n]
            m_new = jnp.maximum(m_prev, jnp.max(s, axis=0, keepdims=True))
            alpha = jnp.exp2(m_prev - m_new)
            p = jnp.exp2(s - m_new)
            l_ref[n] = alpha * l_ref[n] + jnp.sum(p, axis=0, keepdims=True)
            m_ref[n] = m_new
            probs.append(p.astype(BF16))
            alphas.append(alpha)
        for n in streams:
            acc_ref[n] = alphas[n] * acc_ref[n] + jnp.dot(vt_ats[n](j), probs[n], preferred_element_type=F32)

    def body(j, carry):
        step(j, False)
        return carry

    lax.fori_loop(0, n_full, body, 0)
    step(n_full, True)


def _mla_attn_body(qt_ref, k_ref, vt_ref, o_ref, m_ref, l_ref, acc_ref, *, t):
    i = pl.program_id(1)
    rows = lambda j: pl.ds(pl.multiple_of(j * t, t), t)
    lanes = [slice(hh * LANES, (hh + 1) * LANES) for hh in range(2)]
    k_ats = [lambda j, sl=sl: k_ref[rows(j), sl] for sl in lanes]
    vt_ats = [lambda j, hh=hh: vt_ref[hh * MLA_V:(hh + 1) * MLA_V, rows(j)] for hh in range(2)]
    _flash_streams_t([qt_ref[sl, :] for sl in lanes], k_ats, vt_ats, m_ref, l_ref, acc_ref, i, t)
    o_t = jnp.concatenate([acc_ref[0] / l_ref[0], acc_ref[1] / l_ref[1]], axis=0)
    o_ref[...] = o_t.T.astype(o_ref.dtype)


def _mla_attn(qt, k, vt):
    s = k.shape[0]
    t = min(ATTN_TILE, s)
    pairs = MLA_HEADS // 2
    return pl.pallas_call(
        functools.partial(_mla_attn_body, t=t),
        grid=(pairs, s // t),
        in_specs=[pl.BlockSpec((2 * LANES, t), lambda p, i: (p, i)),
                  pl.BlockSpec((s, 2 * LANES), lambda p, i: (0, p)),
                  pl.BlockSpec((2 * MLA_V, s), lambda p, i: (p, 0))],
        out_specs=pl.BlockSpec((t, 2 * MLA_V), lambda p, i: (i, p)),
        out_shape=jax.ShapeDtypeStruct((s, MLA_HEADS * MLA_V), BF16),
        scratch_shapes=[pltpu.VMEM((2, 1, t), F32), pltpu.VMEM((2, 1, t), F32),
                        pltpu.VMEM((2, MLA_V, t), F32)],
        compiler_params=_cparams(("parallel", "arbitrary")),
        name="mla_attn",
    )(qt, k, vt)


def _diff_prep_body(dq_ref, dk_ref, dv_ref, c_ref, su_ref, sd_ref, qt_ref, k_ref, vt_ref):
    half = DIFF_ROT // 2
    c, su, sd = c_ref[...], su_ref[...], sd_ref[...]
    tm = dq_ref.shape[0]
    lane = lax.broadcasted_iota(jnp.int32, (tm, LANES), 1)
    vt_ref[...] = dv_ref[...].astype(F32).T.astype(vt_ref.dtype)
    for h in range(DIFF_HEADS):
        sl = slice(h * LANES, (h + 1) * LANES)
        q = _rope128(dq_ref[:, sl].astype(F32), c, su, sd, half) * (DIFF_SCALE * LOG2E)
        qt_ref[(2 * h) * LANES:(2 * h + 1) * LANES, :] = jnp.where(lane < DIFF_HEAD_DIM, q, 0.0).T.astype(qt_ref.dtype)
        qt_ref[(2 * h + 1) * LANES:(2 * h + 2) * LANES, :] = jnp.where(lane >= DIFF_HEAD_DIM, q, 0.0).T.astype(qt_ref.dtype)
        k_ref[:, sl] = _rope128(dk_ref[:, sl].astype(F32), c, su, sd, half).astype(k_ref.dtype)


def _diff_prep(z, tabs):
    s = z.shape[0]
    tm = min(ROW_TILE, s)
    tab = pl.BlockSpec((tm, LANES), lambda i: (i, 0))
    col = lambda start: pl.BlockSpec((tm, DIFF_WIDTH), lambda i: (i, start // DIFF_WIDTH))
    return pl.pallas_call(
        _diff_prep_body,
        grid=(s // tm,),
        in_specs=[col(Z_DQ), col(Z_DK), col(Z_DV), tab, tab, tab],
        out_specs=[pl.BlockSpec((2 * DIFF_WIDTH, tm), lambda i: (0, i)),
                   pl.BlockSpec((tm, DIFF_WIDTH), lambda i: (i, 0)),
                   pl.BlockSpec((DIFF_WIDTH, tm), lambda i: (0, i))],
        out_shape=[jax.ShapeDtypeStruct((2 * DIFF_WIDTH, s), BF16), jax.ShapeDtypeStruct((s, DIFF_WIDTH), BF16),
                   jax.ShapeDtypeStruct((DIFF_WIDTH, s), BF16)],
        compiler_params=_cparams(("parallel",)),
        name="diff_prep",
    )(z, z, z, *tabs)


def _diff_attn_body(qt_ref, k_ref, vt_ref, lam_ref, g_ref, o_ref, m_ref, l_ref, acc_ref, *, t, lam_init):
    i = pl.program_id(1)
    rows = lambda j: pl.ds(pl.multiple_of(j * t, t), t)
    k_at = lambda j: k_ref[rows(j), :]
    vt_at = lambda j: vt_ref[:, rows(j)]
    q_ts = [qt_ref[c * LANES:(c + 1) * LANES, :] for c in range(2)]
    _flash_streams_t(q_ts, [k_at, k_at], [vt_at, vt_at], m_ref, l_ref, acc_ref, i, t)
    lv = lam_ref[...]
    lam = (jnp.exp(jnp.sum(lv[0:1] * lv[1:2], axis=1, keepdims=True))
           - jnp.exp(jnp.sum(lv[2:3] * lv[3:4], axis=1, keepdims=True)) + lam_init)
    o = (acc_ref[0] / l_ref[0] - lam * (acc_ref[1] / l_ref[1])).T
    o = o * lax.rsqrt(jnp.mean(o * o, axis=-1, keepdims=True) + NORM_EPS)
    o_ref[...] = (o * g_ref[...] * (1.0 - lam_init)).astype(o_ref.dtype)


def _diff_attn(qt, k, vt, lam_vecs, norm_g, lam_init):
    s = k.shape[0]
    t = min(ATTN_TILE, s)
    return pl.pallas_call(
        functools.partial(_diff_attn_body, t=t, lam_init=lam_init),
        grid=(DIFF_HEADS, s // t),
        in_specs=[pl.BlockSpec((2 * LANES, t), lambda h, i: (h, i)),
                  pl.BlockSpec((s, LANES), lambda h, i: (0, h)),
                  pl.BlockSpec((LANES, s), lambda h, i: (h, 0)),
                  pl.BlockSpec(lam_vecs.shape, lambda h, i: (0, 0)),
                  pl.BlockSpec((1, LANES), lambda h, i: (0, 0))],
        out_specs=pl.BlockSpec((t, LANES), lambda h, i: (i, h)),
        out_shape=jax.ShapeDtypeStruct((s, DIFF_WIDTH), BF16),
        scratch_shapes=[pltpu.VMEM((2, 1, t), F32), pltpu.VMEM((2, 1, t), F32),
                        pltpu.VMEM((2, LANES, t), F32)],
        compiler_params=_cparams(("parallel", "arbitrary")),
        name="diff_attn",
    )(qt, k, vt, lam_vecs, norm_g.reshape(1, LANES))


def _conv_body(bg_ref, cg_ref, xv_ref, cgh_ref, xvh_ref, a_ref, gt_ref, ah_ref, gth_ref,
               scw_ref, dww_ref, dwb_ref, lng_ref, lnb_ref, sc_ref, cf_ref, ext_ref):
    tm = bg_ref.shape[0]
    first = pl.program_id(0) == 0

    shifted_rows = HALO + tm - SUBLANES

    def fill(cur, halo, taps):
        ext_ref[0, 0:HALO, :] = jnp.where(first, 0.0, halo)
        ext_ref[0, HALO:HALO + tm, :] = cur
        for r in sorted({(HALO - (taps - 1) + j) % SUBLANES for j in range(taps)} - {0}):
            ext_ref[r, 0:shifted_rows, :] = ext_ref[0, r:r + shifted_rows, :]

    def conv(w_ref, taps):
        acc = jnp.zeros((tm, ext_ref.shape[2]), F32)
        for j in range(taps):
            off = HALO - (taps - 1) + j
            r = off % SUBLANES
            acc = acc + w_ref[j:j + 1, :] * ext_ref[r, off - r:off - r + tm, :]
        return acc

    fill(cg_ref[...].astype(F32) * xv_ref[...].astype(F32), cgh_ref[...].astype(F32) * xvh_ref[...].astype(F32),
         SC_KERNEL)
    sc_ref[...] = (bg_ref[...].astype(F32) * conv(scw_ref, SC_KERNEL)).astype(sc_ref.dtype)

    fill(a_ref[...].astype(F32) * _sigmoid(gt_ref[...].astype(F32)),
         ah_ref[...].astype(F32) * _sigmoid(gth_ref[...].astype(F32)), CONF_KERNEL)
    u = conv(dww_ref, CONF_KERNEL) + dwb_ref[...]
    mu = jnp.mean(u, axis=-1, keepdims=True)
    var = jnp.mean(jnp.square(u - mu), axis=-1, keepdims=True)
    y = (u - mu) * lax.rsqrt(var + LN_EPS) * lng_ref[...] + lnb_ref[...]
    cf_ref[...] = (y * _sigmoid(y)).astype(cf_ref.dtype)


def _conv_branches(z, sc_w, dw_w, dw_b, ln_g, ln_b):
    s = z.shape[0]
    tm = min(ROW_TILE, s)
    w = SC_WIDTH
    cur = lambda col: pl.BlockSpec((tm, w), lambda i: (i, col // w))
    halo = lambda col: pl.BlockSpec((HALO, w), lambda i: (jnp.maximum(i * (tm // HALO) - 1, 0), col // w))
    full = lambda a: pl.BlockSpec(a.shape, lambda i: (0,) * a.ndim)
    vec = lambda a: a.reshape(1, -1)
    args = [sc_w, dw_w, vec(dw_b), vec(ln_g), vec(ln_b)]
    return pl.pallas_call(
        _conv_body,
        grid=(s // tm,),
        in_specs=[cur(Z_SC), cur(Z_SC + w), cur(Z_SC + 2 * w), halo(Z_SC + w), halo(Z_SC + 2 * w),
                  cur(Z_CONF), cur(Z_CONF + w), halo(Z_CONF), halo(Z_CONF + w)] + [full(a) for a in args],
        out_specs=[pl.BlockSpec((tm, w), lambda i: (i, 0)), pl.BlockSpec((tm, w), lambda i: (i, 0))],
        out_shape=[jax.ShapeDtypeStruct((s, w), BF16), jax.ShapeDtypeStruct((s, w), BF16)],
        scratch_shapes=[pltpu.VMEM((SUBLANES, HALO + tm, w), F32)],
        compiler_params=_cparams(("parallel",)),
        name="conv_branches",
    )(*([z] * 9), *args)


def _merge_body(b0, b1, b2, b3, w0, w1, w2, w3, g0, g1, g2, g3, gb_ref, o_ref):
    acc = None
    for n, (b_ref, w_ref, g_ref) in enumerate(((b0, w0, g0), (b1, w1, g1), (b2, w2, g2), (b3, w3, g3))):
        y = jnp.dot(b_ref[...], w_ref[...], preferred_element_type=F32)
        t = _sigmoid(g_ref[...].astype(F32) + gb_ref[n:n + 1, :]) * y
        acc = t if acc is None else acc + t
    o_ref[...] = acc.astype(o_ref.dtype)


def _merge(branches, w_outs, z, gate_b):
    s = z.shape[0]
    tm = min(MM_TILE, s)
    tn = ROW_TILE
    kw = branches[0].shape[1]
    return pl.pallas_call(
        _merge_body,
        grid=(s // tm, D_MODEL // tn),
        in_specs=([pl.BlockSpec((tm, kw), lambda i, j: (i, 0))] * 4
                  + [pl.BlockSpec((kw, tn), lambda i, j: (0, j))] * 4
                  + [pl.BlockSpec((tm, tn), lambda i, j, b=b: (i, (Z_GATE + b * D_MODEL) // tn + j))
                     for b in range(N_BRANCH)]
                  + [pl.BlockSpec((N_BRANCH, tn), lambda i, j: (0, j))]),
        out_specs=pl.BlockSpec((tm, tn), lambda i, j: (i, j)),
        out_shape=jax.ShapeDtypeStruct((s, D_MODEL), BF16),
        compiler_params=_cparams(("parallel", "arbitrary")),
        name="merge",
    )(*branches, *w_outs, z, z, z, z, gate_b)


def _order(v, i, j):
    v[i], v[j] = jnp.maximum(v[i], v[j]), jnp.minimum(v[i], v[j])


def _bitonic_sort_desc(v):
    n = len(v)
    k = 2
    while k <= n:
        j = k // 2
        while j >= 1:
            for i in range(n):
                partner = i ^ j
                if partner > i:
                    if i & k == 0:
                        _order(v, i, partner)
                    else:
                        _order(v, partner, i)
            j //= 2
        k *= 2


def _bitonic_merge_desc(v):
    n = len(v)
    j = n // 2
    while j >= 1:
        for i in range(n):
            if i ^ j > i:
                _order(v, i, i ^ j)
        j //= 2


def _top16_desc(groups):
    v = list(groups)
    _bitonic_sort_desc(v)
    for shift in (SUBLANES // 2, SUBLANES // 4, SUBLANES // 8):
        other = [pltpu.roll(x, shift, 0) for x in v]
        v = [jnp.maximum(v[k], other[PEER_TOPK - 1 - k]) for k in range(PEER_TOPK)]
        _bitonic_merge_desc(v)
    return v


def _by_sublane(v, start):
    sub = lax.broadcasted_iota(jnp.int32, v[0].shape, 0)
    out = v[start]
    for r in range(1, SUBLANES):
        out = jnp.where(sub == r, v[start + r], out)
    return out


def _peer_candidates(a, a_hi, b_lo, b_hi, b0, op):
    groups = [op(a[0], b_lo), op(a[0], b_hi)]
    groups += [op(a[i], b_lo) for i in range(1, SUBLANES)]
    groups.append(op(a_hi, b0))
    return groups


def _peer_scores_body(q_ref, k1_ref, k2_ref, th_ref, s2_ref, e1_ref, e2n_ref):
    tt = q_ref.shape[0]
    n_groups = PEER_N_KEYS // SUBLANES

    def head(h, carry):
        qh = q_ref[:, pl.ds(pl.multiple_of(h * PEER_KEY_DIM, PEER_KEY_DIM), PEER_KEY_DIM)]
        dims = (((1,), (1,)), ((), ()))
        s1 = lax.dot_general(k1_ref[...], qh, dims, precision=lax.Precision.HIGHEST, preferred_element_type=F32)
        s2 = lax.dot_general(k2_ref[...], qh, dims, precision=lax.Precision.HIGHEST, preferred_element_type=F32)
        split = lambda s: [s[g * SUBLANES:(g + 1) * SUBLANES, :] for g in range(n_groups)]
        t1 = _top16_desc(split(s1))
        t2 = _top16_desc(split(s2))
        t1_hi, t2_lo, t2_hi = _by_sublane(t1, SUBLANES), _by_sublane(t2, 0), _by_sublane(t2, SUBLANES)
        cand = _peer_candidates(t1, t1_hi, t2_lo, t2_hi, t2[0], jnp.add)
        pad = [jnp.full((SUBLANES, tt), -jnp.inf, F32)] * (PEER_TOPK - len(cand))
        tau8 = _top16_desc(cand + pad)[PEER_TOPK - 1]

        m1, m2 = t1[0], t2[0]
        c1 = [jnp.exp(x - m1) for x in t1[:SUBLANES]]
        ec = _peer_candidates(c1, jnp.exp(t1_hi - m1), jnp.exp(t2_lo - m2), jnp.exp(t2_hi - m2),
                              jnp.ones((SUBLANES, tt), F32), jnp.multiply)
        zsum = jnp.zeros((SUBLANES, tt), F32)
        for cg, eg in zip(cand, ec):
            zsum = zsum + jnp.where(cg >= tau8, eg, 0.0)
        zden = jnp.sum(zsum, axis=0, keepdims=True)
        tau = tau8[0:1]
        th = jnp.full(s1.shape, jnp.inf, F32)
        for b in range(PEER_TOPK):
            t2b = t2[b][0:1]
            th = jnp.minimum(th, jnp.where(s1 + t2b >= tau, t2b, jnp.inf))
        th_ref[h] = th
        s2_ref[h] = s2
        e1_ref[h] = jnp.exp(s1 - m1[0:1])
        e2n_ref[h] = jnp.exp(s2 - m2[0:1]) / zden
        return carry

    lax.fori_loop(0, PEER_HEADS, head, 0)


def _peer_scores(q, k1, k2):
    s = q.shape[0]
    tt = min(SCORE_TILE, s)
    tab = pl.BlockSpec((PEER_HEADS, PEER_N_KEYS, tt), lambda i: (0, 0, i))
    tab_shape = jax.ShapeDtypeStruct((PEER_HEADS, PEER_N_KEYS, s), F32)
    return pl.pallas_call(
        _peer_scores_body,
        grid=(s // tt,),
        in_specs=[pl.BlockSpec((tt, PEER_HEADS * PEER_KEY_DIM), lambda i: (i, 0)),
                  pl.BlockSpec(k1.shape, lambda i: (0, 0)), pl.BlockSpec(k2.shape, lambda i: (0, 0))],
        out_specs=[tab, tab, tab, tab],
        out_shape=[tab_shape, tab_shape, tab_shape, tab_shape],
        compiler_params=_cparams(("parallel",)),
        name="peer_scores",
    )(q, k1, k2)


def _peer_items(s, n_items, nb):
    split = lambda it: (it // nb, it % nb)
    return (split(jnp.minimum(s, n_items - 1)), split(jnp.clip(s - 1, 0, n_items - 1)),
            split(jnp.clip(s - 2, 0, n_items - 1)))


def _peer_dense_body(ht_ref, u_ref, vt_ref, th_ref, e1_ref, s2_ref, e2n_ref, o_ref, a0, a1, w0, w1,
                     *, n_items, nb):
    s = pl.program_id(0)
    tt = ht_ref.shape[1]
    eb = u_ref.shape[0]
    d = vt_ref.shape[0]
    n_sub = eb // PEER_N_KEYS
    jrows = 2 * SUBLANES

    @pl.when(s == 0)
    def _():
        for ref in (a0, a1, w0, w1):
            ref[...] = jnp.zeros_like(ref)

    _, _, (_, block_c) = _peer_items(s, n_items, nb)

    @pl.when(block_c == 0)
    def _():
        o_ref[...] = jnp.zeros_like(o_ref)

    def run(a_new, a_old, w_new, w_old):
        def stage_a(k):
            rows = pl.ds(pl.multiple_of(k * PEER_CHUNK, PEER_CHUNK), PEER_CHUNK)
            a_t = jnp.dot(u_ref[rows, :], ht_ref[...], preferred_element_type=F32)
            a_new[rows, :] = 0.5 * a_t * (1.0 + lax.erf(a_t * (2.0 ** -0.5)))

        def stage_b(c, jc):
            cols = pl.ds(pl.multiple_of(c * LANES, LANES), LANES)
            jr = slice(jc * jrows, (jc + 1) * jrows)
            g = [jnp.zeros((jrows, LANES), F32) for _ in range(n_sub)]
            for h in range(PEER_HEADS):
                s2c = s2_ref[h, jr, cols]
                e2c = e2n_ref[h, jr, cols]
                for il in range(n_sub):
                    gate = e2c * e1_ref[h, il:il + 1, cols]
                    g[il] = g[il] + jnp.where(s2c >= th_ref[h, il:il + 1, cols], gate, 0.0)
            for il in range(n_sub):
                rows = slice(il * PEER_N_KEYS + jc * jrows, il * PEER_N_KEYS + (jc + 1) * jrows)
                w_new[rows, cols] = (a_old[rows, cols] * g[il]).astype(w_new.dtype)

        def stage_c(n):
            rows = pl.ds(pl.multiple_of(n * PEER_CHUNK, PEER_CHUNK), PEER_CHUNK)
            o_ref[rows, :] += jnp.dot(vt_ref[rows, :], w_old[...], preferred_element_type=F32)

        n_trips = tt // LANES
        assert eb // PEER_CHUNK == n_trips and d // PEER_CHUNK == 2 * n_trips

        def trip(k, carry):
            stage_a(k)
            stage_c(2 * k)
            for jc in range(PEER_N_KEYS // jrows):
                stage_b(k, jc)
            stage_c(2 * k + 1)
            return carry

        lax.fori_loop(0, n_trips, trip, 0)

    parity = lax.rem(s, 2)

    @pl.when(parity == 0)
    def _():
        run(a0, a1, w1, w0)

    @pl.when(parity == 1)
    def _():
        run(a1, a0, w0, w1)


def _peer_dense(hn_t, u, vt, th, e1, s2, e2n):
    d, s = hn_t.shape
    tt = min(PEER_TOKENS, s)
    eb = PEER_EXPERTS
    nb = u.shape[0] // eb
    n_items = (s // tt) * nb
    items = functools.partial(_peer_items, n_items=n_items, nb=nb)
    sub = pl.BlockSpec((PEER_HEADS, eb // PEER_N_KEYS, tt), lambda i: (0, items(i)[1][1], items(i)[1][0]))
    tab = pl.BlockSpec((PEER_HEADS, PEER_N_KEYS, tt), lambda i: (0, 0, items(i)[1][0]))
    return pl.pallas_call(
        functools.partial(_peer_dense_body, n_items=n_items, nb=nb),
        grid=(n_items + 2,),
        in_specs=[pl.BlockSpec((d, tt), lambda i: (0, items(i)[0][0])),
                  pl.BlockSpec((eb, d), lambda i: (items(i)[0][1], 0)),
                  pl.BlockSpec((d, eb), lambda i: (0, items(i)[2][1])),
                  sub, sub, tab, tab],
        out_specs=pl.BlockSpec((d, tt), lambda i: (0, items(i)[2][0])),
        out_shape=jax.ShapeDtypeStruct((d, s), F32),
        scratch_shapes=[pltpu.VMEM((eb, tt), F32), pltpu.VMEM((eb, tt), F32),
                        pltpu.VMEM((eb, tt), BF16), pltpu.VMEM((eb, tt), BF16)],
        compiler_params=_cparams(("arbitrary",)),
        name="peer_dense",
    )(hn_t, u, vt, th, e1, s2, e2n)


def _add_norm_body(x_ref, dt_ref, g_ref, *out_refs):
    x = x_ref[...] + dt_ref[...].T
    y = x * lax.rsqrt(jnp.mean(x * x, axis=-1, keepdims=True) + NORM_EPS)
    out_refs[-1][...] = (y * g_ref[...]).astype(out_refs[-1].dtype)
    if len(out_refs) == 2:
        out_refs[0][...] = x


def _add_norm(x, delta_t, g, norm_dtype, with_sum):
    s, d = x.shape
    tm = min(ROW_TILE, s)
    row = pl.BlockSpec((tm, d), lambda i: (i, 0))
    out_specs = [row, row] if with_sum else [row]
    out_shape = [jax.ShapeDtypeStruct((s, d), norm_dtype)]
    if with_sum:
        out_shape.insert(0, jax.ShapeDtypeStruct((s, d), F32))
    return pl.pallas_call(
        _add_norm_body,
        grid=(s // tm,),
        in_specs=[row, pl.BlockSpec((d, tm), lambda i: (0, i)), pl.BlockSpec((1, d), lambda i: (0, 0))],
        out_specs=out_specs,
        out_shape=out_shape,
        compiler_params=_cparams(("parallel",)),
        name="add_norm",
    )(x, delta_t, g.reshape(1, d))


def _rope_tables(positions, dim, lanes_x1):
    half = dim // 2
    inv_freq = ROPE_THETA ** (-jnp.arange(0, dim, 2, dtype=F32) / dim)
    ang = positions.astype(F32)[:, None] * inv_freq
    cos, sin = jnp.cos(ang), jnp.sin(ang)
    s = positions.shape[0]
    c_parts, su_parts, sd_parts = [], [], []
    pos = 0
    for off in lanes_x1:
        gap = off - pos
        c_parts += [jnp.ones((s, gap), F32), cos, cos]
        su_parts += [jnp.zeros((s, gap), F32), -sin, jnp.zeros((s, half), F32)]
        sd_parts += [jnp.zeros((s, gap + half), F32), sin]
        pos = off + dim
    tail = LANES - pos
    c_parts.append(jnp.ones((s, tail), F32))
    su_parts.append(jnp.zeros((s, tail), F32))
    sd_parts.append(jnp.zeros((s, tail), F32))
    return tuple(jnp.concatenate(p, axis=1) for p in (c_parts, su_parts, sd_parts))


def _layout_w_in_body(w_ref, o_ref):
    w = w_ref[0]
    rows = w.shape[0]
    cast = lambda a, b: w[:, a:b].astype(o_ref.dtype)
    off = dict(zip(("cq", "ckv", "kr", "sc"), IN_OFFSETS))
    run = sum(IN_SIZES[3:])
    o_ref[:, Z_SC:Z_SC + run] = cast(off["sc"], off["sc"] + run)
    o_ref[:, Z_CQ:Z_CQ + MLA_Q_LORA] = cast(off["cq"], off["cq"] + MLA_Q_LORA)
    o_ref[:, Z_CKV:Z_CKV + MLA_KV_LORA] = cast(off["ckv"], off["ckv"] + MLA_KV_LORA)
    zeros = lambda n: jnp.zeros((rows, n), o_ref.dtype)
    o_ref[:, Z_KR:Z_COLS] = jnp.concatenate(
        [zeros(MLA_NOPE), cast(off["kr"], off["kr"] + MLA_ROPE), zeros(Z_COLS - Z_KR - MLA_NOPE - MLA_ROPE)], axis=1)


def _layout_w_in(w_in, layer):
    _, d, n = w_in.shape
    tk = LANES
    return pl.pallas_call(
        _layout_w_in_body,
        grid=(d // tk,),
        in_specs=[pl.BlockSpec((1, tk, n), lambda i: (layer, i, 0))],
        out_specs=pl.BlockSpec((tk, Z_COLS), lambda i: (i, 0)),
        out_shape=jax.ShapeDtypeStruct((d, Z_COLS), BF16),
        compiler_params=_cparams(("parallel",)),
        name="layout_w_in",
    )(w_in)


def _layout_mla_weights(w_uq, w_ukv):
    lq = w_uq.shape[0]
    wq = w_uq.reshape(lq, MLA_HEADS, MLA_NOPE + MLA_ROPE)
    wq = jnp.pad(wq, ((0, 0), (0, 0), (0, LANES - MLA_NOPE - MLA_ROPE))).reshape(lq, MLA_HEADS * LANES)
    lk = w_ukv.shape[0]
    wkv = w_ukv.reshape(lk, MLA_HEADS, MLA_NOPE + MLA_V)
    wk = jnp.pad(wkv[:, :, :MLA_NOPE], ((0, 0), (0, 0), (0, LANES - MLA_NOPE))).reshape(lk, MLA_HEADS * LANES)
    wv = wkv[:, :, MLA_NOPE:].reshape(lk, MLA_HEADS * MLA_V)
    return wq.T.astype(BF16), wk.astype(BF16), wv.T.astype(BF16)


def kernel(x, positions, mix_norm_g, w_in, gate_b, mla_q_norm_g, mla_w_uq, mla_kv_norm_g, mla_w_ukv, mla_w_out,
           sc_conv_w, sc_w_out, diff_lambda, diff_norm_g, diff_w_out, conf_dw_w, conf_dw_b, conf_ln_g, conf_ln_b,
           conf_w_out, w_o, ffn_norm_g, peer_w_q, peer_sub_keys, peer_u, peer_v, final_norm_g):
    b, s, d = x.shape
    assert b == 1 and d == D_MODEL
    xs = x.reshape(s, d)
    pos = positions.reshape(s)
    mla_tabs = _rope_tables(pos, MLA_ROPE, (MLA_NOPE,))
    mla_tabs_t = tuple(t.T for t in mla_tabs)
    diff_tabs = _rope_tables(pos, DIFF_ROT, (0, DIFF_HEAD_DIM))

    for i in range(DEPTH):
        lam_init = 0.8 - 0.6 * math.exp(-0.3 * i)
        if i == 0:
            hn = _rmsnorm(xs, mix_norm_g[i], BF16)
        else:
            xs, hn = _add_norm(xs, delta_t, mix_norm_g[i], BF16, with_sum=True)
        z = _matmul(hn, _layout_w_in(w_in, i), BF16, name="in_proj")
        wqt, wk, wvt = _layout_mla_weights(mla_w_uq[i], mla_w_ukv[i])
        qt, k, vt = _mla_prep(z, mla_tabs, mla_tabs_t, mla_q_norm_g[i].reshape(1, -1),
                              mla_kv_norm_g[i].reshape(1, -1), wqt, wk, wvt)
        o_mla = _mla_attn(qt, k, vt)
        o_sc, o_conf = _conv_branches(z, sc_conv_w[i], conf_dw_w[i], conf_dw_b[i], conf_ln_g[i], conf_ln_b[i])
        qdt, kd, vdt = _diff_prep(z, diff_tabs)
        o_diff = _diff_attn(qdt, kd, vdt, diff_lambda[i], diff_norm_g[i], lam_init)
        merged = _merge((o_mla, o_sc, o_diff, o_conf),
                        tuple(w.astype(BF16) for w in (mla_w_out[i], sc_w_out[i], diff_w_out[i], conf_w_out[i])),
                        z, gate_b[i])
        xs = _matmul(merged, w_o[i].astype(BF16), F32, residual=xs, name="out_proj")
        hf, hf_t = _rmsnorm(xs, ffn_norm_g[i], BF16, with_transposed=True)
        pq = _matmul(hf, peer_w_q[i].astype(BF16), F32, name="peer_q")
        zk = jnp.zeros((PEER_N_KEYS, PEER_HALF), F32)
        k1 = jnp.concatenate([peer_sub_keys[i, 0], zk], axis=1)
        k2 = jnp.concatenate([zk, peer_sub_keys[i, 1]], axis=1)
        th, s2, e1, e2n = _peer_scores(pq, k1, k2)
        delta_t = _peer_dense(hf_t, peer_u[i].astype(BF16), peer_v[i].T.astype(BF16), th, e1, s2, e2n)
    (out,) = _add_norm(xs, delta_t, final_norm_g, F32, with_sum=False)
    return out.reshape(b, s, d)
```

```python
import functools
import math

import jax
import jax.numpy as jnp
import numpy as np
from jax import lax
from jax.experimental import pallas as pl
from jax.experimental.pallas import tpu as pltpu

F32 = jnp.float32
BF16 = jnp.bfloat16

D_MODEL = 2048
DEPTH = 2
CHUNK = 64
ROPE_THETA = 500000.0
NORM_EPS = 1e-6
LN_EPS = 1e-5
NEG_INF = -1e30
N_BRANCH = 4

MLA_HEADS = 8
MLA_Q_LORA = 512
MLA_KV_LORA = 256
MLA_NOPE = 64
MLA_ROPE = 32
MLA_V = 64
MLA_SCALE = (MLA_NOPE + MLA_ROPE) ** -0.5

SC_WIDTH = 512
SC_KERNEL = 3

DIFF_HEADS = 4
DIFF_HEAD_DIM = 64
DIFF_ROT = DIFF_HEAD_DIM // 4
DIFF_SCALE = DIFF_HEAD_DIM ** -0.5
DIFF_WIDTH = DIFF_HEADS * 2 * DIFF_HEAD_DIM

CONF_WIDTH = 512
CONF_KERNEL = 31

PEER_HEADS = 8
PEER_N_KEYS = 128
PEER_N_EXPERTS = PEER_N_KEYS * PEER_N_KEYS
PEER_KEY_DIM = 128
PEER_HALF = PEER_KEY_DIM // 2
PEER_TOPK = 16

IN_SIZES = (MLA_Q_LORA, MLA_KV_LORA, MLA_ROPE, 3 * SC_WIDTH, DIFF_WIDTH, DIFF_WIDTH, DIFF_WIDTH,
            2 * CONF_WIDTH, N_BRANCH * D_MODEL)
IN_OFFSETS = tuple(int(v) for v in np.cumsum((0,) + IN_SIZES)[:-1])

LOG2E = math.log2(math.e)

LANES = 128
SUBLANES = 8
HALO = 32
VMEM_LIMIT = 56 * 1024 * 1024

ROW_TILE = 512
MM_TILE = 1024
ATTN_TILE = 512
SCORE_TILE = 512
PEER_TOKENS = 512
PEER_EXPERTS = 1024
PEER_CHUNK = 256

Z_SC = 0
Z_DQ = 1536
Z_DK = 2048
Z_DV = 2560
Z_CONF = 3072
Z_GATE = 4096
Z_CQ = 12288
Z_CKV = 12800
Z_KR = 13056
Z_COLS = 13312


def _cparams(sem):
    return pltpu.CompilerParams(dimension_semantics=sem, vmem_limit_bytes=VMEM_LIMIT)


def _rmsnorm_body(x_ref, g_ref, o_ref, *maybe_ot_ref):
    x = x_ref[...].astype(F32)
    y = x * lax.rsqrt(jnp.mean(x * x, axis=-1, keepdims=True) + NORM_EPS) * g_ref[...]
    o_ref[...] = y.astype(o_ref.dtype)
    for ot_ref in maybe_ot_ref:
        ot_ref[...] = y.T.astype(ot_ref.dtype)


def _rmsnorm(x, g, out_dtype, with_transposed=False):
    s, d = x.shape
    tm = min(ROW_TILE, s)
    out_specs = [pl.BlockSpec((tm, d), lambda i: (i, 0))]
    out_shape = [jax.ShapeDtypeStruct((s, d), out_dtype)]
    if with_transposed:
        out_specs.append(pl.BlockSpec((d, tm), lambda i: (0, i)))
        out_shape.append(jax.ShapeDtypeStruct((d, s), out_dtype))
    out = pl.pallas_call(
        _rmsnorm_body,
        grid=(s // tm,),
        in_specs=[pl.BlockSpec((tm, d), lambda i: (i, 0)), pl.BlockSpec((1, d), lambda i: (0, 0))],
        out_specs=out_specs,
        out_shape=out_shape,
        compiler_params=_cparams(("parallel",)),
        name="rmsnorm",
    )(x, g.reshape(1, d))
    return out if with_transposed else out[0]


def _matmul_body(a_ref, b_ref, o_ref, *, rhs_contract):
    dims = (((1,), (rhs_contract,)), ((), ()))
    o_ref[...] = lax.dot_general(a_ref[...], b_ref[...], dims, preferred_element_type=F32).astype(o_ref.dtype)


def _matmul_res_body(a_ref, b_ref, r_ref, o_ref, *, rhs_contract):
    dims = (((1,), (rhs_contract,)), ((), ()))
    acc = lax.dot_general(a_ref[...], b_ref[...], dims, preferred_element_type=F32)
    o_ref[...] = (r_ref[...] + acc).astype(o_ref.dtype)


def _matmul(a, b, out_dtype, residual=None, b_transposed=False, name="matmul"):
    m, k = a.shape
    n = b.shape[0] if b_transposed else b.shape[1]
    tm, tn = min(MM_TILE, m), min(MM_TILE, n)
    b_spec = (pl.BlockSpec((tn, k), lambda i, j: (j, 0)) if b_transposed
              else pl.BlockSpec((k, tn), lambda i, j: (0, j)))
    in_specs = [pl.BlockSpec((tm, k), lambda i, j: (i, 0)), b_spec]
    args = [a, b]
    body = _matmul_body
    if residual is not None:
        in_specs.append(pl.BlockSpec((tm, tn), lambda i, j: (i, j)))
        args.append(residual)
        body = _matmul_res_body
    body = functools.partial(body, rhs_contract=1 if b_transposed else 0)
    return pl.pallas_call(
        body,
        grid=(m // tm, n // tn),
        in_specs=in_specs,
        out_specs=pl.BlockSpec((tm, tn), lambda i, j: (i, j)),
        out_shape=jax.ShapeDtypeStruct((m, n), out_dtype),
        compiler_params=_cparams(("parallel", "arbitrary")),
        name=name,
    )(*args)


def _sigmoid(x):
    return 0.5 + 0.5 * jnp.tanh(0.5 * x)


def _rope128(x, c, s_up, s_dn, half):
    return x * c + pltpu.roll(x, LANES - half, 1) * s_up + pltpu.roll(x, half, 1) * s_dn


def _mla_prep_body(cq_ref, ckv_ref, kr_ref, c_ref, su_ref, sd_ref, ct_ref, sut_ref, sdt_ref,
                   qg_ref, kvg_ref, wqt_ref, wk_ref, wvt_ref, qt_ref, k_ref, vt_ref):
    half = MLA_ROPE // 2
    nt = (((1,), (1,)), ((), ()))

    def norm(x_ref, g_ref):
        x = x_ref[...].astype(F32)
        y = x * lax.rsqrt(jnp.mean(x * x, axis=-1, keepdims=True) + NORM_EPS)
        return (y * g_ref[...]).astype(BF16)

    qf_t = lax.dot_general(wqt_ref[...], norm(cq_ref, qg_ref), nt, preferred_element_type=F32)
    ckvn = norm(ckv_ref, kvg_ref)
    kf = jnp.dot(ckvn, wk_ref[...], preferred_element_type=F32)
    vt_ref[...] = lax.dot_general(wvt_ref[...], ckvn, nt, preferred_element_type=F32).astype(vt_ref.dtype)
    kr = _rope128(kr_ref[...].astype(F32), c_ref[...], su_ref[...], sd_ref[...], half)
    ct, sut, sdt = ct_ref[...], sut_ref[...], sdt_ref[...]
    for h in range(MLA_HEADS):
        sl = slice(h * LANES, (h + 1) * LANES)
        x = qf_t[sl, :]
        q = x * ct + pltpu.roll(x, LANES - half, 0) * sut + pltpu.roll(x, half, 0) * sdt
        qt_ref[sl, :] = (q * (MLA_SCALE * LOG2E)).astype(qt_ref.dtype)
        k_ref[:, sl] = (kf[:, sl] + kr).astype(k_ref.dtype)


def _mla_prep(z, tabs, tabs_t, qg, kvg, wqt, wk, wvt):
    s = z.shape[0]
    tm = min(ROW_TILE, s)
    row = lambda w, idx: pl.BlockSpec((tm, w), lambda i: (i, idx))
    full = lambda a: pl.BlockSpec(a.shape, lambda i: (0,) * a.ndim)
    tab = pl.BlockSpec((tm, LANES), lambda i: (i, 0))
    tab_t = pl.BlockSpec((LANES, tm), lambda i: (0, i))
    hw = MLA_HEADS * LANES
    vw = MLA_HEADS * MLA_V
    return pl.pallas_call(
        _mla_prep_body,
        grid=(s // tm,),
        in_specs=[row(MLA_Q_LORA, Z_CQ // MLA_Q_LORA), row(MLA_KV_LORA, Z_CKV // MLA_KV_LORA),
                  row(LANES, Z_KR // LANES), tab, tab, tab, tab_t, tab_t, tab_t,
                  full(qg), full(kvg), full(wqt), full(wk), full(wvt)],
        out_specs=[pl.BlockSpec((hw, tm), lambda i: (0, i)), pl.BlockSpec((tm, hw), lambda i: (i, 0)),
                   pl.BlockSpec((vw, tm), lambda i: (0, i))],
        out_shape=[jax.ShapeDtypeStruct((hw, s), BF16), jax.ShapeDtypeStruct((s, hw), BF16),
                   jax.ShapeDtypeStruct((vw, s), BF16)],
        compiler_params=_cparams(("parallel",)),
        name="mla_prep",
    )(z, z, z, *tabs, *tabs_t, qg, kvg, wqt, wk, wvt)


def _flash_streams_t(q_ts, k_ats, vt_ats, m_ref, l_ref, acc_ref, n_full, t):
    streams = range(len(q_ts))
    for n in streams:
        m_ref[n] = jnp.full((1, t), NEG_INF, F32)
        l_ref[n] = jnp.zeros((1, t), F32)
        acc_ref[n] = jnp.zeros(acc_ref.shape[1:], F32)

    def step(j, masked):
        scores = [jnp.dot(k_ats[n](j), q_ts[n], preferred_element_type=F32) for n in streams]
        probs, alphas = [], []
        for n in streams:
            s = scores[n]
            if masked:
                key_chunk = lax.broadcasted_iota(jnp.int32, (t, t), 0) // CHUNK
                qry_chunk = lax.broadcasted_iota(jnp.int32, (t, t), 1) // CHUNK
                s = jnp.where(key_chunk <= qry_chunk, s, NEG_INF)
            m_prev = m_ref[n]
            m_new = jnp.maximum(m_prev, jnp.max(s, axis=0, keepdims=True))
            alpha = jnp.exp2(m_prev - m_new)
            p = jnp.exp2(s - m_new)
            l_ref[n] = alpha * l_ref[n] + jnp.sum(p, axis=0, keepdims=True)
            m_ref[n] = m_new
            probs.append(p.astype(BF16))
            alphas.append(alpha)
        for n in streams:
            acc_ref[n] = alphas[n] * acc_ref[n] + jnp.dot(vt_ats[n](j), probs[n], preferred_element_type=F32)

    def body(j, carry):
        step(j, False)
        return carry

    lax.fori_loop(0, n_full, body, 0)
    step(n_full, True)


def _mla_attn_body(qt_ref, k_ref, vt_ref, o_ref, m_ref, l_ref, acc_ref, *, t):
    i = pl.program_id(1)
    rows = lambda j: pl.ds(pl.multiple_of(j * t, t), t)
    lanes = [slice(hh * LANES, (hh + 1) * LANES) for hh in range(2)]
    k_ats = [lambda j, sl=sl: k_ref[rows(j), sl] for sl in lanes]
    vt_ats = [lambda j, hh=hh: vt_ref[hh * MLA_V:(hh + 1) * MLA_V, rows(j)] for hh in range(2)]
    _flash_streams_t([qt_ref[sl, :] for sl in lanes], k_ats, vt_ats, m_ref, l_ref, acc_ref, i, t)
    o_t = jnp.concatenate([acc_ref[0] / l_ref[0], acc_ref[1] / l_ref[1]], axis=0)
    o_ref[...] = o_t.T.astype(o_ref.dtype)


def _mla_attn(qt, k, vt):
    s = k.shape[0]
    t = min(ATTN_TILE, s)
    pairs = MLA_HEADS // 2
    return pl.pallas_call(
        functools.partial(_mla_attn_body, t=t),
        grid=(pairs, s // t),
        in_specs=[pl.BlockSpec((2 * LANES, t), lambda p, i: (p, i)),
                  pl.BlockSpec((s, 2 * LANES), lambda p, i: (0, p)),
                  pl.BlockSpec((2 * MLA_V, s), lambda p, i: (p, 0))],
        out_specs=pl.BlockSpec((t, 2 * MLA_V), lambda p, i: (i, p)),
        out_shape=jax.ShapeDtypeStruct((s, MLA_HEADS * MLA_V), BF16),
        scratch_shapes=[pltpu.VMEM((2, 1, t), F32), pltpu.VMEM((2, 1, t), F32),
                        pltpu.VMEM((2, MLA_V, t), F32)],
        compiler_params=_cparams(("parallel", "arbitrary")),
        name="mla_attn",
    )(qt, k, vt)


def _diff_prep_body(dq_ref, dk_ref, dv_ref, c_ref, su_ref, sd_ref, qt_ref, k_ref, vt_ref):
    half = DIFF_ROT // 2
    c, su, sd = c_ref[...], su_ref[...], sd_ref[...]
    tm = dq_ref.shape[0]
    lane = lax.broadcasted_iota(jnp.int32, (tm, LANES), 1)
    vt_ref[...] = dv_ref[...].astype(F32).T.astype(vt_ref.dtype)
    for h in range(DIFF_HEADS):
        sl = slice(h * LANES, (h + 1) * LANES)
        q = _rope128(dq_ref[:, sl].astype(F32), c, su, sd, half) * (DIFF_SCALE * LOG2E)
        qt_ref[(2 * h) * LANES:(2 * h + 1) * LANES, :] = jnp.where(lane < DIFF_HEAD_DIM, q, 0.0).T.astype(qt_ref.dtype)
        qt_ref[(2 * h + 1) * LANES:(2 * h + 2) * LANES, :] = jnp.where(lane >= DIFF_HEAD_DIM, q, 0.0).T.astype(qt_ref.dtype)
        k_ref[:, sl] = _rope128(dk_ref[:, sl].astype(F32), c, su, sd, half).astype(k_ref.dtype)


def _diff_prep(z, tabs):
    s = z.shape[0]
    tm = min(ROW_TILE, s)
    tab = pl.BlockSpec((tm, LANES), lambda i: (i, 0))
    col = lambda start: pl.BlockSpec((tm, DIFF_WIDTH), lambda i: (i, start // DIFF_WIDTH))
    return pl.pallas_call(
        _diff_prep_body,
        grid=(s // tm,),
        in_specs=[col(Z_DQ), col(Z_DK), col(Z_DV), tab, tab, tab],
        out_specs=[pl.BlockSpec((2 * DIFF_WIDTH, tm), lambda i: (0, i)),
                   pl.BlockSpec((tm, DIFF_WIDTH), lambda i: (i, 0)),
                   pl.BlockSpec((DIFF_WIDTH, tm), lambda i: (0, i))],
        out_shape=[jax.ShapeDtypeStruct((2 * DIFF_WIDTH, s), BF16), jax.ShapeDtypeStruct((s, DIFF_WIDTH), BF16),
                   jax.ShapeDtypeStruct((DIFF_WIDTH, s), BF16)],
        compiler_params=_cparams(("parallel",)),
        name="diff_prep",
    )(z, z, z, *tabs)


def _diff_attn_body(qt_ref, k_ref, vt_ref, lam_ref, g_ref, o_ref, m_ref, l_ref, acc_ref, *, t, lam_init):
    i = pl.program_id(1)
    rows = lambda j: pl.ds(pl.multiple_of(j * t, t), t)
    k_at = lambda j: k_ref[rows(j), :]
    vt_at = lambda j: vt_ref[:, rows(j)]
    q_ts = [qt_ref[c * LANES:(c + 1) * LANES, :] for c in range(2)]
    _flash_streams_t(q_ts, [k_at, k_at], [vt_at, vt_at], m_ref, l_ref, acc_ref, i, t)
    lv = lam_ref[...]
    lam = (jnp.exp(jnp.sum(lv[0:1] * lv[1:2], axis=1, keepdims=True))
           - jnp.exp(jnp.sum(lv[2:3] * lv[3:4], axis=1, keepdims=True)) + lam_init)
    o = (acc_ref[0] / l_ref[0] - lam * (acc_ref[1] / l_ref[1])).T
    o = o * lax.rsqrt(jnp.mean(o * o, axis=-1, keepdims=True) + NORM_EPS)
    o_ref[...] = (o * g_ref[...] * (1.0 - lam_init)).astype(o_ref.dtype)


def _diff_attn(qt, k, vt, lam_vecs, norm_g, lam_init):
    s = k.shape[0]
    t = min(ATTN_TILE, s)
    return pl.pallas_call(
        functools.partial(_diff_attn_body, t=t, lam_init=lam_init),
        grid=(DIFF_HEADS, s // t),
        in_specs=[pl.BlockSpec((2 * LANES, t), lambda h, i: (h, i)),
                  pl.BlockSpec((s, LANES), lambda h, i: (0, h)),
                  pl.BlockSpec((LANES, s), lambda h, i: (h, 0)),
                  pl.BlockSpec(lam_vecs.shape, lambda h, i: (0, 0)),
                  pl.BlockSpec((1, LANES), lambda h, i: (0, 0))],
        out_specs=pl.BlockSpec((t, LANES), lambda h, i: (i, h)),
        out_shape=jax.ShapeDtypeStruct((s, DIFF_WIDTH), BF16),
        scratch_shapes=[pltpu.VMEM((2, 1, t), F32), pltpu.VMEM((2, 1, t), F32),
                        pltpu.VMEM((2, LANES, t), F32)],
        compiler_params=_cparams(("parallel", "arbitrary")),
        name="diff_attn",
    )(qt, k, vt, lam_vecs, norm_g.reshape(1, LANES))


def _conv_body(bg_ref, cg_ref, xv_ref, cgh_ref, xvh_ref, a_ref, gt_ref, ah_ref, gth_ref,
               scw_ref, dww_ref, dwb_ref, lng_ref, lnb_ref, sc_ref, cf_ref, ext_ref):
    tm = bg_ref.shape[0]
    first = pl.program_id(0) == 0

    shifted_rows = HALO + tm - SUBLANES

    def fill(cur, halo, taps):
        ext_ref[0, 0:HALO, :] = jnp.where(first, 0.0, halo)
        ext_ref[0, HALO:HALO + tm, :] = cur
        for r in sorted({(HALO - (taps - 1) + j) % SUBLANES for j in range(taps)} - {0}):
            ext_ref[r, 0:shifted_rows, :] = ext_ref[0, r:r + shifted_rows, :]

    def conv(w_ref, taps):
        acc = jnp.zeros((tm, ext_ref.shape[2]), F32)
        for j in range(taps):
            off = HALO - (taps - 1) + j
            r = off % SUBLANES
            acc = acc + w_ref[j:j + 1, :] * ext_ref[r, off - r:off - r + tm, :]
        return acc

    fill(cg_ref[...].astype(F32) * xv_ref[...].astype(F32), cgh_ref[...].astype(F32) * xvh_ref[...].astype(F32),
         SC_KERNEL)
    sc_ref[...] = (bg_ref[...].astype(F32) * conv(scw_ref, SC_KERNEL)).astype(sc_ref.dtype)

    fill(a_ref[...].astype(F32) * _sigmoid(gt_ref[...].astype(F32)),
         ah_ref[...].astype(F32) * _sigmoid(gth_ref[...].astype(F32)), CONF_KERNEL)
    u = conv(dww_ref, CONF_KERNEL) + dwb_ref[...]
    mu = jnp.mean(u, axis=-1, keepdims=True)
    var = jnp.mean(jnp.square(u - mu), axis=-1, keepdims=True)
    y = (u - mu) * lax.rsqrt(var + LN_EPS) * lng_ref[...] + lnb_ref[...]
    cf_ref[...] = (y * _sigmoid(y)).astype(cf_ref.dtype)


def _conv_branches(z, sc_w, dw_w, dw_b, ln_g, ln_b):
    s = z.shape[0]
    tm = min(ROW_TILE, s)
    w = SC_WIDTH
    cur = lambda col: pl.BlockSpec((tm, w), lambda i: (i, col // w))
    halo = lambda col: pl.BlockSpec((HALO, w), lambda i: (jnp.maximum(i * (tm // HALO) - 1, 0), col // w))
    full = lambda a: pl.BlockSpec(a.shape, lambda i: (0,) * a.ndim)
    vec = lambda a: a.reshape(1, -1)
    args = [sc_w, dw_w, vec(dw_b), vec(ln_g), vec(ln_b)]
    return pl.pallas_call(
        _conv_body,
        grid=(s // tm,),
        in_specs=[cur(Z_SC), cur(Z_SC + w), cur(Z_SC + 2 * w), halo(Z_SC + w), halo(Z_SC + 2 * w),
                  cur(Z_CONF), cur(Z_CONF + w), halo(Z_CONF), halo(Z_CONF + w)] + [full(a) for a in args],
        out_specs=[pl.BlockSpec((tm, w), lambda i: (i, 0)), pl.BlockSpec((tm, w), lambda i: (i, 0))],
        out_shape=[jax.ShapeDtypeStruct((s, w), BF16), jax.ShapeDtypeStruct((s, w), BF16)],
        scratch_shapes=[pltpu.VMEM((SUBLANES, HALO + tm, w), F32)],
        compiler_params=_cparams(("parallel",)),
        name="conv_branches",
    )(*([z] * 9), *args)


def _merge_body(b0, b1, b2, b3, w0, w1, w2, w3, g0, g1, g2, g3, gb_ref, o_ref):
    acc = None
    for n, (b_ref, w_ref, g_ref) in enumerate(((b0, w0, g0), (b1, w1, g1), (b2, w2, g2), (b3, w3, g3))):
        y = jnp.dot(b_ref[...], w_ref[...], preferred_element_type=F32)
        t = _sigmoid(g_ref[...].astype(F32) + gb_ref[n:n + 1, :]) * y
        acc = t if acc is None else acc + t
    o_ref[...] = acc.astype(o_ref.dtype)


def _merge(branches, w_outs, z, gate_b):
    s = z.shape[0]
    tm = min(MM_TILE, s)
    tn = ROW_TILE
    kw = branches[0].shape[1]
    return pl.pallas_call(
        _merge_body,
        grid=(s // tm, D_MODEL // tn),
        in_specs=([pl.BlockSpec((tm, kw), lambda i, j: (i, 0))] * 4
                  + [pl.BlockSpec((kw, tn), lambda i, j: (0, j))] * 4
                  + [pl.BlockSpec((tm, tn), lambda i, j, b=b: (i, (Z_GATE + b * D_MODEL) // tn + j))
                     for b in range(N_BRANCH)]
                  + [pl.BlockSpec((N_BRANCH, tn), lambda i, j: (0, j))]),
        out_specs=pl.BlockSpec((tm, tn), lambda i, j: (i, j)),
        out_shape=jax.ShapeDtypeStruct((s, D_MODEL), BF16),
        compiler_params=_cparams(("parallel", "arbitrary")),
        name="merge",
    )(*branches, *w_outs, z, z, z, z, gate_b)


def _order(v, i, j):
    v[i], v[j] = jnp.maximum(v[i], v[j]), jnp.minimum(v[i], v[j])


def _bitonic_sort_desc(v):
    n = len(v)
    k = 2
    while k <= n:
        j = k // 2
        while j >= 1:
            for i in range(n):
                partner = i ^ j
                if partner > i:
                    if i & k == 0:
                        _order(v, i, partner)
                    else:
                        _order(v, partner, i)
            j //= 2
        k *= 2


def _bitonic_merge_desc(v):
    n = len(v)
    j = n // 2
    while j >= 1:
        for i in range(n):
            if i ^ j > i:
                _order(v, i, i ^ j)
        j //= 2


def _top16_desc(groups):
    v = list(groups)
    _bitonic_sort_desc(v)
    for shift in (SUBLANES // 2, SUBLANES // 4, SUBLANES // 8):
        other = [pltpu.roll(x, shift, 0) for x in v]
        v = [jnp.maximum(v[k], other[PEER_TOPK - 1 - k]) for k in range(PEER_TOPK)]
        _bitonic_merge_desc(v)
    return v


def _by_sublane(v, start):
    sub = lax.broadcasted_iota(jnp.int32, v[0].shape, 0)
    out = v[start]
    for r in range(1, SUBLANES):
        out = jnp.where(sub == r, v[start + r], out)
    return out


def _peer_candidates(a, a_hi, b_lo, b_hi, b0, op):
    groups = [op(a[0], b_lo), op(a[0], b_hi)]
    groups += [op(a[i], b_lo) for i in range(1, SUBLANES)]
    groups.append(op(a_hi, b0))
    return groups


def _peer_scores_body(q_ref, k1_ref, k2_ref, th_ref, s2_ref, e1_ref, e2n_ref):
    tt = q_ref.shape[0]
    n_groups = PEER_N_KEYS // SUBLANES

    def head(h, carry):
        qh = q_ref[:, pl.ds(pl.multiple_of(h * PEER_KEY_DIM, PEER_KEY_DIM), PEER_KEY_DIM)]
        dims = (((1,), (1,)), ((), ()))
        s1 = lax.dot_general(k1_ref[...], qh, dims, precision=lax.Precision.HIGHEST, preferred_element_type=F32)
        s2 = lax.dot_general(k2_ref[...], qh, dims, precision=lax.Precision.HIGHEST, preferred_element_type=F32)
        split = lambda s: [s[g * SUBLANES:(g + 1) * SUBLANES, :] for g in range(n_groups)]
        t1 = _top16_desc(split(s1))
        t2 = _top16_desc(split(s2))
        t1_hi, t2_lo, t2_hi = _by_sublane(t1, SUBLANES), _by_sublane(t2, 0), _by_sublane(t2, SUBLANES)
        cand = _peer_candidates(t1, t1_hi, t2_lo, t2_hi, t2[0], jnp.add)
        pad = [jnp.full((SUBLANES, tt), -jnp.inf, F32)] * (PEER_TOPK - len(cand))
        tau8 = _top16_desc(cand + pad)[PEER_TOPK - 1]

        m1, m2 = t1[0], t2[0]
        c1 = [jnp.exp(x - m1) for x in t1[:SUBLANES]]
        ec = _peer_candidates(c1, jnp.exp(t1_hi - m1), jnp.exp(t2_lo - m2), jnp.exp(t2_hi - m2),
                              jnp.ones((SUBLANES, tt), F32), jnp.multiply)
        zsum = jnp.zeros((SUBLANES, tt), F32)
        for cg, eg in zip(cand, ec):
            zsum = zsum + jnp.where(cg >= tau8, eg, 0.0)
        zden = jnp.sum(zsum, axis=0, keepdims=True)
        tau = tau8[0:1]
        th = jnp.full(s1.shape, jnp.inf, F32)
        for b in range(PEER_TOPK):
            t2b = t2[b][0:1]
            th = jnp.minimum(th, jnp.where(s1 + t2b >= tau, t2b, jnp.inf))
        th_ref[h] = th
        s2_ref[h] = s2
        e1_ref[h] = jnp.exp(s1 - m1[0:1])
        e2n_ref[h] = jnp.exp(s2 - m2[0:1]) / zden
        return carry

    lax.fori_loop(0, PEER_HEADS, head, 0)


def _peer_scores(q, k1, k2):
    s = q.shape[0]
    tt = min(SCORE_TILE, s)
    tab = pl.BlockSpec((PEER_HEADS, PEER_N_KEYS, tt), lambda i: (0, 0, i))
    tab_shape = jax.ShapeDtypeStruct((PEER_HEADS, PEER_N_KEYS, s), F32)
    return pl.pallas_call(
        _peer_scores_body,
        grid=(s // tt,),
        in_specs=[pl.BlockSpec((tt, PEER_HEADS * PEER_KEY_DIM), lambda i: (i, 0)),
                  pl.BlockSpec(k1.shape, lambda i: (0, 0)), pl.BlockSpec(k2.shape, lambda i: (0, 0))],
        out_specs=[tab, tab, tab, tab],
        out_shape=[tab_shape, tab_shape, tab_shape, tab_shape],
        compiler_params=_cparams(("parallel",)),
        name="peer_scores",
    )(q, k1, k2)


def _peer_items(s, n_items, nb):
    split = lambda it: (it // nb, it % nb)
    return (split(jnp.minimum(s, n_items - 1)), split(jnp.clip(s - 1, 0, n_items - 1)),
            split(jnp.clip(s - 2, 0, n_items - 1)))


def _peer_dense_body(ht_ref, u_ref, vt_ref, th_ref, e1_ref, s2_ref, e2n_ref, o_ref, a0, a1, w0, w1,
                     *, n_items, nb):
    s = pl.program_id(0)
    tt = ht_ref.shape[1]
    eb = u_ref.shape[0]
    d = vt_ref.shape[0]
    n_sub = eb // PEER_N_KEYS
    jrows = 2 * SUBLANES

    @pl.when(s == 0)
    def _():
        for ref in (a0, a1, w0, w1):
            ref[...] = jnp.zeros_like(ref)

    _, _, (_, block_c) = _peer_items(s, n_items, nb)

    @pl.when(block_c == 0)
    def _():
        o_ref[...] = jnp.zeros_like(o_ref)

    def run(a_new, a_old, w_new, w_old):
        def stage_a(k):
            rows = pl.ds(pl.multiple_of(k * PEER_CHUNK, PEER_CHUNK), PEER_CHUNK)
            a_t = jnp.dot(u_ref[rows, :], ht_ref[...], preferred_element_type=F32)
            a_new[rows, :] = 0.5 * a_t * (1.0 + lax.erf(a_t * (2.0 ** -0.5)))

        def stage_b(c, jc):
            cols = pl.ds(pl.multiple_of(c * LANES, LANES), LANES)
            jr = slice(jc * jrows, (jc + 1) * jrows)
            g = [jnp.zeros((jrows, LANES), F32) for _ in range(n_sub)]
            for h in range(PEER_HEADS):
                s2c = s2_ref[h, jr, cols]
                e2c = e2n_ref[h, jr, cols]
                for il in range(n_sub):
                    gate = e2c * e1_ref[h, il:il + 1, cols]
                    g[il] = g[il] + jnp.where(s2c >= th_ref[h, il:il + 1, cols], gate, 0.0)
            for il in range(n_sub):
                rows = slice(il * PEER_N_KEYS + jc * jrows, il * PEER_N_KEYS + (jc + 1) * jrows)
                w_new[rows, cols] = (a_old[rows, cols] * g[il]).astype(w_new.dtype)

        def stage_c(n):
            rows = pl.ds(pl.multiple_of(n * PEER_CHUNK, PEER_CHUNK), PEER_CHUNK)
            o_ref[rows, :] += jnp.dot(vt_ref[rows, :], w_old[...], preferred_element_type=F32)

        n_trips = tt // LANES
        assert eb // PEER_CHUNK == n_trips and d // PEER_CHUNK == 2 * n_trips

        def trip(k, carry):
            stage_a(k)
            stage_c(2 * k)
            for jc in range(PEER_N_KEYS // jrows):
                stage_b(k, jc)
            stage_c(2 * k + 1)
            return carry

        lax.fori_loop(0, n_trips, trip, 0)

    parity = lax.rem(s, 2)

    @pl.when(parity == 0)
    def _():
        run(a0, a1, w1, w0)

    @pl.when(parity == 1)
    def _():
        run(a1, a0, w0, w1)


def _peer_dense(hn_t, u_all, vt_all, layer, th, e1, s2, e2n):
    d, s = hn_t.shape
    tt = min(PEER_TOKENS, s)
    eb = PEER_EXPERTS
    nb = u_all.shape[1] // eb
    n_items = (s // tt) * nb
    items = functools.partial(_peer_items, n_items=n_items, nb=nb)
    sub = pl.BlockSpec((PEER_HEADS, eb // PEER_N_KEYS, tt), lambda i: (0, items(i)[1][1], items(i)[1][0]))
    tab = pl.BlockSpec((PEER_HEADS, PEER_N_KEYS, tt), lambda i: (0, 0, items(i)[1][0]))
    return pl.pallas_call(
        functools.partial(_peer_dense_body, n_items=n_items, nb=nb),
        grid=(n_items + 2,),
        in_specs=[pl.BlockSpec((d, tt), lambda i: (0, items(i)[0][0])),
                  pl.BlockSpec((None, eb, d), lambda i: (layer, items(i)[0][1], 0)),
                  pl.BlockSpec((None, d, eb), lambda i: (layer, 0, items(i)[2][1])),
                  sub, sub, tab, tab],
        out_specs=pl.BlockSpec((d, tt), lambda i: (0, items(i)[2][0])),
        out_shape=jax.ShapeDtypeStruct((d, s), F32),
        scratch_shapes=[pltpu.VMEM((eb, tt), F32), pltpu.VMEM((eb, tt), F32),
                        pltpu.VMEM((eb, tt), BF16), pltpu.VMEM((eb, tt), BF16)],
        compiler_params=_cparams(("arbitrary",)),
        name="peer_dense",
    )(hn_t, u_all, vt_all, th, e1, s2, e2n)


def _add_norm_body(x_ref, dt_ref, g_ref, *out_refs):
    x = x_ref[...] + dt_ref[...].T
    y = x * lax.rsqrt(jnp.mean(x * x, axis=-1, keepdims=True) + NORM_EPS)
    out_refs[-1][...] = (y * g_ref[...]).astype(out_refs[-1].dtype)
    if len(out_refs) == 2:
        out_refs[0][...] = x


def _add_norm(x, delta_t, g, norm_dtype, with_sum):
    s, d = x.shape
    tm = min(ROW_TILE, s)
    row = pl.BlockSpec((tm, d), lambda i: (i, 0))
    out_specs = [row, row] if with_sum else [row]
    out_shape = [jax.ShapeDtypeStruct((s, d), norm_dtype)]
    if with_sum:
        out_shape.insert(0, jax.ShapeDtypeStruct((s, d), F32))
    return pl.pallas_call(
        _add_norm_body,
        grid=(s // tm,),
        in_specs=[row, pl.BlockSpec((d, tm), lambda i: (0, i)), pl.BlockSpec((1, d), lambda i: (0, 0))],
        out_specs=out_specs,
        out_shape=out_shape,
        compiler_params=_cparams(("parallel",)),
        name="add_norm",
    )(x, delta_t, g.reshape(1, d))


def _rope_tables(positions, dim, lanes_x1):
    half = dim // 2
    inv_freq = ROPE_THETA ** (-jnp.arange(0, dim, 2, dtype=F32) / dim)
    ang = positions.astype(F32)[:, None] * inv_freq
    cos, sin = jnp.cos(ang), jnp.sin(ang)
    s = positions.shape[0]
    c_parts, su_parts, sd_parts = [], [], []
    pos = 0
    for off in lanes_x1:
        gap = off - pos
        c_parts += [jnp.ones((s, gap), F32), cos, cos]
        su_parts += [jnp.zeros((s, gap), F32), -sin, jnp.zeros((s, half), F32)]
        sd_parts += [jnp.zeros((s, gap + half), F32), sin]
        pos = off + dim
    tail = LANES - pos
    c_parts.append(jnp.ones((s, tail), F32))
    su_parts.append(jnp.zeros((s, tail), F32))
    sd_parts.append(jnp.zeros((s, tail), F32))
    return tuple(jnp.concatenate(p, axis=1) for p in (c_parts, su_parts, sd_parts))


W_IN_HEAD = IN_OFFSETS[3]
W_IN_RUN = sum(IN_SIZES[3:])
LAYOUT_ROWS = 1024


def _layout_w_in_body(a_ref, b_ref, o_ref):
    j = pl.program_id(0)
    rb = o_ref.shape[0]
    lead = rb - W_IN_HEAD % rb

    @pl.when(j < W_IN_RUN // rb)
    def _():
        o_ref[0:lead, :] = a_ref[W_IN_HEAD % rb:rb, :].astype(o_ref.dtype)
        o_ref[lead:rb, :] = b_ref[0:rb - lead, :].astype(o_ref.dtype)

    @pl.when(j == W_IN_RUN // rb)
    def _():
        head = MLA_Q_LORA + MLA_KV_LORA
        o_ref[0:head, :] = a_ref[0:head, :].astype(o_ref.dtype)
        o_ref[head:rb, :] = jnp.zeros((rb - head, o_ref.shape[1]), o_ref.dtype)
        o_ref[head + MLA_NOPE:head + MLA_NOPE + MLA_ROPE, :] = a_ref[head:head + MLA_ROPE, :].astype(o_ref.dtype)


def _layout_w_in(w_in_t, layer):
    _, n, d = w_in_t.shape
    rb = LAYOUT_ROWS
    assert W_IN_RUN % rb == 0 and Z_CQ == W_IN_RUN and Z_COLS == W_IN_RUN + rb and W_IN_HEAD < rb
    run_blocks = W_IN_RUN // rb
    first = lambda j: jnp.where(j < run_blocks, j + W_IN_HEAD // rb, 0)
    second = lambda j: jnp.minimum(j + W_IN_HEAD // rb + 1, (n - 1) // rb)
    return pl.pallas_call(
        _layout_w_in_body,
        grid=(Z_COLS // rb,),
        in_specs=[pl.BlockSpec((None, rb, d), lambda j: (layer, first(j), 0)),
                  pl.BlockSpec((None, rb, d), lambda j: (layer, second(j), 0))],
        out_specs=pl.BlockSpec((rb, d), lambda j: (j, 0)),
        out_shape=jax.ShapeDtypeStruct((Z_COLS, d), BF16),
        compiler_params=_cparams(("parallel",)),
        name="layout_w_in",
    )(w_in_t, w_in_t)


def _layout_mla_weights(w_uq, w_ukv):
    lq = w_uq.shape[0]
    wq = w_uq.reshape(lq, MLA_HEADS, MLA_NOPE + MLA_ROPE)
    wq = jnp.pad(wq, ((0, 0), (0, 0), (0, LANES - MLA_NOPE - MLA_ROPE))).reshape(lq, MLA_HEADS * LANES)
    lk = w_ukv.shape[0]
    wkv = w_ukv.reshape(lk, MLA_HEADS, MLA_NOPE + MLA_V)
    wk = jnp.pad(wkv[:, :, :MLA_NOPE], ((0, 0), (0, 0), (0, LANES - MLA_NOPE))).reshape(lk, MLA_HEADS * LANES)
    wv = wkv[:, :, MLA_NOPE:].reshape(lk, MLA_HEADS * MLA_V)
    return wq.T.astype(BF16), wk.astype(BF16), wv.T.astype(BF16)


def kernel(x, positions, mix_norm_g, w_in, gate_b, mla_q_norm_g, mla_w_uq, mla_kv_norm_g, mla_w_ukv, mla_w_out,
           sc_conv_w, sc_w_out, diff_lambda, diff_norm_g, diff_w_out, conf_dw_w, conf_dw_b, conf_ln_g, conf_ln_b,
           conf_w_out, w_o, ffn_norm_g, peer_w_q, peer_sub_keys, peer_u, peer_v, final_norm_g):
    b, s, d = x.shape
    assert b == 1 and d == D_MODEL
    xs = x.reshape(s, d)
    pos = positions.reshape(s)
    mla_tabs = _rope_tables(pos, MLA_ROPE, (MLA_NOPE,))
    mla_tabs_t = tuple(t.T for t in mla_tabs)
    w_in_t = jnp.swapaxes(w_in, 1, 2)
    u_all = peer_u.astype(BF16)
    vt_all = jnp.swapaxes(peer_v, 1, 2).astype(BF16)
    diff_tabs = _rope_tables(pos, DIFF_ROT, (0, DIFF_HEAD_DIM))

    for i in range(DEPTH):
        lam_init = 0.8 - 0.6 * math.exp(-0.3 * i)
        if i == 0:
            hn = _rmsnorm(xs, mix_norm_g[i], BF16)
        else:
            xs, hn = _add_norm(xs, delta_t, mix_norm_g[i], BF16, with_sum=True)
        z = _matmul(hn, _layout_w_in(w_in_t, i), BF16, b_transposed=True, name="in_proj")
        wqt, wk, wvt = _layout_mla_weights(mla_w_uq[i], mla_w_ukv[i])
        qt, k, vt = _mla_prep(z, mla_tabs, mla_tabs_t, mla_q_norm_g[i].reshape(1, -1),
                              mla_kv_norm_g[i].reshape(1, -1), wqt, wk, wvt)
        o_mla = _mla_attn(qt, k, vt)
        o_sc, o_conf = _conv_branches(z, sc_conv_w[i], conf_dw_w[i], conf_dw_b[i], conf_ln_g[i], conf_ln_b[i])
        qdt, kd, vdt = _diff_prep(z, diff_tabs)
        o_diff = _diff_attn(qdt, kd, vdt, diff_lambda[i], diff_norm_g[i], lam_init)
        merged = _merge((o_mla, o_sc, o_diff, o_conf),
                        tuple(w.astype(BF16) for w in (mla_w_out[i], sc_w_out[i], diff_w_out[i], conf_w_out[i])),
                        z, gate_b[i])
        xs = _matmul(merged, w_o[i].astype(BF16), F32, residual=xs, name="out_proj")
        hf, hf_t = _rmsnorm(xs, ffn_norm_g[i], BF16, with_transposed=True)
        pq = _matmul(hf, peer_w_q[i].astype(BF16), F32, name="peer_q")
        zk = jnp.zeros((PEER_N_KEYS, PEER_HALF), F32)
        k1 = jnp.concatenate([peer_sub_keys[i, 0], zk], axis=1)
        k2 = jnp.concatenate([zk, peer_sub_keys[i, 1]], axis=1)
        th, s2, e1, e2n = _peer_scores(pq, k1, k2)
        delta_t = _peer_dense(hf_t, u_all, vt_all, i, th, e1, s2, e2n)
    (out,) = _add_norm(xs, delta_t, final_norm_g, F32, with_sum=False)
    return out.reshape(b, s, d)
```

```python
import functools
import math

import jax
import jax.numpy as jnp
import numpy as np
from jax import lax
from jax.experimental import pallas as pl
from jax.experimental.pallas import tpu as pltpu

F32 = jnp.float32
BF16 = jnp.bfloat16

D_MODEL = 2048
DEPTH = 2
CHUNK = 64
ROPE_THETA = 500000.0
NORM_EPS = 1e-6
LN_EPS = 1e-5
NEG_INF = -1e30
N_BRANCH = 4

MLA_HEADS = 8
MLA_Q_LORA = 512
MLA_KV_LORA = 256
MLA_NOPE = 64
MLA_ROPE = 32
MLA_V = 64
MLA_SCALE = (MLA_NOPE + MLA_ROPE) ** -0.5

SC_WIDTH = 512
SC_KERNEL = 3

DIFF_HEADS = 4
DIFF_HEAD_DIM = 64
DIFF_ROT = DIFF_HEAD_DIM // 4
DIFF_SCALE = DIFF_HEAD_DIM ** -0.5
DIFF_WIDTH = DIFF_HEADS * 2 * DIFF_HEAD_DIM

CONF_WIDTH = 512
CONF_KERNEL = 31

PEER_HEADS = 8
PEER_N_KEYS = 128
PEER_N_EXPERTS = PEER_N_KEYS * PEER_N_KEYS
PEER_KEY_DIM = 128
PEER_HALF = PEER_KEY_DIM // 2
PEER_TOPK = 16

IN_SIZES = (MLA_Q_LORA, MLA_KV_LORA, MLA_ROPE, 3 * SC_WIDTH, DIFF_WIDTH, DIFF_WIDTH, DIFF_WIDTH,
            2 * CONF_WIDTH, N_BRANCH * D_MODEL)
IN_OFFSETS = tuple(int(v) for v in np.cumsum((0,) + IN_SIZES)[:-1])

LOG2E = math.log2(math.e)

LANES = 128
SUBLANES = 8
HALO = 32
VMEM_LIMIT = 56 * 1024 * 1024

ROW_TILE = 512
MM_TILE = 1024
ATTN_TILE = 512
SCORE_TILE = 512
PEER_TOKENS = 512
PEER_EXPERTS = 1024
PEER_CHUNK = 256

Z_SC = 0
Z_DQ = 1536
Z_DK = 2048
Z_DV = 2560
Z_CONF = 3072
Z_GATE = 4096
Z_CQ = 12288
Z_CKV = 12800
Z_KR = 13056
Z_COLS = 13312


def _cparams(sem):
    return pltpu.CompilerParams(dimension_semantics=sem, vmem_limit_bytes=VMEM_LIMIT)


def _rmsnorm_body(x_ref, g_ref, o_ref, *maybe_ot_ref):
    x = x_ref[...].astype(F32)
    y = x * lax.rsqrt(jnp.mean(x * x, axis=-1, keepdims=True) + NORM_EPS) * g_ref[...]
    o_ref[...] = y.astype(o_ref.dtype)
    for ot_ref in maybe_ot_ref:
        ot_ref[...] = y.T.astype(ot_ref.dtype)


def _rmsnorm(x, g, out_dtype, with_transposed=False):
    s, d = x.shape
    tm = min(ROW_TILE, s)
    out_specs = [pl.BlockSpec((tm, d), lambda i: (i, 0))]
    out_shape = [jax.ShapeDtypeStruct((s, d), out_dtype)]
    if with_transposed:
        out_specs.append(pl.BlockSpec((d, tm), lambda i: (0, i)))
        out_shape.append(jax.ShapeDtypeStruct((d, s), out_dtype))
    out = pl.pallas_call(
        _rmsnorm_body,
        grid=(s // tm,),
        in_specs=[pl.BlockSpec((tm, d), lambda i: (i, 0)), pl.BlockSpec((1, d), lambda i: (0, 0))],
        out_specs=out_specs,
        out_shape=out_shape,
        compiler_params=_cparams(("parallel",)),
        name="rmsnorm",
    )(x, g.reshape(1, d))
    return out if with_transposed else out[0]


def _matmul_body(a_ref, b_ref, o_ref, *, rhs_contract):
    dims = (((1,), (rhs_contract,)), ((), ()))
    o_ref[...] = lax.dot_general(a_ref[...], b_ref[...], dims, preferred_element_type=F32).astype(o_ref.dtype)


def _matmul_res_body(a_ref, b_ref, r_ref, o_ref, *, rhs_contract):
    dims = (((1,), (rhs_contract,)), ((), ()))
    acc = lax.dot_general(a_ref[...], b_ref[...], dims, preferred_element_type=F32)
    o_ref[...] = (r_ref[...] + acc).astype(o_ref.dtype)


def _matmul(a, b, out_dtype, residual=None, b_transposed=False, name="matmul"):
    m, k = a.shape
    n = b.shape[0] if b_transposed else b.shape[1]
    tm, tn = min(MM_TILE, m), min(MM_TILE, n)
    b_spec = (pl.BlockSpec((tn, k), lambda i, j: (j, 0)) if b_transposed
              else pl.BlockSpec((k, tn), lambda i, j: (0, j)))
    in_specs = [pl.BlockSpec((tm, k), lambda i, j: (i, 0)), b_spec]
    args = [a, b]
    body = _matmul_body
    if residual is not None:
        in_specs.append(pl.BlockSpec((tm, tn), lambda i, j: (i, j)))
        args.append(residual)
        body = _matmul_res_body
    body = functools.partial(body, rhs_contract=1 if b_transposed else 0)
    return pl.pallas_call(
        body,
        grid=(m // tm, n // tn),
        in_specs=in_specs,
        out_specs=pl.BlockSpec((tm, tn), lambda i, j: (i, j)),
        out_shape=jax.ShapeDtypeStruct((m, n), out_dtype),
        compiler_params=_cparams(("parallel", "arbitrary")),
        name=name,
    )(*args)


def _sigmoid(x):
    return 0.5 + 0.5 * jnp.tanh(0.5 * x)


def _rope128(x, c, s_up, s_dn, half):
    return x * c + pltpu.roll(x, LANES - half, 1) * s_up + pltpu.roll(x, half, 1) * s_dn


def _mla_prep_body(cq_ref, ckv_ref, kr_ref, c_ref, su_ref, sd_ref, ct_ref, sut_ref, sdt_ref,
                   qg_ref, kvg_ref, wqt_ref, wk_ref, wvt_ref, qt_ref, k_ref, vt_ref):
    half = MLA_ROPE // 2
    nt = (((1,), (1,)), ((), ()))

    def norm(x_ref, g_ref):
        x = x_ref[...].astype(F32)
        y = x * lax.rsqrt(jnp.mean(x * x, axis=-1, keepdims=True) + NORM_EPS)
        return (y * g_ref[...]).astype(BF16)

    qf_t = lax.dot_general(wqt_ref[...], norm(cq_ref, qg_ref), nt, preferred_element_type=F32)
    ckvn = norm(ckv_ref, kvg_ref)
    kf = jnp.dot(ckvn, wk_ref[...], preferred_element_type=F32)
    vt_ref[...] = lax.dot_general(wvt_ref[...], ckvn, nt, preferred_element_type=F32).astype(vt_ref.dtype)
    kr = _rope128(kr_ref[...].astype(F32), c_ref[...], su_ref[...], sd_ref[...], half)
    ct, sut, sdt = ct_ref[...], sut_ref[...], sdt_ref[...]
    for h in range(MLA_HEADS):
        sl = slice(h * LANES, (h + 1) * LANES)
        x = qf_t[sl, :]
        q = x * ct + pltpu.roll(x, LANES - half, 0) * sut + pltpu.roll(x, half, 0) * sdt
        qt_ref[sl, :] = (q * (MLA_SCALE * LOG2E)).astype(qt_ref.dtype)
        k_ref[:, sl] = (kf[:, sl] + kr).astype(k_ref.dtype)


def _mla_prep(z, tabs, tabs_t, qg, kvg, wqt, wk, wvt):
    s = z.shape[0]
    tm = min(ROW_TILE, s)
    row = lambda w, idx: pl.BlockSpec((tm, w), lambda i: (i, idx))
    full = lambda a: pl.BlockSpec(a.shape, lambda i: (0,) * a.ndim)
    tab = pl.BlockSpec((tm, LANES), lambda i: (i, 0))
    tab_t = pl.BlockSpec((LANES, tm), lambda i: (0, i))
    hw = MLA_HEADS * LANES
    vw = MLA_HEADS * MLA_V
    return pl.pallas_call(
        _mla_prep_body,
        grid=(s // tm,),
        in_specs=[row(MLA_Q_LORA, Z_CQ // MLA_Q_LORA), row(MLA_KV_LORA, Z_CKV // MLA_KV_LORA),
                  row(LANES, Z_KR // LANES), tab, tab, tab, tab_t, tab_t, tab_t,
                  full(qg), full(kvg), full(wqt), full(wk), full(wvt)],
        out_specs=[pl.BlockSpec((hw, tm), lambda i: (0, i)), pl.BlockSpec((tm, hw), lambda i: (i, 0)),
                   pl.BlockSpec((vw, tm), lambda i: (0, i))],
        out_shape=[jax.ShapeDtypeStruct((hw, s), BF16), jax.ShapeDtypeStruct((s, hw), BF16),
                   jax.ShapeDtypeStruct((vw, s), BF16)],
        compiler_params=_cparams(("parallel",)),
        name="mla_prep",
    )(z, z, z, *tabs, *tabs_t, qg, kvg, wqt, wk, wvt)


def _flash_streams_t(q_ts, k_ats, vt_ats, m_ref, l_ref, acc_ref, s_a, s_b, n_full, t):
    streams = range(len(q_ts))
    for n in streams:
        m_ref[n] = jnp.full((1, t), NEG_INF, F32)
        l_ref[n] = jnp.zeros((1, t), F32)
        acc_ref[n] = jnp.zeros(acc_ref.shape[1:], F32)

    def scores_into(s_ref, j):
        for n in streams:
            s_ref[n] = jnp.dot(k_ats[n](j), q_ts[n], preferred_element_type=F32)

    def consume(s_ref, j, masked):
        probs, alphas = [], []
        for n in streams:
            s = s_ref[n]
            if masked:
                key_chunk = lax.broadcasted_iota(jnp.int32, (t, t), 0) // CHUNK
                qry_chunk = lax.broadcasted_iota(jnp.int32, (t, t), 1) // CHUNK
                s = jnp.where(key_chunk <= qry_chunk, s, NEG_INF)
            m_prev = m_ref[n]
            m_new = jnp.maximum(m_prev, jnp.max(s, axis=0, keepdims=True))
            alpha = jnp.exp2(m_prev - m_new)
            p = jnp.exp2(s - m_new)
            l_ref[n] = alpha * l_ref[n] + jnp.sum(p, axis=0, keepdims=True)
            m_ref[n] = m_new
            probs.append(p.astype(BF16))
            alphas.append(alpha)
        for n in streams:
            acc_ref[n] = alphas[n] * acc_ref[n] + jnp.dot(vt_ats[n](j), probs[n], preferred_element_type=F32)

    scores_into(s_a, 0)

    def pair(jj, carry):
        j = 2 * jj
        scores_into(s_b, j + 1)
        consume(s_a, j, False)
        scores_into(s_a, j + 2)
        consume(s_b, j + 1, False)
        return carry

    lax.fori_loop(0, n_full // 2, pair, 0)

    @pl.when(n_full % 2 == 0)
    def _():
        consume(s_a, n_full, True)

    @pl.when(n_full % 2 == 1)
    def _():
        scores_into(s_b, n_full)
        consume(s_a, n_full - 1, False)
        consume(s_b, n_full, True)


def _mla_attn_body(qt_ref, k_ref, vt_ref, o_ref, m_ref, l_ref, acc_ref, s_a, s_b, *, t):
    i = pl.program_id(1)
    rows = lambda j: pl.ds(pl.multiple_of(j * t, t), t)
    lanes = [slice(hh * LANES, (hh + 1) * LANES) for hh in range(2)]
    k_ats = [lambda j, sl=sl: k_ref[rows(j), sl] for sl in lanes]
    vt_ats = [lambda j, hh=hh: vt_ref[hh * MLA_V:(hh + 1) * MLA_V, rows(j)] for hh in range(2)]
    _flash_streams_t([qt_ref[sl, :] for sl in lanes], k_ats, vt_ats, m_ref, l_ref, acc_ref, s_a, s_b, i, t)
    o_t = jnp.concatenate([acc_ref[0] / l_ref[0], acc_ref[1] / l_ref[1]], axis=0)
    o_ref[...] = o_t.T.astype(o_ref.dtype)


def _mla_attn(qt, k, vt):
    s = k.shape[0]
    t = min(ATTN_TILE, s)
    pairs = MLA_HEADS // 2
    return pl.pallas_call(
        functools.partial(_mla_attn_body, t=t),
        grid=(pairs, s // t),
        in_specs=[pl.BlockSpec((2 * LANES, t), lambda p, i: (p, i)),
                  pl.BlockSpec((s, 2 * LANES), lambda p, i: (0, p)),
                  pl.BlockSpec((2 * MLA_V, s), lambda p, i: (p, 0))],
        out_specs=pl.BlockSpec((t, 2 * MLA_V), lambda p, i: (i, p)),
        out_shape=jax.ShapeDtypeStruct((s, MLA_HEADS * MLA_V), BF16),
        scratch_shapes=[pltpu.VMEM((2, 1, t), F32), pltpu.VMEM((2, 1, t), F32),
                        pltpu.VMEM((2, MLA_V, t), F32),
                        pltpu.VMEM((2, t, t), F32), pltpu.VMEM((2, t, t), F32)],
        compiler_params=_cparams(("parallel", "arbitrary")),
        name="mla_attn",
    )(qt, k, vt)


def _diff_prep_body(dq_ref, dk_ref, dv_ref, c_ref, su_ref, sd_ref, qt_ref, k_ref, vt_ref):
    half = DIFF_ROT // 2
    c, su, sd = c_ref[...], su_ref[...], sd_ref[...]
    tm = dq_ref.shape[0]
    lane = lax.broadcasted_iota(jnp.int32, (tm, LANES), 1)
    vt_ref[...] = dv_ref[...].astype(F32).T.astype(vt_ref.dtype)
    for h in range(DIFF_HEADS):
        sl = slice(h * LANES, (h + 1) * LANES)
        q = _rope128(dq_ref[:, sl].astype(F32), c, su, sd, half) * (DIFF_SCALE * LOG2E)
        qt_ref[(2 * h) * LANES:(2 * h + 1) * LANES, :] = jnp.where(lane < DIFF_HEAD_DIM, q, 0.0).T.astype(qt_ref.dtype)
        qt_ref[(2 * h + 1) * LANES:(2 * h + 2) * LANES, :] = jnp.where(lane >= DIFF_HEAD_DIM, q, 0.0).T.astype(qt_ref.dtype)
        k_ref[:, sl] = _rope128(dk_ref[:, sl].astype(F32), c, su, sd, half).astype(k_ref.dtype)


def _diff_prep(z, tabs):
    s = z.shape[0]
    tm = min(ROW_TILE, s)
    tab = pl.BlockSpec((tm, LANES), lambda i: (i, 0))
    col = lambda start: pl.BlockSpec((tm, DIFF_WIDTH), lambda i: (i, start // DIFF_WIDTH))
    return pl.pallas_call(
        _diff_prep_body,
        grid=(s // tm,),
        in_specs=[col(Z_DQ), col(Z_DK), col(Z_DV), tab, tab, tab],
        out_specs=[pl.BlockSpec((2 * DIFF_WIDTH, tm), lambda i: (0, i)),
                   pl.BlockSpec((tm, DIFF_WIDTH), lambda i: (i, 0)),
                   pl.BlockSpec((DIFF_WIDTH, tm), lambda i: (0, i))],
        out_shape=[jax.ShapeDtypeStruct((2 * DIFF_WIDTH, s), BF16), jax.ShapeDtypeStruct((s, DIFF_WIDTH), BF16),
                   jax.ShapeDtypeStruct((DIFF_WIDTH, s), BF16)],
        compiler_params=_cparams(("parallel",)),
        name="diff_prep",
    )(z, z, z, *tabs)


def _diff_attn_body(qt_ref, k_ref, vt_ref, lam_ref, g_ref, o_ref, m_ref, l_ref, acc_ref, s_a, s_b, *, t, lam_init):
    i = pl.program_id(1)
    rows = lambda j: pl.ds(pl.multiple_of(j * t, t), t)
    k_at = lambda j: k_ref[rows(j), :]
    vt_at = lambda j: vt_ref[:, rows(j)]
    q_ts = [qt_ref[c * LANES:(c + 1) * LANES, :] for c in range(2)]
    _flash_streams_t(q_ts, [k_at, k_at], [vt_at, vt_at], m_ref, l_ref, acc_ref, s_a, s_b, i, t)
    lv = lam_ref[...]
    lam = (jnp.exp(jnp.sum(lv[0:1] * lv[1:2], axis=1, keepdims=True))
           - jnp.exp(jnp.sum(lv[2:3] * lv[3:4], axis=1, keepdims=True)) + lam_init)
    o = (acc_ref[0] / l_ref[0] - lam * (acc_ref[1] / l_ref[1])).T
    o = o * lax.rsqrt(jnp.mean(o * o, axis=-1, keepdims=True) + NORM_EPS)
    o_ref[...] = (o * g_ref[...] * (1.0 - lam_init)).astype(o_ref.dtype)


def _diff_attn(qt, k, vt, lam_vecs, norm_g, lam_init):
    s = k.shape[0]
    t = min(ATTN_TILE, s)
    return pl.pallas_call(
        functools.partial(_diff_attn_body, t=t, lam_init=lam_init),
        grid=(DIFF_HEADS, s // t),
        in_specs=[pl.BlockSpec((2 * LANES, t), lambda h, i: (h, i)),
                  pl.BlockSpec((s, LANES), lambda h, i: (0, h)),
                  pl.BlockSpec((LANES, s), lambda h, i: (h, 0)),
                  pl.BlockSpec(lam_vecs.shape, lambda h, i: (0, 0)),
                  pl.BlockSpec((1, LANES), lambda h, i: (0, 0))],
        out_specs=pl.BlockSpec((t, LANES), lambda h, i: (i, h)),
        out_shape=jax.ShapeDtypeStruct((s, DIFF_WIDTH), BF16),
        scratch_shapes=[pltpu.VMEM((2, 1, t), F32), pltpu.VMEM((2, 1, t), F32),
                        pltpu.VMEM((2, LANES, t), F32),
                        pltpu.VMEM((2, t, t), F32), pltpu.VMEM((2, t, t), F32)],
        compiler_params=_cparams(("parallel", "arbitrary")),
        name="diff_attn",
    )(qt, k, vt, lam_vecs, norm_g.reshape(1, LANES))


def _conv_body(bg_ref, cg_ref, xv_ref, cgh_ref, xvh_ref, a_ref, gt_ref, ah_ref, gth_ref,
               scw_ref, dww_ref, dwb_ref, lng_ref, lnb_ref, sc_ref, cf_ref, ext_ref):
    tm = bg_ref.shape[0]
    first = pl.program_id(0) == 0

    shifted_rows = HALO + tm - SUBLANES

    def fill(cur, halo, taps):
        ext_ref[0, 0:HALO, :] = jnp.where(first, 0.0, halo)
        ext_ref[0, HALO:HALO + tm, :] = cur
        for r in sorted({(HALO - (taps - 1) + j) % SUBLANES for j in range(taps)} - {0}):
            ext_ref[r, 0:shifted_rows, :] = ext_ref[0, r:r + shifted_rows, :]

    def conv(w_ref, taps):
        acc = jnp.zeros((tm, ext_ref.shape[2]), F32)
        for j in range(taps):
            off = HALO - (taps - 1) + j
            r = off % SUBLANES
            acc = acc + w_ref[j:j + 1, :] * ext_ref[r, off - r:off - r + tm, :]
        return acc

    fill(cg_ref[...].astype(F32) * xv_ref[...].astype(F32), cgh_ref[...].astype(F32) * xvh_ref[...].astype(F32),
         SC_KERNEL)
    sc_ref[...] = (bg_ref[...].astype(F32) * conv(scw_ref, SC_KERNEL)).astype(sc_ref.dtype)

    fill(a_ref[...].astype(F32) * _sigmoid(gt_ref[...].astype(F32)),
         ah_ref[...].astype(F32) * _sigmoid(gth_ref[...].astype(F32)), CONF_KERNEL)
    u = conv(dww_ref, CONF_KERNEL) + dwb_ref[...]
    mu = jnp.mean(u, axis=-1, keepdims=True)
    var = jnp.mean(jnp.square(u - mu), axis=-1, keepdims=True)
    y = (u - mu) * lax.rsqrt(var + LN_EPS) * lng_ref[...] + lnb_ref[...]
    cf_ref[...] = (y * _sigmoid(y)).astype(cf_ref.dtype)


def _conv_branches(z, sc_w, dw_w, dw_b, ln_g, ln_b):
    s = z.shape[0]
    tm = min(ROW_TILE, s)
    w = SC_WIDTH
    cur = lambda col: pl.BlockSpec((tm, w), lambda i: (i, col // w))
    halo = lambda col: pl.BlockSpec((HALO, w), lambda i: (jnp.maximum(i * (tm // HALO) - 1, 0), col // w))
    full = lambda a: pl.BlockSpec(a.shape, lambda i: (0,) * a.ndim)
    vec = lambda a: a.reshape(1, -1)
    args = [sc_w, dw_w, vec(dw_b), vec(ln_g), vec(ln_b)]
    return pl.pallas_call(
        _conv_body,
        grid=(s // tm,),
        in_specs=[cur(Z_SC), cur(Z_SC + w), cur(Z_SC + 2 * w), halo(Z_SC + w), halo(Z_SC + 2 * w),
                  cur(Z_CONF), cur(Z_CONF + w), halo(Z_CONF), halo(Z_CONF + w)] + [full(a) for a in args],
        out_specs=[pl.BlockSpec((tm, w), lambda i: (i, 0)), pl.BlockSpec((tm, w), lambda i: (i, 0))],
        out_shape=[jax.ShapeDtypeStruct((s, w), BF16), jax.ShapeDtypeStruct((s, w), BF16)],
        scratch_shapes=[pltpu.VMEM((SUBLANES, HALO + tm, w), F32)],
        compiler_params=_cparams(("parallel",)),
        name="conv_branches",
    )(*([z] * 9), *args)


def _merge_body(b0, b1, b2, b3, w0, w1, w2, w3, g0, g1, g2, g3, gb_ref, o_ref):
    acc = None
    for n, (b_ref, w_ref, g_ref) in enumerate(((b0, w0, g0), (b1, w1, g1), (b2, w2, g2), (b3, w3, g3))):
        y = jnp.dot(b_ref[...], w_ref[...], preferred_element_type=F32)
        t = _sigmoid(g_ref[...].astype(F32) + gb_ref[n:n + 1, :]) * y
        acc = t if acc is None else acc + t
    o_ref[...] = acc.astype(o_ref.dtype)


def _merge(branches, w_outs, z, gate_b):
    s = z.shape[0]
    tm = min(MM_TILE, s)
    tn = ROW_TILE
    kw = branches[0].shape[1]
    return pl.pallas_call(
        _merge_body,
        grid=(s // tm, D_MODEL // tn),
        in_specs=([pl.BlockSpec((tm, kw), lambda i, j: (i, 0))] * 4
                  + [pl.BlockSpec((kw, tn), lambda i, j: (0, j))] * 4
                  + [pl.BlockSpec((tm, tn), lambda i, j, b=b: (i, (Z_GATE + b * D_MODEL) // tn + j))
                     for b in range(N_BRANCH)]
                  + [pl.BlockSpec((N_BRANCH, tn), lambda i, j: (0, j))]),
        out_specs=pl.BlockSpec((tm, tn), lambda i, j: (i, j)),
        out_shape=jax.ShapeDtypeStruct((s, D_MODEL), BF16),
        compiler_params=_cparams(("parallel", "arbitrary")),
        name="merge",
    )(*branches, *w_outs, z, z, z, z, gate_b)


def _order(v, i, j):
    v[i], v[j] = jnp.maximum(v[i], v[j]), jnp.minimum(v[i], v[j])


def _bitonic_sort_desc(v):
    n = len(v)
    k = 2
    while k <= n:
        j = k // 2
        while j >= 1:
            for i in range(n):
                partner = i ^ j
                if partner > i:
                    if i & k == 0:
                        _order(v, i, partner)
                    else:
                        _order(v, partner, i)
            j //= 2
        k *= 2


def _bitonic_merge_desc(v):
    n = len(v)
    j = n // 2
    while j >= 1:
        for i in range(n):
            if i ^ j > i:
                _order(v, i, i ^ j)
        j //= 2


def _top16_desc(groups):
    v = list(groups)
    _bitonic_sort_desc(v)
    for shift in (SUBLANES // 2, SUBLANES // 4, SUBLANES // 8):
        other = [pltpu.roll(x, shift, 0) for x in v]
        v = [jnp.maximum(v[k], other[PEER_TOPK - 1 - k]) for k in range(PEER_TOPK)]
        _bitonic_merge_desc(v)
    return v


def _by_sublane(v, start):
    sub = lax.broadcasted_iota(jnp.int32, v[0].shape, 0)
    out = v[start]
    for r in range(1, SUBLANES):
        out = jnp.where(sub == r, v[start + r], out)
    return out


def _peer_candidates(a, a_hi, b_lo, b_hi, b0, op):
    groups = [op(a[0], b_lo), op(a[0], b_hi)]
    groups += [op(a[i], b_lo) for i in range(1, SUBLANES)]
    groups.append(op(a_hi, b0))
    return groups


def _peer_scores_body(q_ref, k1_ref, k2_ref, th_ref, s2_ref, e1_ref, e2n_ref):
    tt = q_ref.shape[0]
    n_groups = PEER_N_KEYS // SUBLANES

    def head(h, carry):
        qh = q_ref[:, pl.ds(pl.multiple_of(h * PEER_KEY_DIM, PEER_KEY_DIM), PEER_KEY_DIM)]
        dims = (((1,), (1,)), ((), ()))
        s1 = lax.dot_general(k1_ref[...], qh, dims, precision=lax.Precision.HIGHEST, preferred_element_type=F32)
        s2 = lax.dot_general(k2_ref[...], qh, dims, precision=lax.Precision.HIGHEST, preferred_element_type=F32)
        split = lambda s: [s[g * SUBLANES:(g + 1) * SUBLANES, :] for g in range(n_groups)]
        t1 = _top16_desc(split(s1))
        t2 = _top16_desc(split(s2))
        t1_hi, t2_lo, t2_hi = _by_sublane(t1, SUBLANES), _by_sublane(t2, 0), _by_sublane(t2, SUBLANES)
        cand = _peer_candidates(t1, t1_hi, t2_lo, t2_hi, t2[0], jnp.add)
        pad = [jnp.full((SUBLANES, tt), -jnp.inf, F32)] * (PEER_TOPK - len(cand))
        tau8 = _top16_desc(cand + pad)[PEER_TOPK - 1]

        m1, m2 = t1[0], t2[0]
        c1 = [jnp.exp(x - m1) for x in t1[:SUBLANES]]
        ec = _peer_candidates(c1, jnp.exp(t1_hi - m1), jnp.exp(t2_lo - m2), jnp.exp(t2_hi - m2),
                              jnp.ones((SUBLANES, tt), F32), jnp.multiply)
        zsum = jnp.zeros((SUBLANES, tt), F32)
        for cg, eg in zip(cand, ec):
            zsum = zsum + jnp.where(cg >= tau8, eg, 0.0)
        zden = jnp.sum(zsum, axis=0, keepdims=True)
        tau = tau8[0:1]
        th = jnp.full(s1.shape, jnp.inf, F32)
        for b in range(PEER_TOPK):
            t2b = t2[b][0:1]
            th = jnp.minimum(th, jnp.where(s1 + t2b >= tau, t2b, jnp.inf))
        th_ref[h] = th
        s2_ref[h] = s2
        e1_ref[h] = jnp.exp(s1 - m1[0:1])
        e2n_ref[h] = jnp.exp(s2 - m2[0:1]) / zden
        return carry

    lax.fori_loop(0, PEER_HEADS, head, 0)


def _peer_scores(q, k1, k2):
    s = q.shape[0]
    tt = min(SCORE_TILE, s)
    tab = pl.BlockSpec((PEER_HEADS, PEER_N_KEYS, tt), lambda i: (0, 0, i))
    tab_shape = jax.ShapeDtypeStruct((PEER_HEADS, PEER_N_KEYS, s), F32)
    return pl.pallas_call(
        _peer_scores_body,
        grid=(s // tt,),
        in_specs=[pl.BlockSpec((tt, PEER_HEADS * PEER_KEY_DIM), lambda i: (i, 0)),
                  pl.BlockSpec(k1.shape, lambda i: (0, 0)), pl.BlockSpec(k2.shape, lambda i: (0, 0))],
        out_specs=[tab, tab, tab, tab],
        out_shape=[tab_shape, tab_shape, tab_shape, tab_shape],
        compiler_params=_cparams(("parallel",)),
        name="peer_scores",
    )(q, k1, k2)


def _peer_items(s, n_items, nb):
    split = lambda it: (it // nb, it % nb)
    return (split(jnp.minimum(s, n_items - 1)), split(jnp.clip(s - 1, 0, n_items - 1)),
            split(jnp.clip(s - 2, 0, n_items - 1)))


def _peer_dense_body(ht_ref, u_ref, vt_ref, th_ref, e1_ref, s2_ref, e2n_ref, o_ref, a0, a1, w0, w1,
                     *, n_items, nb):
    s = pl.program_id(0)
    tt = ht_ref.shape[1]
    eb = u_ref.shape[0]
    d = vt_ref.shape[0]
    n_sub = eb // PEER_N_KEYS
    jrows = 2 * SUBLANES

    @pl.when(s == 0)
    def _():
        for ref in (a0, a1, w0, w1):
            ref[...] = jnp.zeros_like(ref)

    _, _, (_, block_c) = _peer_items(s, n_items, nb)

    @pl.when(block_c == 0)
    def _():
        o_ref[...] = jnp.zeros_like(o_ref)

    def run(a_new, a_old, w_new, w_old):
        def stage_a(k):
            rows = pl.ds(pl.multiple_of(k * PEER_CHUNK, PEER_CHUNK), PEER_CHUNK)
            a_t = jnp.dot(u_ref[rows, :], ht_ref[...], preferred_element_type=F32)
            a_new[rows, :] = 0.5 * a_t * (1.0 + lax.erf(a_t * (2.0 ** -0.5)))

        def stage_b(c, jc):
            cols = pl.ds(pl.multiple_of(c * LANES, LANES), LANES)
            jr = slice(jc * jrows, (jc + 1) * jrows)
            g = [jnp.zeros((jrows, LANES), F32) for _ in range(n_sub)]
            for h in range(PEER_HEADS):
                s2c = s2_ref[h, jr, cols]
                e2c = e2n_ref[h, jr, cols]
                for il in range(n_sub):
                    gate = e2c * e1_ref[h, il:il + 1, cols]
                    g[il] = g[il] + jnp.where(s2c >= th_ref[h, il:il + 1, cols], gate, 0.0)
            for il in range(n_sub):
                rows = slice(il * PEER_N_KEYS + jc * jrows, il * PEER_N_KEYS + (jc + 1) * jrows)
                w_new[rows, cols] = (a_old[rows, cols] * g[il]).astype(w_new.dtype)

        def stage_c(n):
            rows = pl.ds(pl.multiple_of(n * PEER_CHUNK, PEER_CHUNK), PEER_CHUNK)
            o_ref[rows, :] += jnp.dot(vt_ref[rows, :], w_old[...], preferred_element_type=F32)

        n_trips = tt // LANES
        assert eb // PEER_CHUNK == n_trips and d // PEER_CHUNK == 2 * n_trips

        def trip(k, carry):
            stage_a(k)
            stage_c(2 * k)
            for jc in range(PEER_N_KEYS // jrows):
                stage_b(k, jc)
            stage_c(2 * k + 1)
            return carry

        lax.fori_loop(0, n_trips, trip, 0)

    parity = lax.rem(s, 2)

    @pl.when(parity == 0)
    def _():
        run(a0, a1, w1, w0)

    @pl.when(parity == 1)
    def _():
        run(a1, a0, w0, w1)


def _peer_dense(hn_t, u_all, vt_all, layer, th, e1, s2, e2n):
    d, s = hn_t.shape
    tt = min(PEER_TOKENS, s)
    eb = PEER_EXPERTS
    nb = u_all.shape[1] // eb
    n_items = (s // tt) * nb
    items = functools.partial(_peer_items, n_items=n_items, nb=nb)
    sub = pl.BlockSpec((PEER_HEADS, eb // PEER_N_KEYS, tt), lambda i: (0, items(i)[1][1], items(i)[1][0]))
    tab = pl.BlockSpec((PEER_HEADS, PEER_N_KEYS, tt), lambda i: (0, 0, items(i)[1][0]))
    return pl.pallas_call(
        functools.partial(_peer_dense_body, n_items=n_items, nb=nb),
        grid=(n_items + 2,),
        in_specs=[pl.BlockSpec((d, tt), lambda i: (0, items(i)[0][0])),
                  pl.BlockSpec((None, eb, d), lambda i: (layer, items(i)[0][1], 0)),
                  pl.BlockSpec((None, d, eb), lambda i: (layer, 0, items(i)[2][1])),
                  sub, sub, tab, tab],
        out_specs=pl.BlockSpec((d, tt), lambda i: (0, items(i)[2][0])),
        out_shape=jax.ShapeDtypeStruct((d, s), F32),
        scratch_shapes=[pltpu.VMEM((eb, tt), F32), pltpu.VMEM((eb, tt), F32),
                        pltpu.VMEM((eb, tt), BF16), pltpu.VMEM((eb, tt), BF16)],
        compiler_params=_cparams(("arbitrary",)),
        name="peer_dense",
    )(hn_t, u_all, vt_all, th, e1, s2, e2n)


def _add_norm_body(x_ref, dt_ref, g_ref, *out_refs):
    x = x_ref[...] + dt_ref[...].T
    y = x * lax.rsqrt(jnp.mean(x * x, axis=-1, keepdims=True) + NORM_EPS)
    out_refs[-1][...] = (y * g_ref[...]).astype(out_refs[-1].dtype)
    if len(out_refs) == 2:
        out_refs[0][...] = x


def _add_norm(x, delta_t, g, norm_dtype, with_sum):
    s, d = x.shape
    tm = min(ROW_TILE, s)
    row = pl.BlockSpec((tm, d), lambda i: (i, 0))
    out_specs = [row, row] if with_sum else [row]
    out_shape = [jax.ShapeDtypeStruct((s, d), norm_dtype)]
    if with_sum:
        out_shape.insert(0, jax.ShapeDtypeStruct((s, d), F32))
    return pl.pallas_call(
        _add_norm_body,
        grid=(s // tm,),
        in_specs=[row, pl.BlockSpec((d, tm), lambda i: (0, i)), pl.BlockSpec((1, d), lambda i: (0, 0))],
        out_specs=out_specs,
        out_shape=out_shape,
        compiler_params=_cparams(("parallel",)),
        name="add_norm",
    )(x, delta_t, g.reshape(1, d))


def _rope_tables(positions, dim, lanes_x1):
    half = dim // 2
    inv_freq = ROPE_THETA ** (-jnp.arange(0, dim, 2, dtype=F32) / dim)
    ang = positions.astype(F32)[:, None] * inv_freq
    cos, sin = jnp.cos(ang), jnp.sin(ang)
    s = positions.shape[0]
    c_parts, su_parts, sd_parts = [], [], []
    pos = 0
    for off in lanes_x1:
        gap = off - pos
        c_parts += [jnp.ones((s, gap), F32), cos, cos]
        su_parts += [jnp.zeros((s, gap), F32), -sin, jnp.zeros((s, half), F32)]
        sd_parts += [jnp.zeros((s, gap + half), F32), sin]
        pos = off + dim
    tail = LANES - pos
    c_parts.append(jnp.ones((s, tail), F32))
    su_parts.append(jnp.zeros((s, tail), F32))
    sd_parts.append(jnp.zeros((s, tail), F32))
    return tuple(jnp.concatenate(p, axis=1) for p in (c_parts, su_parts, sd_parts))


W_IN_HEAD = IN_OFFSETS[3]
W_IN_RUN = sum(IN_SIZES[3:])
LAYOUT_ROWS = 1024


def _layout_w_in_body(a_ref, b_ref, o_ref):
    j = pl.program_id(0)
    rb = o_ref.shape[0]
    lead = rb - W_IN_HEAD % rb

    @pl.when(j < W_IN_RUN // rb)
    def _():
        o_ref[0:lead, :] = a_ref[W_IN_HEAD % rb:rb, :].astype(o_ref.dtype)
        o_ref[lead:rb, :] = b_ref[0:rb - lead, :].astype(o_ref.dtype)

    @pl.when(j == W_IN_RUN // rb)
    def _():
        head = MLA_Q_LORA + MLA_KV_LORA
        o_ref[0:head, :] = a_ref[0:head, :].astype(o_ref.dtype)
        o_ref[head:rb, :] = jnp.zeros((rb - head, o_ref.shape[1]), o_ref.dtype)
        o_ref[head + MLA_NOPE:head + MLA_NOPE + MLA_ROPE, :] = a_ref[head:head + MLA_ROPE, :].astype(o_ref.dtype)


def _layout_w_in(w_in_t, layer):
    _, n, d = w_in_t.shape
    rb = LAYOUT_ROWS
    assert W_IN_RUN % rb == 0 and Z_CQ == W_IN_RUN and Z_COLS == W_IN_RUN + rb and W_IN_HEAD < rb
    run_blocks = W_IN_RUN // rb
    first = lambda j: jnp.where(j < run_blocks, j + W_IN_HEAD // rb, 0)
    second = lambda j: jnp.minimum(j + W_IN_HEAD // rb + 1, (n - 1) // rb)
    return pl.pallas_call(
        _layout_w_in_body,
        grid=(Z_COLS // rb,),
        in_specs=[pl.BlockSpec((None, rb, d), lambda j: (layer, first(j), 0)),
                  pl.BlockSpec((None, rb, d), lambda j: (layer, second(j), 0))],
        out_specs=pl.BlockSpec((rb, d), lambda j: (j, 0)),
        out_shape=jax.ShapeDtypeStruct((Z_COLS, d), BF16),
        compiler_params=_cparams(("parallel",)),
        name="layout_w_in",
    )(w_in_t, w_in_t)


def _layout_mla_weights(w_uq, w_ukv):
    lq = w_uq.shape[0]
    wq = w_uq.reshape(lq, MLA_HEADS, MLA_NOPE + MLA_ROPE)
    wq = jnp.pad(wq, ((0, 0), (0, 0), (0, LANES - MLA_NOPE - MLA_ROPE))).reshape(lq, MLA_HEADS * LANES)
    lk = w_ukv.shape[0]
    wkv = w_ukv.reshape(lk, MLA_HEADS, MLA_NOPE + MLA_V)
    wk = jnp.pad(wkv[:, :, :MLA_NOPE], ((0, 0), (0, 0), (0, LANES - MLA_NOPE))).reshape(lk, MLA_HEADS * LANES)
    wv = wkv[:, :, MLA_NOPE:].reshape(lk, MLA_HEADS * MLA_V)
    return wq.T.astype(BF16), wk.astype(BF16), wv.T.astype(BF16)


def kernel(x, positions, mix_norm_g, w_in, gate_b, mla_q_norm_g, mla_w_uq, mla_kv_norm_g, mla_w_ukv, mla_w_out,
           sc_conv_w, sc_w_out, diff_lambda, diff_norm_g, diff_w_out, conf_dw_w, conf_dw_b, conf_ln_g, conf_ln_b,
           conf_w_out, w_o, ffn_norm_g, peer_w_q, peer_sub_keys, peer_u, peer_v, final_norm_g):
    b, s, d = x.shape
    assert b == 1 and d == D_MODEL
    xs = x.reshape(s, d)
    pos = positions.reshape(s)
    mla_tabs = _rope_tables(pos, MLA_ROPE, (MLA_NOPE,))
    mla_tabs_t = tuple(t.T for t in mla_tabs)
    w_in_t = jnp.swapaxes(w_in, 1, 2)
    u_all = peer_u.astype(BF16)
    vt_all = jnp.swapaxes(peer_v, 1, 2).astype(BF16)
    diff_tabs = _rope_tables(pos, DIFF_ROT, (0, DIFF_HEAD_DIM))

    for i in range(DEPTH):
        lam_init = 0.8 - 0.6 * math.exp(-0.3 * i)
        if i == 0:
            hn = _rmsnorm(xs, mix_norm_g[i], BF16)
        else:
            xs, hn = _add_norm(xs, delta_t, mix_norm_g[i], BF16, with_sum=True)
        z = _matmul(hn, _layout_w_in(w_in_t, i), BF16, b_transposed=True, name="in_proj")
        wqt, wk, wvt = _layout_mla_weights(mla_w_uq[i], mla_w_ukv[i])
        qt, k, vt = _mla_prep(z, mla_tabs, mla_tabs_t, mla_q_norm_g[i].reshape(1, -1),
                              mla_kv_norm_g[i].reshape(1, -1), wqt, wk, wvt)
        o_mla = _mla_attn(qt, k, vt)
        o_sc, o_conf = _conv_branches(z, sc_conv_w[i], conf_dw_w[i], conf_dw_b[i], conf_ln_g[i], conf_ln_b[i])
        qdt, kd, vdt = _diff_prep(z, diff_tabs)
        o_diff = _diff_attn(qdt, kd, vdt, diff_lambda[i], diff_norm_g[i], lam_init)
        merged = _merge((o_mla, o_sc, o_diff, o_conf),
                        tuple(w.astype(BF16) for w in (mla_w_out[i], sc_w_out[i], diff_w_out[i], conf_w_out[i])),
                        z, gate_b[i])
        xs = _matmul(merged, w_o[i].astype(BF16), F32, residual=xs, name="out_proj")
        hf, hf_t = _rmsnorm(xs, ffn_norm_g[i], BF16, with_transposed=True)
        pq = _matmul(hf, peer_w_q[i].astype(BF16), F32, name="peer_q")
        zk = jnp.zeros((PEER_N_KEYS, PEER_HALF), F32)
        k1 = jnp.concatenate([peer_sub_keys[i, 0], zk], axis=1)
        k2 = jnp.concatenate([zk, peer_sub_keys[i, 1]], axis=1)
        th, s2, e1, e2n = _peer_scores(pq, k1, k2)
        delta_t = _peer_dense(hf_t, u_all, vt_all, i, th, e1, s2, e2n)
    (out,) = _add_norm(xs, delta_t, final_norm_g, F32, with_sum=False)
    return out.reshape(b, s, d)
```

```python
import functools
import math

import jax
import jax.numpy as jnp
import numpy as np
from jax import lax
from jax.experimental import pallas as pl
from jax.experimental.pallas import tpu as pltpu

F32 = jnp.float32
BF16 = jnp.bfloat16

D_MODEL = 2048
DEPTH = 2
CHUNK = 64
ROPE_THETA = 500000.0
NORM_EPS = 1e-6
LN_EPS = 1e-5
NEG_INF = -1e30
N_BRANCH = 4

MLA_HEADS = 8
MLA_Q_LORA = 512
MLA_KV_LORA = 256
MLA_NOPE = 64
MLA_ROPE = 32
MLA_V = 64
MLA_SCALE = (MLA_NOPE + MLA_ROPE) ** -0.5

SC_WIDTH = 512
SC_KERNEL = 3

DIFF_HEADS = 4
DIFF_HEAD_DIM = 64
DIFF_ROT = DIFF_HEAD_DIM // 4
DIFF_SCALE = DIFF_HEAD_DIM ** -0.5
DIFF_WIDTH = DIFF_HEADS * 2 * DIFF_HEAD_DIM

CONF_WIDTH = 512
CONF_KERNEL = 31

PEER_HEADS = 8
PEER_N_KEYS = 128
PEER_N_EXPERTS = PEER_N_KEYS * PEER_N_KEYS
PEER_KEY_DIM = 128
PEER_HALF = PEER_KEY_DIM // 2
PEER_TOPK = 16

IN_SIZES = (MLA_Q_LORA, MLA_KV_LORA, MLA_ROPE, 3 * SC_WIDTH, DIFF_WIDTH, DIFF_WIDTH, DIFF_WIDTH,
            2 * CONF_WIDTH, N_BRANCH * D_MODEL)
IN_OFFSETS = tuple(int(v) for v in np.cumsum((0,) + IN_SIZES)[:-1])

LOG2E = math.log2(math.e)

LANES = 128
SUBLANES = 8
HALO = 32
VMEM_LIMIT = 56 * 1024 * 1024

ROW_TILE = 512
MM_TILE = 1024
ATTN_TILE = 512
SCORE_TILE = 512
PEER_TOKENS = 512
PEER_EXPERTS = 1024
PEER_CHUNK = 256
PEER_TRIPS = 1

Z_SC = 0
Z_DQ = 1536
Z_DK = 2048
Z_DV = 2560
Z_CONF = 3072
Z_GATE = 4096
Z_CQ = 12288
Z_CKV = 12800
Z_KR = 13056
Z_COLS = 13312


def _cparams(sem):
    return pltpu.CompilerParams(dimension_semantics=sem, vmem_limit_bytes=VMEM_LIMIT)


def _rmsnorm_body(x_ref, g_ref, o_ref, *maybe_ot_ref):
    x = x_ref[...].astype(F32)
    y = x * lax.rsqrt(jnp.mean(x * x, axis=-1, keepdims=True) + NORM_EPS) * g_ref[...]
    o_ref[...] = y.astype(o_ref.dtype)
    for ot_ref in maybe_ot_ref:
        ot_ref[...] = y.T.astype(ot_ref.dtype)


def _rmsnorm(x, g, out_dtype, with_transposed=False):
    s, d = x.shape
    tm = min(ROW_TILE, s)
    out_specs = [pl.BlockSpec((tm, d), lambda i: (i, 0))]
    out_shape = [jax.ShapeDtypeStruct((s, d), out_dtype)]
    if with_transposed:
        out_specs.append(pl.BlockSpec((d, tm), lambda i: (0, i)))
        out_shape.append(jax.ShapeDtypeStruct((d, s), out_dtype))
    out = pl.pallas_call(
        _rmsnorm_body,
        grid=(s // tm,),
        in_specs=[pl.BlockSpec((tm, d), lambda i: (i, 0)), pl.BlockSpec((1, d), lambda i: (0, 0))],
        out_specs=out_specs,
        out_shape=out_shape,
        compiler_params=_cparams(("parallel",)),
        name="rmsnorm",
    )(x, g.reshape(1, d))
    return out if with_transposed else out[0]


def _matmul_body(a_ref, b_ref, o_ref, *, rhs_contract):
    dims = (((1,), (rhs_contract,)), ((), ()))
    o_ref[...] = lax.dot_general(a_ref[...], b_ref[...], dims, preferred_element_type=F32).astype(o_ref.dtype)


def _matmul_res_body(a_ref, b_ref, r_ref, o_ref, *, rhs_contract):
    dims = (((1,), (rhs_contract,)), ((), ()))
    acc = lax.dot_general(a_ref[...], b_ref[...], dims, preferred_element_type=F32)
    o_ref[...] = (r_ref[...] + acc).astype(o_ref.dtype)


def _matmul(a, b, out_dtype, residual=None, b_transposed=False, name="matmul"):
    m, k = a.shape
    n = b.shape[0] if b_transposed else b.shape[1]
    tm, tn = min(MM_TILE, m), min(MM_TILE, n)
    b_spec = (pl.BlockSpec((tn, k), lambda i, j: (j, 0)) if b_transposed
              else pl.BlockSpec((k, tn), lambda i, j: (0, j)))
    in_specs = [pl.BlockSpec((tm, k), lambda i, j: (i, 0)), b_spec]
    args = [a, b]
    body = _matmul_body
    if residual is not None:
        in_specs.append(pl.BlockSpec((tm, tn), lambda i, j: (i, j)))
        args.append(residual)
        body = _matmul_res_body
    body = functools.partial(body, rhs_contract=1 if b_transposed else 0)
    return pl.pallas_call(
        body,
        grid=(m // tm, n // tn),
        in_specs=in_specs,
        out_specs=pl.BlockSpec((tm, tn), lambda i, j: (i, j)),
        out_shape=jax.ShapeDtypeStruct((m, n), out_dtype),
        compiler_params=_cparams(("parallel", "arbitrary")),
        name=name,
    )(*args)


def _sigmoid(x):
    return 0.5 + 0.5 * jnp.tanh(0.5 * x)


def _rope128(x, c, s_up, s_dn, half):
    return x * c + pltpu.roll(x, LANES - half, 1) * s_up + pltpu.roll(x, half, 1) * s_dn


def _mla_prep_body(cq_ref, ckv_ref, kr_ref, c_ref, su_ref, sd_ref, ct_ref, sut_ref, sdt_ref,
                   qg_ref, kvg_ref, wqt_ref, wk_ref, wvt_ref, qt_ref, k_ref, vt_ref):
    half = MLA_ROPE // 2
    nt = (((1,), (1,)), ((), ()))

    def norm(x_ref, g_ref):
        x = x_ref[...].astype(F32)
        y = x * lax.rsqrt(jnp.mean(x * x, axis=-1, keepdims=True) + NORM_EPS)
        return (y * g_ref[...]).astype(BF16)

    qf_t = lax.dot_general(wqt_ref[...], norm(cq_ref, qg_ref), nt, preferred_element_type=F32)
    ckvn = norm(ckv_ref, kvg_ref)
    kf = jnp.dot(ckvn, wk_ref[...], preferred_element_type=F32)
    vt_ref[...] = lax.dot_general(wvt_ref[...], ckvn, nt, preferred_element_type=F32).astype(vt_ref.dtype)
    kr = _rope128(kr_ref[...].astype(F32), c_ref[...], su_ref[...], sd_ref[...], half)
    ct, sut, sdt = ct_ref[...], sut_ref[...], sdt_ref[...]
    for h in range(MLA_HEADS):
        sl = slice(h * LANES, (h + 1) * LANES)
        x = qf_t[sl, :]
        q = x * ct + pltpu.roll(x, LANES - half, 0) * sut + pltpu.roll(x, half, 0) * sdt
        qt_ref[sl, :] = (q * (MLA_SCALE * LOG2E)).astype(qt_ref.dtype)
        k_ref[:, sl] = (kf[:, sl] + kr).astype(k_ref.dtype)


def _mla_prep(z, tabs, tabs_t, qg, kvg, wqt, wk, wvt):
    s = z.shape[0]
    tm = min(ROW_TILE, s)
    row = lambda w, idx: pl.BlockSpec((tm, w), lambda i: (i, idx))
    full = lambda a: pl.BlockSpec(a.shape, lambda i: (0,) * a.ndim)
    tab = pl.BlockSpec((tm, LANES), lambda i: (i, 0))
    tab_t = pl.BlockSpec((LANES, tm), lambda i: (0, i))
    hw = MLA_HEADS * LANES
    vw = MLA_HEADS * MLA_V
    return pl.pallas_call(
        _mla_prep_body,
        grid=(s // tm,),
        in_specs=[row(MLA_Q_LORA, Z_CQ // MLA_Q_LORA), row(MLA_KV_LORA, Z_CKV // MLA_KV_LORA),
                  row(LANES, Z_KR // LANES), tab, tab, tab, tab_t, tab_t, tab_t,
                  full(qg), full(kvg), full(wqt), full(wk), full(wvt)],
        out_specs=[pl.BlockSpec((hw, tm), lambda i: (0, i)), pl.BlockSpec((tm, hw), lambda i: (i, 0)),
                   pl.BlockSpec((vw, tm), lambda i: (0, i))],
        out_shape=[jax.ShapeDtypeStruct((hw, s), BF16), jax.ShapeDtypeStruct((s, hw), BF16),
                   jax.ShapeDtypeStruct((vw, s), BF16)],
        compiler_params=_cparams(("parallel",)),
        name="mla_prep",
    )(z, z, z, *tabs, *tabs_t, qg, kvg, wqt, wk, wvt)


def _flash_streams_t(q_ts, k_ats, vt_ats, m_ref, l_ref, acc_ref, s_a, s_b, n_full, t):
    streams = range(len(q_ts))
    for n in streams:
        m_ref[n] = jnp.full((1, t), NEG_INF, F32)
        l_ref[n] = jnp.zeros((1, t), F32)
        acc_ref[n] = jnp.zeros(acc_ref.shape[1:], F32)

    def scores_into(s_ref, j):
        for n in streams:
            s_ref[n] = jnp.dot(k_ats[n](j), q_ts[n], preferred_element_type=F32)

    def consume(s_ref, j, masked):
        probs, alphas = [], []
        for n in streams:
            s = s_ref[n]
            if masked:
                key_chunk = lax.broadcasted_iota(jnp.int32, (t, t), 0) // CHUNK
                qry_chunk = lax.broadcasted_iota(jnp.int32, (t, t), 1) // CHUNK
                s = jnp.where(key_chunk <= qry_chunk, s, NEG_INF)
            m_prev = m_ref[n]
            m_new = jnp.maximum(m_prev, jnp.max(s, axis=0, keepdims=True))
            alpha = jnp.exp2(m_prev - m_new)
            p = jnp.exp2(s - m_new)
            l_ref[n] = alpha * l_ref[n] + jnp.sum(p, axis=0, keepdims=True)
            m_ref[n] = m_new
            probs.append(p.astype(BF16))
            alphas.append(alpha)
        for n in streams:
            acc_ref[n] = alphas[n] * acc_ref[n] + jnp.dot(vt_ats[n](j), probs[n], preferred_element_type=F32)

    scores_into(s_a, 0)

    def pair(jj, carry):
        j = 2 * jj
        scores_into(s_b, j + 1)
        consume(s_a, j, False)
        scores_into(s_a, j + 2)
        consume(s_b, j + 1, False)
        return carry

    lax.fori_loop(0, n_full // 2, pair, 0)

    @pl.when(n_full % 2 == 0)
    def _():
        consume(s_a, n_full, True)

    @pl.when(n_full % 2 == 1)
    def _():
        scores_into(s_b, n_full)
        consume(s_a, n_full - 1, False)
        consume(s_b, n_full, True)


def _mla_attn_body(qt_ref, k_ref, vt_ref, o_ref, m_ref, l_ref, acc_ref, s_a, s_b, *, t):
    i = pl.program_id(1)
    rows = lambda j: pl.ds(pl.multiple_of(j * t, t), t)
    lanes = [slice(hh * LANES, (hh + 1) * LANES) for hh in range(2)]
    k_ats = [lambda j, sl=sl: k_ref[rows(j), sl] for sl in lanes]
    vt_ats = [lambda j, hh=hh: vt_ref[hh * MLA_V:(hh + 1) * MLA_V, rows(j)] for hh in range(2)]
    _flash_streams_t([qt_ref[sl, :] for sl in lanes], k_ats, vt_ats, m_ref, l_ref, acc_ref, s_a, s_b, i, t)
    o_t = jnp.concatenate([acc_ref[0] / l_ref[0], acc_ref[1] / l_ref[1]], axis=0)
    o_ref[...] = o_t.T.astype(o_ref.dtype)


def _mla_attn(qt, k, vt):
    s = k.shape[0]
    t = min(ATTN_TILE, s)
    pairs = MLA_HEADS // 2
    return pl.pallas_call(
        functools.partial(_mla_attn_body, t=t),
        grid=(pairs, s // t),
        in_specs=[pl.BlockSpec((2 * LANES, t), lambda p, i: (p, i)),
                  pl.BlockSpec((s, 2 * LANES), lambda p, i: (0, p)),
                  pl.BlockSpec((2 * MLA_V, s), lambda p, i: (p, 0))],
        out_specs=pl.BlockSpec((t, 2 * MLA_V), lambda p, i: (i, p)),
        out_shape=jax.ShapeDtypeStruct((s, MLA_HEADS * MLA_V), BF16),
        scratch_shapes=[pltpu.VMEM((2, 1, t), F32), pltpu.VMEM((2, 1, t), F32),
                        pltpu.VMEM((2, MLA_V, t), F32),
                        pltpu.VMEM((2, t, t), F32), pltpu.VMEM((2, t, t), F32)],
        compiler_params=_cparams(("parallel", "arbitrary")),
        name="mla_attn",
    )(qt, k, vt)


def _diff_prep_body(dq_ref, dk_ref, dv_ref, c_ref, su_ref, sd_ref, qt_ref, k_ref, vt_ref):
    half = DIFF_ROT // 2
    c, su, sd = c_ref[...], su_ref[...], sd_ref[...]
    tm = dq_ref.shape[0]
    lane = lax.broadcasted_iota(jnp.int32, (tm, LANES), 1)
    vt_ref[...] = dv_ref[...].astype(F32).T.astype(vt_ref.dtype)
    for h in range(DIFF_HEADS):
        sl = slice(h * LANES, (h + 1) * LANES)
        q = _rope128(dq_ref[:, sl].astype(F32), c, su, sd, half) * (DIFF_SCALE * LOG2E)
        qt_ref[(2 * h) * LANES:(2 * h + 1) * LANES, :] = jnp.where(lane < DIFF_HEAD_DIM, q, 0.0).T.astype(qt_ref.dtype)
        qt_ref[(2 * h + 1) * LANES:(2 * h + 2) * LANES, :] = jnp.where(lane >= DIFF_HEAD_DIM, q, 0.0).T.astype(qt_ref.dtype)
        k_ref[:, sl] = _rope128(dk_ref[:, sl].astype(F32), c, su, sd, half).astype(k_ref.dtype)


def _diff_prep(z, tabs):
    s = z.shape[0]
    tm = min(ROW_TILE, s)
    tab = pl.BlockSpec((tm, LANES), lambda i: (i, 0))
    col = lambda start: pl.BlockSpec((tm, DIFF_WIDTH), lambda i: (i, start // DIFF_WIDTH))
    return pl.pallas_call(
        _diff_prep_body,
        grid=(s // tm,),
        in_specs=[col(Z_DQ), col(Z_DK), col(Z_DV), tab, tab, tab],
        out_specs=[pl.BlockSpec((2 * DIFF_WIDTH, tm), lambda i: (0, i)),
                   pl.BlockSpec((tm, DIFF_WIDTH), lambda i: (i, 0)),
                   pl.BlockSpec((DIFF_WIDTH, tm), lambda i: (0, i))],
        out_shape=[jax.ShapeDtypeStruct((2 * DIFF_WIDTH, s), BF16), jax.ShapeDtypeStruct((s, DIFF_WIDTH), BF16),
                   jax.ShapeDtypeStruct((DIFF_WIDTH, s), BF16)],
        compiler_params=_cparams(("parallel",)),
        name="diff_prep",
    )(z, z, z, *tabs)


def _diff_attn_body(qt_ref, k_ref, vt_ref, lam_ref, g_ref, o_ref, m_ref, l_ref, acc_ref, s_a, s_b, *, t, lam_init):
    i = pl.program_id(1)
    rows = lambda j: pl.ds(pl.multiple_of(j * t, t), t)
    k_at = lambda j: k_ref[rows(j), :]
    vt_at = lambda j: vt_ref[:, rows(j)]
    q_ts = [qt_ref[c * LANES:(c + 1) * LANES, :] for c in range(2)]
    _flash_streams_t(q_ts, [k_at, k_at], [vt_at, vt_at], m_ref, l_ref, acc_ref, s_a, s_b, i, t)
    lv = lam_ref[...]
    lam = (jnp.exp(jnp.sum(lv[0:1] * lv[1:2], axis=1, keepdims=True))
           - jnp.exp(jnp.sum(lv[2:3] * lv[3:4], axis=1, keepdims=True)) + lam_init)
    o = (acc_ref[0] / l_ref[0] - lam * (acc_ref[1] / l_ref[1])).T
    o = o * lax.rsqrt(jnp.mean(o * o, axis=-1, keepdims=True) + NORM_EPS)
    o_ref[...] = (o * g_ref[...] * (1.0 - lam_init)).astype(o_ref.dtype)


def _diff_attn(qt, k, vt, lam_vecs, norm_g, lam_init):
    s = k.shape[0]
    t = min(ATTN_TILE, s)
    return pl.pallas_call(
        functools.partial(_diff_attn_body, t=t, lam_init=lam_init),
        grid=(DIFF_HEADS, s // t),
        in_specs=[pl.BlockSpec((2 * LANES, t), lambda h, i: (h, i)),
                  pl.BlockSpec((s, LANES), lambda h, i: (0, h)),
                  pl.BlockSpec((LANES, s), lambda h, i: (h, 0)),
                  pl.BlockSpec(lam_vecs.shape, lambda h, i: (0, 0)),
                  pl.BlockSpec((1, LANES), lambda h, i: (0, 0))],
        out_specs=pl.BlockSpec((t, LANES), lambda h, i: (i, h)),
        out_shape=jax.ShapeDtypeStruct((s, DIFF_WIDTH), BF16),
        scratch_shapes=[pltpu.VMEM((2, 1, t), F32), pltpu.VMEM((2, 1, t), F32),
                        pltpu.VMEM((2, LANES, t), F32),
                        pltpu.VMEM((2, t, t), F32), pltpu.VMEM((2, t, t), F32)],
        compiler_params=_cparams(("parallel", "arbitrary")),
        name="diff_attn",
    )(qt, k, vt, lam_vecs, norm_g.reshape(1, LANES))


def _conv_body(bg_ref, cg_ref, xv_ref, cgh_ref, xvh_ref, a_ref, gt_ref, ah_ref, gth_ref,
               scw_ref, dww_ref, dwb_ref, lng_ref, lnb_ref, sc_ref, cf_ref, ext_ref):
    tm = bg_ref.shape[0]
    first = pl.program_id(0) == 0

    shifted_rows = HALO + tm - SUBLANES

    def fill(cur, halo, taps):
        ext_ref[0, 0:HALO, :] = jnp.where(first, 0.0, halo)
        ext_ref[0, HALO:HALO + tm, :] = cur
        for r in sorted({(HALO - (taps - 1) + j) % SUBLANES for j in range(taps)} - {0}):
            ext_ref[r, 0:shifted_rows, :] = ext_ref[0, r:r + shifted_rows, :]

    def conv(w_ref, taps):
        acc = jnp.zeros((tm, ext_ref.shape[2]), F32)
        for j in range(taps):
            off = HALO - (taps - 1) + j
            r = off % SUBLANES
            acc = acc + w_ref[j:j + 1, :] * ext_ref[r, off - r:off - r + tm, :]
        return acc

    fill(cg_ref[...].astype(F32) * xv_ref[...].astype(F32), cgh_ref[...].astype(F32) * xvh_ref[...].astype(F32),
         SC_KERNEL)
    sc_ref[...] = (bg_ref[...].astype(F32) * conv(scw_ref, SC_KERNEL)).astype(sc_ref.dtype)

    fill(a_ref[...].astype(F32) * _sigmoid(gt_ref[...].astype(F32)),
         ah_ref[...].astype(F32) * _sigmoid(gth_ref[...].astype(F32)), CONF_KERNEL)
    u = conv(dww_ref, CONF_KERNEL) + dwb_ref[...]
    mu = jnp.mean(u, axis=-1, keepdims=True)
    var = jnp.mean(jnp.square(u - mu), axis=-1, keepdims=True)
    y = (u - mu) * lax.rsqrt(var + LN_EPS) * lng_ref[...] + lnb_ref[...]
    cf_ref[...] = (y * _sigmoid(y)).astype(cf_ref.dtype)


def _conv_branches(z, sc_w, dw_w, dw_b, ln_g, ln_b):
    s = z.shape[0]
    tm = min(ROW_TILE, s)
    w = SC_WIDTH
    cur = lambda col: pl.BlockSpec((tm, w), lambda i: (i, col // w))
    halo = lambda col: pl.BlockSpec((HALO, w), lambda i: (jnp.maximum(i * (tm // HALO) - 1, 0), col // w))
    full = lambda a: pl.BlockSpec(a.shape, lambda i: (0,) * a.ndim)
    vec = lambda a: a.reshape(1, -1)
    args = [sc_w, dw_w, vec(dw_b), vec(ln_g), vec(ln_b)]
    return pl.pallas_call(
        _conv_body,
        grid=(s // tm,),
        in_specs=[cur(Z_SC), cur(Z_SC + w), cur(Z_SC + 2 * w), halo(Z_SC + w), halo(Z_SC + 2 * w),
                  cur(Z_CONF), cur(Z_CONF + w), halo(Z_CONF), halo(Z_CONF + w)] + [full(a) for a in args],
        out_specs=[pl.BlockSpec((tm, w), lambda i: (i, 0)), pl.BlockSpec((tm, w), lambda i: (i, 0))],
        out_shape=[jax.ShapeDtypeStruct((s, w), BF16), jax.ShapeDtypeStruct((s, w), BF16)],
        scratch_shapes=[pltpu.VMEM((SUBLANES, HALO + tm, w), F32)],
        compiler_params=_cparams(("parallel",)),
        name="conv_branches",
    )(*([z] * 9), *args)


def _merge_body(b0, b1, b2, b3, w0, w1, w2, w3, g0, g1, g2, g3, gb_ref, o_ref):
    acc = None
    for n, (b_ref, w_ref, g_ref) in enumerate(((b0, w0, g0), (b1, w1, g1), (b2, w2, g2), (b3, w3, g3))):
        y = jnp.dot(b_ref[...], w_ref[...], preferred_element_type=F32)
        t = _sigmoid(g_ref[...].astype(F32) + gb_ref[n:n + 1, :]) * y
        acc = t if acc is None else acc + t
    o_ref[...] = acc.astype(o_ref.dtype)


def _merge(branches, w_outs, z, gate_b):
    s = z.shape[0]
    tm = min(MM_TILE, s)
    tn = ROW_TILE
    kw = branches[0].shape[1]
    return pl.pallas_call(
        _merge_body,
        grid=(s // tm, D_MODEL // tn),
        in_specs=([pl.BlockSpec((tm, kw), lambda i, j: (i, 0))] * 4
                  + [pl.BlockSpec((kw, tn), lambda i, j: (0, j))] * 4
                  + [pl.BlockSpec((tm, tn), lambda i, j, b=b: (i, (Z_GATE + b * D_MODEL) // tn + j))
                     for b in range(N_BRANCH)]
                  + [pl.BlockSpec((N_BRANCH, tn), lambda i, j: (0, j))]),
        out_specs=pl.BlockSpec((tm, tn), lambda i, j: (i, j)),
        out_shape=jax.ShapeDtypeStruct((s, D_MODEL), BF16),
        compiler_params=_cparams(("parallel", "arbitrary")),
        name="merge",
    )(*branches, *w_outs, z, z, z, z, gate_b)


def _order(v, i, j):
    v[i], v[j] = jnp.maximum(v[i], v[j]), jnp.minimum(v[i], v[j])


def _bitonic_sort_desc(v):
    n = len(v)
    k = 2
    while k <= n:
        j = k // 2
        while j >= 1:
            for i in range(n):
                partner = i ^ j
                if partner > i:
                    if i & k == 0:
                        _order(v, i, partner)
                    else:
                        _order(v, partner, i)
            j //= 2
        k *= 2


def _bitonic_merge_desc(v):
    n = len(v)
    j = n // 2
    while j >= 1:
        for i in range(n):
            if i ^ j > i:
                _order(v, i, i ^ j)
        j //= 2


def _top16_desc(groups):
    v = list(groups)
    _bitonic_sort_desc(v)
    for shift in (SUBLANES // 2, SUBLANES // 4, SUBLANES // 8):
        other = [pltpu.roll(x, shift, 0) for x in v]
        v = [jnp.maximum(v[k], other[PEER_TOPK - 1 - k]) for k in range(PEER_TOPK)]
        _bitonic_merge_desc(v)
    return v


def _by_sublane(v, start):
    sub = lax.broadcasted_iota(jnp.int32, v[0].shape, 0)
    out = v[start]
    for r in range(1, SUBLANES):
        out = jnp.where(sub == r, v[start + r], out)
    return out


def _peer_candidates(a, a_hi, b_lo, b_hi, b0, op):
    groups = [op(a[0], b_lo), op(a[0], b_hi)]
    groups += [op(a[i], b_lo) for i in range(1, SUBLANES)]
    groups.append(op(a_hi, b0))
    return groups


def _peer_scores_body(q_ref, k1_ref, k2_ref, th_ref, s2_ref, e1_ref, e2n_ref):
    tt = q_ref.shape[0]
    n_groups = PEER_N_KEYS // SUBLANES

    def head(h, carry):
        qh = q_ref[:, pl.ds(pl.multiple_of(h * PEER_KEY_DIM, PEER_KEY_DIM), PEER_KEY_DIM)]
        dims = (((1,), (1,)), ((), ()))
        s1 = lax.dot_general(k1_ref[...], qh, dims, precision=lax.Precision.HIGHEST, preferred_element_type=F32)
        s2 = lax.dot_general(k2_ref[...], qh, dims, precision=lax.Precision.HIGHEST, preferred_element_type=F32)
        split = lambda s: [s[g * SUBLANES:(g + 1) * SUBLANES, :] for g in range(n_groups)]
        t1 = _top16_desc(split(s1))
        t2 = _top16_desc(split(s2))
        t1_hi, t2_lo, t2_hi = _by_sublane(t1, SUBLANES), _by_sublane(t2, 0), _by_sublane(t2, SUBLANES)
        cand = _peer_candidates(t1, t1_hi, t2_lo, t2_hi, t2[0], jnp.add)
        pad = [jnp.full((SUBLANES, tt), -jnp.inf, F32)] * (PEER_TOPK - len(cand))
        tau8 = _top16_desc(cand + pad)[PEER_TOPK - 1]

        m1, m2 = t1[0], t2[0]
        c1 = [jnp.exp(x - m1) for x in t1[:SUBLANES]]
        ec = _peer_candidates(c1, jnp.exp(t1_hi - m1), jnp.exp(t2_lo - m2), jnp.exp(t2_hi - m2),
                              jnp.ones((SUBLANES, tt), F32), jnp.multiply)
        zsum = jnp.zeros((SUBLANES, tt), F32)
        for cg, eg in zip(cand, ec):
            zsum = zsum + jnp.where(cg >= tau8, eg, 0.0)
        zden = jnp.sum(zsum, axis=0, keepdims=True)
        tau = tau8[0:1]
        th = jnp.full(s1.shape, jnp.inf, F32)
        for b in range(PEER_TOPK):
            t2b = t2[b][0:1]
            th = jnp.where(s1 + t2b >= tau, t2b, th)
        th_ref[h] = th
        s2_ref[h] = s2
        e1_ref[h] = jnp.exp(s1 - m1[0:1])
        e2n_ref[h] = jnp.exp(s2 - m2[0:1]) / zden
        return carry

    lax.fori_loop(0, PEER_HEADS, head, 0)


def _peer_scores(q, k1, k2):
    s = q.shape[0]
    tt = min(SCORE_TILE, s)
    tab = pl.BlockSpec((PEER_HEADS, PEER_N_KEYS, tt), lambda i: (0, 0, i))
    tab_shape = jax.ShapeDtypeStruct((PEER_HEADS, PEER_N_KEYS, s), F32)
    return pl.pallas_call(
        _peer_scores_body,
        grid=(s // tt,),
        in_specs=[pl.BlockSpec((tt, PEER_HEADS * PEER_KEY_DIM), lambda i: (i, 0)),
                  pl.BlockSpec(k1.shape, lambda i: (0, 0)), pl.BlockSpec(k2.shape, lambda i: (0, 0))],
        out_specs=[tab, tab, tab, tab],
        out_shape=[tab_shape, tab_shape, tab_shape, tab_shape],
        compiler_params=_cparams(("parallel",)),
        name="peer_scores",
    )(q, k1, k2)


def _peer_items(s, n_items, nb):
    split = lambda it: (it // nb, it % nb)
    return (split(jnp.minimum(s, n_items - 1)), split(jnp.clip(s - 1, 0, n_items - 1)),
            split(jnp.clip(s - 2, 0, n_items - 1)))


def _peer_dense_body(ht_ref, u_ref, vt_ref, th_ref, e1_ref, s2_ref, e2n_ref, o_ref, a0, a1, w0, w1,
                     *, n_items, nb):
    s = pl.program_id(0)
    tt = ht_ref.shape[1]
    eb = u_ref.shape[0]
    d = vt_ref.shape[0]
    n_sub = eb // PEER_N_KEYS
    jrows = 2 * SUBLANES
    a_rows = eb // PEER_TRIPS

    @pl.when(s == 0)
    def _():
        for ref in (a0, a1, w0, w1):
            ref[...] = jnp.zeros_like(ref)

    _, _, (_, block_c) = _peer_items(s, n_items, nb)

    @pl.when(block_c == 0)
    def _():
        o_ref[...] = jnp.zeros_like(o_ref)

    def run(a_new, a_old, w_new, w_old):
        def stage_a(k):
            rows = pl.ds(pl.multiple_of(k * a_rows, a_rows), a_rows)
            a_t = jnp.dot(u_ref[rows, :], ht_ref[...], preferred_element_type=F32)
            a_new[rows, :] = 0.5 * a_t * (1.0 + lax.erf(a_t * (2.0 ** -0.5)))

        def stage_b(c, jc):
            cols = pl.ds(pl.multiple_of(c * LANES, LANES), LANES)
            jr = slice(jc * jrows, (jc + 1) * jrows)
            g = [jnp.zeros((jrows, LANES), F32) for _ in range(n_sub)]
            for h in range(PEER_HEADS):
                s2c = s2_ref[h, jr, cols]
                e2c = e2n_ref[h, jr, cols]
                for il in range(n_sub):
                    gate = e2c * e1_ref[h, il:il + 1, cols]
                    g[il] = g[il] + jnp.where(s2c >= th_ref[h, il:il + 1, cols], gate, 0.0)
            for il in range(n_sub):
                rows = slice(il * PEER_N_KEYS + jc * jrows, il * PEER_N_KEYS + (jc + 1) * jrows)
                w_new[rows, cols] = (a_old[rows, cols] * g[il]).astype(w_new.dtype)

        def stage_c(n):
            rows = pl.ds(pl.multiple_of(n * PEER_CHUNK, PEER_CHUNK), PEER_CHUNK)
            o_ref[rows, :] += jnp.dot(vt_ref[rows, :], w_old[...], preferred_element_type=F32)

        n_c = d // PEER_CHUNK // PEER_TRIPS
        n_col = tt // LANES // PEER_TRIPS
        assert n_c * PEER_TRIPS * PEER_CHUNK == d and n_col * PEER_TRIPS * LANES == tt and n_c % 2 == 0

        def trip(k, carry):
            stage_a(k)
            for n in range(n_c // 2):
                stage_c(n_c * k + n)
            for c in range(n_col):
                for jc in range(PEER_N_KEYS // jrows):
                    stage_b(n_col * k + c, jc)
            for n in range(n_c // 2, n_c):
                stage_c(n_c * k + n)
            return carry

        lax.fori_loop(0, PEER_TRIPS, trip, 0)

    parity = lax.rem(s, 2)

    @pl.when(parity == 0)
    def _():
        run(a0, a1, w1, w0)

    @pl.when(parity == 1)
    def _():
        run(a1, a0, w0, w1)


def _peer_dense(hn_t, u_all, vt_all, layer, th, e1, s2, e2n):
    d, s = hn_t.shape
    tt = min(PEER_TOKENS, s)
    eb = PEER_EXPERTS
    nb = u_all.shape[1] // eb
    n_items = (s // tt) * nb
    items = functools.partial(_peer_items, n_items=n_items, nb=nb)
    sub = pl.BlockSpec((PEER_HEADS, eb // PEER_N_KEYS, tt), lambda i: (0, items(i)[1][1], items(i)[1][0]))
    tab = pl.BlockSpec((PEER_HEADS, PEER_N_KEYS, tt), lambda i: (0, 0, items(i)[1][0]))
    return pl.pallas_call(
        functools.partial(_peer_dense_body, n_items=n_items, nb=nb),
        grid=(n_items + 2,),
        in_specs=[pl.BlockSpec((d, tt), lambda i: (0, items(i)[0][0])),
                  pl.BlockSpec((None, eb, d), lambda i: (layer, items(i)[0][1], 0)),
                  pl.BlockSpec((None, d, eb), lambda i: (layer, 0, items(i)[2][1])),
                  sub, sub, tab, tab],
        out_specs=pl.BlockSpec((d, tt), lambda i: (0, items(i)[2][0])),
        out_shape=jax.ShapeDtypeStruct((d, s), F32),
        scratch_shapes=[pltpu.VMEM((eb, tt), F32), pltpu.VMEM((eb, tt), F32),
                        pltpu.VMEM((eb, tt), BF16), pltpu.VMEM((eb, tt), BF16)],
        compiler_params=_cparams(("arbitrary",)),
        name="peer_dense",
    )(hn_t, u_all, vt_all, th, e1, s2, e2n)


def _add_norm_body(x_ref, dt_ref, g_ref, *out_refs):
    x = x_ref[...] + dt_ref[...].T
    y = x * lax.rsqrt(jnp.mean(x * x, axis=-1, keepdims=True) + NORM_EPS)
    out_refs[-1][...] = (y * g_ref[...]).astype(out_refs[-1].dtype)
    if len(out_refs) == 2:
        out_refs[0][...] = x


def _add_norm(x, delta_t, g, norm_dtype, with_sum):
    s, d = x.shape
    tm = min(ROW_TILE, s)
    row = pl.BlockSpec((tm, d), lambda i: (i, 0))
    out_specs = [row, row] if with_sum else [row]
    out_shape = [jax.ShapeDtypeStruct((s, d), norm_dtype)]
    if with_sum:
        out_shape.insert(0, jax.ShapeDtypeStruct((s, d), F32))
    return pl.pallas_call(
        _add_norm_body,
        grid=(s // tm,),
        in_specs=[row, pl.BlockSpec((d, tm), lambda i: (0, i)), pl.BlockSpec((1, d), lambda i: (0, 0))],
        out_specs=out_specs,
        out_shape=out_shape,
        compiler_params=_cparams(("parallel",)),
        name="add_norm",
    )(x, delta_t, g.reshape(1, d))


def _rope_tables(positions, dim, lanes_x1):
    half = dim // 2
    inv_freq = ROPE_THETA ** (-jnp.arange(0, dim, 2, dtype=F32) / dim)
    ang = positions.astype(F32)[:, None] * inv_freq
    cos, sin = jnp.cos(ang), jnp.sin(ang)
    s = positions.shape[0]
    c_parts, su_parts, sd_parts = [], [], []
    pos = 0
    for off in lanes_x1:
        gap = off - pos
        c_parts += [jnp.ones((s, gap), F32), cos, cos]
        su_parts += [jnp.zeros((s, gap), F32), -sin, jnp.zeros((s, half), F32)]
        sd_parts += [jnp.zeros((s, gap + half), F32), sin]
        pos = off + dim
    tail = LANES - pos
    c_parts.append(jnp.ones((s, tail), F32))
    su_parts.append(jnp.zeros((s, tail), F32))
    sd_parts.append(jnp.zeros((s, tail), F32))
    return tuple(jnp.concatenate(p, axis=1) for p in (c_parts, su_parts, sd_parts))


W_IN_HEAD = IN_OFFSETS[3]
W_IN_RUN = sum(IN_SIZES[3:])
LAYOUT_ROWS = 1024


def _layout_w_in_body(a_ref, b_ref, o_ref):
    j = pl.program_id(0)
    rb = o_ref.shape[0]
    lead = rb - W_IN_HEAD % rb

    @pl.when(j < W_IN_RUN // rb)
    def _():
        o_ref[0:lead, :] = a_ref[W_IN_HEAD % rb:rb, :].astype(o_ref.dtype)
        o_ref[lead:rb, :] = b_ref[0:rb - lead, :].astype(o_ref.dtype)

    @pl.when(j == W_IN_RUN // rb)
    def _():
        head = MLA_Q_LORA + MLA_KV_LORA
        o_ref[0:head, :] = a_ref[0:head, :].astype(o_ref.dtype)
        o_ref[head:rb, :] = jnp.zeros((rb - head, o_ref.shape[1]), o_ref.dtype)
        o_ref[head + MLA_NOPE:head + MLA_NOPE + MLA_ROPE, :] = a_ref[head:head + MLA_ROPE, :].astype(o_ref.dtype)


def _layout_w_in(w_in_t, layer):
    _, n, d = w_in_t.shape
    rb = LAYOUT_ROWS
    assert W_IN_RUN % rb == 0 and Z_CQ == W_IN_RUN and Z_COLS == W_IN_RUN + rb and W_IN_HEAD < rb
    run_blocks = W_IN_RUN // rb
    first = lambda j: jnp.where(j < run_blocks, j + W_IN_HEAD // rb, 0)
    second = lambda j: jnp.minimum(j + W_IN_HEAD // rb + 1, (n - 1) // rb)
    return pl.pallas_call(
        _layout_w_in_body,
        grid=(Z_COLS // rb,),
        in_specs=[pl.BlockSpec((None, rb, d), lambda j: (layer, first(j), 0)),
                  pl.BlockSpec((None, rb, d), lambda j: (layer, second(j), 0))],
        out_specs=pl.BlockSpec((rb, d), lambda j: (j, 0)),
        out_shape=jax.ShapeDtypeStruct((Z_COLS, d), BF16),
        compiler_params=_cparams(("parallel",)),
        name="layout_w_in",
    )(w_in_t, w_in_t)


def _layout_mla_weights(w_uq, w_ukv):
    lq = w_uq.shape[0]
    wq = w_uq.reshape(lq, MLA_HEADS, MLA_NOPE + MLA_ROPE)
    wq = jnp.pad(wq, ((0, 0), (0, 0), (0, LANES - MLA_NOPE - MLA_ROPE))).reshape(lq, MLA_HEADS * LANES)
    lk = w_ukv.shape[0]
    wkv = w_ukv.reshape(lk, MLA_HEADS, MLA_NOPE + MLA_V)
    wk = jnp.pad(wkv[:, :, :MLA_NOPE], ((0, 0), (0, 0), (0, LANES - MLA_NOPE))).reshape(lk, MLA_HEADS * LANES)
    wv = wkv[:, :, MLA_NOPE:].reshape(lk, MLA_HEADS * MLA_V)
    return wq.T.astype(BF16), wk.astype(BF16), wv.T.astype(BF16)


def kernel(x, positions, mix_norm_g, w_in, gate_b, mla_q_norm_g, mla_w_uq, mla_kv_norm_g, mla_w_ukv, mla_w_out,
           sc_conv_w, sc_w_out, diff_lambda, diff_norm_g, diff_w_out, conf_dw_w, conf_dw_b, conf_ln_g, conf_ln_b,
           conf_w_out, w_o, ffn_norm_g, peer_w_q, peer_sub_keys, peer_u, peer_v, final_norm_g):
    b, s, d = x.shape
    assert b == 1 and d == D_MODEL
    xs = x.reshape(s, d)
    pos = positions.reshape(s)
    mla_tabs = _rope_tables(pos, MLA_ROPE, (MLA_NOPE,))
    mla_tabs_t = tuple(t.T for t in mla_tabs)
    w_in_t = jnp.swapaxes(w_in, 1, 2)
    u_all = peer_u.astype(BF16)
    vt_all = jnp.swapaxes(peer_v, 1, 2).astype(BF16)
    diff_tabs = _rope_tables(pos, DIFF_ROT, (0, DIFF_HEAD_DIM))

    for i in range(DEPTH):
        lam_init = 0.8 - 0.6 * math.exp(-0.3 * i)
        if i == 0:
            hn = _rmsnorm(xs, mix_norm_g[i], BF16)
        else:
            xs, hn = _add_norm(xs, delta_t, mix_norm_g[i], BF16, with_sum=True)
        z = _matmul(hn, _layout_w_in(w_in_t, i), BF16, b_transposed=True, name="in_proj")
        wqt, wk, wvt = _layout_mla_weights(mla_w_uq[i], mla_w_ukv[i])
        qt, k, vt = _mla_prep(z, mla_tabs, mla_tabs_t, mla_q_norm_g[i].reshape(1, -1),
                              mla_kv_norm_g[i].reshape(1, -1), wqt, wk, wvt)
        o_mla = _mla_attn(qt, k, vt)
        o_sc, o_conf = _conv_branches(z, sc_conv_w[i], conf_dw_w[i], conf_dw_b[i], conf_ln_g[i], conf_ln_b[i])
        qdt, kd, vdt = _diff_prep(z, diff_tabs)
        o_diff = _diff_attn(qdt, kd, vdt, diff_lambda[i], diff_norm_g[i], lam_init)
        merged = _merge((o_mla, o_sc, o_diff, o_conf),
                        tuple(w.astype(BF16) for w in (mla_w_out[i], sc_w_out[i], diff_w_out[i], conf_w_out[i])),
                        z, gate_b[i])
        xs = _matmul(merged, w_o[i].astype(BF16), F32, residual=xs, name="out_proj")
        hf, hf_t = _rmsnorm(xs, ffn_norm_g[i], BF16, with_transposed=True)
        pq = _matmul(hf, peer_w_q[i].astype(BF16), F32, name="peer_q")
        zk = jnp.zeros((PEER_N_KEYS, PEER_HALF), F32)
        k1 = jnp.concatenate([peer_sub_keys[i, 0], zk], axis=1)
        k2 = jnp.concatenate([zk, peer_sub_keys[i, 1]], axis=1)
        th, s2, e1, e2n = _peer_scores(pq, k1, k2)
        delta_t = _peer_dense(hf_t, u_all, vt_all, i, th, e1, s2, e2n)
    (out,) = _add_norm(xs, delta_t, final_norm_g, F32, with_sum=False)
    return out.reshape(b, s, d)
```

```python
import functools
import math

import jax
import jax.numpy as jnp
import numpy as np
from jax import lax
from jax.experimental import pallas as pl
from jax.experimental.pallas import tpu as pltpu

F32 = jnp.float32
BF16 = jnp.bfloat16

D_MODEL = 2048
DEPTH = 2
CHUNK = 64
ROPE_THETA = 500000.0
NORM_EPS = 1e-6
LN_EPS = 1e-5
NEG_INF = -1e30
N_BRANCH = 4

MLA_HEADS = 8
MLA_Q_LORA = 512
MLA_KV_LORA = 256
MLA_NOPE = 64
MLA_ROPE = 32
MLA_V = 64
MLA_SCALE = (MLA_NOPE + MLA_ROPE) ** -0.5

SC_WIDTH = 512
SC_KERNEL = 3

DIFF_HEADS = 4
DIFF_HEAD_DIM = 64
DIFF_ROT = DIFF_HEAD_DIM // 4
DIFF_SCALE = DIFF_HEAD_DIM ** -0.5
DIFF_WIDTH = DIFF_HEADS * 2 * DIFF_HEAD_DIM

CONF_WIDTH = 512
CONF_KERNEL = 31

PEER_HEADS = 8
PEER_N_KEYS = 128
PEER_N_EXPERTS = PEER_N_KEYS * PEER_N_KEYS
PEER_KEY_DIM = 128
PEER_HALF = PEER_KEY_DIM // 2
PEER_TOPK = 16

IN_SIZES = (MLA_Q_LORA, MLA_KV_LORA, MLA_ROPE, 3 * SC_WIDTH, DIFF_WIDTH, DIFF_WIDTH, DIFF_WIDTH,
            2 * CONF_WIDTH, N_BRANCH * D_MODEL)
IN_OFFSETS = tuple(int(v) for v in np.cumsum((0,) + IN_SIZES)[:-1])

LOG2E = math.log2(math.e)

LANES = 128
SUBLANES = 8
HALO = 32
VMEM_LIMIT = 56 * 1024 * 1024

ROW_TILE = 512
MM_TILE = 1024
ATTN_TILE = 512
SCORE_TILE = 512
PEER_TOKENS = 512
PEER_EXPERTS = 1024
PEER_CHUNK = 256
PEER_TRIPS = 1

Z_SC = 0
Z_DQ = 1536
Z_DK = 2048
Z_DV = 2560
Z_CONF = 3072
Z_GATE = 4096
Z_CQ = 12288
Z_CKV = 12800
Z_KR = 13056
Z_COLS = 13312


def _cparams(sem):
    return pltpu.CompilerParams(dimension_semantics=sem, vmem_limit_bytes=VMEM_LIMIT)


def _rmsnorm_body(x_ref, g_ref, o_ref, *maybe_ot_ref):
    x = x_ref[...].astype(F32)
    y = x * lax.rsqrt(jnp.mean(x * x, axis=-1, keepdims=True) + NORM_EPS) * g_ref[...]
    o_ref[...] = y.astype(o_ref.dtype)
    for ot_ref in maybe_ot_ref:
        ot_ref[...] = y.T.astype(ot_ref.dtype)


def _rmsnorm(x, g, out_dtype, with_transposed=False):
    s, d = x.shape
    tm = min(ROW_TILE, s)
    out_specs = [pl.BlockSpec((tm, d), lambda i: (i, 0))]
    out_shape = [jax.ShapeDtypeStruct((s, d), out_dtype)]
    if with_transposed:
        out_specs.append(pl.BlockSpec((d, tm), lambda i: (0, i)))
        out_shape.append(jax.ShapeDtypeStruct((d, s), out_dtype))
    out = pl.pallas_call(
        _rmsnorm_body,
        grid=(s // tm,),
        in_specs=[pl.BlockSpec((tm, d), lambda i: (i, 0)), pl.BlockSpec((1, d), lambda i: (0, 0))],
        out_specs=out_specs,
        out_shape=out_shape,
        compiler_params=_cparams(("parallel",)),
        name="rmsnorm",
    )(x, g.reshape(1, d))
    return out if with_transposed else out[0]


def _matmul_body(a_ref, b_ref, o_ref):
    o_ref[...] = jnp.dot(a_ref[...], b_ref[...], preferred_element_type=F32).astype(o_ref.dtype)


def _matmul_res_body(a_ref, b_ref, r_ref, o_ref):
    o_ref[...] = (r_ref[...] + jnp.dot(a_ref[...], b_ref[...], preferred_element_type=F32)).astype(o_ref.dtype)


def _matmul(a, b, out_dtype, residual=None, name="matmul"):
    m, k = a.shape
    n = b.shape[1]
    tm, tn = min(MM_TILE, m), min(MM_TILE, n)
    in_specs = [pl.BlockSpec((tm, k), lambda i, j: (i, 0)), pl.BlockSpec((k, tn), lambda i, j: (0, j))]
    args = [a, b]
    body = _matmul_body
    if residual is not None:
        in_specs.append(pl.BlockSpec((tm, tn), lambda i, j: (i, j)))
        args.append(residual)
        body = _matmul_res_body
    return pl.pallas_call(
        body,
        grid=(m // tm, n // tn),
        in_specs=in_specs,
        out_specs=pl.BlockSpec((tm, tn), lambda i, j: (i, j)),
        out_shape=jax.ShapeDtypeStruct((m, n), out_dtype),
        compiler_params=_cparams(("parallel", "arbitrary")),
        name=name,
    )(*args)


def _sigmoid(x):
    return 0.5 + 0.5 * jnp.tanh(0.5 * x)


def _rope128(x, c, s_up, s_dn, half):
    return x * c + pltpu.roll(x, LANES - half, 1) * s_up + pltpu.roll(x, half, 1) * s_dn


def _mla_prep_body(cq_ref, ckv_ref, kr_ref, c_ref, su_ref, sd_ref, ct_ref, sut_ref, sdt_ref,
                   qg_ref, kvg_ref, wqt_ref, wk_ref, wvt_ref, qt_ref, k_ref, vt_ref):
    half = MLA_ROPE // 2
    nt = (((1,), (1,)), ((), ()))

    def norm(x_ref, g_ref):
        x = x_ref[...].astype(F32)
        y = x * lax.rsqrt(jnp.mean(x * x, axis=-1, keepdims=True) + NORM_EPS)
        return (y * g_ref[...]).astype(BF16)

    qf_t = lax.dot_general(wqt_ref[...], norm(cq_ref, qg_ref), nt, preferred_element_type=F32)
    ckvn = norm(ckv_ref, kvg_ref)
    kf = jnp.dot(ckvn, wk_ref[...], preferred_element_type=F32)
    vt_ref[...] = lax.dot_general(wvt_ref[...], ckvn, nt, preferred_element_type=F32).astype(vt_ref.dtype)
    kr = _rope128(kr_ref[...].astype(F32), c_ref[...], su_ref[...], sd_ref[...], half)
    ct, sut, sdt = ct_ref[...], sut_ref[...], sdt_ref[...]
    for h in range(MLA_HEADS):
        sl = slice(h * LANES, (h + 1) * LANES)
        x = qf_t[sl, :]
        q = x * ct + pltpu.roll(x, LANES - half, 0) * sut + pltpu.roll(x, half, 0) * sdt
        qt_ref[sl, :] = (q * (MLA_SCALE * LOG2E)).astype(qt_ref.dtype)
        k_ref[:, sl] = (kf[:, sl] + kr).astype(k_ref.dtype)


def _mla_prep(z, tabs, tabs_t, qg, kvg, wqt, wk, wvt):
    s = z.shape[0]
    tm = min(ROW_TILE, s)
    row = lambda w, idx: pl.BlockSpec((tm, w), lambda i: (i, idx))
    full = lambda a: pl.BlockSpec(a.shape, lambda i: (0,) * a.ndim)
    tab = pl.BlockSpec((tm, LANES), lambda i: (i, 0))
    tab_t = pl.BlockSpec((LANES, tm), lambda i: (0, i))
    hw = MLA_HEADS * LANES
    vw = MLA_HEADS * MLA_V
    return pl.pallas_call(
        _mla_prep_body,
        grid=(s // tm,),
        in_specs=[row(MLA_Q_LORA, Z_CQ // MLA_Q_LORA), row(MLA_KV_LORA, Z_CKV // MLA_KV_LORA),
                  row(LANES, Z_KR // LANES), tab, tab, tab, tab_t, tab_t, tab_t,
                  full(qg), full(kvg), full(wqt), full(wk), full(wvt)],
        out_specs=[pl.BlockSpec((hw, tm), lambda i: (0, i)), pl.BlockSpec((tm, hw), lambda i: (i, 0)),
                   pl.BlockSpec((vw, tm), lambda i: (0, i))],
        out_shape=[jax.ShapeDtypeStruct((hw, s), BF16), jax.ShapeDtypeStruct((s, hw), BF16),
                   jax.ShapeDtypeStruct((vw, s), BF16)],
        compiler_params=_cparams(("parallel",)),
        name="mla_prep",
    )(z, z, z, *tabs, *tabs_t, qg, kvg, wqt, wk, wvt)


def _flash_streams_t(q_ts, k_ats, vt_ats, m_ref, l_ref, acc_ref, s_a, s_b, n_full, t):
    streams = range(len(q_ts))
    for n in streams:
        m_ref[n] = jnp.full((1, t), NEG_INF, F32)
        l_ref[n] = jnp.zeros((1, t), F32)
        acc_ref[n] = jnp.zeros(acc_ref.shape[1:], F32)

    def scores_into(s_ref, j):
        for n in streams:
            s_ref[n] = jnp.dot(k_ats[n](j), q_ts[n], preferred_element_type=F32)

    def consume(s_ref, j, masked):
        probs, alphas = [], []
        for n in streams:
            s = s_ref[n]
            if masked:
                key_chunk = lax.broadcasted_iota(jnp.int32, (t, t), 0) // CHUNK
                qry_chunk = lax.broadcasted_iota(jnp.int32, (t, t), 1) // CHUNK
                s = jnp.where(key_chunk <= qry_chunk, s, NEG_INF)
            m_prev = m_ref[n]
            m_new = jnp.maximum(m_prev, jnp.max(s, axis=0, keepdims=True))
            alpha = jnp.exp2(m_prev - m_new)
            p = jnp.exp2(s - m_new)
            l_ref[n] = alpha * l_ref[n] + jnp.sum(p, axis=0, keepdims=True)
            m_ref[n] = m_new
            probs.append(p.astype(BF16))
            alphas.append(alpha)
        for n in streams:
            acc_ref[n] = alphas[n] * acc_ref[n] + jnp.dot(vt_ats[n](j), probs[n], preferred_element_type=F32)

    scores_into(s_a, 0)

    def pair(jj, carry):
        j = 2 * jj
        scores_into(s_b, j + 1)
        consume(s_a, j, False)
        scores_into(s_a, j + 2)
        consume(s_b, j + 1, False)
        return carry

    lax.fori_loop(0, n_full // 2, pair, 0)

    @pl.when(n_full % 2 == 0)
    def _():
        consume(s_a, n_full, True)

    @pl.when(n_full % 2 == 1)
    def _():
        scores_into(s_b, n_full)
        consume(s_a, n_full - 1, False)
        consume(s_b, n_full, True)


def _mla_attn_body(qt_ref, k_ref, vt_ref, o_ref, m_ref, l_ref, acc_ref, s_a, s_b, *, t):
    i = pl.program_id(1)
    rows = lambda j: pl.ds(pl.multiple_of(j * t, t), t)
    lanes = [slice(hh * LANES, (hh + 1) * LANES) for hh in range(2)]
    k_ats = [lambda j, sl=sl: k_ref[rows(j), sl] for sl in lanes]
    vt_ats = [lambda j, hh=hh: vt_ref[hh * MLA_V:(hh + 1) * MLA_V, rows(j)] for hh in range(2)]
    _flash_streams_t([qt_ref[sl, :] for sl in lanes], k_ats, vt_ats, m_ref, l_ref, acc_ref, s_a, s_b, i, t)
    o_t = jnp.concatenate([acc_ref[0] / l_ref[0], acc_ref[1] / l_ref[1]], axis=0)
    o_ref[...] = o_t.T.astype(o_ref.dtype)


def _mla_attn(qt, k, vt):
    s = k.shape[0]
    t = min(ATTN_TILE, s)
    pairs = MLA_HEADS // 2
    return pl.pallas_call(
        functools.partial(_mla_attn_body, t=t),
        grid=(pairs, s // t),
        in_specs=[pl.BlockSpec((2 * LANES, t), lambda p, i: (p, i)),
                  pl.BlockSpec((s, 2 * LANES), lambda p, i: (0, p)),
                  pl.BlockSpec((2 * MLA_V, s), lambda p, i: (p, 0))],
        out_specs=pl.BlockSpec((t, 2 * MLA_V), lambda p, i: (i, p)),
        out_shape=jax.ShapeDtypeStruct((s, MLA_HEADS * MLA_V), BF16),
        scratch_shapes=[pltpu.VMEM((2, 1, t), F32), pltpu.VMEM((2, 1, t), F32),
                        pltpu.VMEM((2, MLA_V, t), F32),
                        pltpu.VMEM((2, t, t), F32), pltpu.VMEM((2, t, t), F32)],
        compiler_params=_cparams(("parallel", "arbitrary")),
        name="mla_attn",
    )(qt, k, vt)


def _diff_prep_body(dq_ref, dk_ref, dv_ref, c_ref, su_ref, sd_ref, qt_ref, k_ref, vt_ref):
    half = DIFF_ROT // 2
    c, su, sd = c_ref[...], su_ref[...], sd_ref[...]
    tm = dq_ref.shape[0]
    lane = lax.broadcasted_iota(jnp.int32, (tm, LANES), 1)
    vt_ref[...] = dv_ref[...].astype(F32).T.astype(vt_ref.dtype)
    for h in range(DIFF_HEADS):
        sl = slice(h * LANES, (h + 1) * LANES)
        q = _rope128(dq_ref[:, sl].astype(F32), c, su, sd, half) * (DIFF_SCALE * LOG2E)
        qt_ref[(2 * h) * LANES:(2 * h + 1) * LANES, :] = jnp.where(lane < DIFF_HEAD_DIM, q, 0.0).T.astype(qt_ref.dtype)
        qt_ref[(2 * h + 1) * LANES:(2 * h + 2) * LANES, :] = jnp.where(lane >= DIFF_HEAD_DIM, q, 0.0).T.astype(qt_ref.dtype)
        k_ref[:, sl] = _rope128(dk_ref[:, sl].astype(F32), c, su, sd, half).astype(k_ref.dtype)


def _diff_prep(z, tabs):
    s = z.shape[0]
    tm = min(ROW_TILE, s)
    tab = pl.BlockSpec((tm, LANES), lambda i: (i, 0))
    col = lambda start: pl.BlockSpec((tm, DIFF_WIDTH), lambda i: (i, start // DIFF_WIDTH))
    return pl.pallas_call(
        _diff_prep_body,
        grid=(s // tm,),
        in_specs=[col(Z_DQ), col(Z_DK), col(Z_DV), tab, tab, tab],
        out_specs=[pl.BlockSpec((2 * DIFF_WIDTH, tm), lambda i: (0, i)),
                   pl.BlockSpec((tm, DIFF_WIDTH), lambda i: (i, 0)),
                   pl.BlockSpec((DIFF_WIDTH, tm), lambda i: (0, i))],
        out_shape=[jax.ShapeDtypeStruct((2 * DIFF_WIDTH, s), BF16), jax.ShapeDtypeStruct((s, DIFF_WIDTH), BF16),
                   jax.ShapeDtypeStruct((DIFF_WIDTH, s), BF16)],
        compiler_params=_cparams(("parallel",)),
        name="diff_prep",
    )(z, z, z, *tabs)


def _diff_attn_body(qt_ref, k_ref, vt_ref, lam_ref, g_ref, o_ref, m_ref, l_ref, acc_ref, s_a, s_b, *, t, lam_init):
    i = pl.program_id(1)
    rows = lambda j: pl.ds(pl.multiple_of(j * t, t), t)
    k_at = lambda j: k_ref[rows(j), :]
    vt_at = lambda j: vt_ref[:, rows(j)]
    q_ts = [qt_ref[c * LANES:(c + 1) * LANES, :] for c in range(2)]
    _flash_streams_t(q_ts, [k_at, k_at], [vt_at, vt_at], m_ref, l_ref, acc_ref, s_a, s_b, i, t)
    lv = lam_ref[...]
    lam = (jnp.exp(jnp.sum(lv[0:1] * lv[1:2], axis=1, keepdims=True))
           - jnp.exp(jnp.sum(lv[2:3] * lv[3:4], axis=1, keepdims=True)) + lam_init)
    o = (acc_ref[0] / l_ref[0] - lam * (acc_ref[1] / l_ref[1])).T
    o = o * lax.rsqrt(jnp.mean(o * o, axis=-1, keepdims=True) + NORM_EPS)
    o_ref[...] = (o * g_ref[...] * (1.0 - lam_init)).astype(o_ref.dtype)


def _diff_attn(qt, k, vt, lam_vecs, norm_g, lam_init):
    s = k.shape[0]
    t = min(ATTN_TILE, s)
    return pl.pallas_call(
        functools.partial(_diff_attn_body, t=t, lam_init=lam_init),
        grid=(DIFF_HEADS, s // t),
        in_specs=[pl.BlockSpec((2 * LANES, t), lambda h, i: (h, i)),
                  pl.BlockSpec((s, LANES), lambda h, i: (0, h)),
                  pl.BlockSpec((LANES, s), lambda h, i: (h, 0)),
                  pl.BlockSpec(lam_vecs.shape, lambda h, i: (0, 0)),
                  pl.BlockSpec((1, LANES), lambda h, i: (0, 0))],
        out_specs=pl.BlockSpec((t, LANES), lambda h, i: (i, h)),
        out_shape=jax.ShapeDtypeStruct((s, DIFF_WIDTH), BF16),
        scratch_shapes=[pltpu.VMEM((2, 1, t), F32), pltpu.VMEM((2, 1, t), F32),
                        pltpu.VMEM((2, LANES, t), F32),
                        pltpu.VMEM((2, t, t), F32), pltpu.VMEM((2, t, t), F32)],
        compiler_params=_cparams(("parallel", "arbitrary")),
        name="diff_attn",
    )(qt, k, vt, lam_vecs, norm_g.reshape(1, LANES))


def _conv_body(bg_ref, cg_ref, xv_ref, cgh_ref, xvh_ref, a_ref, gt_ref, ah_ref, gth_ref,
               scw_ref, dww_ref, dwb_ref, lng_ref, lnb_ref, sc_ref, cf_ref, ext_ref):
    tm = bg_ref.shape[0]
    first = pl.program_id(0) == 0

    shifted_rows = HALO + tm - SUBLANES

    def fill(cur, halo, taps):
        ext_ref[0, 0:HALO, :] = jnp.where(first, 0.0, halo)
        ext_ref[0, HALO:HALO + tm, :] = cur
        for r in sorted({(HALO - (taps - 1) + j) % SUBLANES for j in range(taps)} - {0}):
            ext_ref[r, 0:shifted_rows, :] = ext_ref[0, r:r + shifted_rows, :]

    def conv(w_ref, taps):
        acc = jnp.zeros((tm, ext_ref.shape[2]), F32)
        for j in range(taps):
            off = HALO - (taps - 1) + j
            r = off % SUBLANES
            acc = acc + w_ref[j:j + 1, :] * ext_ref[r, off - r:off - r + tm, :]
        return acc

    fill(cg_ref[...].astype(F32) * xv_ref[...].astype(F32), cgh_ref[...].astype(F32) * xvh_ref[...].astype(F32),
         SC_KERNEL)
    sc_ref[...] = (bg_ref[...].astype(F32) * conv(scw_ref, SC_KERNEL)).astype(sc_ref.dtype)

    fill(a_ref[...].astype(F32) * _sigmoid(gt_ref[...].astype(F32)),
         ah_ref[...].astype(F32) * _sigmoid(gth_ref[...].astype(F32)), CONF_KERNEL)
    u = conv(dww_ref, CONF_KERNEL) + dwb_ref[...]
    mu = jnp.mean(u, axis=-1, keepdims=True)
    var = jnp.mean(jnp.square(u - mu), axis=-1, keepdims=True)
    y = (u - mu) * lax.rsqrt(var + LN_EPS) * lng_ref[...] + lnb_ref[...]
    cf_ref[...] = (y * _sigmoid(y)).astype(cf_ref.dtype)


def _conv_branches(z, sc_w, dw_w, dw_b, ln_g, ln_b):
    s = z.shape[0]
    tm = min(ROW_TILE, s)
    w = SC_WIDTH
    cur = lambda col: pl.BlockSpec((tm, w), lambda i: (i, col // w))
    halo = lambda col: pl.BlockSpec((HALO, w), lambda i: (jnp.maximum(i * (tm // HALO) - 1, 0), col // w))
    full = lambda a: pl.BlockSpec(a.shape, lambda i: (0,) * a.ndim)
    vec = lambda a: a.reshape(1, -1)
    args = [sc_w, dw_w, vec(dw_b), vec(ln_g), vec(ln_b)]
    return pl.pallas_call(
        _conv_body,
        grid=(s // tm,),
        in_specs=[cur(Z_SC), cur(Z_SC + w), cur(Z_SC + 2 * w), halo(Z_SC + w), halo(Z_SC + 2 * w),
                  cur(Z_CONF), cur(Z_CONF + w), halo(Z_CONF), halo(Z_CONF + w)] + [full(a) for a in args],
        out_specs=[pl.BlockSpec((tm, w), lambda i: (i, 0)), pl.BlockSpec((tm, w), lambda i: (i, 0))],
        out_shape=[jax.ShapeDtypeStruct((s, w), BF16), jax.ShapeDtypeStruct((s, w), BF16)],
        scratch_shapes=[pltpu.VMEM((SUBLANES, HALO + tm, w), F32)],
        compiler_params=_cparams(("parallel",)),
        name="conv_branches",
    )(*([z] * 9), *args)


def _merge_body(b0, b1, b2, b3, w0, w1, w2, w3, g0, g1, g2, g3, gb_ref, o_ref):
    acc = None
    for n, (b_ref, w_ref, g_ref) in enumerate(((b0, w0, g0), (b1, w1, g1), (b2, w2, g2), (b3, w3, g3))):
        y = jnp.dot(b_ref[...], w_ref[...], preferred_element_type=F32)
        t = _sigmoid(g_ref[...].astype(F32) + gb_ref[n:n + 1, :]) * y
        acc = t if acc is None else acc + t
    o_ref[...] = acc.astype(o_ref.dtype)


def _merge(branches, w_outs, z, gate_b):
    s = z.shape[0]
    tm = min(MM_TILE, s)
    tn = ROW_TILE
    kw = branches[0].shape[1]
    return pl.pallas_call(
        _merge_body,
        grid=(s // tm, D_MODEL // tn),
        in_specs=([pl.BlockSpec((tm, kw), lambda i, j: (i, 0))] * 4
                  + [pl.BlockSpec((kw, tn), lambda i, j: (0, j))] * 4
                  + [pl.BlockSpec((tm, tn), lambda i, j, b=b: (i, (Z_GATE + b * D_MODEL) // tn + j))
                     for b in range(N_BRANCH)]
                  + [pl.BlockSpec((N_BRANCH, tn), lambda i, j: (0, j))]),
        out_specs=pl.BlockSpec((tm, tn), lambda i, j: (i, j)),
        out_shape=jax.ShapeDtypeStruct((s, D_MODEL), BF16),
        compiler_params=_cparams(("parallel", "arbitrary")),
        name="merge",
    )(*branches, *w_outs, z, z, z, z, gate_b)


def _order(v, i, j):
    v[i], v[j] = jnp.maximum(v[i], v[j]), jnp.minimum(v[i], v[j])


def _bitonic_sort_desc(v):
    n = len(v)
    k = 2
    while k <= n:
        j = k // 2
        while j >= 1:
            for i in range(n):
                partner = i ^ j
                if partner > i:
                    if i & k == 0:
                        _order(v, i, partner)
                    else:
                        _order(v, partner, i)
            j //= 2
        k *= 2


def _bitonic_merge_desc(v):
    n = len(v)
    j = n // 2
    while j >= 1:
        for i in range(n):
            if i ^ j > i:
                _order(v, i, i ^ j)
        j //= 2


def _top16_desc(groups):
    v = list(groups)
    _bitonic_sort_desc(v)
    for shift in (SUBLANES // 2, SUBLANES // 4, SUBLANES // 8):
        other = [pltpu.roll(x, shift, 0) for x in v]
        v = [jnp.maximum(v[k], other[PEER_TOPK - 1 - k]) for k in range(PEER_TOPK)]
        _bitonic_merge_desc(v)
    return v


def _by_sublane(v, start):
    sub = lax.broadcasted_iota(jnp.int32, v[0].shape, 0)
    out = v[start]
    for r in range(1, SUBLANES):
        out = jnp.where(sub == r, v[start + r], out)
    return out


def _peer_candidates(a, a_hi, b_lo, b_hi, b0, op):
    groups = [op(a[0], b_lo), op(a[0], b_hi)]
    groups += [op(a[i], b_lo) for i in range(1, SUBLANES)]
    groups.append(op(a_hi, b0))
    return groups


def _peer_scores_body(q_ref, k1_ref, k2_ref, th_ref, s2_ref, e1_ref, e2n_ref):
    tt = q_ref.shape[0]
    n_groups = PEER_N_KEYS // SUBLANES

    def head(h, carry):
        qh = q_ref[:, pl.ds(pl.multiple_of(h * PEER_KEY_DIM, PEER_KEY_DIM), PEER_KEY_DIM)]
        dims = (((1,), (1,)), ((), ()))
        s1 = lax.dot_general(k1_ref[...], qh, dims, precision=lax.Precision.HIGHEST, preferred_element_type=F32)
        s2 = lax.dot_general(k2_ref[...], qh, dims, precision=lax.Precision.HIGHEST, preferred_element_type=F32)
        split = lambda s: [s[g * SUBLANES:(g + 1) * SUBLANES, :] for g in range(n_groups)]
        t1 = _top16_desc(split(s1))
        t2 = _top16_desc(split(s2))
        t1_hi, t2_lo, t2_hi = _by_sublane(t1, SUBLANES), _by_sublane(t2, 0), _by_sublane(t2, SUBLANES)
        cand = _peer_candidates(t1, t1_hi, t2_lo, t2_hi, t2[0], jnp.add)
        pad = [jnp.full((SUBLANES, tt), -jnp.inf, F32)] * (PEER_TOPK - len(cand))
        tau8 = _top16_desc(cand + pad)[PEER_TOPK - 1]

        m1, m2 = t1[0], t2[0]
        c1 = [jnp.exp(x - m1) for x in t1[:SUBLANES]]
        ec = _peer_candidates(c1, jnp.exp(t1_hi - m1), jnp.exp(t2_lo - m2), jnp.exp(t2_hi - m2),
                              jnp.ones((SUBLANES, tt), F32), jnp.multiply)
        zsum = jnp.zeros((SUBLANES, tt), F32)
        for cg, eg in zip(cand, ec):
            zsum = zsum + jnp.where(cg >= tau8, eg, 0.0)
        zden = jnp.sum(zsum, axis=0, keepdims=True)
        tau = tau8[0:1]
        th = jnp.full(s1.shape, jnp.inf, F32)
        for b in range(PEER_TOPK):
            t2b = t2[b][0:1]
            th = jnp.where(s1 + t2b >= tau, t2b, th)
        th_ref[h] = th
        s2_ref[h] = s2
        e1_ref[h] = jnp.exp(s1 - m1[0:1])
        e2n_ref[h] = jnp.exp(s2 - m2[0:1]) / zden
        return carry

    lax.fori_loop(0, PEER_HEADS, head, 0)


def _peer_scores(q, k1, k2):
    s = q.shape[0]
    tt = min(SCORE_TILE, s)
    tab = pl.BlockSpec((PEER_HEADS, PEER_N_KEYS, tt), lambda i: (0, 0, i))
    tab_shape = jax.ShapeDtypeStruct((PEER_HEADS, PEER_N_KEYS, s), F32)
    return pl.pallas_call(
        _peer_scores_body,
        grid=(s // tt,),
        in_specs=[pl.BlockSpec((tt, PEER_HEADS * PEER_KEY_DIM), lambda i: (i, 0)),
                  pl.BlockSpec(k1.shape, lambda i: (0, 0)), pl.BlockSpec(k2.shape, lambda i: (0, 0))],
        out_specs=[tab, tab, tab, tab],
        out_shape=[tab_shape, tab_shape, tab_shape, tab_shape],
        compiler_params=_cparams(("parallel",)),
        name="peer_scores",
    )(q, k1, k2)


def _peer_items(s, n_items, nb):
    split = lambda it: (it // nb, it % nb)
    return (split(jnp.minimum(s, n_items - 1)), split(jnp.clip(s - 1, 0, n_items - 1)),
            split(jnp.clip(s - 2, 0, n_items - 1)))


def _peer_dense_body(ht_ref, u_ref, vt_ref, th_ref, e1_ref, s2_ref, e2n_ref, o_ref, a0, a1, w0, w1,
                     *, n_items, nb):
    s = pl.program_id(0)
    tt = ht_ref.shape[1]
    eb = u_ref.shape[0]
    d = vt_ref.shape[0]
    n_sub = eb // PEER_N_KEYS
    jrows = 2 * SUBLANES
    a_rows = eb // PEER_TRIPS

    @pl.when(s == 0)
    def _():
        for ref in (a0, a1, w0, w1):
            ref[...] = jnp.zeros_like(ref)

    _, _, (_, block_c) = _peer_items(s, n_items, nb)

    @pl.when(block_c == 0)
    def _():
        o_ref[...] = jnp.zeros_like(o_ref)

    def run(a_new, a_old, w_new, w_old):
        def stage_a(k):
            rows = pl.ds(pl.multiple_of(k * a_rows, a_rows), a_rows)
            a_t = jnp.dot(u_ref[rows, :], ht_ref[...], preferred_element_type=F32)
            a_new[rows, :] = 0.5 * a_t * (1.0 + lax.erf(a_t * (2.0 ** -0.5)))

        def stage_b(c, jc):
            cols = pl.ds(pl.multiple_of(c * LANES, LANES), LANES)
            jr = slice(jc * jrows, (jc + 1) * jrows)
            g = [jnp.zeros((jrows, LANES), F32) for _ in range(n_sub)]
            for h in range(PEER_HEADS):
                s2c = s2_ref[h, jr, cols]
                e2c = e2n_ref[h, jr, cols]
                for il in range(n_sub):
                    gate = e2c * e1_ref[h, il:il + 1, cols]
                    g[il] = g[il] + jnp.where(s2c >= th_ref[h, il:il + 1, cols], gate, 0.0)
            for il in range(n_sub):
                rows = slice(il * PEER_N_KEYS + jc * jrows, il * PEER_N_KEYS + (jc + 1) * jrows)
                w_new[rows, cols] = (a_old[rows, cols] * g[il]).astype(w_new.dtype)

        def stage_c(n):
            rows = pl.ds(pl.multiple_of(n * PEER_CHUNK, PEER_CHUNK), PEER_CHUNK)
            o_ref[rows, :] += jnp.dot(vt_ref[rows, :], w_old[...], preferred_element_type=F32)

        n_c = d // PEER_CHUNK // PEER_TRIPS
        n_col = tt // LANES // PEER_TRIPS
        assert n_c * PEER_TRIPS * PEER_CHUNK == d and n_col * PEER_TRIPS * LANES == tt and n_c % 2 == 0

        def trip(k, carry):
            stage_a(k)
            for n in range(n_c // 2):
                stage_c(n_c * k + n)
            for c in range(n_col):
                for jc in range(PEER_N_KEYS // jrows):
                    stage_b(n_col * k + c, jc)
            for n in range(n_c // 2, n_c):
                stage_c(n_c * k + n)
            return carry

        lax.fori_loop(0, PEER_TRIPS, trip, 0)

    parity = lax.rem(s, 2)

    @pl.when(parity == 0)
    def _():
        run(a0, a1, w1, w0)

    @pl.when(parity == 1)
    def _():
        run(a1, a0, w0, w1)


def _peer_dense(hn_t, u_all, vt_all, layer, th, e1, s2, e2n):
    d, s = hn_t.shape
    tt = min(PEER_TOKENS, s)
    eb = PEER_EXPERTS
    nb = u_all.shape[1] // eb
    n_items = (s // tt) * nb
    items = functools.partial(_peer_items, n_items=n_items, nb=nb)
    sub = pl.BlockSpec((PEER_HEADS, eb // PEER_N_KEYS, tt), lambda i: (0, items(i)[1][1], items(i)[1][0]))
    tab = pl.BlockSpec((PEER_HEADS, PEER_N_KEYS, tt), lambda i: (0, 0, items(i)[1][0]))
    return pl.pallas_call(
        functools.partial(_peer_dense_body, n_items=n_items, nb=nb),
        grid=(n_items + 2,),
        in_specs=[pl.BlockSpec((d, tt), lambda i: (0, items(i)[0][0])),
                  pl.BlockSpec((None, eb, d), lambda i: (layer, items(i)[0][1], 0)),
                  pl.BlockSpec((None, d, eb), lambda i: (layer, 0, items(i)[2][1])),
                  sub, sub, tab, tab],
        out_specs=pl.BlockSpec((d, tt), lambda i: (0, items(i)[2][0])),
        out_shape=jax.ShapeDtypeStruct((d, s), F32),
        scratch_shapes=[pltpu.VMEM((eb, tt), F32), pltpu.VMEM((eb, tt), F32),
                        pltpu.VMEM((eb, tt), BF16), pltpu.VMEM((eb, tt), BF16)],
        compiler_params=_cparams(("arbitrary",)),
        name="peer_dense",
    )(hn_t, u_all, vt_all, th, e1, s2, e2n)


def _add_norm_body(x_ref, dt_ref, g_ref, *out_refs):
    x = x_ref[...] + dt_ref[...].T
    y = x * lax.rsqrt(jnp.mean(x * x, axis=-1, keepdims=True) + NORM_EPS)
    out_refs[-1][...] = (y * g_ref[...]).astype(out_refs[-1].dtype)
    if len(out_refs) == 2:
        out_refs[0][...] = x


def _add_norm(x, delta_t, g, norm_dtype, with_sum):
    s, d = x.shape
    tm = min(ROW_TILE, s)
    row = pl.BlockSpec((tm, d), lambda i: (i, 0))
    out_specs = [row, row] if with_sum else [row]
    out_shape = [jax.ShapeDtypeStruct((s, d), norm_dtype)]
    if with_sum:
        out_shape.insert(0, jax.ShapeDtypeStruct((s, d), F32))
    return pl.pallas_call(
        _add_norm_body,
        grid=(s // tm,),
        in_specs=[row, pl.BlockSpec((d, tm), lambda i: (0, i)), pl.BlockSpec((1, d), lambda i: (0, 0))],
        out_specs=out_specs,
        out_shape=out_shape,
        compiler_params=_cparams(("parallel",)),
        name="add_norm",
    )(x, delta_t, g.reshape(1, d))


def _rope_tables(positions, dim, lanes_x1):
    half = dim // 2
    inv_freq = ROPE_THETA ** (-jnp.arange(0, dim, 2, dtype=F32) / dim)
    ang = positions.astype(F32)[:, None] * inv_freq
    cos, sin = jnp.cos(ang), jnp.sin(ang)
    s = positions.shape[0]
    c_parts, su_parts, sd_parts = [], [], []
    pos = 0
    for off in lanes_x1:
        gap = off - pos
        c_parts += [jnp.ones((s, gap), F32), cos, cos]
        su_parts += [jnp.zeros((s, gap), F32), -sin, jnp.zeros((s, half), F32)]
        sd_parts += [jnp.zeros((s, gap + half), F32), sin]
        pos = off + dim
    tail = LANES - pos
    c_parts.append(jnp.ones((s, tail), F32))
    su_parts.append(jnp.zeros((s, tail), F32))
    sd_parts.append(jnp.zeros((s, tail), F32))
    return tuple(jnp.concatenate(p, axis=1) for p in (c_parts, su_parts, sd_parts))


W_IN_HEAD = IN_OFFSETS[3]
W_IN_RUN = sum(IN_SIZES[3:])
LAYOUT_ROWS = 1024


def _in_proj_body(h_ref, a_ref, b_ref, o_ref, wt_ref):
    j = pl.program_id(0)
    rb = wt_ref.shape[0]
    lead = rb - W_IN_HEAD % rb
    new_block = pl.program_id(1) == 0

    @pl.when(new_block & (j < W_IN_RUN // rb))
    def _():
        wt_ref[0:lead, :] = a_ref[W_IN_HEAD % rb:rb, :].astype(wt_ref.dtype)
        wt_ref[lead:rb, :] = b_ref[0:rb - lead, :].astype(wt_ref.dtype)

    @pl.when(new_block & (j == W_IN_RUN // rb))
    def _():
        head = MLA_Q_LORA + MLA_KV_LORA
        wt_ref[0:head, :] = a_ref[0:head, :].astype(wt_ref.dtype)
        wt_ref[head:rb, :] = jnp.zeros((rb - head, wt_ref.shape[1]), wt_ref.dtype)
        wt_ref[head + MLA_NOPE:head + MLA_NOPE + MLA_ROPE, :] = a_ref[head:head + MLA_ROPE, :].astype(wt_ref.dtype)

    o_ref[...] = lax.dot_general(h_ref[...], wt_ref[...], (((1,), (1,)), ((), ())),
                                 preferred_element_type=F32).astype(o_ref.dtype)


def _in_proj(hn, w_in_t, layer):
    s, d = hn.shape
    n = w_in_t.shape[1]
    rb = LAYOUT_ROWS
    tm = min(MM_TILE, s)
    assert W_IN_RUN % rb == 0 and Z_CQ == W_IN_RUN and Z_COLS == W_IN_RUN + rb and W_IN_HEAD < rb
    run_blocks = W_IN_RUN // rb
    first = lambda j: jnp.where(j < run_blocks, j + W_IN_HEAD // rb, 0)
    second = lambda j: jnp.minimum(j + W_IN_HEAD // rb + 1, (n - 1) // rb)
    return pl.pallas_call(
        _in_proj_body,
        grid=(Z_COLS // rb, s // tm),
        in_specs=[pl.BlockSpec((tm, d), lambda j, i: (i, 0)),
                  pl.BlockSpec((None, rb, d), lambda j, i: (layer, first(j), 0)),
                  pl.BlockSpec((None, rb, d), lambda j, i: (layer, second(j), 0))],
        out_specs=pl.BlockSpec((tm, rb), lambda j, i: (i, j)),
        out_shape=jax.ShapeDtypeStruct((s, Z_COLS), BF16),
        scratch_shapes=[pltpu.VMEM((rb, d), BF16)],
        compiler_params=_cparams(("parallel", "arbitrary")),
        name="in_proj",
    )(hn, w_in_t, w_in_t)


def _layout_mla_weights(w_uq, w_ukv):
    lq = w_uq.shape[0]
    wq = w_uq.reshape(lq, MLA_HEADS, MLA_NOPE + MLA_ROPE)
    wq = jnp.pad(wq, ((0, 0), (0, 0), (0, LANES - MLA_NOPE - MLA_ROPE))).reshape(lq, MLA_HEADS * LANES)
    lk = w_ukv.shape[0]
    wkv = w_ukv.reshape(lk, MLA_HEADS, MLA_NOPE + MLA_V)
    wk = jnp.pad(wkv[:, :, :MLA_NOPE], ((0, 0), (0, 0), (0, LANES - MLA_NOPE))).reshape(lk, MLA_HEADS * LANES)
    wv = wkv[:, :, MLA_NOPE:].reshape(lk, MLA_HEADS * MLA_V)
    return wq.T.astype(BF16), wk.astype(BF16), wv.T.astype(BF16)


def kernel(x, positions, mix_norm_g, w_in, gate_b, mla_q_norm_g, mla_w_uq, mla_kv_norm_g, mla_w_ukv, mla_w_out,
           sc_conv_w, sc_w_out, diff_lambda, diff_norm_g, diff_w_out, conf_dw_w, conf_dw_b, conf_ln_g, conf_ln_b,
           conf_w_out, w_o, ffn_norm_g, peer_w_q, peer_sub_keys, peer_u, peer_v, final_norm_g):
    b, s, d = x.shape
    assert b == 1 and d == D_MODEL
    xs = x.reshape(s, d)
    pos = positions.reshape(s)
    mla_tabs = _rope_tables(pos, MLA_ROPE, (MLA_NOPE,))
    mla_tabs_t = tuple(t.T for t in mla_tabs)
    w_in_t = jnp.swapaxes(w_in, 1, 2)
    u_all = peer_u.astype(BF16)
    vt_all = jnp.swapaxes(peer_v, 1, 2).astype(BF16)
    diff_tabs = _rope_tables(pos, DIFF_ROT, (0, DIFF_HEAD_DIM))

    for i in range(DEPTH):
        lam_init = 0.8 - 0.6 * math.exp(-0.3 * i)
        if i == 0:
            hn = _rmsnorm(xs, mix_norm_g[i], BF16)
        else:
            xs, hn = _add_norm(xs, delta_t, mix_norm_g[i], BF16, with_sum=True)
        z = _in_proj(hn, w_in_t, i)
        wqt, wk, wvt = _layout_mla_weights(mla_w_uq[i], mla_w_ukv[i])
        qt, k, vt = _mla_prep(z, mla_tabs, mla_tabs_t, mla_q_norm_g[i].reshape(1, -1),
                              mla_kv_norm_g[i].reshape(1, -1), wqt, wk, wvt)
        o_mla = _mla_attn(qt, k, vt)
        o_sc, o_conf = _conv_branches(z, sc_conv_w[i], conf_dw_w[i], conf_dw_b[i], conf_ln_g[i], conf_ln_b[i])
        qdt, kd, vdt = _diff_prep(z, diff_tabs)
        o_diff = _diff_attn(qdt, kd, vdt, diff_lambda[i], diff_norm_g[i], lam_init)
        merged = _merge((o_mla, o_sc, o_diff, o_conf),
                        tuple(w.astype(BF16) for w in (mla_w_out[i], sc_w_out[i], diff_w_out[i], conf_w_out[i])),
                        z, gate_b[i])
        xs = _matmul(merged, w_o[i].astype(BF16), F32, residual=xs, name="out_proj")
        hf, hf_t = _rmsnorm(xs, ffn_norm_g[i], BF16, with_transposed=True)
        pq = _matmul(hf, peer_w_q[i].astype(BF16), F32, name="peer_q")
        zk = jnp.zeros((PEER_N_KEYS, PEER_HALF), F32)
        k1 = jnp.concatenate([peer_sub_keys[i, 0], zk], axis=1)
        k2 = jnp.concatenate([zk, peer_sub_keys[i, 1]], axis=1)
        th, s2, e1, e2n = _peer_scores(pq, k1, k2)
        delta_t = _peer_dense(hf_t, u_all, vt_all, i, th, e1, s2, e2n)
    (out,) = _add_norm(xs, delta_t, final_norm_g, F32, with_sum=False)
    return out.reshape(b, s, d)
```

```python
import functools
import math

import jax
import jax.numpy as jnp
import numpy as np
from jax import lax
from jax.experimental import pallas as pl
from jax.experimental.pallas import tpu as pltpu

F32 = jnp.float32
BF16 = jnp.bfloat16

D_MODEL = 2048
DEPTH = 2
CHUNK = 64
ROPE_THETA = 500000.0
NORM_EPS = 1e-6
LN_EPS = 1e-5
NEG_INF = -1e30
N_BRANCH = 4

MLA_HEADS = 8
MLA_Q_LORA = 512
MLA_KV_LORA = 256
MLA_NOPE = 64
MLA_ROPE = 32
MLA_V = 64
MLA_SCALE = (MLA_NOPE + MLA_ROPE) ** -0.5

SC_WIDTH = 512
SC_KERNEL = 3

DIFF_HEADS = 4
DIFF_HEAD_DIM = 64
DIFF_ROT = DIFF_HEAD_DIM // 4
DIFF_SCALE = DIFF_HEAD_DIM ** -0.5
DIFF_WIDTH = DIFF_HEADS * 2 * DIFF_HEAD_DIM

CONF_WIDTH = 512
CONF_KERNEL = 31

PEER_HEADS = 8
PEER_N_KEYS = 128
PEER_N_EXPERTS = PEER_N_KEYS * PEER_N_KEYS
PEER_KEY_DIM = 128
PEER_HALF = PEER_KEY_DIM // 2
PEER_TOPK = 16

IN_SIZES = (MLA_Q_LORA, MLA_KV_LORA, MLA_ROPE, 3 * SC_WIDTH, DIFF_WIDTH, DIFF_WIDTH, DIFF_WIDTH,
            2 * CONF_WIDTH, N_BRANCH * D_MODEL)
IN_OFFSETS = tuple(int(v) for v in np.cumsum((0,) + IN_SIZES)[:-1])

LOG2E = math.log2(math.e)

LANES = 128
SUBLANES = 8
HALO = 32
VMEM_LIMIT = 56 * 1024 * 1024

ROW_TILE = 512
MM_TILE = 1024
ATTN_TILE = 512
SCORE_TILE = 512
PEER_TOKENS = 512
PEER_EXPERTS = 1024
PEER_CHUNK = 256
PEER_TRIPS = 1

Z_SC = 0
Z_DQ = 1536
Z_DK = 2048
Z_DV = 2560
Z_CONF = 3072
Z_GATE = 4096
Z_CQ = 12288
Z_CKV = 12800
Z_KR = 13056
Z_COLS = 13312


def _cparams(sem):
    return pltpu.CompilerParams(dimension_semantics=sem, vmem_limit_bytes=VMEM_LIMIT)


def _rmsnorm_body(x_ref, g_ref, o_ref, *maybe_ot_ref):
    x = x_ref[...].astype(F32)
    y = x * lax.rsqrt(jnp.mean(x * x, axis=-1, keepdims=True) + NORM_EPS) * g_ref[...]
    o_ref[...] = y.astype(o_ref.dtype)
    for ot_ref in maybe_ot_ref:
        ot_ref[...] = y.T.astype(ot_ref.dtype)


def _rmsnorm(x, g, out_dtype, with_transposed=False):
    s, d = x.shape
    tm = min(ROW_TILE, s)
    out_specs = [pl.BlockSpec((tm, d), lambda i: (i, 0))]
    out_shape = [jax.ShapeDtypeStruct((s, d), out_dtype)]
    if with_transposed:
        out_specs.append(pl.BlockSpec((d, tm), lambda i: (0, i)))
        out_shape.append(jax.ShapeDtypeStruct((d, s), out_dtype))
    out = pl.pallas_call(
        _rmsnorm_body,
        grid=(s // tm,),
        in_specs=[pl.BlockSpec((tm, d), lambda i: (i, 0)), pl.BlockSpec((1, d), lambda i: (0, 0))],
        out_specs=out_specs,
        out_shape=out_shape,
        compiler_params=_cparams(("parallel",)),
        name="rmsnorm",
    )(x, g.reshape(1, d))
    return out if with_transposed else out[0]


def _matmul_body(a_ref, b_ref, o_ref):
    o_ref[...] = jnp.dot(a_ref[...], b_ref[...], preferred_element_type=F32).astype(o_ref.dtype)


def _matmul_res_body(a_ref, b_ref, r_ref, o_ref):
    o_ref[...] = (r_ref[...] + jnp.dot(a_ref[...], b_ref[...], preferred_element_type=F32)).astype(o_ref.dtype)


def _matmul(a, b, out_dtype, residual=None, name="matmul"):
    m, k = a.shape
    n = b.shape[1]
    tm, tn = min(MM_TILE, m), min(MM_TILE, n)
    in_specs = [pl.BlockSpec((tm, k), lambda i, j: (i, 0)), pl.BlockSpec((k, tn), lambda i, j: (0, j))]
    args = [a, b]
    body = _matmul_body
    if residual is not None:
        in_specs.append(pl.BlockSpec((tm, tn), lambda i, j: (i, j)))
        args.append(residual)
        body = _matmul_res_body
    return pl.pallas_call(
        body,
        grid=(m // tm, n // tn),
        in_specs=in_specs,
        out_specs=pl.BlockSpec((tm, tn), lambda i, j: (i, j)),
        out_shape=jax.ShapeDtypeStruct((m, n), out_dtype),
        compiler_params=_cparams(("parallel", "arbitrary")),
        name=name,
    )(*args)


def _sigmoid(x):
    return 0.5 + 0.5 * jnp.tanh(0.5 * x)


def _rope128(x, c, s_up, s_dn, half):
    return x * c + pltpu.roll(x, LANES - half, 1) * s_up + pltpu.roll(x, half, 1) * s_dn


def _mla_prep_body(cq_ref, ckv_ref, kr_ref, c_ref, su_ref, sd_ref, ct_ref, sut_ref, sdt_ref,
                   qg_ref, kvg_ref, wqt_ref, wk_ref, wvt_ref, qt_ref, k_ref, vt_ref):
    half = MLA_ROPE // 2
    nt = (((1,), (1,)), ((), ()))

    def norm(x_ref, g_ref):
        x = x_ref[...].astype(F32)
        y = x * lax.rsqrt(jnp.mean(x * x, axis=-1, keepdims=True) + NORM_EPS)
        return (y * g_ref[...]).astype(BF16)

    qf_t = lax.dot_general(wqt_ref[...], norm(cq_ref, qg_ref), nt, preferred_element_type=F32)
    ckvn = norm(ckv_ref, kvg_ref)
    kf = jnp.dot(ckvn, wk_ref[...], preferred_element_type=F32)
    vt_ref[...] = lax.dot_general(wvt_ref[...], ckvn, nt, preferred_element_type=F32).astype(vt_ref.dtype)
    kr = _rope128(kr_ref[...].astype(F32), c_ref[...], su_ref[...], sd_ref[...], half)
    ct, sut, sdt = ct_ref[...], sut_ref[...], sdt_ref[...]
    for h in range(MLA_HEADS):
        sl = slice(h * LANES, (h + 1) * LANES)
        x = qf_t[sl, :]
        q = x * ct + pltpu.roll(x, LANES - half, 0) * sut + pltpu.roll(x, half, 0) * sdt
        qt_ref[sl, :] = (q * (MLA_SCALE * LOG2E)).astype(qt_ref.dtype)
        k_ref[:, sl] = (kf[:, sl] + kr).astype(k_ref.dtype)


def _mla_prep(z, tabs, tabs_t, qg, kvg, wqt, wk, wvt):
    s = z.shape[0]
    tm = min(ROW_TILE, s)
    row = lambda w, idx: pl.BlockSpec((tm, w), lambda i: (i, idx))
    full = lambda a: pl.BlockSpec(a.shape, lambda i: (0,) * a.ndim)
    tab = pl.BlockSpec((tm, LANES), lambda i: (i, 0))
    tab_t = pl.BlockSpec((LANES, tm), lambda i: (0, i))
    hw = MLA_HEADS * LANES
    vw = MLA_HEADS * MLA_V
    return pl.pallas_call(
        _mla_prep_body,
        grid=(s // tm,),
        in_specs=[row(MLA_Q_LORA, Z_CQ // MLA_Q_LORA), row(MLA_KV_LORA, Z_CKV // MLA_KV_LORA),
                  row(LANES, Z_KR // LANES), tab, tab, tab, tab_t, tab_t, tab_t,
                  full(qg), full(kvg), full(wqt), full(wk), full(wvt)],
        out_specs=[pl.BlockSpec((hw, tm), lambda i: (0, i)), pl.BlockSpec((tm, hw), lambda i: (i, 0)),
                   pl.BlockSpec((vw, tm), lambda i: (0, i))],
        out_shape=[jax.ShapeDtypeStruct((hw, s), BF16), jax.ShapeDtypeStruct((s, hw), BF16),
                   jax.ShapeDtypeStruct((vw, s), BF16)],
        compiler_params=_cparams(("parallel",)),
        name="mla_prep",
    )(z, z, z, *tabs, *tabs_t, qg, kvg, wqt, wk, wvt)


def _flash_streams_t(q_ts, k_ats, vt_ats, m_ref, l_ref, acc_ref, s_a, s_b, n_full, t):
    streams = range(len(q_ts))
    for n in streams:
        m_ref[n] = jnp.full((1, t), NEG_INF, F32)
        l_ref[n] = jnp.zeros((1, t), F32)
        acc_ref[n] = jnp.zeros(acc_ref.shape[1:], F32)

    def scores_into(s_ref, j):
        for n in streams:
            s_ref[n] = jnp.dot(k_ats[n](j), q_ts[n], preferred_element_type=F32)

    def consume(s_ref, j, masked):
        probs, alphas = [], []
        for n in streams:
            s = s_ref[n]
            if masked:
                key_chunk = lax.broadcasted_iota(jnp.int32, (t, t), 0) // CHUNK
                qry_chunk = lax.broadcasted_iota(jnp.int32, (t, t), 1) // CHUNK
                s = jnp.where(key_chunk <= qry_chunk, s, NEG_INF)
            m_prev = m_ref[n]
            m_new = jnp.maximum(m_prev, jnp.max(s, axis=0, keepdims=True))
            alpha = jnp.exp2(m_prev - m_new)
            p = jnp.exp2(s - m_new)
            l_ref[n] = alpha * l_ref[n] + jnp.sum(p, axis=0, keepdims=True)
            m_ref[n] = m_new
            probs.append(p.astype(BF16))
            alphas.append(alpha)
        for n in streams:
            acc_ref[n] = alphas[n] * acc_ref[n] + jnp.dot(vt_ats[n](j), probs[n], preferred_element_type=F32)

    scores_into(s_a, 0)

    def pair(jj, carry):
        j = 2 * jj
        scores_into(s_b, j + 1)
        consume(s_a, j, False)
        scores_into(s_a, j + 2)
        consume(s_b, j + 1, False)
        return carry

    lax.fori_loop(0, n_full // 2, pair, 0)

    @pl.when(n_full % 2 == 0)
    def _():
        consume(s_a, n_full, True)

    @pl.when(n_full % 2 == 1)
    def _():
        scores_into(s_b, n_full)
        consume(s_a, n_full - 1, False)
        consume(s_b, n_full, True)


def _mla_attn_body(qt_ref, k_ref, vt_ref, o_ref, m_ref, l_ref, acc_ref, s_a, s_b, *, t):
    i = pl.program_id(1)
    rows = lambda j: pl.ds(pl.multiple_of(j * t, t), t)
    lanes = [slice(hh * LANES, (hh + 1) * LANES) for hh in range(2)]
    k_ats = [lambda j, sl=sl: k_ref[rows(j), sl] for sl in lanes]
    vt_ats = [lambda j, hh=hh: vt_ref[hh * MLA_V:(hh + 1) * MLA_V, rows(j)] for hh in range(2)]
    _flash_streams_t([qt_ref[sl, :] for sl in lanes], k_ats, vt_ats, m_ref, l_ref, acc_ref, s_a, s_b, i, t)
    o_t = jnp.concatenate([acc_ref[0] / l_ref[0], acc_ref[1] / l_ref[1]], axis=0)
    o_ref[...] = o_t.T.astype(o_ref.dtype)


def _mla_attn(qt, k, vt):
    s = k.shape[0]
    t = min(ATTN_TILE, s)
    pairs = MLA_HEADS // 2
    return pl.pallas_call(
        functools.partial(_mla_attn_body, t=t),
        grid=(pairs, s // t),
        in_specs=[pl.BlockSpec((2 * LANES, t), lambda p, i: (p, i)),
                  pl.BlockSpec((s, 2 * LANES), lambda p, i: (0, p)),
                  pl.BlockSpec((2 * MLA_V, s), lambda p, i: (p, 0))],
        out_specs=pl.BlockSpec((t, 2 * MLA_V), lambda p, i: (i, p)),
        out_shape=jax.ShapeDtypeStruct((s, MLA_HEADS * MLA_V), BF16),
        scratch_shapes=[pltpu.VMEM((2, 1, t), F32), pltpu.VMEM((2, 1, t), F32),
                        pltpu.VMEM((2, MLA_V, t), F32),
                        pltpu.VMEM((2, t, t), F32), pltpu.VMEM((2, t, t), F32)],
        compiler_params=_cparams(("parallel", "arbitrary")),
        name="mla_attn",
    )(qt, k, vt)


def _diff_prep_body(dq_ref, dk_ref, dv_ref, c_ref, su_ref, sd_ref, qt_ref, k_ref, vt_ref):
    half = DIFF_ROT // 2
    c, su, sd = c_ref[...], su_ref[...], sd_ref[...]
    tm = dq_ref.shape[0]
    lane = lax.broadcasted_iota(jnp.int32, (tm, LANES), 1)
    vt_ref[...] = dv_ref[...].astype(F32).T.astype(vt_ref.dtype)
    for h in range(DIFF_HEADS):
        sl = slice(h * LANES, (h + 1) * LANES)
        q = _rope128(dq_ref[:, sl].astype(F32), c, su, sd, half) * (DIFF_SCALE * LOG2E)
        qt_ref[(2 * h) * LANES:(2 * h + 1) * LANES, :] = jnp.where(lane < DIFF_HEAD_DIM, q, 0.0).T.astype(qt_ref.dtype)
        qt_ref[(2 * h + 1) * LANES:(2 * h + 2) * LANES, :] = jnp.where(lane >= DIFF_HEAD_DIM, q, 0.0).T.astype(qt_ref.dtype)
        k_ref[:, sl] = _rope128(dk_ref[:, sl].astype(F32), c, su, sd, half).astype(k_ref.dtype)


def _diff_prep(z, tabs):
    s = z.shape[0]
    tm = min(ROW_TILE, s)
    tab = pl.BlockSpec((tm, LANES), lambda i: (i, 0))
    col = lambda start: pl.BlockSpec((tm, DIFF_WIDTH), lambda i: (i, start // DIFF_WIDTH))
    return pl.pallas_call(
        _diff_prep_body,
        grid=(s // tm,),
        in_specs=[col(Z_DQ), col(Z_DK), col(Z_DV), tab, tab, tab],
        out_specs=[pl.BlockSpec((2 * DIFF_WIDTH, tm), lambda i: (0, i)),
                   pl.BlockSpec((tm, DIFF_WIDTH), lambda i: (i, 0)),
                   pl.BlockSpec((DIFF_WIDTH, tm), lambda i: (0, i))],
        out_shape=[jax.ShapeDtypeStruct((2 * DIFF_WIDTH, s), BF16), jax.ShapeDtypeStruct((s, DIFF_WIDTH), BF16),
                   jax.ShapeDtypeStruct((DIFF_WIDTH, s), BF16)],
        compiler_params=_cparams(("parallel",)),
        name="diff_prep",
    )(z, z, z, *tabs)


def _diff_attn_body(qt_ref, k_ref, vt_ref, lam_ref, g_ref, o_ref, m_ref, l_ref, acc_ref, s_a, s_b, *, t, lam_init):
    i = pl.program_id(1)
    rows = lambda j: pl.ds(pl.multiple_of(j * t, t), t)
    k_at = lambda j: k_ref[rows(j), :]
    vt_at = lambda j: vt_ref[:, rows(j)]
    q_ts = [qt_ref[c * LANES:(c + 1) * LANES, :] for c in range(2)]
    _flash_streams_t(q_ts, [k_at, k_at], [vt_at, vt_at], m_ref, l_ref, acc_ref, s_a, s_b, i, t)
    lv = lam_ref[...]
    lam = (jnp.exp(jnp.sum(lv[0:1] * lv[1:2], axis=1, keepdims=True))
           - jnp.exp(jnp.sum(lv[2:3] * lv[3:4], axis=1, keepdims=True)) + lam_init)
    o = (acc_ref[0] / l_ref[0] - lam * (acc_ref[1] / l_ref[1])).T
    o = o * lax.rsqrt(jnp.mean(o * o, axis=-1, keepdims=True) + NORM_EPS)
    o_ref[...] = (o * g_ref[...] * (1.0 - lam_init)).astype(o_ref.dtype)


def _diff_attn(qt, k, vt, lam_vecs, norm_g, lam_init):
    s = k.shape[0]
    t = min(ATTN_TILE, s)
    return pl.pallas_call(
        functools.partial(_diff_attn_body, t=t, lam_init=lam_init),
        grid=(DIFF_HEADS, s // t),
        in_specs=[pl.BlockSpec((2 * LANES, t), lambda h, i: (h, i)),
                  pl.BlockSpec((s, LANES), lambda h, i: (0, h)),
                  pl.BlockSpec((LANES, s), lambda h, i: (h, 0)),
                  pl.BlockSpec(lam_vecs.shape, lambda h, i: (0, 0)),
                  pl.BlockSpec((1, LANES), lambda h, i: (0, 0))],
        out_specs=pl.BlockSpec((t, LANES), lambda h, i: (i, h)),
        out_shape=jax.ShapeDtypeStruct((s, DIFF_WIDTH), BF16),
        scratch_shapes=[pltpu.VMEM((2, 1, t), F32), pltpu.VMEM((2, 1, t), F32),
                        pltpu.VMEM((2, LANES, t), F32),
                        pltpu.VMEM((2, t, t), F32), pltpu.VMEM((2, t, t), F32)],
        compiler_params=_cparams(("parallel", "arbitrary")),
        name="diff_attn",
    )(qt, k, vt, lam_vecs, norm_g.reshape(1, LANES))


def _conv_body(bg_ref, cg_ref, xv_ref, cgh_ref, xvh_ref, a_ref, gt_ref, ah_ref, gth_ref,
               scw_ref, dww_ref, dwb_ref, lng_ref, lnb_ref, sc_ref, cf_ref, ext_ref):
    tm = bg_ref.shape[0]
    first = pl.program_id(0) == 0

    shifted_rows = HALO + tm - SUBLANES

    def fill(cur, halo, taps):
        ext_ref[0, 0:HALO, :] = jnp.where(first, 0.0, halo)
        ext_ref[0, HALO:HALO + tm, :] = cur
        for r in sorted({(HALO - (taps - 1) + j) % SUBLANES for j in range(taps)} - {0}):
            ext_ref[r, 0:shifted_rows, :] = ext_ref[0, r:r + shifted_rows, :]

    def conv(w_ref, taps):
        acc = jnp.zeros((tm, ext_ref.shape[2]), F32)
        for j in range(taps):
            off = HALO - (taps - 1) + j
            r = off % SUBLANES
            acc = acc + w_ref[j:j + 1, :] * ext_ref[r, off - r:off - r + tm, :]
        return acc

    fill(cg_ref[...].astype(F32) * xv_ref[...].astype(F32), cgh_ref[...].astype(F32) * xvh_ref[...].astype(F32),
         SC_KERNEL)
    sc_ref[...] = (bg_ref[...].astype(F32) * conv(scw_ref, SC_KERNEL)).astype(sc_ref.dtype)

    fill(a_ref[...].astype(F32) * _sigmoid(gt_ref[...].astype(F32)),
         ah_ref[...].astype(F32) * _sigmoid(gth_ref[...].astype(F32)), CONF_KERNEL)
    u = conv(dww_ref, CONF_KERNEL) + dwb_ref[...]
    mu = jnp.mean(u, axis=-1, keepdims=True)
    var = jnp.mean(jnp.square(u - mu), axis=-1, keepdims=True)
    y = (u - mu) * lax.rsqrt(var + LN_EPS) * lng_ref[...] + lnb_ref[...]
    cf_ref[...] = (y * _sigmoid(y)).astype(cf_ref.dtype)


def _conv_branches(z, sc_w, dw_w, dw_b, ln_g, ln_b):
    s = z.shape[0]
    tm = min(ROW_TILE, s)
    w = SC_WIDTH
    cur = lambda col: pl.BlockSpec((tm, w), lambda i: (i, col // w))
    halo = lambda col: pl.BlockSpec((HALO, w), lambda i: (jnp.maximum(i * (tm // HALO) - 1, 0), col // w))
    full = lambda a: pl.BlockSpec(a.shape, lambda i: (0,) * a.ndim)
    vec = lambda a: a.reshape(1, -1)
    args = [sc_w, dw_w, vec(dw_b), vec(ln_g), vec(ln_b)]
    return pl.pallas_call(
        _conv_body,
        grid=(s // tm,),
        in_specs=[cur(Z_SC), cur(Z_SC + w), cur(Z_SC + 2 * w), halo(Z_SC + w), halo(Z_SC + 2 * w),
                  cur(Z_CONF), cur(Z_CONF + w), halo(Z_CONF), halo(Z_CONF + w)] + [full(a) for a in args],
        out_specs=[pl.BlockSpec((tm, w), lambda i: (i, 0)), pl.BlockSpec((tm, w), lambda i: (i, 0))],
        out_shape=[jax.ShapeDtypeStruct((s, w), BF16), jax.ShapeDtypeStruct((s, w), BF16)],
        scratch_shapes=[pltpu.VMEM((SUBLANES, HALO + tm, w), F32)],
        compiler_params=_cparams(("parallel",)),
        name="conv_branches",
    )(*([z] * 9), *args)


def _merge_body(b0, b1, b2, b3, w0, w1, w2, w3, g0, g1, g2, g3, gb_ref, o_ref):
    acc = None
    for n, (b_ref, w_ref, g_ref) in enumerate(((b0, w0, g0), (b1, w1, g1), (b2, w2, g2), (b3, w3, g3))):
        y = jnp.dot(b_ref[...], w_ref[...], preferred_element_type=F32)
        t = _sigmoid(g_ref[...].astype(F32) + gb_ref[n:n + 1, :]) * y
        acc = t if acc is None else acc + t
    o_ref[...] = acc.astype(o_ref.dtype)


def _merge(branches, w_outs, z, gate_b):
    s = z.shape[0]
    tm = min(MM_TILE, s)
    tn = ROW_TILE
    kw = branches[0].shape[1]
    return pl.pallas_call(
        _merge_body,
        grid=(s // tm, D_MODEL // tn),
        in_specs=([pl.BlockSpec((tm, kw), lambda i, j: (i, 0))] * 4
                  + [pl.BlockSpec((kw, tn), lambda i, j: (0, j))] * 4
                  + [pl.BlockSpec((tm, tn), lambda i, j, b=b: (i, (Z_GATE + b * D_MODEL) // tn + j))
                     for b in range(N_BRANCH)]
                  + [pl.BlockSpec((N_BRANCH, tn), lambda i, j: (0, j))]),
        out_specs=pl.BlockSpec((tm, tn), lambda i, j: (i, j)),
        out_shape=jax.ShapeDtypeStruct((s, D_MODEL), BF16),
        compiler_params=_cparams(("parallel", "arbitrary")),
        name="merge",
    )(*branches, *w_outs, z, z, z, z, gate_b)


def _order(v, i, j):
    v[i], v[j] = jnp.maximum(v[i], v[j]), jnp.minimum(v[i], v[j])


def _bitonic_sort_desc(v):
    n = len(v)
    k = 2
    while k <= n:
        j = k // 2
        while j >= 1:
            for i in range(n):
                partner = i ^ j
                if partner > i:
                    if i & k == 0:
                        _order(v, i, partner)
                    else:
                        _order(v, partner, i)
            j //= 2
        k *= 2


def _bitonic_merge_desc(v):
    n = len(v)
    j = n // 2
    while j >= 1:
        for i in range(n):
            if i ^ j > i:
                _order(v, i, i ^ j)
        j //= 2


def _top16_desc(groups):
    v = list(groups)
    _bitonic_sort_desc(v)
    for shift in (SUBLANES // 2, SUBLANES // 4, SUBLANES // 8):
        other = [pltpu.roll(x, shift, 0) for x in v]
        v = [jnp.maximum(v[k], other[PEER_TOPK - 1 - k]) for k in range(PEER_TOPK)]
        _bitonic_merge_desc(v)
    return v


def _by_sublane(v, start):
    sub = lax.broadcasted_iota(jnp.int32, v[0].shape, 0)
    out = v[start]
    for r in range(1, SUBLANES):
        out = jnp.where(sub == r, v[start + r], out)
    return out


def _peer_candidates(a, a_hi, b_lo, b_hi, b0, op):
    groups = [op(a[0], b_lo), op(a[0], b_hi)]
    groups += [op(a[i], b_lo) for i in range(1, SUBLANES)]
    groups.append(op(a_hi, b0))
    return groups


def _peer_scores_body(q_ref, k1_ref, k2_ref, th_ref, s2_ref, e1_ref, e2n_ref):
    tt = q_ref.shape[0]
    n_groups = PEER_N_KEYS // SUBLANES

    def head(h, carry):
        qh = q_ref[:, pl.ds(pl.multiple_of(h * PEER_KEY_DIM, PEER_KEY_DIM), PEER_KEY_DIM)]
        dims = (((1,), (1,)), ((), ()))
        s1 = lax.dot_general(k1_ref[...], qh, dims, precision=lax.Precision.HIGHEST, preferred_element_type=F32)
        s2 = lax.dot_general(k2_ref[...], qh, dims, precision=lax.Precision.HIGHEST, preferred_element_type=F32)
        split = lambda s: [s[g * SUBLANES:(g + 1) * SUBLANES, :] for g in range(n_groups)]
        t1 = _top16_desc(split(s1))
        t2 = _top16_desc(split(s2))
        t1_hi, t2_lo, t2_hi = _by_sublane(t1, SUBLANES), _by_sublane(t2, 0), _by_sublane(t2, SUBLANES)
        cand = _peer_candidates(t1, t1_hi, t2_lo, t2_hi, t2[0], jnp.add)
        pad = [jnp.full((SUBLANES, tt), -jnp.inf, F32)] * (PEER_TOPK - len(cand))
        tau8 = _top16_desc(cand + pad)[PEER_TOPK - 1]

        m1, m2 = t1[0], t2[0]
        c1 = [jnp.exp(x - m1) for x in t1[:SUBLANES]]
        ec = _peer_candidates(c1, jnp.exp(t1_hi - m1), jnp.exp(t2_lo - m2), jnp.exp(t2_hi - m2),
                              jnp.ones((SUBLANES, tt), F32), jnp.multiply)
        zsum = jnp.zeros((SUBLANES, tt), F32)
        for cg, eg in zip(cand, ec):
            zsum = zsum + jnp.where(cg >= tau8, eg, 0.0)
        zden = jnp.sum(zsum, axis=0, keepdims=True)
        tau = tau8[0:1]
        th = jnp.full(s1.shape, jnp.inf, F32)
        for b in range(PEER_TOPK):
            t2b = t2[b][0:1]
            th = jnp.where(s1 + t2b >= tau, t2b, th)
        th_ref[h] = th
        s2_ref[h] = s2
        e1_ref[h] = jnp.exp(s1 - m1[0:1])
        e2n_ref[h] = jnp.exp(s2 - m2[0:1]) / zden
        return carry

    lax.fori_loop(0, PEER_HEADS, head, 0)


def _peer_scores(q, k1, k2):
    s = q.shape[0]
    tt = min(SCORE_TILE, s)
    tab = pl.BlockSpec((PEER_HEADS, PEER_N_KEYS, tt), lambda i: (0, 0, i))
    tab_shape = jax.ShapeDtypeStruct((PEER_HEADS, PEER_N_KEYS, s), F32)
    return pl.pallas_call(
        _peer_scores_body,
        grid=(s // tt,),
        in_specs=[pl.BlockSpec((tt, PEER_HEADS * PEER_KEY_DIM), lambda i: (i, 0)),
                  pl.BlockSpec(k1.shape, lambda i: (0, 0)), pl.BlockSpec(k2.shape, lambda i: (0, 0))],
        out_specs=[tab, tab, tab, tab],
        out_shape=[tab_shape, tab_shape, tab_shape, tab_shape],
        compiler_params=_cparams(("parallel",)),
        name="peer_scores",
    )(q, k1, k2)


def _peer_items(s, n_items, nb):
    split = lambda it: (it // nb, it % nb)
    return (split(jnp.minimum(s, n_items - 1)), split(jnp.clip(s - 1, 0, n_items - 1)),
            split(jnp.clip(s - 2, 0, n_items - 1)))


def _peer_dense_body(ht_ref, u_ref, v_ref, th_ref, e1_ref, s2_ref, e2n_ref, o_ref, a0, a1, w0, w1,
                     *, n_items, nb):
    s = pl.program_id(0)
    tt = ht_ref.shape[1]
    eb = u_ref.shape[0]
    d = ht_ref.shape[0]
    n_sub = eb // PEER_N_KEYS
    jrows = 2 * SUBLANES
    a_rows = eb // PEER_TRIPS

    @pl.when(s == 0)
    def _():
        for ref in (a0, a1, w0, w1):
            ref[...] = jnp.zeros_like(ref)

    _, _, (_, block_c) = _peer_items(s, n_items, nb)

    @pl.when(block_c == 0)
    def _():
        o_ref[...] = jnp.zeros_like(o_ref)

    def run(a_new, a_old, w_new, w_old):
        def stage_a(k):
            rows = pl.ds(pl.multiple_of(k * a_rows, a_rows), a_rows)
            a_t = jnp.dot(u_ref[rows, :], ht_ref[...], preferred_element_type=F32)
            a_new[rows, :] = 0.5 * a_t * (1.0 + lax.erf(a_t * (2.0 ** -0.5)))

        def stage_b(c, jc):
            cols = pl.ds(pl.multiple_of(c * LANES, LANES), LANES)
            jr = slice(jc * jrows, (jc + 1) * jrows)
            g = [jnp.zeros((jrows, LANES), F32) for _ in range(n_sub)]
            for h in range(PEER_HEADS):
                s2c = s2_ref[h, jr, cols]
                e2c = e2n_ref[h, jr, cols]
                for il in range(n_sub):
                    gate = e2c * e1_ref[h, il:il + 1, cols]
                    g[il] = g[il] + jnp.where(s2c >= th_ref[h, il:il + 1, cols], gate, 0.0)
            for il in range(n_sub):
                rows = slice(il * PEER_N_KEYS + jc * jrows, il * PEER_N_KEYS + (jc + 1) * jrows)
                w_new[rows, cols] = (a_old[rows, cols] * g[il]).astype(w_new.dtype)

        def stage_c(n):
            rows = pl.ds(pl.multiple_of(n * PEER_CHUNK, PEER_CHUNK), PEER_CHUNK)
            o_ref[rows, :] += lax.dot_general(v_ref[:, rows], w_old[...], (((0,), (0,)), ((), ())),
                                              preferred_element_type=F32)

        n_c = d // PEER_CHUNK // PEER_TRIPS
        n_col = tt // LANES // PEER_TRIPS
        assert n_c * PEER_TRIPS * PEER_CHUNK == d and n_col * PEER_TRIPS * LANES == tt and n_c % 2 == 0

        def trip(k, carry):
            stage_a(k)
            for n in range(n_c // 2):
                stage_c(n_c * k + n)
            for c in range(n_col):
                for jc in range(PEER_N_KEYS // jrows):
                    stage_b(n_col * k + c, jc)
            for n in range(n_c // 2, n_c):
                stage_c(n_c * k + n)
            return carry

        lax.fori_loop(0, PEER_TRIPS, trip, 0)

    parity = lax.rem(s, 2)

    @pl.when(parity == 0)
    def _():
        run(a0, a1, w1, w0)

    @pl.when(parity == 1)
    def _():
        run(a1, a0, w0, w1)


def _peer_dense(hn_t, u_all, v_all, layer, th, e1, s2, e2n):
    d, s = hn_t.shape
    tt = min(PEER_TOKENS, s)
    eb = PEER_EXPERTS
    nb = u_all.shape[1] // eb
    n_items = (s // tt) * nb
    items = functools.partial(_peer_items, n_items=n_items, nb=nb)
    sub = pl.BlockSpec((PEER_HEADS, eb // PEER_N_KEYS, tt), lambda i: (0, items(i)[1][1], items(i)[1][0]))
    tab = pl.BlockSpec((PEER_HEADS, PEER_N_KEYS, tt), lambda i: (0, 0, items(i)[1][0]))
    return pl.pallas_call(
        functools.partial(_peer_dense_body, n_items=n_items, nb=nb),
        grid=(n_items + 2,),
        in_specs=[pl.BlockSpec((d, tt), lambda i: (0, items(i)[0][0])),
                  pl.BlockSpec((None, eb, d), lambda i: (layer, items(i)[0][1], 0)),
                  pl.BlockSpec((None, eb, d), lambda i: (layer, items(i)[2][1], 0)),
                  sub, sub, tab, tab],
        out_specs=pl.BlockSpec((d, tt), lambda i: (0, items(i)[2][0])),
        out_shape=jax.ShapeDtypeStruct((d, s), F32),
        scratch_shapes=[pltpu.VMEM((eb, tt), F32), pltpu.VMEM((eb, tt), F32),
                        pltpu.VMEM((eb, tt), BF16), pltpu.VMEM((eb, tt), BF16)],
        compiler_params=_cparams(("arbitrary",)),
        name="peer_dense",
    )(hn_t, u_all, v_all, th, e1, s2, e2n)


def _add_norm_body(x_ref, dt_ref, g_ref, *out_refs):
    x = x_ref[...] + dt_ref[...].T
    y = x * lax.rsqrt(jnp.mean(x * x, axis=-1, keepdims=True) + NORM_EPS)
    out_refs[-1][...] = (y * g_ref[...]).astype(out_refs[-1].dtype)
    if len(out_refs) == 2:
        out_refs[0][...] = x


def _add_norm(x, delta_t, g, norm_dtype, with_sum):
    s, d = x.shape
    tm = min(ROW_TILE, s)
    row = pl.BlockSpec((tm, d), lambda i: (i, 0))
    out_specs = [row, row] if with_sum else [row]
    out_shape = [jax.ShapeDtypeStruct((s, d), norm_dtype)]
    if with_sum:
        out_shape.insert(0, jax.ShapeDtypeStruct((s, d), F32))
    return pl.pallas_call(
        _add_norm_body,
        grid=(s // tm,),
        in_specs=[row, pl.BlockSpec((d, tm), lambda i: (0, i)), pl.BlockSpec((1, d), lambda i: (0, 0))],
        out_specs=out_specs,
        out_shape=out_shape,
        compiler_params=_cparams(("parallel",)),
        name="add_norm",
    )(x, delta_t, g.reshape(1, d))


def _rope_tables(positions, dim, lanes_x1):
    half = dim // 2
    inv_freq = ROPE_THETA ** (-jnp.arange(0, dim, 2, dtype=F32) / dim)
    ang = positions.astype(F32)[:, None] * inv_freq
    cos, sin = jnp.cos(ang), jnp.sin(ang)
    s = positions.shape[0]
    c_parts, su_parts, sd_parts = [], [], []
    pos = 0
    for off in lanes_x1:
        gap = off - pos
        c_parts += [jnp.ones((s, gap), F32), cos, cos]
        su_parts += [jnp.zeros((s, gap), F32), -sin, jnp.zeros((s, half), F32)]
        sd_parts += [jnp.zeros((s, gap + half), F32), sin]
        pos = off + dim
    tail = LANES - pos
    c_parts.append(jnp.ones((s, tail), F32))
    su_parts.append(jnp.zeros((s, tail), F32))
    sd_parts.append(jnp.zeros((s, tail), F32))
    return tuple(jnp.concatenate(p, axis=1) for p in (c_parts, su_parts, sd_parts))


W_IN_HEAD = IN_OFFSETS[3]
W_IN_RUN = sum(IN_SIZES[3:])
LAYOUT_ROWS = 1024


def _in_proj_body(h_ref, a_ref, b_ref, o_ref, wt_ref):
    j = pl.program_id(0)
    rb = wt_ref.shape[0]
    lead = rb - W_IN_HEAD % rb
    new_block = pl.program_id(1) == 0

    @pl.when(new_block & (j < W_IN_RUN // rb))
    def _():
        wt_ref[0:lead, :] = a_ref[W_IN_HEAD % rb:rb, :].astype(wt_ref.dtype)
        wt_ref[lead:rb, :] = b_ref[0:rb - lead, :].astype(wt_ref.dtype)

    @pl.when(new_block & (j == W_IN_RUN // rb))
    def _():
        head = MLA_Q_LORA + MLA_KV_LORA
        wt_ref[0:head, :] = a_ref[0:head, :].astype(wt_ref.dtype)
        wt_ref[head:rb, :] = jnp.zeros((rb - head, wt_ref.shape[1]), wt_ref.dtype)
        wt_ref[head + MLA_NOPE:head + MLA_NOPE + MLA_ROPE, :] = a_ref[head:head + MLA_ROPE, :].astype(wt_ref.dtype)

    o_ref[...] = lax.dot_general(h_ref[...], wt_ref[...], (((1,), (1,)), ((), ())),
                                 preferred_element_type=F32).astype(o_ref.dtype)


def _in_proj(hn, w_in_t, layer):
    s, d = hn.shape
    n = w_in_t.shape[1]
    rb = LAYOUT_ROWS
    tm = min(MM_TILE, s)
    assert W_IN_RUN % rb == 0 and Z_CQ == W_IN_RUN and Z_COLS == W_IN_RUN + rb and W_IN_HEAD < rb
    run_blocks = W_IN_RUN // rb
    first = lambda j: jnp.where(j < run_blocks, j + W_IN_HEAD // rb, 0)
    second = lambda j: jnp.minimum(j + W_IN_HEAD // rb + 1, (n - 1) // rb)
    return pl.pallas_call(
        _in_proj_body,
        grid=(Z_COLS // rb, s // tm),
        in_specs=[pl.BlockSpec((tm, d), lambda j, i: (i, 0)),
                  pl.BlockSpec((None, rb, d), lambda j, i: (layer, first(j), 0)),
                  pl.BlockSpec((None, rb, d), lambda j, i: (layer, second(j), 0))],
        out_specs=pl.BlockSpec((tm, rb), lambda j, i: (i, j)),
        out_shape=jax.ShapeDtypeStruct((s, Z_COLS), BF16),
        scratch_shapes=[pltpu.VMEM((rb, d), BF16)],
        compiler_params=_cparams(("parallel", "arbitrary")),
        name="in_proj",
    )(hn, w_in_t, w_in_t)


def _layout_mla_weights(w_uq, w_ukv):
    lq = w_uq.shape[0]
    wq = w_uq.reshape(lq, MLA_HEADS, MLA_NOPE + MLA_ROPE)
    wq = jnp.pad(wq, ((0, 0), (0, 0), (0, LANES - MLA_NOPE - MLA_ROPE))).reshape(lq, MLA_HEADS * LANES)
    lk = w_ukv.shape[0]
    wkv = w_ukv.reshape(lk, MLA_HEADS, MLA_NOPE + MLA_V)
    wk = jnp.pad(wkv[:, :, :MLA_NOPE], ((0, 0), (0, 0), (0, LANES - MLA_NOPE))).reshape(lk, MLA_HEADS * LANES)
    wv = wkv[:, :, MLA_NOPE:].reshape(lk, MLA_HEADS * MLA_V)
    return wq.T.astype(BF16), wk.astype(BF16), wv.T.astype(BF16)


def kernel(x, positions, mix_norm_g, w_in, gate_b, mla_q_norm_g, mla_w_uq, mla_kv_norm_g, mla_w_ukv, mla_w_out,
           sc_conv_w, sc_w_out, diff_lambda, diff_norm_g, diff_w_out, conf_dw_w, conf_dw_b, conf_ln_g, conf_ln_b,
           conf_w_out, w_o, ffn_norm_g, peer_w_q, peer_sub_keys, peer_u, peer_v, final_norm_g):
    b, s, d = x.shape
    assert b == 1 and d == D_MODEL
    xs = x.reshape(s, d)
    pos = positions.reshape(s)
    mla_tabs = _rope_tables(pos, MLA_ROPE, (MLA_NOPE,))
    mla_tabs_t = tuple(t.T for t in mla_tabs)
    w_in_t = jnp.swapaxes(w_in, 1, 2)
    u_all = peer_u.astype(BF16)
    v_all = peer_v.astype(BF16)
    diff_tabs = _rope_tables(pos, DIFF_ROT, (0, DIFF_HEAD_DIM))

    for i in range(DEPTH):
        lam_init = 0.8 - 0.6 * math.exp(-0.3 * i)
        if i == 0:
            hn = _rmsnorm(xs, mix_norm_g[i], BF16)
        else:
            xs, hn = _add_norm(xs, delta_t, mix_norm_g[i], BF16, with_sum=True)
        z = _in_proj(hn, w_in_t, i)
        wqt, wk, wvt = _layout_mla_weights(mla_w_uq[i], mla_w_ukv[i])
        qt, k, vt = _mla_prep(z, mla_tabs, mla_tabs_t, mla_q_norm_g[i].reshape(1, -1),
                              mla_kv_norm_g[i].reshape(1, -1), wqt, wk, wvt)
        o_mla = _mla_attn(qt, k, vt)
        o_sc, o_conf = _conv_branches(z, sc_conv_w[i], conf_dw_w[i], conf_dw_b[i], conf_ln_g[i], conf_ln_b[i])
        qdt, kd, vdt = _diff_prep(z, diff_tabs)
        o_diff = _diff_attn(qdt, kd, vdt, diff_lambda[i], diff_norm_g[i], lam_init)
        merged = _merge((o_mla, o_sc, o_diff, o_conf),
                        tuple(w.astype(BF16) for w in (mla_w_out[i], sc_w_out[i], diff_w_out[i], conf_w_out[i])),
                        z, gate_b[i])
        xs = _matmul(merged, w_o[i].astype(BF16), F32, residual=xs, name="out_proj")
        hf, hf_t = _rmsnorm(xs, ffn_norm_g[i], BF16, with_transposed=True)
        pq = _matmul(hf, peer_w_q[i].astype(BF16), F32, name="peer_q")
        zk = jnp.zeros((PEER_N_KEYS, PEER_HALF), F32)
        k1 = jnp.concatenate([peer_sub_keys[i, 0], zk], axis=1)
        k2 = jnp.concatenate([zk, peer_sub_keys[i, 1]], axis=1)
        th, s2, e1, e2n = _peer_scores(pq, k1, k2)
        delta_t = _peer_dense(hf_t, u_all, v_all, i, th, e1, s2, e2n)
    (out,) = _add_norm(xs, delta_t, final_norm_g, F32, with_sum=False)
    return out.reshape(b, s, d)
```

```python
import functools
import math

import jax
import jax.numpy as jnp
import numpy as np
from jax import lax
from jax.experimental import pallas as pl
from jax.experimental.pallas import tpu as pltpu

F32 = jnp.float32
BF16 = jnp.bfloat16

D_MODEL = 2048
DEPTH = 2
CHUNK = 64
ROPE_THETA = 500000.0
NORM_EPS = 1e-6
LN_EPS = 1e-5
NEG_INF = -1e30
N_BRANCH = 4

MLA_HEADS = 8
MLA_Q_LORA = 512
MLA_KV_LORA = 256
MLA_NOPE = 64
MLA_ROPE = 32
MLA_V = 64
MLA_SCALE = (MLA_NOPE + MLA_ROPE) ** -0.5

SC_WIDTH = 512
SC_KERNEL = 3

DIFF_HEADS = 4
DIFF_HEAD_DIM = 64
DIFF_ROT = DIFF_HEAD_DIM // 4
DIFF_SCALE = DIFF_HEAD_DIM ** -0.5
DIFF_WIDTH = DIFF_HEADS * 2 * DIFF_HEAD_DIM

CONF_WIDTH = 512
CONF_KERNEL = 31

PEER_HEADS = 8
PEER_N_KEYS = 128
PEER_N_EXPERTS = PEER_N_KEYS * PEER_N_KEYS
PEER_KEY_DIM = 128
PEER_HALF = PEER_KEY_DIM // 2
PEER_TOPK = 16

IN_SIZES = (MLA_Q_LORA, MLA_KV_LORA, MLA_ROPE, 3 * SC_WIDTH, DIFF_WIDTH, DIFF_WIDTH, DIFF_WIDTH,
            2 * CONF_WIDTH, N_BRANCH * D_MODEL)
IN_OFFSETS = tuple(int(v) for v in np.cumsum((0,) + IN_SIZES)[:-1])

LOG2E = math.log2(math.e)

LANES = 128
SUBLANES = 8
HALO = 32
VMEM_LIMIT = 56 * 1024 * 1024

ROW_TILE = 512
MM_TILE = 1024
ATTN_TILE = 512
MLA_GROUP = 4
DIFF_GROUP = 2
SCORE_TILE = 512
PEER_TOKENS = 512
PEER_EXPERTS = 1024
PEER_CHUNK = 256
PEER_TRIPS = 1

Z_SC = 0
Z_DQ = 1536
Z_DK = 2048
Z_DV = 2560
Z_CONF = 3072
Z_GATE = 4096
Z_CQ = 12288
Z_CKV = 12800
Z_KR = 13056
Z_COLS = 13312


def _cparams(sem):
    return pltpu.CompilerParams(dimension_semantics=sem, vmem_limit_bytes=VMEM_LIMIT)


def _rmsnorm_body(x_ref, g_ref, o_ref, *maybe_ot_ref):
    x = x_ref[...].astype(F32)
    y = x * lax.rsqrt(jnp.mean(x * x, axis=-1, keepdims=True) + NORM_EPS) * g_ref[...]
    o_ref[...] = y.astype(o_ref.dtype)
    for ot_ref in maybe_ot_ref:
        ot_ref[...] = y.T.astype(ot_ref.dtype)


def _rmsnorm(x, g, out_dtype, with_transposed=False):
    s, d = x.shape
    tm = min(ROW_TILE, s)
    out_specs = [pl.BlockSpec((tm, d), lambda i: (i, 0))]
    out_shape = [jax.ShapeDtypeStruct((s, d), out_dtype)]
    if with_transposed:
        out_specs.append(pl.BlockSpec((d, tm), lambda i: (0, i)))
        out_shape.append(jax.ShapeDtypeStruct((d, s), out_dtype))
    out = pl.pallas_call(
        _rmsnorm_body,
        grid=(s // tm,),
        in_specs=[pl.BlockSpec((tm, d), lambda i: (i, 0)), pl.BlockSpec((1, d), lambda i: (0, 0))],
        out_specs=out_specs,
        out_shape=out_shape,
        compiler_params=_cparams(("parallel",)),
        name="rmsnorm",
    )(x, g.reshape(1, d))
    return out if with_transposed else out[0]


def _matmul_body(a_ref, b_ref, o_ref):
    o_ref[...] = jnp.dot(a_ref[...], b_ref[...], preferred_element_type=F32).astype(o_ref.dtype)


def _matmul_res_body(a_ref, b_ref, r_ref, o_ref):
    o_ref[...] = (r_ref[...] + jnp.dot(a_ref[...], b_ref[...], preferred_element_type=F32)).astype(o_ref.dtype)


def _matmul(a, b, out_dtype, residual=None, name="matmul"):
    m, k = a.shape
    n = b.shape[1]
    tm, tn = min(MM_TILE, m), min(MM_TILE, n)
    in_specs = [pl.BlockSpec((tm, k), lambda i, j: (i, 0)), pl.BlockSpec((k, tn), lambda i, j: (0, j))]
    args = [a, b]
    body = _matmul_body
    if residual is not None:
        in_specs.append(pl.BlockSpec((tm, tn), lambda i, j: (i, j)))
        args.append(residual)
        body = _matmul_res_body
    return pl.pallas_call(
        body,
        grid=(m // tm, n // tn),
        in_specs=in_specs,
        out_specs=pl.BlockSpec((tm, tn), lambda i, j: (i, j)),
        out_shape=jax.ShapeDtypeStruct((m, n), out_dtype),
        compiler_params=_cparams(("parallel", "arbitrary")),
        name=name,
    )(*args)


def _sigmoid(x):
    return 0.5 + 0.5 * jnp.tanh(0.5 * x)


def _rope128(x, c, s_up, s_dn, half):
    return x * c + pltpu.roll(x, LANES - half, 1) * s_up + pltpu.roll(x, half, 1) * s_dn


def _mla_prep_body(cq_ref, ckv_ref, kr_ref, c_ref, su_ref, sd_ref, ct_ref, sut_ref, sdt_ref,
                   qg_ref, kvg_ref, wqt_ref, wk_ref, wvt_ref, qt_ref, k_ref, vt_ref):
    half = MLA_ROPE // 2
    nt = (((1,), (1,)), ((), ()))

    def norm(x_ref, g_ref):
        x = x_ref[...].astype(F32)
        y = x * lax.rsqrt(jnp.mean(x * x, axis=-1, keepdims=True) + NORM_EPS)
        return (y * g_ref[...]).astype(BF16)

    qf_t = lax.dot_general(wqt_ref[...], norm(cq_ref, qg_ref), nt, preferred_element_type=F32)
    ckvn = norm(ckv_ref, kvg_ref)
    kf = jnp.dot(ckvn, wk_ref[...], preferred_element_type=F32)
    vt_ref[...] = lax.dot_general(wvt_ref[...], ckvn, nt, preferred_element_type=F32).astype(vt_ref.dtype)
    kr = _rope128(kr_ref[...].astype(F32), c_ref[...], su_ref[...], sd_ref[...], half)
    ct, sut, sdt = ct_ref[...], sut_ref[...], sdt_ref[...]
    for h in range(MLA_HEADS):
        sl = slice(h * LANES, (h + 1) * LANES)
        x = qf_t[sl, :]
        q = x * ct + pltpu.roll(x, LANES - half, 0) * sut + pltpu.roll(x, half, 0) * sdt
        qt_ref[sl, :] = (q * (MLA_SCALE * LOG2E)).astype(qt_ref.dtype)
        k_ref[:, sl] = (kf[:, sl] + kr).astype(k_ref.dtype)


def _mla_prep(z, tabs, tabs_t, qg, kvg, wqt, wk, wvt):
    s = z.shape[0]
    tm = min(ROW_TILE, s)
    row = lambda w, idx: pl.BlockSpec((tm, w), lambda i: (i, idx))
    full = lambda a: pl.BlockSpec(a.shape, lambda i: (0,) * a.ndim)
    tab = pl.BlockSpec((tm, LANES), lambda i: (i, 0))
    tab_t = pl.BlockSpec((LANES, tm), lambda i: (0, i))
    hw = MLA_HEADS * LANES
    vw = MLA_HEADS * MLA_V
    return pl.pallas_call(
        _mla_prep_body,
        grid=(s // tm,),
        in_specs=[row(MLA_Q_LORA, Z_CQ // MLA_Q_LORA), row(MLA_KV_LORA, Z_CKV // MLA_KV_LORA),
                  row(LANES, Z_KR // LANES), tab, tab, tab, tab_t, tab_t, tab_t,
                  full(qg), full(kvg), full(wqt), full(wk), full(wvt)],
        out_specs=[pl.BlockSpec((hw, tm), lambda i: (0, i)), pl.BlockSpec((tm, hw), lambda i: (i, 0)),
                   pl.BlockSpec((vw, tm), lambda i: (0, i))],
        out_shape=[jax.ShapeDtypeStruct((hw, s), BF16), jax.ShapeDtypeStruct((s, hw), BF16),
                   jax.ShapeDtypeStruct((vw, s), BF16)],
        compiler_params=_cparams(("parallel",)),
        name="mla_prep",
    )(z, z, z, *tabs, *tabs_t, qg, kvg, wqt, wk, wvt)


def _flash_streams_t(q_ts, k_ats, vt_ats, m_ref, l_ref, acc_ref, s_a, s_b, n_full, t):
    streams = range(len(q_ts))
    for n in streams:
        m_ref[n] = jnp.full((1, t), NEG_INF, F32)
        l_ref[n] = jnp.zeros((1, t), F32)
        acc_ref[n] = jnp.zeros(acc_ref.shape[1:], F32)

    def scores_into(s_ref, j):
        for n in streams:
            s_ref[n] = jnp.dot(k_ats[n](j), q_ts[n], preferred_element_type=F32)

    def consume(s_ref, j, masked):
        probs, alphas = [], []
        for n in streams:
            s = s_ref[n]
            if masked:
                key_chunk = lax.broadcasted_iota(jnp.int32, (t, t), 0) // CHUNK
                qry_chunk = lax.broadcasted_iota(jnp.int32, (t, t), 1) // CHUNK
                s = jnp.where(key_chunk <= qry_chunk, s, NEG_INF)
            m_prev = m_ref[n]
            m_new = jnp.maximum(m_prev, jnp.max(s, axis=0, keepdims=True))
            alpha = jnp.exp2(m_prev - m_new)
            p = jnp.exp2(s - m_new)
            l_ref[n] = alpha * l_ref[n] + jnp.sum(p, axis=0, keepdims=True)
            m_ref[n] = m_new
            probs.append(p.astype(BF16))
            alphas.append(alpha)
        for n in streams:
            acc_ref[n] = alphas[n] * acc_ref[n] + jnp.dot(vt_ats[n](j), probs[n], preferred_element_type=F32)

    scores_into(s_a, 0)

    def pair(jj, carry):
        j = 2 * jj
        scores_into(s_b, j + 1)
        consume(s_a, j, False)
        scores_into(s_a, j + 2)
        consume(s_b, j + 1, False)
        return carry

    lax.fori_loop(0, n_full // 2, pair, 0)

    @pl.when(n_full % 2 == 0)
    def _():
        consume(s_a, n_full, True)

    @pl.when(n_full % 2 == 1)
    def _():
        scores_into(s_b, n_full)
        consume(s_a, n_full - 1, False)
        consume(s_b, n_full, True)


def _mla_attn_body(qt_ref, k_ref, vt_ref, o_ref, m_ref, l_ref, acc_ref, s_a, s_b, *, t):
    i = pl.program_id(1)
    rows = lambda j: pl.ds(pl.multiple_of(j * t, t), t)
    heads = range(MLA_GROUP)
    lanes = [slice(hh * LANES, (hh + 1) * LANES) for hh in heads]
    k_ats = [lambda j, sl=sl: k_ref[rows(j), sl] for sl in lanes]
    vt_ats = [lambda j, hh=hh: vt_ref[hh * MLA_V:(hh + 1) * MLA_V, rows(j)] for hh in heads]
    _flash_streams_t([qt_ref[sl, :] for sl in lanes], k_ats, vt_ats, m_ref, l_ref, acc_ref, s_a, s_b, i, t)
    o_t = jnp.concatenate([acc_ref[hh] / l_ref[hh] for hh in heads], axis=0)
    o_ref[...] = o_t.T.astype(o_ref.dtype)


def _mla_attn(qt, k, vt):
    s = k.shape[0]
    t = min(ATTN_TILE, s)
    g = MLA_GROUP
    return pl.pallas_call(
        functools.partial(_mla_attn_body, t=t),
        grid=(MLA_HEADS // g, s // t),
        in_specs=[pl.BlockSpec((g * LANES, t), lambda p, i: (p, i)),
                  pl.BlockSpec((s, g * LANES), lambda p, i: (0, p)),
                  pl.BlockSpec((g * MLA_V, s), lambda p, i: (p, 0))],
        out_specs=pl.BlockSpec((t, g * MLA_V), lambda p, i: (i, p)),
        out_shape=jax.ShapeDtypeStruct((s, MLA_HEADS * MLA_V), BF16),
        scratch_shapes=[pltpu.VMEM((g, 1, t), F32), pltpu.VMEM((g, 1, t), F32),
                        pltpu.VMEM((g, MLA_V, t), F32),
                        pltpu.VMEM((g, t, t), F32), pltpu.VMEM((g, t, t), F32)],
        compiler_params=_cparams(("parallel", "arbitrary")),
        name="mla_attn",
    )(qt, k, vt)


def _diff_prep_body(dq_ref, dk_ref, dv_ref, c_ref, su_ref, sd_ref, qt_ref, k_ref, vt_ref):
    half = DIFF_ROT // 2
    c, su, sd = c_ref[...], su_ref[...], sd_ref[...]
    tm = dq_ref.shape[0]
    lane = lax.broadcasted_iota(jnp.int32, (tm, LANES), 1)
    vt_ref[...] = dv_ref[...].astype(F32).T.astype(vt_ref.dtype)
    for h in range(DIFF_HEADS):
        sl = slice(h * LANES, (h + 1) * LANES)
        q = _rope128(dq_ref[:, sl].astype(F32), c, su, sd, half) * (DIFF_SCALE * LOG2E)
        qt_ref[(2 * h) * LANES:(2 * h + 1) * LANES, :] = jnp.where(lane < DIFF_HEAD_DIM, q, 0.0).T.astype(qt_ref.dtype)
        qt_ref[(2 * h + 1) * LANES:(2 * h + 2) * LANES, :] = jnp.where(lane >= DIFF_HEAD_DIM, q, 0.0).T.astype(qt_ref.dtype)
        k_ref[:, sl] = _rope128(dk_ref[:, sl].astype(F32), c, su, sd, half).astype(k_ref.dtype)


def _diff_prep(z, tabs):
    s = z.shape[0]
    tm = min(ROW_TILE, s)
    tab = pl.BlockSpec((tm, LANES), lambda i: (i, 0))
    col = lambda start: pl.BlockSpec((tm, DIFF_WIDTH), lambda i: (i, start // DIFF_WIDTH))
    return pl.pallas_call(
        _diff_prep_body,
        grid=(s // tm,),
        in_specs=[col(Z_DQ), col(Z_DK), col(Z_DV), tab, tab, tab],
        out_specs=[pl.BlockSpec((2 * DIFF_WIDTH, tm), lambda i: (0, i)),
                   pl.BlockSpec((tm, DIFF_WIDTH), lambda i: (i, 0)),
                   pl.BlockSpec((DIFF_WIDTH, tm), lambda i: (0, i))],
        out_shape=[jax.ShapeDtypeStruct((2 * DIFF_WIDTH, s), BF16), jax.ShapeDtypeStruct((s, DIFF_WIDTH), BF16),
                   jax.ShapeDtypeStruct((DIFF_WIDTH, s), BF16)],
        compiler_params=_cparams(("parallel",)),
        name="diff_prep",
    )(z, z, z, *tabs)


def _diff_attn_body(qt_ref, k_ref, vt_ref, lam_ref, g_ref, o_ref, m_ref, l_ref, acc_ref, s_a, s_b, *, t, lam_init):
    i = pl.program_id(1)
    rows = lambda j: pl.ds(pl.multiple_of(j * t, t), t)
    q_ts, k_ats, vt_ats = [], [], []
    for hh in range(DIFF_GROUP):
        sl = slice(hh * LANES, (hh + 1) * LANES)
        for c in range(2):
            q_ts.append(qt_ref[(2 * hh + c) * LANES:(2 * hh + c + 1) * LANES, :])
            k_ats.append(lambda j, sl=sl: k_ref[rows(j), sl])
            vt_ats.append(lambda j, sl=sl: vt_ref[sl, rows(j)])
    _flash_streams_t(q_ts, k_ats, vt_ats, m_ref, l_ref, acc_ref, s_a, s_b, i, t)
    lv = lam_ref[...]
    lam = (jnp.exp(jnp.sum(lv[0:1] * lv[1:2], axis=1, keepdims=True))
           - jnp.exp(jnp.sum(lv[2:3] * lv[3:4], axis=1, keepdims=True)) + lam_init)
    for hh in range(DIFF_GROUP):
        o = (acc_ref[2 * hh] / l_ref[2 * hh] - lam * (acc_ref[2 * hh + 1] / l_ref[2 * hh + 1])).T
        o = o * lax.rsqrt(jnp.mean(o * o, axis=-1, keepdims=True) + NORM_EPS)
        o_ref[:, hh * LANES:(hh + 1) * LANES] = (o * g_ref[...] * (1.0 - lam_init)).astype(o_ref.dtype)


def _diff_attn(qt, k, vt, lam_vecs, norm_g, lam_init):
    s = k.shape[0]
    t = min(ATTN_TILE, s)
    g = DIFF_GROUP
    return pl.pallas_call(
        functools.partial(_diff_attn_body, t=t, lam_init=lam_init),
        grid=(DIFF_HEADS // g, s // t),
        in_specs=[pl.BlockSpec((2 * g * LANES, t), lambda h, i: (h, i)),
                  pl.BlockSpec((s, g * LANES), lambda h, i: (0, h)),
                  pl.BlockSpec((g * LANES, s), lambda h, i: (h, 0)),
                  pl.BlockSpec(lam_vecs.shape, lambda h, i: (0, 0)),
                  pl.BlockSpec((1, LANES), lambda h, i: (0, 0))],
        out_specs=pl.BlockSpec((t, g * LANES), lambda h, i: (i, h)),
        out_shape=jax.ShapeDtypeStruct((s, DIFF_WIDTH), BF16),
        scratch_shapes=[pltpu.VMEM((2 * g, 1, t), F32), pltpu.VMEM((2 * g, 1, t), F32),
                        pltpu.VMEM((2 * g, LANES, t), F32),
                        pltpu.VMEM((2 * g, t, t), F32), pltpu.VMEM((2 * g, t, t), F32)],
        compiler_params=_cparams(("parallel", "arbitrary")),
        name="diff_attn",
    )(qt, k, vt, lam_vecs, norm_g.reshape(1, LANES))


def _conv_body(bg_ref, cg_ref, xv_ref, cgh_ref, xvh_ref, a_ref, gt_ref, ah_ref, gth_ref,
               scw_ref, dww_ref, dwb_ref, lng_ref, lnb_ref, sc_ref, cf_ref, ext_ref):
    tm = bg_ref.shape[0]
    first = pl.program_id(0) == 0

    shifted_rows = HALO + tm - SUBLANES

    def fill(cur, halo, taps):
        ext_ref[0, 0:HALO, :] = jnp.where(first, 0.0, halo)
        ext_ref[0, HALO:HALO + tm, :] = cur
        for r in sorted({(HALO - (taps - 1) + j) % SUBLANES for j in range(taps)} - {0}):
            ext_ref[r, 0:shifted_rows, :] = ext_ref[0, r:r + shifted_rows, :]

    def conv(w_ref, taps):
        acc = jnp.zeros((tm, ext_ref.shape[2]), F32)
        for j in range(taps):
            off = HALO - (taps - 1) + j
            r = off % SUBLANES
            acc = acc + w_ref[j:j + 1, :] * ext_ref[r, off - r:off - r + tm, :]
        return acc

    fill(cg_ref[...].astype(F32) * xv_ref[...].astype(F32), cgh_ref[...].astype(F32) * xvh_ref[...].astype(F32),
         SC_KERNEL)
    sc_ref[...] = (bg_ref[...].astype(F32) * conv(scw_ref, SC_KERNEL)).astype(sc_ref.dtype)

    fill(a_ref[...].astype(F32) * _sigmoid(gt_ref[...].astype(F32)),
         ah_ref[...].astype(F32) * _sigmoid(gth_ref[...].astype(F32)), CONF_KERNEL)
    u = conv(dww_ref, CONF_KERNEL) + dwb_ref[...]
    mu = jnp.mean(u, axis=-1, keepdims=True)
    var = jnp.mean(jnp.square(u - mu), axis=-1, keepdims=True)
    y = (u - mu) * lax.rsqrt(var + LN_EPS) * lng_ref[...] + lnb_ref[...]
    cf_ref[...] = (y * _sigmoid(y)).astype(cf_ref.dtype)


def _conv_branches(z, sc_w, dw_w, dw_b, ln_g, ln_b):
    s = z.shape[0]
    tm = min(ROW_TILE, s)
    w = SC_WIDTH
    cur = lambda col: pl.BlockSpec((tm, w), lambda i: (i, col // w))
    halo = lambda col: pl.BlockSpec((HALO, w), lambda i: (jnp.maximum(i * (tm // HALO) - 1, 0), col // w))
    full = lambda a: pl.BlockSpec(a.shape, lambda i: (0,) * a.ndim)
    vec = lambda a: a.reshape(1, -1)
    args = [sc_w, dw_w, vec(dw_b), vec(ln_g), vec(ln_b)]
    return pl.pallas_call(
        _conv_body,
        grid=(s // tm,),
        in_specs=[cur(Z_SC), cur(Z_SC + w), cur(Z_SC + 2 * w), halo(Z_SC + w), halo(Z_SC + 2 * w),
                  cur(Z_CONF), cur(Z_CONF + w), halo(Z_CONF), halo(Z_CONF + w)] + [full(a) for a in args],
        out_specs=[pl.BlockSpec((tm, w), lambda i: (i, 0)), pl.BlockSpec((tm, w), lambda i: (i, 0))],
        out_shape=[jax.ShapeDtypeStruct((s, w), BF16), jax.ShapeDtypeStruct((s, w), BF16)],
        scratch_shapes=[pltpu.VMEM((SUBLANES, HALO + tm, w), F32)],
        compiler_params=_cparams(("parallel",)),
        name="conv_branches",
    )(*([z] * 9), *args)


def _merge_body(b0, b1, b2, b3, w0, w1, w2, w3, g0, g1, g2, g3, gb_ref, o_ref):
    acc = None
    for n, (b_ref, w_ref, g_ref) in enumerate(((b0, w0, g0), (b1, w1, g1), (b2, w2, g2), (b3, w3, g3))):
        y = jnp.dot(b_ref[...], w_ref[...], preferred_element_type=F32)
        t = _sigmoid(g_ref[...].astype(F32) + gb_ref[n:n + 1, :]) * y
        acc = t if acc is None else acc + t
    o_ref[...] = acc.astype(o_ref.dtype)


def _merge(branches, w_outs, z, gate_b):
    s = z.shape[0]
    tm = min(MM_TILE, s)
    tn = ROW_TILE
    kw = branches[0].shape[1]
    return pl.pallas_call(
        _merge_body,
        grid=(s // tm, D_MODEL // tn),
        in_specs=([pl.BlockSpec((tm, kw), lambda i, j: (i, 0))] * 4
                  + [pl.BlockSpec((kw, tn), lambda i, j: (0, j))] * 4
                  + [pl.BlockSpec((tm, tn), lambda i, j, b=b: (i, (Z_GATE + b * D_MODEL) // tn + j))
                     for b in range(N_BRANCH)]
                  + [pl.BlockSpec((N_BRANCH, tn), lambda i, j: (0, j))]),
        out_specs=pl.BlockSpec((tm, tn), lambda i, j: (i, j)),
        out_shape=jax.ShapeDtypeStruct((s, D_MODEL), BF16),
        compiler_params=_cparams(("parallel", "arbitrary")),
        name="merge",
    )(*branches, *w_outs, z, z, z, z, gate_b)


def _order(v, i, j):
    v[i], v[j] = jnp.maximum(v[i], v[j]), jnp.minimum(v[i], v[j])


def _bitonic_sort_desc(v):
    n = len(v)
    k = 2
    while k <= n:
        j = k // 2
        while j >= 1:
            for i in range(n):
                partner = i ^ j
                if partner > i:
                    if i & k == 0:
                        _order(v, i, partner)
                    else:
                        _order(v, partner, i)
            j //= 2
        k *= 2


def _bitonic_merge_desc(v):
    n = len(v)
    j = n // 2
    while j >= 1:
        for i in range(n):
            if i ^ j > i:
                _order(v, i, i ^ j)
        j //= 2


def _top16_desc(groups):
    v = list(groups)
    _bitonic_sort_desc(v)
    for shift in (SUBLANES // 2, SUBLANES // 4, SUBLANES // 8):
        other = [pltpu.roll(x, shift, 0) for x in v]
        v = [jnp.maximum(v[k], other[PEER_TOPK - 1 - k]) for k in range(PEER_TOPK)]
        _bitonic_merge_desc(v)
    return v


def _by_sublane(v, start):
    sub = lax.broadcasted_iota(jnp.int32, v[0].shape, 0)
    out = v[start]
    for r in range(1, SUBLANES):
        out = jnp.where(sub == r, v[start + r], out)
    return out


def _peer_candidates(a, a_hi, b_lo, b_hi, b0, op):
    groups = [op(a[0], b_lo), op(a[0], b_hi)]
    groups += [op(a[i], b_lo) for i in range(1, SUBLANES)]
    groups.append(op(a_hi, b0))
    return groups


def _peer_scores_body(q_ref, k1_ref, k2_ref, th_ref, s2_ref, e1_ref, e2n_ref):
    tt = q_ref.shape[0]
    n_groups = PEER_N_KEYS // SUBLANES

    def head(h, carry):
        qh = q_ref[:, pl.ds(pl.multiple_of(h * PEER_KEY_DIM, PEER_KEY_DIM), PEER_KEY_DIM)]
        dims = (((1,), (1,)), ((), ()))
        s1 = lax.dot_general(k1_ref[...], qh, dims, precision=lax.Precision.HIGHEST, preferred_element_type=F32)
        s2 = lax.dot_general(k2_ref[...], qh, dims, precision=lax.Precision.HIGHEST, preferred_element_type=F32)
        split = lambda s: [s[g * SUBLANES:(g + 1) * SUBLANES, :] for g in range(n_groups)]
        t1 = _top16_desc(split(s1))
        t2 = _top16_desc(split(s2))
        t1_hi, t2_lo, t2_hi = _by_sublane(t1, SUBLANES), _by_sublane(t2, 0), _by_sublane(t2, SUBLANES)
        cand = _peer_candidates(t1, t1_hi, t2_lo, t2_hi, t2[0], jnp.add)
        pad = [jnp.full((SUBLANES, tt), -jnp.inf, F32)] * (PEER_TOPK - len(cand))
        tau8 = _top16_desc(cand + pad)[PEER_TOPK - 1]

        m1, m2 = t1[0], t2[0]
        c1 = [jnp.exp(x - m1) for x in t1[:SUBLANES]]
        ec = _peer_candidates(c1, jnp.exp(t1_hi - m1), jnp.exp(t2_lo - m2), jnp.exp(t2_hi - m2),
                              jnp.ones((SUBLANES, tt), F32), jnp.multiply)
        zsum = jnp.zeros((SUBLANES, tt), F32)
        for cg, eg in zip(cand, ec):
            zsum = zsum + jnp.where(cg >= tau8, eg, 0.0)
        zden = jnp.sum(zsum, axis=0, keepdims=True)
        tau = tau8[0:1]
        th = jnp.full(s1.shape, jnp.inf, F32)
        for b in range(PEER_TOPK):
            t2b = t2[b][0:1]
            th = jnp.where(s1 + t2b >= tau, t2b, th)
        th_ref[h] = th
        s2_ref[h] = s2
        e1_ref[h] = jnp.exp(s1 - m1[0:1])
        e2n_ref[h] = jnp.exp(s2 - m2[0:1]) / zden
        return carry

    lax.fori_loop(0, PEER_HEADS, head, 0)


def _peer_scores(q, k1, k2):
    s = q.shape[0]
    tt = min(SCORE_TILE, s)
    tab = pl.BlockSpec((PEER_HEADS, PEER_N_KEYS, tt), lambda i: (0, 0, i))
    tab_shape = jax.ShapeDtypeStruct((PEER_HEADS, PEER_N_KEYS, s), F32)
    return pl.pallas_call(
        _peer_scores_body,
        grid=(s // tt,),
        in_specs=[pl.BlockSpec((tt, PEER_HEADS * PEER_KEY_DIM), lambda i: (i, 0)),
                  pl.BlockSpec(k1.shape, lambda i: (0, 0)), pl.BlockSpec(k2.shape, lambda i: (0, 0))],
        out_specs=[tab, tab, tab, tab],
        out_shape=[tab_shape, tab_shape, tab_shape, tab_shape],
        compiler_params=_cparams(("parallel",)),
        name="peer_scores",
    )(q, k1, k2)


def _peer_items(s, n_items, nb):
    split = lambda it: (it // nb, it % nb)
    return (split(jnp.minimum(s, n_items - 1)), split(jnp.clip(s - 1, 0, n_items - 1)),
            split(jnp.clip(s - 2, 0, n_items - 1)))


def _peer_dense_body(ht_ref, u_ref, v_ref, th_ref, e1_ref, s2_ref, e2n_ref, o_ref, a0, a1, w0, w1,
                     *, n_items, nb):
    s = pl.program_id(0)
    tt = ht_ref.shape[1]
    eb = u_ref.shape[0]
    d = ht_ref.shape[0]
    n_sub = eb // PEER_N_KEYS
    jrows = 2 * SUBLANES
    a_rows = eb // PEER_TRIPS

    @pl.when(s == 0)
    def _():
        for ref in (a0, a1, w0, w1):
            ref[...] = jnp.zeros_like(ref)

    _, _, (_, block_c) = _peer_items(s, n_items, nb)

    @pl.when(block_c == 0)
    def _():
        o_ref[...] = jnp.zeros_like(o_ref)

    def run(a_new, a_old, w_new, w_old):
        def stage_a(k):
            rows = pl.ds(pl.multiple_of(k * a_rows, a_rows), a_rows)
            a_t = jnp.dot(u_ref[rows, :], ht_ref[...], preferred_element_type=F32)
            a_new[rows, :] = 0.5 * a_t * (1.0 + lax.erf(a_t * (2.0 ** -0.5)))

        def stage_b(c, jc):
            cols = pl.ds(pl.multiple_of(c * LANES, LANES), LANES)
            jr = slice(jc * jrows, (jc + 1) * jrows)
            g = [jnp.zeros((jrows, LANES), F32) for _ in range(n_sub)]
            for h in range(PEER_HEADS):
                s2c = s2_ref[h, jr, cols]
                e2c = e2n_ref[h, jr, cols]
                for il in range(n_sub):
                    gate = e2c * e1_ref[h, il:il + 1, cols]
                    g[il] = g[il] + jnp.where(s2c >= th_ref[h, il:il + 1, cols], gate, 0.0)
            for il in range(n_sub):
                rows = slice(il * PEER_N_KEYS + jc * jrows, il * PEER_N_KEYS + (jc + 1) * jrows)
                w_new[rows, cols] = (a_old[rows, cols] * g[il]).astype(w_new.dtype)

        def stage_c(n):
            rows = pl.ds(pl.multiple_of(n * PEER_CHUNK, PEER_CHUNK), PEER_CHUNK)
            o_ref[rows, :] += lax.dot_general(v_ref[:, rows], w_old[...], (((0,), (0,)), ((), ())),
                                              preferred_element_type=F32)

        n_c = d // PEER_CHUNK // PEER_TRIPS
        n_col = tt // LANES // PEER_TRIPS
        assert n_c * PEER_TRIPS * PEER_CHUNK == d and n_col * PEER_TRIPS * LANES == tt and n_c % 2 == 0

        def trip(k, carry):
            stage_a(k)
            for n in range(n_c // 2):
                stage_c(n_c * k + n)
            for c in range(n_col):
                for jc in range(PEER_N_KEYS // jrows):
                    stage_b(n_col * k + c, jc)
            for n in range(n_c // 2, n_c):
                stage_c(n_c * k + n)
            return carry

        lax.fori_loop(0, PEER_TRIPS, trip, 0)

    parity = lax.rem(s, 2)

    @pl.when(parity == 0)
    def _():
        run(a0, a1, w1, w0)

    @pl.when(parity == 1)
    def _():
        run(a1, a0, w0, w1)


def _peer_dense(hn_t, u_all, v_all, layer, th, e1, s2, e2n):
    d, s = hn_t.shape
    tt = min(PEER_TOKENS, s)
    eb = PEER_EXPERTS
    nb = u_all.shape[1] // eb
    n_items = (s // tt) * nb
    items = functools.partial(_peer_items, n_items=n_items, nb=nb)
    sub = pl.BlockSpec((PEER_HEADS, eb // PEER_N_KEYS, tt), lambda i: (0, items(i)[1][1], items(i)[1][0]))
    tab = pl.BlockSpec((PEER_HEADS, PEER_N_KEYS, tt), lambda i: (0, 0, items(i)[1][0]))
    return pl.pallas_call(
        functools.partial(_peer_dense_body, n_items=n_items, nb=nb),
        grid=(n_items + 2,),
        in_specs=[pl.BlockSpec((d, tt), lambda i: (0, items(i)[0][0])),
                  pl.BlockSpec((None, eb, d), lambda i: (layer, items(i)[0][1], 0)),
                  pl.BlockSpec((None, eb, d), lambda i: (layer, items(i)[2][1], 0)),
                  sub, sub, tab, tab],
        out_specs=pl.BlockSpec((d, tt), lambda i: (0, items(i)[2][0])),
        out_shape=jax.ShapeDtypeStruct((d, s), F32),
        scratch_shapes=[pltpu.VMEM((eb, tt), F32), pltpu.VMEM((eb, tt), F32),
                        pltpu.VMEM((eb, tt), BF16), pltpu.VMEM((eb, tt), BF16)],
        compiler_params=_cparams(("arbitrary",)),
        name="peer_dense",
    )(hn_t, u_all, v_all, th, e1, s2, e2n)


def _add_norm_body(x_ref, dt_ref, g_ref, *out_refs):
    x = x_ref[...] + dt_ref[...].T
    y = x * lax.rsqrt(jnp.mean(x * x, axis=-1, keepdims=True) + NORM_EPS)
    out_refs[-1][...] = (y * g_ref[...]).astype(out_refs[-1].dtype)
    if len(out_refs) == 2:
        out_refs[0][...] = x


def _add_norm(x, delta_t, g, norm_dtype, with_sum):
    s, d = x.shape
    tm = min(ROW_TILE, s)
    row = pl.BlockSpec((tm, d), lambda i: (i, 0))
    out_specs = [row, row] if with_sum else [row]
    out_shape = [jax.ShapeDtypeStruct((s, d), norm_dtype)]
    if with_sum:
        out_shape.insert(0, jax.ShapeDtypeStruct((s, d), F32))
    return pl.pallas_call(
        _add_norm_body,
        grid=(s // tm,),
        in_specs=[row, pl.BlockSpec((d, tm), lambda i: (0, i)), pl.BlockSpec((1, d), lambda i: (0, 0))],
        out_specs=out_specs,
        out_shape=out_shape,
        compiler_params=_cparams(("parallel",)),
        name="add_norm",
    )(x, delta_t, g.reshape(1, d))


def _rope_tables(positions, dim, lanes_x1):
    half = dim // 2
    inv_freq = ROPE_THETA ** (-jnp.arange(0, dim, 2, dtype=F32) / dim)
    ang = positions.astype(F32)[:, None] * inv_freq
    cos, sin = jnp.cos(ang), jnp.sin(ang)
    s = positions.shape[0]
    c_parts, su_parts, sd_parts = [], [], []
    pos = 0
    for off in lanes_x1:
        gap = off - pos
        c_parts += [jnp.ones((s, gap), F32), cos, cos]
        su_parts += [jnp.zeros((s, gap), F32), -sin, jnp.zeros((s, half), F32)]
        sd_parts += [jnp.zeros((s, gap + half), F32), sin]
        pos = off + dim
    tail = LANES - pos
    c_parts.append(jnp.ones((s, tail), F32))
    su_parts.append(jnp.zeros((s, tail), F32))
    sd_parts.append(jnp.zeros((s, tail), F32))
    return tuple(jnp.concatenate(p, axis=1) for p in (c_parts, su_parts, sd_parts))


W_IN_HEAD = IN_OFFSETS[3]
W_IN_RUN = sum(IN_SIZES[3:])
LAYOUT_ROWS = 1024


def _in_proj_body(h_ref, a_ref, b_ref, o_ref, wt_ref):
    j = pl.program_id(0)
    rb = wt_ref.shape[0]
    lead = rb - W_IN_HEAD % rb
    new_block = pl.program_id(1) == 0

    @pl.when(new_block & (j < W_IN_RUN // rb))
    def _():
        wt_ref[0:lead, :] = a_ref[W_IN_HEAD % rb:rb, :].astype(wt_ref.dtype)
        wt_ref[lead:rb, :] = b_ref[0:rb - lead, :].astype(wt_ref.dtype)

    @pl.when(new_block & (j == W_IN_RUN // rb))
    def _():
        head = MLA_Q_LORA + MLA_KV_LORA
        wt_ref[0:head, :] = a_ref[0:head, :].astype(wt_ref.dtype)
        wt_ref[head:rb, :] = jnp.zeros((rb - head, wt_ref.shape[1]), wt_ref.dtype)
        wt_ref[head + MLA_NOPE:head + MLA_NOPE + MLA_ROPE, :] = a_ref[head:head + MLA_ROPE, :].astype(wt_ref.dtype)

    o_ref[...] = lax.dot_general(h_ref[...], wt_ref[...], (((1,), (1,)), ((), ())),
                                 preferred_element_type=F32).astype(o_ref.dtype)


def _in_proj(hn, w_in_t, layer):
    s, d = hn.shape
    n = w_in_t.shape[1]
    rb = LAYOUT_ROWS
    tm = min(MM_TILE, s)
    assert W_IN_RUN % rb == 0 and Z_CQ == W_IN_RUN and Z_COLS == W_IN_RUN + rb and W_IN_HEAD < rb
    run_blocks = W_IN_RUN // rb
    first = lambda j: jnp.where(j < run_blocks, j + W_IN_HEAD // rb, 0)
    second = lambda j: jnp.minimum(j + W_IN_HEAD // rb + 1, (n - 1) // rb)
    return pl.pallas_call(
        _in_proj_body,
        grid=(Z_COLS // rb, s // tm),
        in_specs=[pl.BlockSpec((tm, d), lambda j, i: (i, 0)),
                  pl.BlockSpec((None, rb, d), lambda j, i: (layer, first(j), 0)),
                  pl.BlockSpec((None, rb, d), lambda j, i: (layer, second(j), 0))],
        out_specs=pl.BlockSpec((tm, rb), lambda j, i: (i, j)),
        out_shape=jax.ShapeDtypeStruct((s, Z_COLS), BF16),
        scratch_shapes=[pltpu.VMEM((rb, d), BF16)],
        compiler_params=_cparams(("parallel", "arbitrary")),
        name="in_proj",
    )(hn, w_in_t, w_in_t)


def _layout_mla_weights(w_uq, w_ukv):
    lq = w_uq.shape[0]
    wq = w_uq.reshape(lq, MLA_HEADS, MLA_NOPE + MLA_ROPE)
    wq = jnp.pad(wq, ((0, 0), (0, 0), (0, LANES - MLA_NOPE - MLA_ROPE))).reshape(lq, MLA_HEADS * LANES)
    lk = w_ukv.shape[0]
    wkv = w_ukv.reshape(lk, MLA_HEADS, MLA_NOPE + MLA_V)
    wk = jnp.pad(wkv[:, :, :MLA_NOPE], ((0, 0), (0, 0), (0, LANES - MLA_NOPE))).reshape(lk, MLA_HEADS * LANES)
    wv = wkv[:, :, MLA_NOPE:].reshape(lk, MLA_HEADS * MLA_V)
    return wq.T.astype(BF16), wk.astype(BF16), wv.T.astype(BF16)


def kernel(x, positions, mix_norm_g, w_in, gate_b, mla_q_norm_g, mla_w_uq, mla_kv_norm_g, mla_w_ukv, mla_w_out,
           sc_conv_w, sc_w_out, diff_lambda, diff_norm_g, diff_w_out, conf_dw_w, conf_dw_b, conf_ln_g, conf_ln_b,
           conf_w_out, w_o, ffn_norm_g, peer_w_q, peer_sub_keys, peer_u, peer_v, final_norm_g):
    b, s, d = x.shape
    assert b == 1 and d == D_MODEL
    xs = x.reshape(s, d)
    pos = positions.reshape(s)
    mla_tabs = _rope_tables(pos, MLA_ROPE, (MLA_NOPE,))
    mla_tabs_t = tuple(t.T for t in mla_tabs)
    w_in_t = jnp.swapaxes(w_in, 1, 2)
    u_all = peer_u.astype(BF16)
    v_all = peer_v.astype(BF16)
    diff_tabs = _rope_tables(pos, DIFF_ROT, (0, DIFF_HEAD_DIM))

    for i in range(DEPTH):
        lam_init = 0.8 - 0.6 * math.exp(-0.3 * i)
        if i == 0:
            hn = _rmsnorm(xs, mix_norm_g[i], BF16)
        else:
            xs, hn = _add_norm(xs, delta_t, mix_norm_g[i], BF16, with_sum=True)
        z = _in_proj(hn, w_in_t, i)
        wqt, wk, wvt = _layout_mla_weights(mla_w_uq[i], mla_w_ukv[i])
        qt, k, vt = _mla_prep(z, mla_tabs, mla_tabs_t, mla_q_norm_g[i].reshape(1, -1),
                              mla_kv_norm_g[i].reshape(1, -1), wqt, wk, wvt)
        o_mla = _mla_attn(qt, k, vt)
        o_sc, o_conf = _conv_branches(z, sc_conv_w[i], conf_dw_w[i], conf_dw_b[i], conf_ln_g[i], conf_ln_b[i])
        qdt, kd, vdt = _diff_prep(z, diff_tabs)
        o_diff = _diff_attn(qdt, kd, vdt, diff_lambda[i], diff_norm_g[i], lam_init)
        merged = _merge((o_mla, o_sc, o_diff, o_conf),
                        tuple(w.astype(BF16) for w in (mla_w_out[i], sc_w_out[i], diff_w_out[i], conf_w_out[i])),
                        z, gate_b[i])
        xs = _matmul(merged, w_o[i].astype(BF16), F32, residual=xs, name="out_proj")
        hf, hf_t = _rmsnorm(xs, ffn_norm_g[i], BF16, with_transposed=True)
        pq = _matmul(hf, peer_w_q[i].astype(BF16), F32, name="peer_q")
        zk = jnp.zeros((PEER_N_KEYS, PEER_HALF), F32)
        k1 = jnp.concatenate([peer_sub_keys[i, 0], zk], axis=1)
        k2 = jnp.concatenate([zk, peer_sub_keys[i, 1]], axis=1)
        th, s2, e1, e2n = _peer_scores(pq, k1, k2)
        delta_t = _peer_dense(hf_t, u_all, v_all, i, th, e1, s2, e2n)
    (out,) = _add_norm(xs, delta_t, final_norm_g, F32, with_sum=False)
    return out.reshape(b, s, d)
```

```python
import functools
import math

import jax
import jax.numpy as jnp
import numpy as np
from jax import lax
from jax.experimental import pallas as pl
from jax.experimental.pallas import tpu as pltpu

F32 = jnp.float32
BF16 = jnp.bfloat16

D_MODEL = 2048
DEPTH = 2
CHUNK = 64
ROPE_THETA = 500000.0
NORM_EPS = 1e-6
LN_EPS = 1e-5
NEG_INF = -1e30
N_BRANCH = 4

MLA_HEADS = 8
MLA_Q_LORA = 512
MLA_KV_LORA = 256
MLA_NOPE = 64
MLA_ROPE = 32
MLA_V = 64
MLA_SCALE = (MLA_NOPE + MLA_ROPE) ** -0.5

SC_WIDTH = 512
SC_KERNEL = 3

DIFF_HEADS = 4
DIFF_HEAD_DIM = 64
DIFF_ROT = DIFF_HEAD_DIM // 4
DIFF_SCALE = DIFF_HEAD_DIM ** -0.5
DIFF_WIDTH = DIFF_HEADS * 2 * DIFF_HEAD_DIM

CONF_WIDTH = 512
CONF_KERNEL = 31

PEER_HEADS = 8
PEER_N_KEYS = 128
PEER_N_EXPERTS = PEER_N_KEYS * PEER_N_KEYS
PEER_KEY_DIM = 128
PEER_HALF = PEER_KEY_DIM // 2
PEER_TOPK = 16

IN_SIZES = (MLA_Q_LORA, MLA_KV_LORA, MLA_ROPE, 3 * SC_WIDTH, DIFF_WIDTH, DIFF_WIDTH, DIFF_WIDTH,
            2 * CONF_WIDTH, N_BRANCH * D_MODEL)
IN_OFFSETS = tuple(int(v) for v in np.cumsum((0,) + IN_SIZES)[:-1])

LOG2E = math.log2(math.e)

LANES = 128
SUBLANES = 8
HALO = 32
VMEM_LIMIT = 56 * 1024 * 1024

ROW_TILE = 512
MM_TILE = 1024
ATTN_TILE = 512
MLA_GROUP = 4
DIFF_GROUP = 4
SCORE_TILE = 512
PEER_TOKENS = 512
PEER_EXPERTS = 1024
PEER_CHUNK = 256
PEER_TRIPS = 1

Z_SC = 0
Z_DQ = 1536
Z_DK = 2048
Z_DV = 2560
Z_CONF = 3072
Z_GATE = 4096
Z_CQ = 12288
Z_CKV = 12800
Z_KR = 13056
Z_COLS = 13312


def _cparams(sem):
    return pltpu.CompilerParams(dimension_semantics=sem, vmem_limit_bytes=VMEM_LIMIT)


def _rmsnorm_body(x_ref, g_ref, o_ref, *maybe_ot_ref):
    x = x_ref[...].astype(F32)
    y = x * lax.rsqrt(jnp.mean(x * x, axis=-1, keepdims=True) + NORM_EPS) * g_ref[...]
    o_ref[...] = y.astype(o_ref.dtype)
    for ot_ref in maybe_ot_ref:
        ot_ref[...] = y.T.astype(ot_ref.dtype)


def _rmsnorm(x, g, out_dtype, with_transposed=False):
    s, d = x.shape
    tm = min(ROW_TILE, s)
    out_specs = [pl.BlockSpec((tm, d), lambda i: (i, 0))]
    out_shape = [jax.ShapeDtypeStruct((s, d), out_dtype)]
    if with_transposed:
        out_specs.append(pl.BlockSpec((d, tm), lambda i: (0, i)))
        out_shape.append(jax.ShapeDtypeStruct((d, s), out_dtype))
    out = pl.pallas_call(
        _rmsnorm_body,
        grid=(s // tm,),
        in_specs=[pl.BlockSpec((tm, d), lambda i: (i, 0)), pl.BlockSpec((1, d), lambda i: (0, 0))],
        out_specs=out_specs,
        out_shape=out_shape,
        compiler_params=_cparams(("parallel",)),
        name="rmsnorm",
    )(x, g.reshape(1, d))
    return out if with_transposed else out[0]


def _matmul_body(a_ref, b_ref, o_ref):
    o_ref[...] = jnp.dot(a_ref[...], b_ref[...], preferred_element_type=F32).astype(o_ref.dtype)


def _matmul_res_body(a_ref, b_ref, r_ref, o_ref):
    o_ref[...] = (r_ref[...] + jnp.dot(a_ref[...], b_ref[...], preferred_element_type=F32)).astype(o_ref.dtype)


def _matmul(a, b, out_dtype, residual=None, name="matmul"):
    m, k = a.shape
    n = b.shape[1]
    tm, tn = min(MM_TILE, m), min(MM_TILE, n)
    in_specs = [pl.BlockSpec((tm, k), lambda i, j: (i, 0)), pl.BlockSpec((k, tn), lambda i, j: (0, j))]
    args = [a, b]
    body = _matmul_body
    if residual is not None:
        in_specs.append(pl.BlockSpec((tm, tn), lambda i, j: (i, j)))
        args.append(residual)
        body = _matmul_res_body
    return pl.pallas_call(
        body,
        grid=(m // tm, n // tn),
        in_specs=in_specs,
        out_specs=pl.BlockSpec((tm, tn), lambda i, j: (i, j)),
        out_shape=jax.ShapeDtypeStruct((m, n), out_dtype),
        compiler_params=_cparams(("parallel", "arbitrary")),
        name=name,
    )(*args)


def _sigmoid(x):
    return 0.5 + 0.5 * jnp.tanh(0.5 * x)


def _rope128(x, c, s_up, s_dn, half):
    return x * c + pltpu.roll(x, LANES - half, 1) * s_up + pltpu.roll(x, half, 1) * s_dn


def _mla_prep_body(cq_ref, ckv_ref, kr_ref, c_ref, su_ref, sd_ref, ct_ref, sut_ref, sdt_ref,
                   qg_ref, kvg_ref, wqt_ref, wk_ref, wvt_ref, qt_ref, k_ref, vt_ref):
    half = MLA_ROPE // 2
    nt = (((1,), (1,)), ((), ()))

    def norm(x_ref, g_ref):
        x = x_ref[...].astype(F32)
        y = x * lax.rsqrt(jnp.mean(x * x, axis=-1, keepdims=True) + NORM_EPS)
        return (y * g_ref[...]).astype(BF16)

    qf_t = lax.dot_general(wqt_ref[...], norm(cq_ref, qg_ref), nt, preferred_element_type=F32)
    ckvn = norm(ckv_ref, kvg_ref)
    kf = jnp.dot(ckvn, wk_ref[...], preferred_element_type=F32)
    vt_ref[...] = lax.dot_general(wvt_ref[...], ckvn, nt, preferred_element_type=F32).astype(vt_ref.dtype)
    kr = _rope128(kr_ref[...].astype(F32), c_ref[...], su_ref[...], sd_ref[...], half)
    ct, sut, sdt = ct_ref[...], sut_ref[...], sdt_ref[...]
    for h in range(MLA_HEADS):
        sl = slice(h * LANES, (h + 1) * LANES)
        x = qf_t[sl, :]
        q = x * ct + pltpu.roll(x, LANES - half, 0) * sut + pltpu.roll(x, half, 0) * sdt
        qt_ref[sl, :] = (q * (MLA_SCALE * LOG2E)).astype(qt_ref.dtype)
        k_ref[:, sl] = (kf[:, sl] + kr).astype(k_ref.dtype)


def _mla_prep(z, tabs, tabs_t, qg, kvg, wqt, wk, wvt):
    s = z.shape[0]
    tm = min(ROW_TILE, s)
    row = lambda w, idx: pl.BlockSpec((tm, w), lambda i: (i, idx))
    full = lambda a: pl.BlockSpec(a.shape, lambda i: (0,) * a.ndim)
    tab = pl.BlockSpec((tm, LANES), lambda i: (i, 0))
    tab_t = pl.BlockSpec((LANES, tm), lambda i: (0, i))
    hw = MLA_HEADS * LANES
    vw = MLA_HEADS * MLA_V
    return pl.pallas_call(
        _mla_prep_body,
        grid=(s // tm,),
        in_specs=[row(MLA_Q_LORA, Z_CQ // MLA_Q_LORA), row(MLA_KV_LORA, Z_CKV // MLA_KV_LORA),
                  row(LANES, Z_KR // LANES), tab, tab, tab, tab_t, tab_t, tab_t,
                  full(qg), full(kvg), full(wqt), full(wk), full(wvt)],
        out_specs=[pl.BlockSpec((hw, tm), lambda i: (0, i)), pl.BlockSpec((tm, hw), lambda i: (i, 0)),
                   pl.BlockSpec((vw, tm), lambda i: (0, i))],
        out_shape=[jax.ShapeDtypeStruct((hw, s), BF16), jax.ShapeDtypeStruct((s, hw), BF16),
                   jax.ShapeDtypeStruct((vw, s), BF16)],
        compiler_params=_cparams(("parallel",)),
        name="mla_prep",
    )(z, z, z, *tabs, *tabs_t, qg, kvg, wqt, wk, wvt)


def _flash_streams_t(q_ts, k_ats, vt_ats, m_ref, l_ref, acc_ref, s_a, s_b, n_full, t):
    streams = range(len(q_ts))
    for n in streams:
        m_ref[n] = jnp.full((1, t), NEG_INF, F32)
        l_ref[n] = jnp.zeros((1, t), F32)
        acc_ref[n] = jnp.zeros(acc_ref.shape[1:], F32)

    def scores_into(s_ref, j):
        for n in streams:
            s_ref[n] = jnp.dot(k_ats[n](j), q_ts[n], preferred_element_type=F32)

    def consume(s_ref, j, masked):
        probs, alphas = [], []
        for n in streams:
            s = s_ref[n]
            if masked:
                key_chunk = lax.broadcasted_iota(jnp.int32, (t, t), 0) // CHUNK
                qry_chunk = lax.broadcasted_iota(jnp.int32, (t, t), 1) // CHUNK
                s = jnp.where(key_chunk <= qry_chunk, s, NEG_INF)
            m_prev = m_ref[n]
            m_new = jnp.maximum(m_prev, jnp.max(s, axis=0, keepdims=True))
            alpha = jnp.exp2(m_prev - m_new)
            p = jnp.exp2(s - m_new)
            l_ref[n] = alpha * l_ref[n] + jnp.sum(p, axis=0, keepdims=True)
            m_ref[n] = m_new
            probs.append(p.astype(BF16))
            alphas.append(alpha)
        for n in streams:
            acc_ref[n] = alphas[n] * acc_ref[n] + jnp.dot(vt_ats[n](j), probs[n], preferred_element_type=F32)

    scores_into(s_a, 0)

    def pair(jj, carry):
        j = 2 * jj
        scores_into(s_b, j + 1)
        consume(s_a, j, False)
        scores_into(s_a, j + 2)
        consume(s_b, j + 1, False)
        return carry

    lax.fori_loop(0, n_full // 2, pair, 0)

    @pl.when(n_full % 2 == 0)
    def _():
        consume(s_a, n_full, True)

    @pl.when(n_full % 2 == 1)
    def _():
        scores_into(s_b, n_full)
        consume(s_a, n_full - 1, False)
        consume(s_b, n_full, True)


def _mla_attn_body(qt_ref, k_ref, vt_ref, o_ref, m_ref, l_ref, acc_ref, s_a, s_b, *, t):
    i = pl.program_id(1)
    rows = lambda j: pl.ds(pl.multiple_of(j * t, t), t)
    heads = range(MLA_GROUP)
    lanes = [slice(hh * LANES, (hh + 1) * LANES) for hh in heads]
    k_ats = [lambda j, sl=sl: k_ref[rows(j), sl] for sl in lanes]
    vt_ats = [lambda j, hh=hh: vt_ref[hh * MLA_V:(hh + 1) * MLA_V, rows(j)] for hh in heads]
    _flash_streams_t([qt_ref[sl, :] for sl in lanes], k_ats, vt_ats, m_ref, l_ref, acc_ref, s_a, s_b, i, t)
    o_t = jnp.concatenate([acc_ref[hh] / l_ref[hh] for hh in heads], axis=0)
    o_ref[...] = o_t.T.astype(o_ref.dtype)


def _mla_attn(qt, k, vt):
    s = k.shape[0]
    t = min(ATTN_TILE, s)
    g = MLA_GROUP
    return pl.pallas_call(
        functools.partial(_mla_attn_body, t=t),
        grid=(MLA_HEADS // g, s // t),
        in_specs=[pl.BlockSpec((g * LANES, t), lambda p, i: (p, i)),
                  pl.BlockSpec((s, g * LANES), lambda p, i: (0, p)),
                  pl.BlockSpec((g * MLA_V, s), lambda p, i: (p, 0))],
        out_specs=pl.BlockSpec((t, g * MLA_V), lambda p, i: (i, p)),
        out_shape=jax.ShapeDtypeStruct((s, MLA_HEADS * MLA_V), BF16),
        scratch_shapes=[pltpu.VMEM((g, 1, t), F32), pltpu.VMEM((g, 1, t), F32),
                        pltpu.VMEM((g, MLA_V, t), F32),
                        pltpu.VMEM((g, t, t), F32), pltpu.VMEM((g, t, t), F32)],
        compiler_params=_cparams(("parallel", "arbitrary")),
        name="mla_attn",
    )(qt, k, vt)


def _diff_prep_body(dq_ref, dk_ref, dv_ref, c_ref, su_ref, sd_ref, qt_ref, k_ref, vt_ref):
    half = DIFF_ROT // 2
    c, su, sd = c_ref[...], su_ref[...], sd_ref[...]
    tm = dq_ref.shape[0]
    lane = lax.broadcasted_iota(jnp.int32, (tm, LANES), 1)
    vt_ref[...] = dv_ref[...].astype(F32).T.astype(vt_ref.dtype)
    for h in range(DIFF_HEADS):
        sl = slice(h * LANES, (h + 1) * LANES)
        q = _rope128(dq_ref[:, sl].astype(F32), c, su, sd, half) * (DIFF_SCALE * LOG2E)
        qt_ref[(2 * h) * LANES:(2 * h + 1) * LANES, :] = jnp.where(lane < DIFF_HEAD_DIM, q, 0.0).T.astype(qt_ref.dtype)
        qt_ref[(2 * h + 1) * LANES:(2 * h + 2) * LANES, :] = jnp.where(lane >= DIFF_HEAD_DIM, q, 0.0).T.astype(qt_ref.dtype)
        k_ref[:, sl] = _rope128(dk_ref[:, sl].astype(F32), c, su, sd, half).astype(k_ref.dtype)


def _diff_prep(z, tabs):
    s = z.shape[0]
    tm = min(ROW_TILE, s)
    tab = pl.BlockSpec((tm, LANES), lambda i: (i, 0))
    col = lambda start: pl.BlockSpec((tm, DIFF_WIDTH), lambda i: (i, start // DIFF_WIDTH))
    return pl.pallas_call(
        _diff_prep_body,
        grid=(s // tm,),
        in_specs=[col(Z_DQ), col(Z_DK), col(Z_DV), tab, tab, tab],
        out_specs=[pl.BlockSpec((2 * DIFF_WIDTH, tm), lambda i: (0, i)),
                   pl.BlockSpec((tm, DIFF_WIDTH), lambda i: (i, 0)),
                   pl.BlockSpec((DIFF_WIDTH, tm), lambda i: (0, i))],
        out_shape=[jax.ShapeDtypeStruct((2 * DIFF_WIDTH, s), BF16), jax.ShapeDtypeStruct((s, DIFF_WIDTH), BF16),
                   jax.ShapeDtypeStruct((DIFF_WIDTH, s), BF16)],
        compiler_params=_cparams(("parallel",)),
        name="diff_prep",
    )(z, z, z, *tabs)


def _diff_attn_body(qt_ref, k_ref, vt_ref, lam_ref, g_ref, o_ref, m_ref, l_ref, acc_ref, s_a, s_b, *, t, lam_init):
    i = pl.program_id(1)
    rows = lambda j: pl.ds(pl.multiple_of(j * t, t), t)
    q_ts, k_ats, vt_ats = [], [], []
    for hh in range(DIFF_GROUP):
        sl = slice(hh * LANES, (hh + 1) * LANES)
        for c in range(2):
            q_ts.append(qt_ref[(2 * hh + c) * LANES:(2 * hh + c + 1) * LANES, :])
            k_ats.append(lambda j, sl=sl: k_ref[rows(j), sl])
            vt_ats.append(lambda j, sl=sl: vt_ref[sl, rows(j)])
    _flash_streams_t(q_ts, k_ats, vt_ats, m_ref, l_ref, acc_ref, s_a, s_b, i, t)
    lv = lam_ref[...]
    lam = (jnp.exp(jnp.sum(lv[0:1] * lv[1:2], axis=1, keepdims=True))
           - jnp.exp(jnp.sum(lv[2:3] * lv[3:4], axis=1, keepdims=True)) + lam_init)
    for hh in range(DIFF_GROUP):
        o = (acc_ref[2 * hh] / l_ref[2 * hh] - lam * (acc_ref[2 * hh + 1] / l_ref[2 * hh + 1])).T
        o = o * lax.rsqrt(jnp.mean(o * o, axis=-1, keepdims=True) + NORM_EPS)
        o_ref[:, hh * LANES:(hh + 1) * LANES] = (o * g_ref[...] * (1.0 - lam_init)).astype(o_ref.dtype)


def _diff_attn(qt, k, vt, lam_vecs, norm_g, lam_init):
    s = k.shape[0]
    t = min(ATTN_TILE, s)
    g = DIFF_GROUP
    return pl.pallas_call(
        functools.partial(_diff_attn_body, t=t, lam_init=lam_init),
        grid=(DIFF_HEADS // g, s // t),
        in_specs=[pl.BlockSpec((2 * g * LANES, t), lambda h, i: (h, i)),
                  pl.BlockSpec((s, g * LANES), lambda h, i: (0, h)),
                  pl.BlockSpec((g * LANES, s), lambda h, i: (h, 0)),
                  pl.BlockSpec(lam_vecs.shape, lambda h, i: (0, 0)),
                  pl.BlockSpec((1, LANES), lambda h, i: (0, 0))],
        out_specs=pl.BlockSpec((t, g * LANES), lambda h, i: (i, h)),
        out_shape=jax.ShapeDtypeStruct((s, DIFF_WIDTH), BF16),
        scratch_shapes=[pltpu.VMEM((2 * g, 1, t), F32), pltpu.VMEM((2 * g, 1, t), F32),
                        pltpu.VMEM((2 * g, LANES, t), F32),
                        pltpu.VMEM((2 * g, t, t), F32), pltpu.VMEM((2 * g, t, t), F32)],
        compiler_params=_cparams(("parallel", "arbitrary")),
        name="diff_attn",
    )(qt, k, vt, lam_vecs, norm_g.reshape(1, LANES))


def _conv_body(bg_ref, cg_ref, xv_ref, cgh_ref, xvh_ref, a_ref, gt_ref, ah_ref, gth_ref,
               scw_ref, dww_ref, dwb_ref, lng_ref, lnb_ref, sc_ref, cf_ref, ext_ref):
    tm = bg_ref.shape[0]
    first = pl.program_id(0) == 0

    shifted_rows = HALO + tm - SUBLANES

    def fill(cur, halo, taps):
        ext_ref[0, 0:HALO, :] = jnp.where(first, 0.0, halo)
        ext_ref[0, HALO:HALO + tm, :] = cur
        for r in sorted({(HALO - (taps - 1) + j) % SUBLANES for j in range(taps)} - {0}):
            ext_ref[r, 0:shifted_rows, :] = ext_ref[0, r:r + shifted_rows, :]

    def conv(w_ref, taps):
        acc = jnp.zeros((tm, ext_ref.shape[2]), F32)
        for j in range(taps):
            off = HALO - (taps - 1) + j
            r = off % SUBLANES
            acc = acc + w_ref[j:j + 1, :] * ext_ref[r, off - r:off - r + tm, :]
        return acc

    fill(cg_ref[...].astype(F32) * xv_ref[...].astype(F32), cgh_ref[...].astype(F32) * xvh_ref[...].astype(F32),
         SC_KERNEL)
    sc_ref[...] = (bg_ref[...].astype(F32) * conv(scw_ref, SC_KERNEL)).astype(sc_ref.dtype)

    fill(a_ref[...].astype(F32) * _sigmoid(gt_ref[...].astype(F32)),
         ah_ref[...].astype(F32) * _sigmoid(gth_ref[...].astype(F32)), CONF_KERNEL)
    u = conv(dww_ref, CONF_KERNEL) + dwb_ref[...]
    mu = jnp.mean(u, axis=-1, keepdims=True)
    var = jnp.mean(jnp.square(u - mu), axis=-1, keepdims=True)
    y = (u - mu) * lax.rsqrt(var + LN_EPS) * lng_ref[...] + lnb_ref[...]
    cf_ref[...] = (y * _sigmoid(y)).astype(cf_ref.dtype)


def _conv_branches(z, sc_w, dw_w, dw_b, ln_g, ln_b):
    s = z.shape[0]
    tm = min(ROW_TILE, s)
    w = SC_WIDTH
    cur = lambda col: pl.BlockSpec((tm, w), lambda i: (i, col // w))
    halo = lambda col: pl.BlockSpec((HALO, w), lambda i: (jnp.maximum(i * (tm // HALO) - 1, 0), col // w))
    full = lambda a: pl.BlockSpec(a.shape, lambda i: (0,) * a.ndim)
    vec = lambda a: a.reshape(1, -1)
    args = [sc_w, dw_w, vec(dw_b), vec(ln_g), vec(ln_b)]
    return pl.pallas_call(
        _conv_body,
        grid=(s // tm,),
        in_specs=[cur(Z_SC), cur(Z_SC + w), cur(Z_SC + 2 * w), halo(Z_SC + w), halo(Z_SC + 2 * w),
                  cur(Z_CONF), cur(Z_CONF + w), halo(Z_CONF), halo(Z_CONF + w)] + [full(a) for a in args],
        out_specs=[pl.BlockSpec((tm, w), lambda i: (i, 0)), pl.BlockSpec((tm, w), lambda i: (i, 0))],
        out_shape=[jax.ShapeDtypeStruct((s, w), BF16), jax.ShapeDtypeStruct((s, w), BF16)],
        scratch_shapes=[pltpu.VMEM((SUBLANES, HALO + tm, w), F32)],
        compiler_params=_cparams(("parallel",)),
        name="conv_branches",
    )(*([z] * 9), *args)


def _merge_body(b0, b1, b2, b3, w0, w1, w2, w3, g0, g1, g2, g3, gb_ref, o_ref):
    acc = None
    for n, (b_ref, w_ref, g_ref) in enumerate(((b0, w0, g0), (b1, w1, g1), (b2, w2, g2), (b3, w3, g3))):
        y = jnp.dot(b_ref[...], w_ref[...], preferred_element_type=F32)
        t = _sigmoid(g_ref[...].astype(F32) + gb_ref[n:n + 1, :]) * y
        acc = t if acc is None else acc + t
    o_ref[...] = acc.astype(o_ref.dtype)


def _merge(branches, w_outs, z, gate_b):
    s = z.shape[0]
    tm = min(MM_TILE, s)
    tn = ROW_TILE
    kw = branches[0].shape[1]
    return pl.pallas_call(
        _merge_body,
        grid=(s // tm, D_MODEL // tn),
        in_specs=([pl.BlockSpec((tm, kw), lambda i, j: (i, 0))] * 4
                  + [pl.BlockSpec((kw, tn), lambda i, j: (0, j))] * 4
                  + [pl.BlockSpec((tm, tn), lambda i, j, b=b: (i, (Z_GATE + b * D_MODEL) // tn + j))
                     for b in range(N_BRANCH)]
                  + [pl.BlockSpec((N_BRANCH, tn), lambda i, j: (0, j))]),
        out_specs=pl.BlockSpec((tm, tn), lambda i, j: (i, j)),
        out_shape=jax.ShapeDtypeStruct((s, D_MODEL), BF16),
        compiler_params=_cparams(("parallel", "arbitrary")),
        name="merge",
    )(*branches, *w_outs, z, z, z, z, gate_b)


def _order(v, i, j):
    v[i], v[j] = jnp.maximum(v[i], v[j]), jnp.minimum(v[i], v[j])


def _bitonic_sort_desc(v):
    n = len(v)
    k = 2
    while k <= n:
        j = k // 2
        while j >= 1:
            for i in range(n):
                partner = i ^ j
                if partner > i:
                    if i & k == 0:
                        _order(v, i, partner)
                    else:
                        _order(v, partner, i)
            j //= 2
        k *= 2


def _bitonic_merge_desc(v):
    n = len(v)
    j = n // 2
    while j >= 1:
        for i in range(n):
            if i ^ j > i:
                _order(v, i, i ^ j)
        j //= 2


def _top16_desc(groups):
    v = list(groups)
    _bitonic_sort_desc(v)
    for shift in (SUBLANES // 2, SUBLANES // 4, SUBLANES // 8):
        other = [pltpu.roll(x, shift, 0) for x in v]
        v = [jnp.maximum(v[k], other[PEER_TOPK - 1 - k]) for k in range(PEER_TOPK)]
        _bitonic_merge_desc(v)
    return v


def _by_sublane(v, start):
    sub = lax.broadcasted_iota(jnp.int32, v[0].shape, 0)
    out = v[start]
    for r in range(1, SUBLANES):
        out = jnp.where(sub == r, v[start + r], out)
    return out


def _peer_candidates(a, a_hi, b_lo, b_hi, b0, op):
    groups = [op(a[0], b_lo), op(a[0], b_hi)]
    groups += [op(a[i], b_lo) for i in range(1, SUBLANES)]
    groups.append(op(a_hi, b0))
    return groups


def _peer_scores_body(q_ref, k1_ref, k2_ref, th_ref, s2_ref, e1_ref, e2n_ref):
    tt = q_ref.shape[0]
    n_groups = PEER_N_KEYS // SUBLANES

    def head(h, carry):
        qh = q_ref[:, pl.ds(pl.multiple_of(h * PEER_KEY_DIM, PEER_KEY_DIM), PEER_KEY_DIM)]
        dims = (((1,), (1,)), ((), ()))
        s1 = lax.dot_general(k1_ref[...], qh, dims, precision=lax.Precision.HIGHEST, preferred_element_type=F32)
        s2 = lax.dot_general(k2_ref[...], qh, dims, precision=lax.Precision.HIGHEST, preferred_element_type=F32)
        split = lambda s: [s[g * SUBLANES:(g + 1) * SUBLANES, :] for g in range(n_groups)]
        t1 = _top16_desc(split(s1))
        t2 = _top16_desc(split(s2))
        t1_hi, t2_lo, t2_hi = _by_sublane(t1, SUBLANES), _by_sublane(t2, 0), _by_sublane(t2, SUBLANES)
        cand = _peer_candidates(t1, t1_hi, t2_lo, t2_hi, t2[0], jnp.add)
        pad = [jnp.full((SUBLANES, tt), -jnp.inf, F32)] * (PEER_TOPK - len(cand))
        tau8 = _top16_desc(cand + pad)[PEER_TOPK - 1]

        m1, m2 = t1[0], t2[0]
        c1 = [jnp.exp(x - m1) for x in t1[:SUBLANES]]
        ec = _peer_candidates(c1, jnp.exp(t1_hi - m1), jnp.exp(t2_lo - m2), jnp.exp(t2_hi - m2),
                              jnp.ones((SUBLANES, tt), F32), jnp.multiply)
        zsum = jnp.zeros((SUBLANES, tt), F32)
        for cg, eg in zip(cand, ec):
            zsum = zsum + jnp.where(cg >= tau8, eg, 0.0)
        zden = jnp.sum(zsum, axis=0, keepdims=True)
        tau = tau8[0:1]
        th = jnp.full(s1.shape, jnp.inf, F32)
        for b in range(PEER_TOPK):
            t2b = t2[b][0:1]
            th = jnp.where(s1 + t2b >= tau, t2b, th)
        th_ref[h] = th
        s2_ref[h] = s2
        e1_ref[h] = jnp.exp(s1 - m1[0:1])
        e2n_ref[h] = jnp.exp(s2 - m2[0:1]) / zden
        return carry

    lax.fori_loop(0, PEER_HEADS, head, 0)


def _peer_scores(q, k1, k2):
    s = q.shape[0]
    tt = min(SCORE_TILE, s)
    tab = pl.BlockSpec((PEER_HEADS, PEER_N_KEYS, tt), lambda i: (0, 0, i))
    tab_shape = jax.ShapeDtypeStruct((PEER_HEADS, PEER_N_KEYS, s), F32)
    return pl.pallas_call(
        _peer_scores_body,
        grid=(s // tt,),
        in_specs=[pl.BlockSpec((tt, PEER_HEADS * PEER_KEY_DIM), lambda i: (i, 0)),
                  pl.BlockSpec(k1.shape, lambda i: (0, 0)), pl.BlockSpec(k2.shape, lambda i: (0, 0))],
        out_specs=[tab, tab, tab, tab],
        out_shape=[tab_shape, tab_shape, tab_shape, tab_shape],
        compiler_params=_cparams(("parallel",)),
        name="peer_scores",
    )(q, k1, k2)


def _peer_items(s, n_items, nb):
    split = lambda it: (it // nb, it % nb)
    return (split(jnp.minimum(s, n_items - 1)), split(jnp.clip(s - 1, 0, n_items - 1)),
            split(jnp.clip(s - 2, 0, n_items - 1)))


def _peer_dense_body(ht_ref, u_ref, v_ref, th_ref, e1_ref, s2_ref, e2n_ref, o_ref, a0, a1, w0, w1,
                     *, n_items, nb):
    s = pl.program_id(0)
    tt = ht_ref.shape[1]
    eb = u_ref.shape[0]
    d = ht_ref.shape[0]
    n_sub = eb // PEER_N_KEYS
    jrows = 2 * SUBLANES
    a_rows = eb // PEER_TRIPS

    @pl.when(s == 0)
    def _():
        for ref in (a0, a1, w0, w1):
            ref[...] = jnp.zeros_like(ref)

    _, _, (_, block_c) = _peer_items(s, n_items, nb)

    @pl.when(block_c == 0)
    def _():
        o_ref[...] = jnp.zeros_like(o_ref)

    def run(a_new, a_old, w_new, w_old):
        def stage_a(k):
            rows = pl.ds(pl.multiple_of(k * a_rows, a_rows), a_rows)
            a_t = jnp.dot(u_ref[rows, :], ht_ref[...], preferred_element_type=F32)
            a_new[rows, :] = 0.5 * a_t * (1.0 + lax.erf(a_t * (2.0 ** -0.5)))

        def stage_b(c, jc):
            cols = pl.ds(pl.multiple_of(c * LANES, LANES), LANES)
            jr = slice(jc * jrows, (jc + 1) * jrows)
            g = [jnp.zeros((jrows, LANES), F32) for _ in range(n_sub)]
            for h in range(PEER_HEADS):
                s2c = s2_ref[h, jr, cols]
                e2c = e2n_ref[h, jr, cols]
                for il in range(n_sub):
                    gate = e2c * e1_ref[h, il:il + 1, cols]
                    g[il] = g[il] + jnp.where(s2c >= th_ref[h, il:il + 1, cols], gate, 0.0)
            for il in range(n_sub):
                rows = slice(il * PEER_N_KEYS + jc * jrows, il * PEER_N_KEYS + (jc + 1) * jrows)
                w_new[rows, cols] = (a_old[rows, cols] * g[il]).astype(w_new.dtype)

        def stage_c(n):
            rows = pl.ds(pl.multiple_of(n * PEER_CHUNK, PEER_CHUNK), PEER_CHUNK)
            o_ref[rows, :] += lax.dot_general(v_ref[:, rows], w_old[...], (((0,), (0,)), ((), ())),
                                              preferred_element_type=F32)

        n_c = d // PEER_CHUNK // PEER_TRIPS
        n_col = tt // LANES // PEER_TRIPS
        assert n_c * PEER_TRIPS * PEER_CHUNK == d and n_col * PEER_TRIPS * LANES == tt and n_c % 2 == 0

        def trip(k, carry):
            stage_a(k)
            for n in range(n_c // 2):
                stage_c(n_c * k + n)
            for c in range(n_col):
                for jc in range(PEER_N_KEYS // jrows):
                    stage_b(n_col * k + c, jc)
            for n in range(n_c // 2, n_c):
                stage_c(n_c * k + n)
            return carry

        lax.fori_loop(0, PEER_TRIPS, trip, 0)

    parity = lax.rem(s, 2)

    @pl.when(parity == 0)
    def _():
        run(a0, a1, w1, w0)

    @pl.when(parity == 1)
    def _():
        run(a1, a0, w0, w1)


def _peer_dense(hn_t, u_all, v_all, layer, th, e1, s2, e2n):
    d, s = hn_t.shape
    tt = min(PEER_TOKENS, s)
    eb = PEER_EXPERTS
    nb = u_all.shape[1] // eb
    n_items = (s // tt) * nb
    items = functools.partial(_peer_items, n_items=n_items, nb=nb)
    sub = pl.BlockSpec((PEER_HEADS, eb // PEER_N_KEYS, tt), lambda i: (0, items(i)[1][1], items(i)[1][0]))
    tab = pl.BlockSpec((PEER_HEADS, PEER_N_KEYS, tt), lambda i: (0, 0, items(i)[1][0]))
    return pl.pallas_call(
        functools.partial(_peer_dense_body, n_items=n_items, nb=nb),
        grid=(n_items + 2,),
        in_specs=[pl.BlockSpec((d, tt), lambda i: (0, items(i)[0][0])),
                  pl.BlockSpec((None, eb, d), lambda i: (layer, items(i)[0][1], 0)),
                  pl.BlockSpec((None, eb, d), lambda i: (layer, items(i)[2][1], 0)),
                  sub, sub, tab, tab],
        out_specs=pl.BlockSpec((d, tt), lambda i: (0, items(i)[2][0])),
        out_shape=jax.ShapeDtypeStruct((d, s), F32),
        scratch_shapes=[pltpu.VMEM((eb, tt), F32), pltpu.VMEM((eb, tt), F32),
                        pltpu.VMEM((eb, tt), BF16), pltpu.VMEM((eb, tt), BF16)],
        compiler_params=_cparams(("arbitrary",)),
        name="peer_dense",
    )(hn_t, u_all, v_all, th, e1, s2, e2n)


def _add_norm_body(x_ref, dt_ref, g_ref, *out_refs):
    x = x_ref[...] + dt_ref[...].T
    y = x * lax.rsqrt(jnp.mean(x * x, axis=-1, keepdims=True) + NORM_EPS)
    out_refs[-1][...] = (y * g_ref[...]).astype(out_refs[-1].dtype)
    if len(out_refs) == 2:
        out_refs[0][...] = x


def _add_norm(x, delta_t, g, norm_dtype, with_sum):
    s, d = x.shape
    tm = min(ROW_TILE, s)
    row = pl.BlockSpec((tm, d), lambda i: (i, 0))
    out_specs = [row, row] if with_sum else [row]
    out_shape = [jax.ShapeDtypeStruct((s, d), norm_dtype)]
    if with_sum:
        out_shape.insert(0, jax.ShapeDtypeStruct((s, d), F32))
    return pl.pallas_call(
        _add_norm_body,
        grid=(s // tm,),
        in_specs=[row, pl.BlockSpec((d, tm), lambda i: (0, i)), pl.BlockSpec((1, d), lambda i: (0, 0))],
        out_specs=out_specs,
        out_shape=out_shape,
        compiler_params=_cparams(("parallel",)),
        name="add_norm",
    )(x, delta_t, g.reshape(1, d))


def _rope_tables(positions, dim, lanes_x1):
    half = dim // 2
    inv_freq = ROPE_THETA ** (-jnp.arange(0, dim, 2, dtype=F32) / dim)
    ang = positions.astype(F32)[:, None] * inv_freq
    cos, sin = jnp.cos(ang), jnp.sin(ang)
    s = positions.shape[0]
    c_parts, su_parts, sd_parts = [], [], []
    pos = 0
    for off in lanes_x1:
        gap = off - pos
        c_parts += [jnp.ones((s, gap), F32), cos, cos]
        su_parts += [jnp.zeros((s, gap), F32), -sin, jnp.zeros((s, half), F32)]
        sd_parts += [jnp.zeros((s, gap + half), F32), sin]
        pos = off + dim
    tail = LANES - pos
    c_parts.append(jnp.ones((s, tail), F32))
    su_parts.append(jnp.zeros((s, tail), F32))
    sd_parts.append(jnp.zeros((s, tail), F32))
    return tuple(jnp.concatenate(p, axis=1) for p in (c_parts, su_parts, sd_parts))


W_IN_HEAD = IN_OFFSETS[3]
W_IN_RUN = sum(IN_SIZES[3:])
LAYOUT_ROWS = 1024


def _in_proj_body(h_ref, a_ref, b_ref, o_ref, wt_ref):
    j = pl.program_id(0)
    rb = wt_ref.shape[0]
    lead = rb - W_IN_HEAD % rb
    new_block = pl.program_id(1) == 0

    @pl.when(new_block & (j < W_IN_RUN // rb))
    def _():
        wt_ref[0:lead, :] = a_ref[W_IN_HEAD % rb:rb, :].astype(wt_ref.dtype)
        wt_ref[lead:rb, :] = b_ref[0:rb - lead, :].astype(wt_ref.dtype)

    @pl.when(new_block & (j == W_IN_RUN // rb))
    def _():
        head = MLA_Q_LORA + MLA_KV_LORA
        wt_ref[0:head, :] = a_ref[0:head, :].astype(wt_ref.dtype)
        wt_ref[head:rb, :] = jnp.zeros((rb - head, wt_ref.shape[1]), wt_ref.dtype)
        wt_ref[head + MLA_NOPE:head + MLA_NOPE + MLA_ROPE, :] = a_ref[head:head + MLA_ROPE, :].astype(wt_ref.dtype)

    o_ref[...] = lax.dot_general(h_ref[...], wt_ref[...], (((1,), (1,)), ((), ())),
                                 preferred_element_type=F32).astype(o_ref.dtype)


def _in_proj(hn, w_in_t, layer):
    s, d = hn.shape
    n = w_in_t.shape[1]
    rb = LAYOUT_ROWS
    tm = min(MM_TILE, s)
    assert W_IN_RUN % rb == 0 and Z_CQ == W_IN_RUN and Z_COLS == W_IN_RUN + rb and W_IN_HEAD < rb
    run_blocks = W_IN_RUN // rb
    first = lambda j: jnp.where(j < run_blocks, j + W_IN_HEAD // rb, 0)
    second = lambda j: jnp.minimum(j + W_IN_HEAD // rb + 1, (n - 1) // rb)
    return pl.pallas_call(
        _in_proj_body,
        grid=(Z_COLS // rb, s // tm),
        in_specs=[pl.BlockSpec((tm, d), lambda j, i: (i, 0)),
                  pl.BlockSpec((None, rb, d), lambda j, i: (layer, first(j), 0)),
                  pl.BlockSpec((None, rb, d), lambda j, i: (layer, second(j), 0))],
        out_specs=pl.BlockSpec((tm, rb), lambda j, i: (i, j)),
        out_shape=jax.ShapeDtypeStruct((s, Z_COLS), BF16),
        scratch_shapes=[pltpu.VMEM((rb, d), BF16)],
        compiler_params=_cparams(("parallel", "arbitrary")),
        name="in_proj",
    )(hn, w_in_t, w_in_t)


def _layout_mla_weights(w_uq, w_ukv):
    lq = w_uq.shape[0]
    wq = w_uq.reshape(lq, MLA_HEADS, MLA_NOPE + MLA_ROPE)
    wq = jnp.pad(wq, ((0, 0), (0, 0), (0, LANES - MLA_NOPE - MLA_ROPE))).reshape(lq, MLA_HEADS * LANES)
    lk = w_ukv.shape[0]
    wkv = w_ukv.reshape(lk, MLA_HEADS, MLA_NOPE + MLA_V)
    wk = jnp.pad(wkv[:, :, :MLA_NOPE], ((0, 0), (0, 0), (0, LANES - MLA_NOPE))).reshape(lk, MLA_HEADS * LANES)
    wv = wkv[:, :, MLA_NOPE:].reshape(lk, MLA_HEADS * MLA_V)
    return wq.T.astype(BF16), wk.astype(BF16), wv.T.astype(BF16)


def kernel(x, positions, mix_norm_g, w_in, gate_b, mla_q_norm_g, mla_w_uq, mla_kv_norm_g, mla_w_ukv, mla_w_out,
           sc_conv_w, sc_w_out, diff_lambda, diff_norm_g, diff_w_out, conf_dw_w, conf_dw_b, conf_ln_g, conf_ln_b,
           conf_w_out, w_o, ffn_norm_g, peer_w_q, peer_sub_keys, peer_u, peer_v, final_norm_g):
    b, s, d = x.shape
    assert b == 1 and d == D_MODEL
    xs = x.reshape(s, d)
    pos = positions.reshape(s)
    mla_tabs = _rope_tables(pos, MLA_ROPE, (MLA_NOPE,))
    mla_tabs_t = tuple(t.T for t in mla_tabs)
    w_in_t = jnp.swapaxes(w_in, 1, 2)
    u_all = peer_u.astype(BF16)
    v_all = peer_v.astype(BF16)
    diff_tabs = _rope_tables(pos, DIFF_ROT, (0, DIFF_HEAD_DIM))

    for i in range(DEPTH):
        lam_init = 0.8 - 0.6 * math.exp(-0.3 * i)
        if i == 0:
            hn = _rmsnorm(xs, mix_norm_g[i], BF16)
        else:
            xs, hn = _add_norm(xs, delta_t, mix_norm_g[i], BF16, with_sum=True)
        z = _in_proj(hn, w_in_t, i)
        wqt, wk, wvt = _layout_mla_weights(mla_w_uq[i], mla_w_ukv[i])
        qt, k, vt = _mla_prep(z, mla_tabs, mla_tabs_t, mla_q_norm_g[i].reshape(1, -1),
                              mla_kv_norm_g[i].reshape(1, -1), wqt, wk, wvt)
        o_mla = _mla_attn(qt, k, vt)
        o_sc, o_conf = _conv_branches(z, sc_conv_w[i], conf_dw_w[i], conf_dw_b[i], conf_ln_g[i], conf_ln_b[i])
        qdt, kd, vdt = _diff_prep(z, diff_tabs)
        o_diff = _diff_attn(qdt, kd, vdt, diff_lambda[i], diff_norm_g[i], lam_init)
        merged = _merge((o_mla, o_sc, o_diff, o_conf),
                        tuple(w.astype(BF16) for w in (mla_w_out[i], sc_w_out[i], diff_w_out[i], conf_w_out[i])),
                        z, gate_b[i])
        xs = _matmul(merged, w_o[i].astype(BF16), F32, residual=xs, name="out_proj")
        hf, hf_t = _rmsnorm(xs, ffn_norm_g[i], BF16, with_transposed=True)
        pq = _matmul(hf, peer_w_q[i].astype(BF16), F32, name="peer_q")
        zk = jnp.zeros((PEER_N_KEYS, PEER_HALF), F32)
        k1 = jnp.concatenate([peer_sub_keys[i, 0], zk], axis=1)
        k2 = jnp.concatenate([zk, peer_sub_keys[i, 1]], axis=1)
        th, s2, e1, e2n = _peer_scores(pq, k1, k2)
        delta_t = _peer_dense(hf_t, u_all, v_all, i, th, e1, s2, e2n)
    (out,) = _add_norm(xs, delta_t, final_norm_g, F32, with_sum=False)
    return out.reshape(b, s, d)
```

```python
import functools
import math

import jax
import jax.numpy as jnp
import numpy as np
from jax import lax
from jax.experimental import pallas as pl
from jax.experimental.pallas import tpu as pltpu

F32 = jnp.float32
BF16 = jnp.bfloat16

D_MODEL = 2048
DEPTH = 2
CHUNK = 64
ROPE_THETA = 500000.0
NORM_EPS = 1e-6
LN_EPS = 1e-5
NEG_INF = -1e30
N_BRANCH = 4

MLA_HEADS = 8
MLA_Q_LORA = 512
MLA_KV_LORA = 256
MLA_NOPE = 64
MLA_ROPE = 32
MLA_V = 64
MLA_SCALE = (MLA_NOPE + MLA_ROPE) ** -0.5

SC_WIDTH = 512
SC_KERNEL = 3

DIFF_HEADS = 4
DIFF_HEAD_DIM = 64
DIFF_ROT = DIFF_HEAD_DIM // 4
DIFF_SCALE = DIFF_HEAD_DIM ** -0.5
DIFF_WIDTH = DIFF_HEADS * 2 * DIFF_HEAD_DIM

CONF_WIDTH = 512
CONF_KERNEL = 31

PEER_HEADS = 8
PEER_N_KEYS = 128
PEER_N_EXPERTS = PEER_N_KEYS * PEER_N_KEYS
PEER_KEY_DIM = 128
PEER_HALF = PEER_KEY_DIM // 2
PEER_TOPK = 16

IN_SIZES = (MLA_Q_LORA, MLA_KV_LORA, MLA_ROPE, 3 * SC_WIDTH, DIFF_WIDTH, DIFF_WIDTH, DIFF_WIDTH,
            2 * CONF_WIDTH, N_BRANCH * D_MODEL)
IN_OFFSETS = tuple(int(v) for v in np.cumsum((0,) + IN_SIZES)[:-1])

LOG2E = math.log2(math.e)

LANES = 128
SUBLANES = 8
HALO = 32
VMEM_LIMIT = 56 * 1024 * 1024

ROW_TILE = 512
MM_TILE = 1024
ATTN_TILE = 512
MLA_GROUP = 8
DIFF_GROUP = 4
SCORE_TILE = 512
PEER_TOKENS = 512
PEER_EXPERTS = 1024
PEER_CHUNK = 256
PEER_TRIPS = 1

Z_SC = 0
Z_DQ = 1536
Z_DK = 2048
Z_DV = 2560
Z_CONF = 3072
Z_GATE = 4096
Z_CQ = 12288
Z_CKV = 12800
Z_KR = 13056
Z_COLS = 13312


def _cparams(sem):
    return pltpu.CompilerParams(dimension_semantics=sem, vmem_limit_bytes=VMEM_LIMIT)


def _rmsnorm_body(x_ref, g_ref, o_ref, *maybe_ot_ref):
    x = x_ref[...].astype(F32)
    y = x * lax.rsqrt(jnp.mean(x * x, axis=-1, keepdims=True) + NORM_EPS) * g_ref[...]
    o_ref[...] = y.astype(o_ref.dtype)
    for ot_ref in maybe_ot_ref:
        ot_ref[...] = y.T.astype(ot_ref.dtype)


def _rmsnorm(x, g, out_dtype, with_transposed=False):
    s, d = x.shape
    tm = min(ROW_TILE, s)
    out_specs = [pl.BlockSpec((tm, d), lambda i: (i, 0))]
    out_shape = [jax.ShapeDtypeStruct((s, d), out_dtype)]
    if with_transposed:
        out_specs.append(pl.BlockSpec((d, tm), lambda i: (0, i)))
        out_shape.append(jax.ShapeDtypeStruct((d, s), out_dtype))
    out = pl.pallas_call(
        _rmsnorm_body,
        grid=(s // tm,),
        in_specs=[pl.BlockSpec((tm, d), lambda i: (i, 0)), pl.BlockSpec((1, d), lambda i: (0, 0))],
        out_specs=out_specs,
        out_shape=out_shape,
        compiler_params=_cparams(("parallel",)),
        name="rmsnorm",
    )(x, g.reshape(1, d))
    return out if with_transposed else out[0]


def _matmul_body(a_ref, b_ref, o_ref):
    o_ref[...] = jnp.dot(a_ref[...], b_ref[...], preferred_element_type=F32).astype(o_ref.dtype)


def _matmul_res_body(a_ref, b_ref, r_ref, o_ref):
    o_ref[...] = (r_ref[...] + jnp.dot(a_ref[...], b_ref[...], preferred_element_type=F32)).astype(o_ref.dtype)


def _matmul(a, b, out_dtype, residual=None, name="matmul"):
    m, k = a.shape
    n = b.shape[1]
    tm, tn = min(MM_TILE, m), min(MM_TILE, n)
    in_specs = [pl.BlockSpec((tm, k), lambda i, j: (i, 0)), pl.BlockSpec((k, tn), lambda i, j: (0, j))]
    args = [a, b]
    body = _matmul_body
    if residual is not None:
        in_specs.append(pl.BlockSpec((tm, tn), lambda i, j: (i, j)))
        args.append(residual)
        body = _matmul_res_body
    return pl.pallas_call(
        body,
        grid=(m // tm, n // tn),
        in_specs=in_specs,
        out_specs=pl.BlockSpec((tm, tn), lambda i, j: (i, j)),
        out_shape=jax.ShapeDtypeStruct((m, n), out_dtype),
        compiler_params=_cparams(("parallel", "arbitrary")),
        name=name,
    )(*args)


def _sigmoid(x):
    return 0.5 + 0.5 * jnp.tanh(0.5 * x)


def _rope128(x, c, s_up, s_dn, half):
    return x * c + pltpu.roll(x, LANES - half, 1) * s_up + pltpu.roll(x, half, 1) * s_dn


def _mla_prep_body(cq_ref, ckv_ref, kr_ref, c_ref, su_ref, sd_ref, ct_ref, sut_ref, sdt_ref,
                   qg_ref, kvg_ref, wqt_ref, wk_ref, wvt_ref, qt_ref, k_ref, vt_ref):
    half = MLA_ROPE // 2
    nt = (((1,), (1,)), ((), ()))

    def norm(x_ref, g_ref):
        x = x_ref[...].astype(F32)
        y = x * lax.rsqrt(jnp.mean(x * x, axis=-1, keepdims=True) + NORM_EPS)
        return (y * g_ref[...]).astype(BF16)

    qf_t = lax.dot_general(wqt_ref[...], norm(cq_ref, qg_ref), nt, preferred_element_type=F32)
    ckvn = norm(ckv_ref, kvg_ref)
    kf = jnp.dot(ckvn, wk_ref[...], preferred_element_type=F32)
    vt_ref[...] = lax.dot_general(wvt_ref[...], ckvn, nt, preferred_element_type=F32).astype(vt_ref.dtype)
    kr = _rope128(kr_ref[...].astype(F32), c_ref[...], su_ref[...], sd_ref[...], half)
    ct, sut, sdt = ct_ref[...], sut_ref[...], sdt_ref[...]
    for h in range(MLA_HEADS):
        sl = slice(h * LANES, (h + 1) * LANES)
        x = qf_t[sl, :]
        q = x * ct + pltpu.roll(x, LANES - half, 0) * sut + pltpu.roll(x, half, 0) * sdt
        qt_ref[sl, :] = (q * (MLA_SCALE * LOG2E)).astype(qt_ref.dtype)
        k_ref[:, sl] = (kf[:, sl] + kr).astype(k_ref.dtype)


def _mla_prep(z, tabs, tabs_t, qg, kvg, wqt, wk, wvt):
    s = z.shape[0]
    tm = min(ROW_TILE, s)
    row = lambda w, idx: pl.BlockSpec((tm, w), lambda i: (i, idx))
    full = lambda a: pl.BlockSpec(a.shape, lambda i: (0,) * a.ndim)
    tab = pl.BlockSpec((tm, LANES), lambda i: (i, 0))
    tab_t = pl.BlockSpec((LANES, tm), lambda i: (0, i))
    hw = MLA_HEADS * LANES
    vw = MLA_HEADS * MLA_V
    return pl.pallas_call(
        _mla_prep_body,
        grid=(s // tm,),
        in_specs=[row(MLA_Q_LORA, Z_CQ // MLA_Q_LORA), row(MLA_KV_LORA, Z_CKV // MLA_KV_LORA),
                  row(LANES, Z_KR // LANES), tab, tab, tab, tab_t, tab_t, tab_t,
                  full(qg), full(kvg), full(wqt), full(wk), full(wvt)],
        out_specs=[pl.BlockSpec((hw, tm), lambda i: (0, i)), pl.BlockSpec((tm, hw), lambda i: (i, 0)),
                   pl.BlockSpec((vw, tm), lambda i: (0, i))],
        out_shape=[jax.ShapeDtypeStruct((hw, s), BF16), jax.ShapeDtypeStruct((s, hw), BF16),
                   jax.ShapeDtypeStruct((vw, s), BF16)],
        compiler_params=_cparams(("parallel",)),
        name="mla_prep",
    )(z, z, z, *tabs, *tabs_t, qg, kvg, wqt, wk, wvt)


def _flash_streams_t(q_ts, k_ats, vt_ats, m_ref, l_ref, acc_ref, s_a, s_b, n_full, t):
    streams = range(len(q_ts))
    for n in streams:
        m_ref[n] = jnp.full((1, t), NEG_INF, F32)
        l_ref[n] = jnp.zeros((1, t), F32)
        acc_ref[n] = jnp.zeros(acc_ref.shape[1:], F32)

    def scores_into(s_ref, j):
        for n in streams:
            s_ref[n] = jnp.dot(k_ats[n](j), q_ts[n], preferred_element_type=F32)

    def consume(s_ref, j, masked):
        probs, alphas = [], []
        for n in streams:
            s = s_ref[n]
            if masked:
                key_chunk = lax.broadcasted_iota(jnp.int32, (t, t), 0) // CHUNK
                qry_chunk = lax.broadcasted_iota(jnp.int32, (t, t), 1) // CHUNK
                s = jnp.where(key_chunk <= qry_chunk, s, NEG_INF)
            m_prev = m_ref[n]
            m_new = jnp.maximum(m_prev, jnp.max(s, axis=0, keepdims=True))
            alpha = jnp.exp2(m_prev - m_new)
            p = jnp.exp2(s - m_new)
            l_ref[n] = alpha * l_ref[n] + jnp.sum(p, axis=0, keepdims=True)
            m_ref[n] = m_new
            probs.append(p.astype(BF16))
            alphas.append(alpha)
        for n in streams:
            acc_ref[n] = alphas[n] * acc_ref[n] + jnp.dot(vt_ats[n](j), probs[n], preferred_element_type=F32)

    scores_into(s_a, 0)

    def pair(jj, carry):
        j = 2 * jj
        scores_into(s_b, j + 1)
        consume(s_a, j, False)
        scores_into(s_a, j + 2)
        consume(s_b, j + 1, False)
        return carry

    lax.fori_loop(0, n_full // 2, pair, 0)

    @pl.when(n_full % 2 == 0)
    def _():
        consume(s_a, n_full, True)

    @pl.when(n_full % 2 == 1)
    def _():
        scores_into(s_b, n_full)
        consume(s_a, n_full - 1, False)
        consume(s_b, n_full, True)


def _mla_attn_body(qt_ref, k_ref, vt_ref, o_ref, m_ref, l_ref, acc_ref, s_a, s_b, *, t):
    i = pl.program_id(1)
    rows = lambda j: pl.ds(pl.multiple_of(j * t, t), t)
    heads = range(MLA_GROUP)
    lanes = [slice(hh * LANES, (hh + 1) * LANES) for hh in heads]
    k_ats = [lambda j, sl=sl: k_ref[rows(j), sl] for sl in lanes]
    vt_ats = [lambda j, hh=hh: vt_ref[hh * MLA_V:(hh + 1) * MLA_V, rows(j)] for hh in heads]
    _flash_streams_t([qt_ref[sl, :] for sl in lanes], k_ats, vt_ats, m_ref, l_ref, acc_ref, s_a, s_b, i, t)
    o_t = jnp.concatenate([acc_ref[hh] / l_ref[hh] for hh in heads], axis=0)
    o_ref[...] = o_t.T.astype(o_ref.dtype)


def _mla_attn(qt, k, vt):
    s = k.shape[0]
    t = min(ATTN_TILE, s)
    g = MLA_GROUP
    return pl.pallas_call(
        functools.partial(_mla_attn_body, t=t),
        grid=(MLA_HEADS // g, s // t),
        in_specs=[pl.BlockSpec((g * LANES, t), lambda p, i: (p, i)),
                  pl.BlockSpec((s, g * LANES), lambda p, i: (0, p), pipeline_mode=pl.Buffered(1)),
                  pl.BlockSpec((g * MLA_V, s), lambda p, i: (p, 0), pipeline_mode=pl.Buffered(1))],
        out_specs=pl.BlockSpec((t, g * MLA_V), lambda p, i: (i, p)),
        out_shape=jax.ShapeDtypeStruct((s, MLA_HEADS * MLA_V), BF16),
        scratch_shapes=[pltpu.VMEM((g, 1, t), F32), pltpu.VMEM((g, 1, t), F32),
                        pltpu.VMEM((g, MLA_V, t), F32),
                        pltpu.VMEM((g, t, t), F32), pltpu.VMEM((g, t, t), F32)],
        compiler_params=_cparams(("parallel", "arbitrary")),
        name="mla_attn",
    )(qt, k, vt)


def _diff_prep_body(dq_ref, dk_ref, dv_ref, c_ref, su_ref, sd_ref, qt_ref, k_ref, vt_ref):
    half = DIFF_ROT // 2
    c, su, sd = c_ref[...], su_ref[...], sd_ref[...]
    tm = dq_ref.shape[0]
    lane = lax.broadcasted_iota(jnp.int32, (tm, LANES), 1)
    vt_ref[...] = dv_ref[...].astype(F32).T.astype(vt_ref.dtype)
    for h in range(DIFF_HEADS):
        sl = slice(h * LANES, (h + 1) * LANES)
        q = _rope128(dq_ref[:, sl].astype(F32), c, su, sd, half) * (DIFF_SCALE * LOG2E)
        qt_ref[(2 * h) * LANES:(2 * h + 1) * LANES, :] = jnp.where(lane < DIFF_HEAD_DIM, q, 0.0).T.astype(qt_ref.dtype)
        qt_ref[(2 * h + 1) * LANES:(2 * h + 2) * LANES, :] = jnp.where(lane >= DIFF_HEAD_DIM, q, 0.0).T.astype(qt_ref.dtype)
        k_ref[:, sl] = _rope128(dk_ref[:, sl].astype(F32), c, su, sd, half).astype(k_ref.dtype)


def _diff_prep(z, tabs):
    s = z.shape[0]
    tm = min(ROW_TILE, s)
    tab = pl.BlockSpec((tm, LANES), lambda i: (i, 0))
    col = lambda start: pl.BlockSpec((tm, DIFF_WIDTH), lambda i: (i, start // DIFF_WIDTH))
    return pl.pallas_call(
        _diff_prep_body,
        grid=(s // tm,),
        in_specs=[col(Z_DQ), col(Z_DK), col(Z_DV), tab, tab, tab],
        out_specs=[pl.BlockSpec((2 * DIFF_WIDTH, tm), lambda i: (0, i)),
                   pl.BlockSpec((tm, DIFF_WIDTH), lambda i: (i, 0)),
                   pl.BlockSpec((DIFF_WIDTH, tm), lambda i: (0, i))],
        out_shape=[jax.ShapeDtypeStruct((2 * DIFF_WIDTH, s), BF16), jax.ShapeDtypeStruct((s, DIFF_WIDTH), BF16),
                   jax.ShapeDtypeStruct((DIFF_WIDTH, s), BF16)],
        compiler_params=_cparams(("parallel",)),
        name="diff_prep",
    )(z, z, z, *tabs)


def _diff_attn_body(qt_ref, k_ref, vt_ref, lam_ref, g_ref, o_ref, m_ref, l_ref, acc_ref, s_a, s_b, *, t, lam_init):
    i = pl.program_id(1)
    rows = lambda j: pl.ds(pl.multiple_of(j * t, t), t)
    q_ts, k_ats, vt_ats = [], [], []
    for hh in range(DIFF_GROUP):
        sl = slice(hh * LANES, (hh + 1) * LANES)
        for c in range(2):
            q_ts.append(qt_ref[(2 * hh + c) * LANES:(2 * hh + c + 1) * LANES, :])
            k_ats.append(lambda j, sl=sl: k_ref[rows(j), sl])
            vt_ats.append(lambda j, sl=sl: vt_ref[sl, rows(j)])
    _flash_streams_t(q_ts, k_ats, vt_ats, m_ref, l_ref, acc_ref, s_a, s_b, i, t)
    lv = lam_ref[...]
    lam = (jnp.exp(jnp.sum(lv[0:1] * lv[1:2], axis=1, keepdims=True))
           - jnp.exp(jnp.sum(lv[2:3] * lv[3:4], axis=1, keepdims=True)) + lam_init)
    for hh in range(DIFF_GROUP):
        o = (acc_ref[2 * hh] / l_ref[2 * hh] - lam * (acc_ref[2 * hh + 1] / l_ref[2 * hh + 1])).T
        o = o * lax.rsqrt(jnp.mean(o * o, axis=-1, keepdims=True) + NORM_EPS)
        o_ref[:, hh * LANES:(hh + 1) * LANES] = (o * g_ref[...] * (1.0 - lam_init)).astype(o_ref.dtype)


def _diff_attn(qt, k, vt, lam_vecs, norm_g, lam_init):
    s = k.shape[0]
    t = min(ATTN_TILE, s)
    g = DIFF_GROUP
    return pl.pallas_call(
        functools.partial(_diff_attn_body, t=t, lam_init=lam_init),
        grid=(DIFF_HEADS // g, s // t),
        in_specs=[pl.BlockSpec((2 * g * LANES, t), lambda h, i: (h, i)),
                  pl.BlockSpec((s, g * LANES), lambda h, i: (0, h)),
                  pl.BlockSpec((g * LANES, s), lambda h, i: (h, 0)),
                  pl.BlockSpec(lam_vecs.shape, lambda h, i: (0, 0)),
                  pl.BlockSpec((1, LANES), lambda h, i: (0, 0))],
        out_specs=pl.BlockSpec((t, g * LANES), lambda h, i: (i, h)),
        out_shape=jax.ShapeDtypeStruct((s, DIFF_WIDTH), BF16),
        scratch_shapes=[pltpu.VMEM((2 * g, 1, t), F32), pltpu.VMEM((2 * g, 1, t), F32),
                        pltpu.VMEM((2 * g, LANES, t), F32),
                        pltpu.VMEM((2 * g, t, t), F32), pltpu.VMEM((2 * g, t, t), F32)],
        compiler_params=_cparams(("parallel", "arbitrary")),
        name="diff_attn",
    )(qt, k, vt, lam_vecs, norm_g.reshape(1, LANES))


def _conv_body(bg_ref, cg_ref, xv_ref, cgh_ref, xvh_ref, a_ref, gt_ref, ah_ref, gth_ref,
               scw_ref, dww_ref, dwb_ref, lng_ref, lnb_ref, sc_ref, cf_ref, ext_ref):
    tm = bg_ref.shape[0]
    first = pl.program_id(0) == 0

    shifted_rows = HALO + tm - SUBLANES

    def fill(cur, halo, taps):
        ext_ref[0, 0:HALO, :] = jnp.where(first, 0.0, halo)
        ext_ref[0, HALO:HALO + tm, :] = cur
        for r in sorted({(HALO - (taps - 1) + j) % SUBLANES for j in range(taps)} - {0}):
            ext_ref[r, 0:shifted_rows, :] = ext_ref[0, r:r + shifted_rows, :]

    def conv(w_ref, taps):
        acc = jnp.zeros((tm, ext_ref.shape[2]), F32)
        for j in range(taps):
            off = HALO - (taps - 1) + j
            r = off % SUBLANES
            acc = acc + w_ref[j:j + 1, :] * ext_ref[r, off - r:off - r + tm, :]
        return acc

    fill(cg_ref[...].astype(F32) * xv_ref[...].astype(F32), cgh_ref[...].astype(F32) * xvh_ref[...].astype(F32),
         SC_KERNEL)
    sc_ref[...] = (bg_ref[...].astype(F32) * conv(scw_ref, SC_KERNEL)).astype(sc_ref.dtype)

    fill(a_ref[...].astype(F32) * _sigmoid(gt_ref[...].astype(F32)),
         ah_ref[...].astype(F32) * _sigmoid(gth_ref[...].astype(F32)), CONF_KERNEL)
    u = conv(dww_ref, CONF_KERNEL) + dwb_ref[...]
    mu = jnp.mean(u, axis=-1, keepdims=True)
    var = jnp.mean(jnp.square(u - mu), axis=-1, keepdims=True)
    y = (u - mu) * lax.rsqrt(var + LN_EPS) * lng_ref[...] + lnb_ref[...]
    cf_ref[...] = (y * _sigmoid(y)).astype(cf_ref.dtype)


def _conv_branches(z, sc_w, dw_w, dw_b, ln_g, ln_b):
    s = z.shape[0]
    tm = min(ROW_TILE, s)
    w = SC_WIDTH
    cur = lambda col: pl.BlockSpec((tm, w), lambda i: (i, col // w))
    halo = lambda col: pl.BlockSpec((HALO, w), lambda i: (jnp.maximum(i * (tm // HALO) - 1, 0), col // w))
    full = lambda a: pl.BlockSpec(a.shape, lambda i: (0,) * a.ndim)
    vec = lambda a: a.reshape(1, -1)
    args = [sc_w, dw_w, vec(dw_b), vec(ln_g), vec(ln_b)]
    return pl.pallas_call(
        _conv_body,
        grid=(s // tm,),
        in_specs=[cur(Z_SC), cur(Z_SC + w), cur(Z_SC + 2 * w), halo(Z_SC + w), halo(Z_SC + 2 * w),
                  cur(Z_CONF), cur(Z_CONF + w), halo(Z_CONF), halo(Z_CONF + w)] + [full(a) for a in args],
        out_specs=[pl.BlockSpec((tm, w), lambda i: (i, 0)), pl.BlockSpec((tm, w), lambda i: (i, 0))],
        out_shape=[jax.ShapeDtypeStruct((s, w), BF16), jax.ShapeDtypeStruct((s, w), BF16)],
        scratch_shapes=[pltpu.VMEM((SUBLANES, HALO + tm, w), F32)],
        compiler_params=_cparams(("parallel",)),
        name="conv_branches",
    )(*([z] * 9), *args)


def _merge_body(b0, b1, b2, b3, w0, w1, w2, w3, g0, g1, g2, g3, gb_ref, o_ref):
    acc = None
    for n, (b_ref, w_ref, g_ref) in enumerate(((b0, w0, g0), (b1, w1, g1), (b2, w2, g2), (b3, w3, g3))):
        y = jnp.dot(b_ref[...], w_ref[...], preferred_element_type=F32)
        t = _sigmoid(g_ref[...].astype(F32) + gb_ref[n:n + 1, :]) * y
        acc = t if acc is None else acc + t
    o_ref[...] = acc.astype(o_ref.dtype)


def _merge(branches, w_outs, z, gate_b):
    s = z.shape[0]
    tm = min(MM_TILE, s)
    tn = ROW_TILE
    kw = branches[0].shape[1]
    return pl.pallas_call(
        _merge_body,
        grid=(s // tm, D_MODEL // tn),
        in_specs=([pl.BlockSpec((tm, kw), lambda i, j: (i, 0))] * 4
                  + [pl.BlockSpec((kw, tn), lambda i, j: (0, j))] * 4
                  + [pl.BlockSpec((tm, tn), lambda i, j, b=b: (i, (Z_GATE + b * D_MODEL) // tn + j))
                     for b in range(N_BRANCH)]
                  + [pl.BlockSpec((N_BRANCH, tn), lambda i, j: (0, j))]),
        out_specs=pl.BlockSpec((tm, tn), lambda i, j: (i, j)),
        out_shape=jax.ShapeDtypeStruct((s, D_MODEL), BF16),
        compiler_params=_cparams(("parallel", "arbitrary")),
        name="merge",
    )(*branches, *w_outs, z, z, z, z, gate_b)


def _order(v, i, j):
    v[i], v[j] = jnp.maximum(v[i], v[j]), jnp.minimum(v[i], v[j])


def _bitonic_sort_desc(v):
    n = len(v)
    k = 2
    while k <= n:
        j = k // 2
        while j >= 1:
            for i in range(n):
                partner = i ^ j
                if partner > i:
                    if i & k == 0:
                        _order(v, i, partner)
                    else:
                        _order(v, partner, i)
            j //= 2
        k *= 2


def _bitonic_merge_desc(v):
    n = len(v)
    j = n // 2
    while j >= 1:
        for i in range(n):
            if i ^ j > i:
                _order(v, i, i ^ j)
        j //= 2


def _top16_desc(groups):
    v = list(groups)
    _bitonic_sort_desc(v)
    for shift in (SUBLANES // 2, SUBLANES // 4, SUBLANES // 8):
        other = [pltpu.roll(x, shift, 0) for x in v]
        v = [jnp.maximum(v[k], other[PEER_TOPK - 1 - k]) for k in range(PEER_TOPK)]
        _bitonic_merge_desc(v)
    return v


def _by_sublane(v, start):
    sub = lax.broadcasted_iota(jnp.int32, v[0].shape, 0)
    out = v[start]
    for r in range(1, SUBLANES):
        out = jnp.where(sub == r, v[start + r], out)
    return out


def _peer_candidates(a, a_hi, b_lo, b_hi, b0, op):
    groups = [op(a[0], b_lo), op(a[0], b_hi)]
    groups += [op(a[i], b_lo) for i in range(1, SUBLANES)]
    groups.append(op(a_hi, b0))
    return groups


def _peer_scores_body(q_ref, k1_ref, k2_ref, th_ref, s2_ref, e1_ref, e2n_ref):
    tt = q_ref.shape[0]
    n_groups = PEER_N_KEYS // SUBLANES

    def head(h, carry):
        qh = q_ref[:, pl.ds(pl.multiple_of(h * PEER_KEY_DIM, PEER_KEY_DIM), PEER_KEY_DIM)]
        dims = (((1,), (1,)), ((), ()))
        s1 = lax.dot_general(k1_ref[...], qh, dims, precision=lax.Precision.HIGHEST, preferred_element_type=F32)
        s2 = lax.dot_general(k2_ref[...], qh, dims, precision=lax.Precision.HIGHEST, preferred_element_type=F32)
        split = lambda s: [s[g * SUBLANES:(g + 1) * SUBLANES, :] for g in range(n_groups)]
        t1 = _top16_desc(split(s1))
        t2 = _top16_desc(split(s2))
        t1_hi, t2_lo, t2_hi = _by_sublane(t1, SUBLANES), _by_sublane(t2, 0), _by_sublane(t2, SUBLANES)
        cand = _peer_candidates(t1, t1_hi, t2_lo, t2_hi, t2[0], jnp.add)
        pad = [jnp.full((SUBLANES, tt), -jnp.inf, F32)] * (PEER_TOPK - len(cand))
        tau8 = _top16_desc(cand + pad)[PEER_TOPK - 1]

        m1, m2 = t1[0], t2[0]
        c1 = [jnp.exp(x - m1) for x in t1[:SUBLANES]]
        ec = _peer_candidates(c1, jnp.exp(t1_hi - m1), jnp.exp(t2_lo - m2), jnp.exp(t2_hi - m2),
                              jnp.ones((SUBLANES, tt), F32), jnp.multiply)
        zsum = jnp.zeros((SUBLANES, tt), F32)
        for cg, eg in zip(cand, ec):
            zsum = zsum + jnp.where(cg >= tau8, eg, 0.0)
        zden = jnp.sum(zsum, axis=0, keepdims=True)
        tau = tau8[0:1]
        th = jnp.full(s1.shape, jnp.inf, F32)
        for b in range(PEER_TOPK):
            t2b = t2[b][0:1]
            th = jnp.where(s1 + t2b >= tau, t2b, th)
        th_ref[h] = th
        s2_ref[h] = s2
        e1_ref[h] = jnp.exp(s1 - m1[0:1])
        e2n_ref[h] = jnp.exp(s2 - m2[0:1]) / zden
        return carry

    lax.fori_loop(0, PEER_HEADS, head, 0)


def _peer_scores(q, k1, k2):
    s = q.shape[0]
    tt = min(SCORE_TILE, s)
    tab = pl.BlockSpec((PEER_HEADS, PEER_N_KEYS, tt), lambda i: (0, 0, i))
    tab_shape = jax.ShapeDtypeStruct((PEER_HEADS, PEER_N_KEYS, s), F32)
    return pl.pallas_call(
        _peer_scores_body,
        grid=(s // tt,),
        in_specs=[pl.BlockSpec((tt, PEER_HEADS * PEER_KEY_DIM), lambda i: (i, 0)),
                  pl.BlockSpec(k1.shape, lambda i: (0, 0)), pl.BlockSpec(k2.shape, lambda i: (0, 0))],
        out_specs=[tab, tab, tab, tab],
        out_shape=[tab_shape, tab_shape, tab_shape, tab_shape],
        compiler_params=_cparams(("parallel",)),
        name="peer_scores",
    )(q, k1, k2)


def _peer_items(s, n_items, nb):
    split = lambda it: (it // nb, it % nb)
    return (split(jnp.minimum(s, n_items - 1)), split(jnp.clip(s - 1, 0, n_items - 1)),
            split(jnp.clip(s - 2, 0, n_items - 1)))


def _peer_dense_body(ht_ref, u_ref, v_ref, th_ref, e1_ref, s2_ref, e2n_ref, o_ref, a0, a1, w0, w1,
                     *, n_items, nb):
    s = pl.program_id(0)
    tt = ht_ref.shape[1]
    eb = u_ref.shape[0]
    d = ht_ref.shape[0]
    n_sub = eb // PEER_N_KEYS
    jrows = 2 * SUBLANES
    a_rows = eb // PEER_TRIPS

    @pl.when(s == 0)
    def _():
        for ref in (a0, a1, w0, w1):
            ref[...] = jnp.zeros_like(ref)

    _, _, (_, block_c) = _peer_items(s, n_items, nb)

    @pl.when(block_c == 0)
    def _():
        o_ref[...] = jnp.zeros_like(o_ref)

    def run(a_new, a_old, w_new, w_old):
        def stage_a(k):
            rows = pl.ds(pl.multiple_of(k * a_rows, a_rows), a_rows)
            a_t = jnp.dot(u_ref[rows, :], ht_ref[...], preferred_element_type=F32)
            a_new[rows, :] = 0.5 * a_t * (1.0 + lax.erf(a_t * (2.0 ** -0.5)))

        def stage_b(c, jc):
            cols = pl.ds(pl.multiple_of(c * LANES, LANES), LANES)
            jr = slice(jc * jrows, (jc + 1) * jrows)
            g = [jnp.zeros((jrows, LANES), F32) for _ in range(n_sub)]
            for h in range(PEER_HEADS):
                s2c = s2_ref[h, jr, cols]
                e2c = e2n_ref[h, jr, cols]
                for il in range(n_sub):
                    gate = e2c * e1_ref[h, il:il + 1, cols]
                    g[il] = g[il] + jnp.where(s2c >= th_ref[h, il:il + 1, cols], gate, 0.0)
            for il in range(n_sub):
                rows = slice(il * PEER_N_KEYS + jc * jrows, il * PEER_N_KEYS + (jc + 1) * jrows)
                w_new[rows, cols] = (a_old[rows, cols] * g[il]).astype(w_new.dtype)

        def stage_c(n):
            rows = pl.ds(pl.multiple_of(n * PEER_CHUNK, PEER_CHUNK), PEER_CHUNK)
            o_ref[rows, :] += lax.dot_general(v_ref[:, rows], w_old[...], (((0,), (0,)), ((), ())),
                                              preferred_element_type=F32)

        n_c = d // PEER_CHUNK // PEER_TRIPS
        n_col = tt // LANES // PEER_TRIPS
        assert n_c * PEER_TRIPS * PEER_CHUNK == d and n_col * PEER_TRIPS * LANES == tt and n_c % 2 == 0

        def trip(k, carry):
            stage_a(k)
            for n in range(n_c // 2):
                stage_c(n_c * k + n)
            for c in range(n_col):
                for jc in range(PEER_N_KEYS // jrows):
                    stage_b(n_col * k + c, jc)
            for n in range(n_c // 2, n_c):
                stage_c(n_c * k + n)
            return carry

        lax.fori_loop(0, PEER_TRIPS, trip, 0)

    parity = lax.rem(s, 2)

    @pl.when(parity == 0)
    def _():
        run(a0, a1, w1, w0)

    @pl.when(parity == 1)
    def _():
        run(a1, a0, w0, w1)


def _peer_dense(hn_t, u_all, v_all, layer, th, e1, s2, e2n):
    d, s = hn_t.shape
    tt = min(PEER_TOKENS, s)
    eb = PEER_EXPERTS
    nb = u_all.shape[1] // eb
    n_items = (s // tt) * nb
    items = functools.partial(_peer_items, n_items=n_items, nb=nb)
    sub = pl.BlockSpec((PEER_HEADS, eb // PEER_N_KEYS, tt), lambda i: (0, items(i)[1][1], items(i)[1][0]))
    tab = pl.BlockSpec((PEER_HEADS, PEER_N_KEYS, tt), lambda i: (0, 0, items(i)[1][0]))
    return pl.pallas_call(
        functools.partial(_peer_dense_body, n_items=n_items, nb=nb),
        grid=(n_items + 2,),
        in_specs=[pl.BlockSpec((d, tt), lambda i: (0, items(i)[0][0])),
                  pl.BlockSpec((None, eb, d), lambda i: (layer, items(i)[0][1], 0)),
                  pl.BlockSpec((None, eb, d), lambda i: (layer, items(i)[2][1], 0)),
                  sub, sub, tab, tab],
        out_specs=pl.BlockSpec((d, tt), lambda i: (0, items(i)[2][0])),
        out_shape=jax.ShapeDtypeStruct((d, s), F32),
        scratch_shapes=[pltpu.VMEM((eb, tt), F32), pltpu.VMEM((eb, tt), F32),
                        pltpu.VMEM((eb, tt), BF16), pltpu.VMEM((eb, tt), BF16)],
        compiler_params=_cparams(("arbitrary",)),
        name="peer_dense",
    )(hn_t, u_all, v_all, th, e1, s2, e2n)


def _add_norm_body(x_ref, dt_ref, g_ref, *out_refs):
    x = x_ref[...] + dt_ref[...].T
    y = x * lax.rsqrt(jnp.mean(x * x, axis=-1, keepdims=True) + NORM_EPS)
    out_refs[-1][...] = (y * g_ref[...]).astype(out_refs[-1].dtype)
    if len(out_refs) == 2:
        out_refs[0][...] = x


def _add_norm(x, delta_t, g, norm_dtype, with_sum):
    s, d = x.shape
    tm = min(ROW_TILE, s)
    row = pl.BlockSpec((tm, d), lambda i: (i, 0))
    out_specs = [row, row] if with_sum else [row]
    out_shape = [jax.ShapeDtypeStruct((s, d), norm_dtype)]
    if with_sum:
        out_shape.insert(0, jax.ShapeDtypeStruct((s, d), F32))
    return pl.pallas_call(
        _add_norm_body,
        grid=(s // tm,),
        in_specs=[row, pl.BlockSpec((d, tm), lambda i: (0, i)), pl.BlockSpec((1, d), lambda i: (0, 0))],
        out_specs=out_specs,
        out_shape=out_shape,
        compiler_params=_cparams(("parallel",)),
        name="add_norm",
    )(x, delta_t, g.reshape(1, d))


def _rope_tables(positions, dim, lanes_x1):
    half = dim // 2
    inv_freq = ROPE_THETA ** (-jnp.arange(0, dim, 2, dtype=F32) / dim)
    ang = positions.astype(F32)[:, None] * inv_freq
    cos, sin = jnp.cos(ang), jnp.sin(ang)
    s = positions.shape[0]
    c_parts, su_parts, sd_parts = [], [], []
    pos = 0
    for off in lanes_x1:
        gap = off - pos
        c_parts += [jnp.ones((s, gap), F32), cos, cos]
        su_parts += [jnp.zeros((s, gap), F32), -sin, jnp.zeros((s, half), F32)]
        sd_parts += [jnp.zeros((s, gap + half), F32), sin]
        pos = off + dim
    tail = LANES - pos
    c_parts.append(jnp.ones((s, tail), F32))
    su_parts.append(jnp.zeros((s, tail), F32))
    sd_parts.append(jnp.zeros((s, tail), F32))
    return tuple(jnp.concatenate(p, axis=1) for p in (c_parts, su_parts, sd_parts))


W_IN_HEAD = IN_OFFSETS[3]
W_IN_RUN = sum(IN_SIZES[3:])
LAYOUT_ROWS = 1024


def _in_proj_body(h_ref, a_ref, b_ref, o_ref, wt_ref):
    j = pl.program_id(0)
    rb = wt_ref.shape[0]
    lead = rb - W_IN_HEAD % rb
    new_block = pl.program_id(1) == 0

    @pl.when(new_block & (j < W_IN_RUN // rb))
    def _():
        wt_ref[0:lead, :] = a_ref[W_IN_HEAD % rb:rb, :].astype(wt_ref.dtype)
        wt_ref[lead:rb, :] = b_ref[0:rb - lead, :].astype(wt_ref.dtype)

    @pl.when(new_block & (j == W_IN_RUN // rb))
    def _():
        head = MLA_Q_LORA + MLA_KV_LORA
        wt_ref[0:head, :] = a_ref[0:head, :].astype(wt_ref.dtype)
        wt_ref[head:rb, :] = jnp.zeros((rb - head, wt_ref.shape[1]), wt_ref.dtype)
        wt_ref[head + MLA_NOPE:head + MLA_NOPE + MLA_ROPE, :] = a_ref[head:head + MLA_ROPE, :].astype(wt_ref.dtype)

    o_ref[...] = lax.dot_general(h_ref[...], wt_ref[...], (((1,), (1,)), ((), ())),
                                 preferred_element_type=F32).astype(o_ref.dtype)


def _in_proj(hn, w_in_t, layer):
    s, d = hn.shape
    n = w_in_t.shape[1]
    rb = LAYOUT_ROWS
    tm = min(MM_TILE, s)
    assert W_IN_RUN % rb == 0 and Z_CQ == W_IN_RUN and Z_COLS == W_IN_RUN + rb and W_IN_HEAD < rb
    run_blocks = W_IN_RUN // rb
    first = lambda j: jnp.where(j < run_blocks, j + W_IN_HEAD // rb, 0)
    second = lambda j: jnp.minimum(j + W_IN_HEAD // rb + 1, (n - 1) // rb)
    return pl.pallas_call(
        _in_proj_body,
        grid=(Z_COLS // rb, s // tm),
        in_specs=[pl.BlockSpec((tm, d), lambda j, i: (i, 0)),
                  pl.BlockSpec((None, rb, d), lambda j, i: (layer, first(j), 0)),
                  pl.BlockSpec((None, rb, d), lambda j, i: (layer, second(j), 0))],
        out_specs=pl.BlockSpec((tm, rb), lambda j, i: (i, j)),
        out_shape=jax.ShapeDtypeStruct((s, Z_COLS), BF16),
        scratch_shapes=[pltpu.VMEM((rb, d), BF16)],
        compiler_params=_cparams(("parallel", "arbitrary")),
        name="in_proj",
    )(hn, w_in_t, w_in_t)


def _layout_mla_weights(w_uq, w_ukv):
    lq = w_uq.shape[0]
    wq = w_uq.reshape(lq, MLA_HEADS, MLA_NOPE + MLA_ROPE)
    wq = jnp.pad(wq, ((0, 0), (0, 0), (0, LANES - MLA_NOPE - MLA_ROPE))).reshape(lq, MLA_HEADS * LANES)
    lk = w_ukv.shape[0]
    wkv = w_ukv.reshape(lk, MLA_HEADS, MLA_NOPE + MLA_V)
    wk = jnp.pad(wkv[:, :, :MLA_NOPE], ((0, 0), (0, 0), (0, LANES - MLA_NOPE))).reshape(lk, MLA_HEADS * LANES)
    wv = wkv[:, :, MLA_NOPE:].reshape(lk, MLA_HEADS * MLA_V)
    return wq.T.astype(BF16), wk.astype(BF16), wv.T.astype(BF16)


def kernel(x, positions, mix_norm_g, w_in, gate_b, mla_q_norm_g, mla_w_uq, mla_kv_norm_g, mla_w_ukv, mla_w_out,
           sc_conv_w, sc_w_out, diff_lambda, diff_norm_g, diff_w_out, conf_dw_w, conf_dw_b, conf_ln_g, conf_ln_b,
           conf_w_out, w_o, ffn_norm_g, peer_w_q, peer_sub_keys, peer_u, peer_v, final_norm_g):
    b, s, d = x.shape
    assert b == 1 and d == D_MODEL
    xs = x.reshape(s, d)
    pos = positions.reshape(s)
    mla_tabs = _rope_tables(pos, MLA_ROPE, (MLA_NOPE,))
    mla_tabs_t = tuple(t.T for t in mla_tabs)
    w_in_t = jnp.swapaxes(w_in, 1, 2)
    u_all = peer_u.astype(BF16)
    v_all = peer_v.astype(BF16)
    diff_tabs = _rope_tables(pos, DIFF_ROT, (0, DIFF_HEAD_DIM))

    for i in range(DEPTH):
        lam_init = 0.8 - 0.6 * math.exp(-0.3 * i)
        if i == 0:
            hn = _rmsnorm(xs, mix_norm_g[i], BF16)
        else:
            xs, hn = _add_norm(xs, delta_t, mix_norm_g[i], BF16, with_sum=True)
        z = _in_proj(hn, w_in_t, i)
        wqt, wk, wvt = _layout_mla_weights(mla_w_uq[i], mla_w_ukv[i])
        qt, k, vt = _mla_prep(z, mla_tabs, mla_tabs_t, mla_q_norm_g[i].reshape(1, -1),
                              mla_kv_norm_g[i].reshape(1, -1), wqt, wk, wvt)
        o_mla = _mla_attn(qt, k, vt)
        o_sc, o_conf = _conv_branches(z, sc_conv_w[i], conf_dw_w[i], conf_dw_b[i], conf_ln_g[i], conf_ln_b[i])
        qdt, kd, vdt = _diff_prep(z, diff_tabs)
        o_diff = _diff_attn(qdt, kd, vdt, diff_lambda[i], diff_norm_g[i], lam_init)
        merged = _merge((o_mla, o_sc, o_diff, o_conf),
                        tuple(w.astype(BF16) for w in (mla_w_out[i], sc_w_out[i], diff_w_out[i], conf_w_out[i])),
                        z, gate_b[i])
        xs = _matmul(merged, w_o[i].astype(BF16), F32, residual=xs, name="out_proj")
        hf, hf_t = _rmsnorm(xs, ffn_norm_g[i], BF16, with_transposed=True)
        pq = _matmul(hf, peer_w_q[i].astype(BF16), F32, name="peer_q")
        zk = jnp.zeros((PEER_N_KEYS, PEER_HALF), F32)
        k1 = jnp.concatenate([peer_sub_keys[i, 0], zk], axis=1)
        k2 = jnp.concatenate([zk, peer_sub_keys[i, 1]], axis=1)
        th, s2, e1, e2n = _peer_scores(pq, k1, k2)
        delta_t = _peer_dense(hf_t, u_all, v_all, i, th, e1, s2, e2n)
    (out,) = _add_norm(xs, delta_t, final_norm_g, F32, with_sum=False)
    return out.reshape(b, s, d)
```

```python
import functools
import math

import jax
import jax.numpy as jnp
import numpy as np
from jax import lax
from jax.experimental import pallas as pl
from jax.experimental.pallas import tpu as pltpu

F32 = jnp.float32
BF16 = jnp.bfloat16

D_MODEL = 2048
DEPTH = 2
CHUNK = 64
ROPE_THETA = 500000.0
NORM_EPS = 1e-6
LN_EPS = 1e-5
NEG_INF = -1e30
N_BRANCH = 4

MLA_HEADS = 8
MLA_Q_LORA = 512
MLA_KV_LORA = 256
MLA_NOPE = 64
MLA_ROPE = 32
MLA_V = 64
MLA_SCALE = (MLA_NOPE + MLA_ROPE) ** -0.5

SC_WIDTH = 512
SC_KERNEL = 3

DIFF_HEADS = 4
DIFF_HEAD_DIM = 64
DIFF_ROT = DIFF_HEAD_DIM // 4
DIFF_SCALE = DIFF_HEAD_DIM ** -0.5
DIFF_WIDTH = DIFF_HEADS * 2 * DIFF_HEAD_DIM

CONF_WIDTH = 512
CONF_KERNEL = 31

PEER_HEADS = 8
PEER_N_KEYS = 128
PEER_N_EXPERTS = PEER_N_KEYS * PEER_N_KEYS
PEER_KEY_DIM = 128
PEER_HALF = PEER_KEY_DIM // 2
PEER_TOPK = 16

IN_SIZES = (MLA_Q_LORA, MLA_KV_LORA, MLA_ROPE, 3 * SC_WIDTH, DIFF_WIDTH, DIFF_WIDTH, DIFF_WIDTH,
            2 * CONF_WIDTH, N_BRANCH * D_MODEL)
IN_OFFSETS = tuple(int(v) for v in np.cumsum((0,) + IN_SIZES)[:-1])

LOG2E = math.log2(math.e)

LANES = 128
SUBLANES = 8
HALO = 32
VMEM_LIMIT = 56 * 1024 * 1024

ROW_TILE = 512
MM_TILE = 1024
ATTN_TILE = 512
MLA_GROUP = 8
DIFF_GROUP = 4
SCORE_TILE = 1024
PEER_TOKENS = 512
PEER_EXPERTS = 1024
PEER_CHUNK = 256
PEER_TRIPS = 1

Z_SC = 0
Z_DQ = 1536
Z_DK = 2048
Z_DV = 2560
Z_CONF = 3072
Z_GATE = 4096
Z_CQ = 12288
Z_CKV = 12800
Z_KR = 13056
Z_COLS = 13312


def _cparams(sem):
    return pltpu.CompilerParams(dimension_semantics=sem, vmem_limit_bytes=VMEM_LIMIT)


def _rmsnorm_body(x_ref, g_ref, o_ref, *maybe_ot_ref):
    x = x_ref[...].astype(F32)
    y = x * lax.rsqrt(jnp.mean(x * x, axis=-1, keepdims=True) + NORM_EPS) * g_ref[...]
    o_ref[...] = y.astype(o_ref.dtype)
    for ot_ref in maybe_ot_ref:
        ot_ref[...] = y.T.astype(ot_ref.dtype)


def _rmsnorm(x, g, out_dtype, with_transposed=False):
    s, d = x.shape
    tm = min(ROW_TILE, s)
    out_specs = [pl.BlockSpec((tm, d), lambda i: (i, 0))]
    out_shape = [jax.ShapeDtypeStruct((s, d), out_dtype)]
    if with_transposed:
        out_specs.append(pl.BlockSpec((d, tm), lambda i: (0, i)))
        out_shape.append(jax.ShapeDtypeStruct((d, s), out_dtype))
    out = pl.pallas_call(
        _rmsnorm_body,
        grid=(s // tm,),
        in_specs=[pl.BlockSpec((tm, d), lambda i: (i, 0)), pl.BlockSpec((1, d), lambda i: (0, 0))],
        out_specs=out_specs,
        out_shape=out_shape,
        compiler_params=_cparams(("parallel",)),
        name="rmsnorm",
    )(x, g.reshape(1, d))
    return out if with_transposed else out[0]


def _matmul_body(a_ref, b_ref, o_ref):
    o_ref[...] = jnp.dot(a_ref[...], b_ref[...], preferred_element_type=F32).astype(o_ref.dtype)


def _matmul_res_body(a_ref, b_ref, r_ref, o_ref):
    o_ref[...] = (r_ref[...] + jnp.dot(a_ref[...], b_ref[...], preferred_element_type=F32)).astype(o_ref.dtype)


def _matmul(a, b, out_dtype, residual=None, name="matmul"):
    m, k = a.shape
    n = b.shape[1]
    tm, tn = min(MM_TILE, m), min(MM_TILE, n)
    in_specs = [pl.BlockSpec((tm, k), lambda i, j: (i, 0)), pl.BlockSpec((k, tn), lambda i, j: (0, j))]
    args = [a, b]
    body = _matmul_body
    if residual is not None:
        in_specs.append(pl.BlockSpec((tm, tn), lambda i, j: (i, j)))
        args.append(residual)
        body = _matmul_res_body
    return pl.pallas_call(
        body,
        grid=(m // tm, n // tn),
        in_specs=in_specs,
        out_specs=pl.BlockSpec((tm, tn), lambda i, j: (i, j)),
        out_shape=jax.ShapeDtypeStruct((m, n), out_dtype),
        compiler_params=_cparams(("parallel", "arbitrary")),
        name=name,
    )(*args)


def _sigmoid(x):
    return 0.5 + 0.5 * jnp.tanh(0.5 * x)


def _rope128(x, c, s_up, s_dn, half):
    return x * c + pltpu.roll(x, LANES - half, 1) * s_up + pltpu.roll(x, half, 1) * s_dn


def _mla_prep_body(cq_ref, ckv_ref, kr_ref, c_ref, su_ref, sd_ref, ct_ref, sut_ref, sdt_ref,
                   qg_ref, kvg_ref, wqt_ref, wk_ref, wvt_ref, qt_ref, k_ref, vt_ref):
    half = MLA_ROPE // 2
    nt = (((1,), (1,)), ((), ()))

    def norm(x_ref, g_ref):
        x = x_ref[...].astype(F32)
        y = x * lax.rsqrt(jnp.mean(x * x, axis=-1, keepdims=True) + NORM_EPS)
        return (y * g_ref[...]).astype(BF16)

    qf_t = lax.dot_general(wqt_ref[...], norm(cq_ref, qg_ref), nt, preferred_element_type=F32)
    ckvn = norm(ckv_ref, kvg_ref)
    kf = jnp.dot(ckvn, wk_ref[...], preferred_element_type=F32)
    vt_ref[...] = lax.dot_general(wvt_ref[...], ckvn, nt, preferred_element_type=F32).astype(vt_ref.dtype)
    kr = _rope128(kr_ref[...].astype(F32), c_ref[...], su_ref[...], sd_ref[...], half)
    ct, sut, sdt = ct_ref[...], sut_ref[...], sdt_ref[...]
    for h in range(MLA_HEADS):
        sl = slice(h * LANES, (h + 1) * LANES)
        x = qf_t[sl, :]
        q = x * ct + pltpu.roll(x, LANES - half, 0) * sut + pltpu.roll(x, half, 0) * sdt
        qt_ref[sl, :] = (q * (MLA_SCALE * LOG2E)).astype(qt_ref.dtype)
        k_ref[:, sl] = (kf[:, sl] + kr).astype(k_ref.dtype)


def _mla_prep(z, tabs, tabs_t, qg, kvg, wqt, wk, wvt):
    s = z.shape[0]
    tm = min(ROW_TILE, s)
    row = lambda w, idx: pl.BlockSpec((tm, w), lambda i: (i, idx))
    full = lambda a: pl.BlockSpec(a.shape, lambda i: (0,) * a.ndim)
    tab = pl.BlockSpec((tm, LANES), lambda i: (i, 0))
    tab_t = pl.BlockSpec((LANES, tm), lambda i: (0, i))
    hw = MLA_HEADS * LANES
    vw = MLA_HEADS * MLA_V
    return pl.pallas_call(
        _mla_prep_body,
        grid=(s // tm,),
        in_specs=[row(MLA_Q_LORA, Z_CQ // MLA_Q_LORA), row(MLA_KV_LORA, Z_CKV // MLA_KV_LORA),
                  row(LANES, Z_KR // LANES), tab, tab, tab, tab_t, tab_t, tab_t,
                  full(qg), full(kvg), full(wqt), full(wk), full(wvt)],
        out_specs=[pl.BlockSpec((hw, tm), lambda i: (0, i)), pl.BlockSpec((tm, hw), lambda i: (i, 0)),
                   pl.BlockSpec((vw, tm), lambda i: (0, i))],
        out_shape=[jax.ShapeDtypeStruct((hw, s), BF16), jax.ShapeDtypeStruct((s, hw), BF16),
                   jax.ShapeDtypeStruct((vw, s), BF16)],
        compiler_params=_cparams(("parallel",)),
        name="mla_prep",
    )(z, z, z, *tabs, *tabs_t, qg, kvg, wqt, wk, wvt)


def _flash_streams_t(q_ts, k_ats, vt_ats, m_ref, l_ref, acc_ref, s_a, s_b, n_full, t):
    streams = range(len(q_ts))
    for n in streams:
        m_ref[n] = jnp.full((1, t), NEG_INF, F32)
        l_ref[n] = jnp.zeros((1, t), F32)
        acc_ref[n] = jnp.zeros(acc_ref.shape[1:], F32)

    def scores_into(s_ref, j):
        for n in streams:
            s_ref[n] = jnp.dot(k_ats[n](j), q_ts[n], preferred_element_type=F32)

    def consume(s_ref, j, masked):
        probs, alphas = [], []
        for n in streams:
            s = s_ref[n]
            if masked:
                key_chunk = lax.broadcasted_iota(jnp.int32, (t, t), 0) // CHUNK
                qry_chunk = lax.broadcasted_iota(jnp.int32, (t, t), 1) // CHUNK
                s = jnp.where(key_chunk <= qry_chunk, s, NEG_INF)
            m_prev = m_ref[n]
            m_new = jnp.maximum(m_prev, jnp.max(s, axis=0, keepdims=True))
            alpha = jnp.exp2(m_prev - m_new)
            p = jnp.exp2(s - m_new)
            l_ref[n] = alpha * l_ref[n] + jnp.sum(p, axis=0, keepdims=True)
            m_ref[n] = m_new
            probs.append(p.astype(BF16))
            alphas.append(alpha)
        for n in streams:
            acc_ref[n] = alphas[n] * acc_ref[n] + jnp.dot(vt_ats[n](j), probs[n], preferred_element_type=F32)

    scores_into(s_a, 0)

    def pair(jj, carry):
        j = 2 * jj
        scores_into(s_b, j + 1)
        consume(s_a, j, False)
        scores_into(s_a, j + 2)
        consume(s_b, j + 1, False)
        return carry

    lax.fori_loop(0, n_full // 2, pair, 0)

    @pl.when(n_full % 2 == 0)
    def _():
        consume(s_a, n_full, True)

    @pl.when(n_full % 2 == 1)
    def _():
        scores_into(s_b, n_full)
        consume(s_a, n_full - 1, False)
        consume(s_b, n_full, True)


def _mla_attn_body(qt_ref, k_ref, vt_ref, o_ref, m_ref, l_ref, acc_ref, s_a, s_b, *, t):
    i = pl.program_id(1)
    rows = lambda j: pl.ds(pl.multiple_of(j * t, t), t)
    heads = range(MLA_GROUP)
    lanes = [slice(hh * LANES, (hh + 1) * LANES) for hh in heads]
    k_ats = [lambda j, sl=sl: k_ref[rows(j), sl] for sl in lanes]
    vt_ats = [lambda j, hh=hh: vt_ref[hh * MLA_V:(hh + 1) * MLA_V, rows(j)] for hh in heads]
    _flash_streams_t([qt_ref[sl, :] for sl in lanes], k_ats, vt_ats, m_ref, l_ref, acc_ref, s_a, s_b, i, t)
    o_t = jnp.concatenate([acc_ref[hh] / l_ref[hh] for hh in heads], axis=0)
    o_ref[...] = o_t.T.astype(o_ref.dtype)


def _mla_attn(qt, k, vt):
    s = k.shape[0]
    t = min(ATTN_TILE, s)
    g = MLA_GROUP
    return pl.pallas_call(
        functools.partial(_mla_attn_body, t=t),
        grid=(MLA_HEADS // g, s // t),
        in_specs=[pl.BlockSpec((g * LANES, t), lambda p, i: (p, i)),
                  pl.BlockSpec((s, g * LANES), lambda p, i: (0, p), pipeline_mode=pl.Buffered(1)),
                  pl.BlockSpec((g * MLA_V, s), lambda p, i: (p, 0), pipeline_mode=pl.Buffered(1))],
        out_specs=pl.BlockSpec((t, g * MLA_V), lambda p, i: (i, p)),
        out_shape=jax.ShapeDtypeStruct((s, MLA_HEADS * MLA_V), BF16),
        scratch_shapes=[pltpu.VMEM((g, 1, t), F32), pltpu.VMEM((g, 1, t), F32),
                        pltpu.VMEM((g, MLA_V, t), F32),
                        pltpu.VMEM((g, t, t), F32), pltpu.VMEM((g, t, t), F32)],
        compiler_params=_cparams(("parallel", "arbitrary")),
        name="mla_attn",
    )(qt, k, vt)


def _diff_prep_body(dq_ref, dk_ref, dv_ref, c_ref, su_ref, sd_ref, qt_ref, k_ref, vt_ref):
    half = DIFF_ROT // 2
    c, su, sd = c_ref[...], su_ref[...], sd_ref[...]
    tm = dq_ref.shape[0]
    lane = lax.broadcasted_iota(jnp.int32, (tm, LANES), 1)
    vt_ref[...] = dv_ref[...].astype(F32).T.astype(vt_ref.dtype)
    for h in range(DIFF_HEADS):
        sl = slice(h * LANES, (h + 1) * LANES)
        q = _rope128(dq_ref[:, sl].astype(F32), c, su, sd, half) * (DIFF_SCALE * LOG2E)
        qt_ref[(2 * h) * LANES:(2 * h + 1) * LANES, :] = jnp.where(lane < DIFF_HEAD_DIM, q, 0.0).T.astype(qt_ref.dtype)
        qt_ref[(2 * h + 1) * LANES:(2 * h + 2) * LANES, :] = jnp.where(lane >= DIFF_HEAD_DIM, q, 0.0).T.astype(qt_ref.dtype)
        k_ref[:, sl] = _rope128(dk_ref[:, sl].astype(F32), c, su, sd, half).astype(k_ref.dtype)


def _diff_prep(z, tabs):
    s = z.shape[0]
    tm = min(ROW_TILE, s)
    tab = pl.BlockSpec((tm, LANES), lambda i: (i, 0))
    col = lambda start: pl.BlockSpec((tm, DIFF_WIDTH), lambda i: (i, start // DIFF_WIDTH))
    return pl.pallas_call(
        _diff_prep_body,
        grid=(s // tm,),
        in_specs=[col(Z_DQ), col(Z_DK), col(Z_DV), tab, tab, tab],
        out_specs=[pl.BlockSpec((2 * DIFF_WIDTH, tm), lambda i: (0, i)),
                   pl.BlockSpec((tm, DIFF_WIDTH), lambda i: (i, 0)),
                   pl.BlockSpec((DIFF_WIDTH, tm), lambda i: (0, i))],
        out_shape=[jax.ShapeDtypeStruct((2 * DIFF_WIDTH, s), BF16), jax.ShapeDtypeStruct((s, DIFF_WIDTH), BF16),
                   jax.ShapeDtypeStruct((DIFF_WIDTH, s), BF16)],
        compiler_params=_cparams(("parallel",)),
        name="diff_prep",
    )(z, z, z, *tabs)


def _diff_attn_body(qt_ref, k_ref, vt_ref, lam_ref, g_ref, o_ref, m_ref, l_ref, acc_ref, s_a, s_b, *, t, lam_init):
    i = pl.program_id(1)
    rows = lambda j: pl.ds(pl.multiple_of(j * t, t), t)
    q_ts, k_ats, vt_ats = [], [], []
    for hh in range(DIFF_GROUP):
        sl = slice(hh * LANES, (hh + 1) * LANES)
        for c in range(2):
            q_ts.append(qt_ref[(2 * hh + c) * LANES:(2 * hh + c + 1) * LANES, :])
            k_ats.append(lambda j, sl=sl: k_ref[rows(j), sl])
            vt_ats.append(lambda j, sl=sl: vt_ref[sl, rows(j)])
    _flash_streams_t(q_ts, k_ats, vt_ats, m_ref, l_ref, acc_ref, s_a, s_b, i, t)
    lv = lam_ref[...]
    lam = (jnp.exp(jnp.sum(lv[0:1] * lv[1:2], axis=1, keepdims=True))
           - jnp.exp(jnp.sum(lv[2:3] * lv[3:4], axis=1, keepdims=True)) + lam_init)
    for hh in range(DIFF_GROUP):
        o = (acc_ref[2 * hh] / l_ref[2 * hh] - lam * (acc_ref[2 * hh + 1] / l_ref[2 * hh + 1])).T
        o = o * lax.rsqrt(jnp.mean(o * o, axis=-1, keepdims=True) + NORM_EPS)
        o_ref[:, hh * LANES:(hh + 1) * LANES] = (o * g_ref[...] * (1.0 - lam_init)).astype(o_ref.dtype)


def _diff_attn(qt, k, vt, lam_vecs, norm_g, lam_init):
    s = k.shape[0]
    t = min(ATTN_TILE, s)
    g = DIFF_GROUP
    return pl.pallas_call(
        functools.partial(_diff_attn_body, t=t, lam_init=lam_init),
        grid=(DIFF_HEADS // g, s // t),
        in_specs=[pl.BlockSpec((2 * g * LANES, t), lambda h, i: (h, i)),
                  pl.BlockSpec((s, g * LANES), lambda h, i: (0, h)),
                  pl.BlockSpec((g * LANES, s), lambda h, i: (h, 0)),
                  pl.BlockSpec(lam_vecs.shape, lambda h, i: (0, 0)),
                  pl.BlockSpec((1, LANES), lambda h, i: (0, 0))],
        out_specs=pl.BlockSpec((t, g * LANES), lambda h, i: (i, h)),
        out_shape=jax.ShapeDtypeStruct((s, DIFF_WIDTH), BF16),
        scratch_shapes=[pltpu.VMEM((2 * g, 1, t), F32), pltpu.VMEM((2 * g, 1, t), F32),
                        pltpu.VMEM((2 * g, LANES, t), F32),
                        pltpu.VMEM((2 * g, t, t), F32), pltpu.VMEM((2 * g, t, t), F32)],
        compiler_params=_cparams(("parallel", "arbitrary")),
        name="diff_attn",
    )(qt, k, vt, lam_vecs, norm_g.reshape(1, LANES))


def _conv_body(bg_ref, cg_ref, xv_ref, cgh_ref, xvh_ref, a_ref, gt_ref, ah_ref, gth_ref,
               scw_ref, dww_ref, dwb_ref, lng_ref, lnb_ref, sc_ref, cf_ref, ext_ref):
    tm = bg_ref.shape[0]
    first = pl.program_id(0) == 0

    shifted_rows = HALO + tm - SUBLANES

    def fill(cur, halo, taps):
        ext_ref[0, 0:HALO, :] = jnp.where(first, 0.0, halo)
        ext_ref[0, HALO:HALO + tm, :] = cur
        for r in sorted({(HALO - (taps - 1) + j) % SUBLANES for j in range(taps)} - {0}):
            ext_ref[r, 0:shifted_rows, :] = ext_ref[0, r:r + shifted_rows, :]

    def conv(w_ref, taps):
        acc = jnp.zeros((tm, ext_ref.shape[2]), F32)
        for j in range(taps):
            off = HALO - (taps - 1) + j
            r = off % SUBLANES
            acc = acc + w_ref[j:j + 1, :] * ext_ref[r, off - r:off - r + tm, :]
        return acc

    fill(cg_ref[...].astype(F32) * xv_ref[...].astype(F32), cgh_ref[...].astype(F32) * xvh_ref[...].astype(F32),
         SC_KERNEL)
    sc_ref[...] = (bg_ref[...].astype(F32) * conv(scw_ref, SC_KERNEL)).astype(sc_ref.dtype)

    fill(a_ref[...].astype(F32) * _sigmoid(gt_ref[...].astype(F32)),
         ah_ref[...].astype(F32) * _sigmoid(gth_ref[...].astype(F32)), CONF_KERNEL)
    u = conv(dww_ref, CONF_KERNEL) + dwb_ref[...]
    mu = jnp.mean(u, axis=-1, keepdims=True)
    var = jnp.mean(jnp.square(u - mu), axis=-1, keepdims=True)
    y = (u - mu) * lax.rsqrt(var + LN_EPS) * lng_ref[...] + lnb_ref[...]
    cf_ref[...] = (y * _sigmoid(y)).astype(cf_ref.dtype)


def _conv_branches(z, sc_w, dw_w, dw_b, ln_g, ln_b):
    s = z.shape[0]
    tm = min(ROW_TILE, s)
    w = SC_WIDTH
    cur = lambda col: pl.BlockSpec((tm, w), lambda i: (i, col // w))
    halo = lambda col: pl.BlockSpec((HALO, w), lambda i: (jnp.maximum(i * (tm // HALO) - 1, 0), col // w))
    full = lambda a: pl.BlockSpec(a.shape, lambda i: (0,) * a.ndim)
    vec = lambda a: a.reshape(1, -1)
    args = [sc_w, dw_w, vec(dw_b), vec(ln_g), vec(ln_b)]
    return pl.pallas_call(
        _conv_body,
        grid=(s // tm,),
        in_specs=[cur(Z_SC), cur(Z_SC + w), cur(Z_SC + 2 * w), halo(Z_SC + w), halo(Z_SC + 2 * w),
                  cur(Z_CONF), cur(Z_CONF + w), halo(Z_CONF), halo(Z_CONF + w)] + [full(a) for a in args],
        out_specs=[pl.BlockSpec((tm, w), lambda i: (i, 0)), pl.BlockSpec((tm, w), lambda i: (i, 0))],
        out_shape=[jax.ShapeDtypeStruct((s, w), BF16), jax.ShapeDtypeStruct((s, w), BF16)],
        scratch_shapes=[pltpu.VMEM((SUBLANES, HALO + tm, w), F32)],
        compiler_params=_cparams(("parallel",)),
        name="conv_branches",
    )(*([z] * 9), *args)


def _merge_body(b0, b1, b2, b3, w0, w1, w2, w3, g0, g1, g2, g3, gb_ref, o_ref):
    acc = None
    for n, (b_ref, w_ref, g_ref) in enumerate(((b0, w0, g0), (b1, w1, g1), (b2, w2, g2), (b3, w3, g3))):
        y = jnp.dot(b_ref[...], w_ref[...], preferred_element_type=F32)
        t = _sigmoid(g_ref[...].astype(F32) + gb_ref[n:n + 1, :]) * y
        acc = t if acc is None else acc + t
    o_ref[...] = acc.astype(o_ref.dtype)


def _merge(branches, w_outs, z, gate_b):
    s = z.shape[0]
    tm = min(MM_TILE, s)
    tn = MM_TILE
    kw = branches[0].shape[1]
    return pl.pallas_call(
        _merge_body,
        grid=(s // tm, D_MODEL // tn),
        in_specs=([pl.BlockSpec((tm, kw), lambda i, j: (i, 0))] * 4
                  + [pl.BlockSpec((kw, tn), lambda i, j: (0, j))] * 4
                  + [pl.BlockSpec((tm, tn), lambda i, j, b=b: (i, (Z_GATE + b * D_MODEL) // tn + j))
                     for b in range(N_BRANCH)]
                  + [pl.BlockSpec((N_BRANCH, tn), lambda i, j: (0, j))]),
        out_specs=pl.BlockSpec((tm, tn), lambda i, j: (i, j)),
        out_shape=jax.ShapeDtypeStruct((s, D_MODEL), BF16),
        compiler_params=_cparams(("parallel", "arbitrary")),
        name="merge",
    )(*branches, *w_outs, z, z, z, z, gate_b)


def _order(v, i, j):
    v[i], v[j] = jnp.maximum(v[i], v[j]), jnp.minimum(v[i], v[j])


def _bitonic_sort_desc(v):
    n = len(v)
    k = 2
    while k <= n:
        j = k // 2
        while j >= 1:
            for i in range(n):
                partner = i ^ j
                if partner > i:
                    if i & k == 0:
                        _order(v, i, partner)
                    else:
                        _order(v, partner, i)
            j //= 2
        k *= 2


def _bitonic_merge_desc(v):
    n = len(v)
    j = n // 2
    while j >= 1:
        for i in range(n):
            if i ^ j > i:
                _order(v, i, i ^ j)
        j //= 2


def _top16_desc(groups):
    v = list(groups)
    _bitonic_sort_desc(v)
    for shift in (SUBLANES // 2, SUBLANES // 4, SUBLANES // 8):
        other = [pltpu.roll(x, shift, 0) for x in v]
        v = [jnp.maximum(v[k], other[PEER_TOPK - 1 - k]) for k in range(PEER_TOPK)]
        _bitonic_merge_desc(v)
    return v


def _by_sublane(v, start):
    sub = lax.broadcasted_iota(jnp.int32, v[0].shape, 0)
    out = v[start]
    for r in range(1, SUBLANES):
        out = jnp.where(sub == r, v[start + r], out)
    return out


def _peer_candidates(a, a_hi, b_lo, b_hi, b0, op):
    groups = [op(a[0], b_lo), op(a[0], b_hi)]
    groups += [op(a[i], b_lo) for i in range(1, SUBLANES)]
    groups.append(op(a_hi, b0))
    return groups


def _peer_scores_body(q_ref, k1_ref, k2_ref, th_ref, s2_ref, e1_ref, e2n_ref):
    tt = q_ref.shape[0]
    n_groups = PEER_N_KEYS // SUBLANES

    def head(h, carry):
        qh = q_ref[:, pl.ds(pl.multiple_of(h * PEER_KEY_DIM, PEER_KEY_DIM), PEER_KEY_DIM)]
        dims = (((1,), (1,)), ((), ()))
        s1 = lax.dot_general(k1_ref[...], qh, dims, precision=lax.Precision.HIGHEST, preferred_element_type=F32)
        s2 = lax.dot_general(k2_ref[...], qh, dims, precision=lax.Precision.HIGHEST, preferred_element_type=F32)
        split = lambda s: [s[g * SUBLANES:(g + 1) * SUBLANES, :] for g in range(n_groups)]
        t1 = _top16_desc(split(s1))
        t2 = _top16_desc(split(s2))
        t1_hi, t2_lo, t2_hi = _by_sublane(t1, SUBLANES), _by_sublane(t2, 0), _by_sublane(t2, SUBLANES)
        cand = _peer_candidates(t1, t1_hi, t2_lo, t2_hi, t2[0], jnp.add)
        pad = [jnp.full((SUBLANES, tt), -jnp.inf, F32)] * (PEER_TOPK - len(cand))
        tau8 = _top16_desc(cand + pad)[PEER_TOPK - 1]

        m1, m2 = t1[0], t2[0]
        c1 = [jnp.exp(x - m1) for x in t1[:SUBLANES]]
        ec = _peer_candidates(c1, jnp.exp(t1_hi - m1), jnp.exp(t2_lo - m2), jnp.exp(t2_hi - m2),
                              jnp.ones((SUBLANES, tt), F32), jnp.multiply)
        zsum = jnp.zeros((SUBLANES, tt), F32)
        for cg, eg in zip(cand, ec):
            zsum = zsum + jnp.where(cg >= tau8, eg, 0.0)
        zden = jnp.sum(zsum, axis=0, keepdims=True)
        tau = tau8[0:1]
        th = jnp.full(s1.shape, jnp.inf, F32)
        for b in range(PEER_TOPK):
            t2b = t2[b][0:1]
            th = jnp.where(s1 + t2b >= tau, t2b, th)
        th_ref[h] = th
        s2_ref[h] = s2
        e1_ref[h] = jnp.exp(s1 - m1[0:1])
        e2n_ref[h] = jnp.exp(s2 - m2[0:1]) / zden
        return carry

    lax.fori_loop(0, PEER_HEADS, head, 0)


def _peer_scores(q, k1, k2):
    s = q.shape[0]
    tt = min(SCORE_TILE, s)
    tab = pl.BlockSpec((PEER_HEADS, PEER_N_KEYS, tt), lambda i: (0, 0, i))
    tab_shape = jax.ShapeDtypeStruct((PEER_HEADS, PEER_N_KEYS, s), F32)
    return pl.pallas_call(
        _peer_scores_body,
        grid=(s // tt,),
        in_specs=[pl.BlockSpec((tt, PEER_HEADS * PEER_KEY_DIM), lambda i: (i, 0)),
                  pl.BlockSpec(k1.shape, lambda i: (0, 0)), pl.BlockSpec(k2.shape, lambda i: (0, 0))],
        out_specs=[tab, tab, tab, tab],
        out_shape=[tab_shape, tab_shape, tab_shape, tab_shape],
        compiler_params=_cparams(("parallel",)),
        name="peer_scores",
    )(q, k1, k2)


def _peer_items(s, n_items, nb):
    split = lambda it: (it // nb, it % nb)
    return (split(jnp.minimum(s, n_items - 1)), split(jnp.clip(s - 1, 0, n_items - 1)),
            split(jnp.clip(s - 2, 0, n_items - 1)))


def _peer_dense_body(ht_ref, u_ref, v_ref, th_ref, e1_ref, s2_ref, e2n_ref, o_ref, a0, a1, w0, w1,
                     *, n_items, nb):
    s = pl.program_id(0)
    tt = ht_ref.shape[1]
    eb = u_ref.shape[0]
    d = ht_ref.shape[0]
    n_sub = eb // PEER_N_KEYS
    jrows = 2 * SUBLANES
    a_rows = eb // PEER_TRIPS

    @pl.when(s == 0)
    def _():
        for ref in (a0, a1, w0, w1):
            ref[...] = jnp.zeros_like(ref)

    _, _, (_, block_c) = _peer_items(s, n_items, nb)

    @pl.when(block_c == 0)
    def _():
        o_ref[...] = jnp.zeros_like(o_ref)

    def run(a_new, a_old, w_new, w_old):
        def stage_a(k):
            rows = pl.ds(pl.multiple_of(k * a_rows, a_rows), a_rows)
            a_t = jnp.dot(u_ref[rows, :], ht_ref[...], preferred_element_type=F32)
            a_new[rows, :] = 0.5 * a_t * (1.0 + lax.erf(a_t * (2.0 ** -0.5)))

        def stage_b(c, jc):
            cols = pl.ds(pl.multiple_of(c * LANES, LANES), LANES)
            jr = slice(jc * jrows, (jc + 1) * jrows)
            g = [jnp.zeros((jrows, LANES), F32) for _ in range(n_sub)]
            for h in range(PEER_HEADS):
                s2c = s2_ref[h, jr, cols]
                e2c = e2n_ref[h, jr, cols]
                for il in range(n_sub):
                    gate = e2c * e1_ref[h, il:il + 1, cols]
                    g[il] = g[il] + jnp.where(s2c >= th_ref[h, il:il + 1, cols], gate, 0.0)
            for il in range(n_sub):
                rows = slice(il * PEER_N_KEYS + jc * jrows, il * PEER_N_KEYS + (jc + 1) * jrows)
                w_new[rows, cols] = (a_old[rows, cols] * g[il]).astype(w_new.dtype)

        def stage_c(n):
            rows = pl.ds(pl.multiple_of(n * PEER_CHUNK, PEER_CHUNK), PEER_CHUNK)
            o_ref[rows, :] += lax.dot_general(v_ref[:, rows], w_old[...], (((0,), (0,)), ((), ())),
                                              preferred_element_type=F32)

        n_c = d // PEER_CHUNK // PEER_TRIPS
        n_col = tt // LANES // PEER_TRIPS
        assert n_c * PEER_TRIPS * PEER_CHUNK == d and n_col * PEER_TRIPS * LANES == tt and n_c % 2 == 0

        def trip(k, carry):
            stage_a(k)
            for n in range(n_c // 2):
                stage_c(n_c * k + n)
            for c in range(n_col):
                for jc in range(PEER_N_KEYS // jrows):
                    stage_b(n_col * k + c, jc)
            for n in range(n_c // 2, n_c):
                stage_c(n_c * k + n)
            return carry

        lax.fori_loop(0, PEER_TRIPS, trip, 0)

    parity = lax.rem(s, 2)

    @pl.when(parity == 0)
    def _():
        run(a0, a1, w1, w0)

    @pl.when(parity == 1)
    def _():
        run(a1, a0, w0, w1)


def _peer_dense(hn_t, u_all, v_all, layer, th, e1, s2, e2n):
    d, s = hn_t.shape
    tt = min(PEER_TOKENS, s)
    eb = PEER_EXPERTS
    nb = u_all.shape[1] // eb
    n_items = (s // tt) * nb
    items = functools.partial(_peer_items, n_items=n_items, nb=nb)
    sub = pl.BlockSpec((PEER_HEADS, eb // PEER_N_KEYS, tt), lambda i: (0, items(i)[1][1], items(i)[1][0]))
    tab = pl.BlockSpec((PEER_HEADS, PEER_N_KEYS, tt), lambda i: (0, 0, items(i)[1][0]))
    return pl.pallas_call(
        functools.partial(_peer_dense_body, n_items=n_items, nb=nb),
        grid=(n_items + 2,),
        in_specs=[pl.BlockSpec((d, tt), lambda i: (0, items(i)[0][0])),
                  pl.BlockSpec((None, eb, d), lambda i: (layer, items(i)[0][1], 0)),
                  pl.BlockSpec((None, eb, d), lambda i: (layer, items(i)[2][1], 0)),
                  sub, sub, tab, tab],
        out_specs=pl.BlockSpec((d, tt), lambda i: (0, items(i)[2][0])),
        out_shape=jax.ShapeDtypeStruct((d, s), F32),
        scratch_shapes=[pltpu.VMEM((eb, tt), F32), pltpu.VMEM((eb, tt), F32),
                        pltpu.VMEM((eb, tt), BF16), pltpu.VMEM((eb, tt), BF16)],
        compiler_params=_cparams(("arbitrary",)),
        name="peer_dense",
    )(hn_t, u_all, v_all, th, e1, s2, e2n)


def _add_norm_body(x_ref, dt_ref, g_ref, *out_refs):
    x = x_ref[...] + dt_ref[...].T
    y = x * lax.rsqrt(jnp.mean(x * x, axis=-1, keepdims=True) + NORM_EPS)
    out_refs[-1][...] = (y * g_ref[...]).astype(out_refs[-1].dtype)
    if len(out_refs) == 2:
        out_refs[0][...] = x


def _add_norm(x, delta_t, g, norm_dtype, with_sum):
    s, d = x.shape
    tm = min(ROW_TILE, s)
    row = pl.BlockSpec((tm, d), lambda i: (i, 0))
    out_specs = [row, row] if with_sum else [row]
    out_shape = [jax.ShapeDtypeStruct((s, d), norm_dtype)]
    if with_sum:
        out_shape.insert(0, jax.ShapeDtypeStruct((s, d), F32))
    return pl.pallas_call(
        _add_norm_body,
        grid=(s // tm,),
        in_specs=[row, pl.BlockSpec((d, tm), lambda i: (0, i)), pl.BlockSpec((1, d), lambda i: (0, 0))],
        out_specs=out_specs,
        out_shape=out_shape,
        compiler_params=_cparams(("parallel",)),
        name="add_norm",
    )(x, delta_t, g.reshape(1, d))


def _rope_tables(positions, dim, lanes_x1):
    half = dim // 2
    inv_freq = ROPE_THETA ** (-jnp.arange(0, dim, 2, dtype=F32) / dim)
    ang = positions.astype(F32)[:, None] * inv_freq
    cos, sin = jnp.cos(ang), jnp.sin(ang)
    s = positions.shape[0]
    c_parts, su_parts, sd_parts = [], [], []
    pos = 0
    for off in lanes_x1:
        gap = off - pos
        c_parts += [jnp.ones((s, gap), F32), cos, cos]
        su_parts += [jnp.zeros((s, gap), F32), -sin, jnp.zeros((s, half), F32)]
        sd_parts += [jnp.zeros((s, gap + half), F32), sin]
        pos = off + dim
    tail = LANES - pos
    c_parts.append(jnp.ones((s, tail), F32))
    su_parts.append(jnp.zeros((s, tail), F32))
    sd_parts.append(jnp.zeros((s, tail), F32))
    return tuple(jnp.concatenate(p, axis=1) for p in (c_parts, su_parts, sd_parts))


W_IN_HEAD = IN_OFFSETS[3]
W_IN_RUN = sum(IN_SIZES[3:])
LAYOUT_ROWS = 1024


def _in_proj_body(h_ref, a_ref, b_ref, o_ref, wt_ref):
    j = pl.program_id(0)
    rb = wt_ref.shape[0]
    lead = rb - W_IN_HEAD % rb
    new_block = pl.program_id(1) == 0

    @pl.when(new_block & (j < W_IN_RUN // rb))
    def _():
        wt_ref[0:lead, :] = a_ref[W_IN_HEAD % rb:rb, :].astype(wt_ref.dtype)
        wt_ref[lead:rb, :] = b_ref[0:rb - lead, :].astype(wt_ref.dtype)

    @pl.when(new_block & (j == W_IN_RUN // rb))
    def _():
        head = MLA_Q_LORA + MLA_KV_LORA
        wt_ref[0:head, :] = a_ref[0:head, :].astype(wt_ref.dtype)
        wt_ref[head:rb, :] = jnp.zeros((rb - head, wt_ref.shape[1]), wt_ref.dtype)
        wt_ref[head + MLA_NOPE:head + MLA_NOPE + MLA_ROPE, :] = a_ref[head:head + MLA_ROPE, :].astype(wt_ref.dtype)

    o_ref[...] = lax.dot_general(h_ref[...], wt_ref[...], (((1,), (1,)), ((), ())),
                                 preferred_element_type=F32).astype(o_ref.dtype)


def _in_proj(hn, w_in_t, layer):
    s, d = hn.shape
    n = w_in_t.shape[1]
    rb = LAYOUT_ROWS
    tm = min(MM_TILE, s)
    assert W_IN_RUN % rb == 0 and Z_CQ == W_IN_RUN and Z_COLS == W_IN_RUN + rb and W_IN_HEAD < rb
    run_blocks = W_IN_RUN // rb
    first = lambda j: jnp.where(j < run_blocks, j + W_IN_HEAD // rb, 0)
    second = lambda j: jnp.minimum(j + W_IN_HEAD // rb + 1, (n - 1) // rb)
    return pl.pallas_call(
        _in_proj_body,
        grid=(Z_COLS // rb, s // tm),
        in_specs=[pl.BlockSpec((tm, d), lambda j, i: (i, 0)),
                  pl.BlockSpec((None, rb, d), lambda j, i: (layer, first(j), 0)),
                  pl.BlockSpec((None, rb, d), lambda j, i: (layer, second(j), 0))],
        out_specs=pl.BlockSpec((tm, rb), lambda j, i: (i, j)),
        out_shape=jax.ShapeDtypeStruct((s, Z_COLS), BF16),
        scratch_shapes=[pltpu.VMEM((rb, d), BF16)],
        compiler_params=_cparams(("parallel", "arbitrary")),
        name="in_proj",
    )(hn, w_in_t, w_in_t)


def _layout_mla_weights(w_uq, w_ukv):
    lq = w_uq.shape[0]
    wq = w_uq.reshape(lq, MLA_HEADS, MLA_NOPE + MLA_ROPE)
    wq = jnp.pad(wq, ((0, 0), (0, 0), (0, LANES - MLA_NOPE - MLA_ROPE))).reshape(lq, MLA_HEADS * LANES)
    lk = w_ukv.shape[0]
    wkv = w_ukv.reshape(lk, MLA_HEADS, MLA_NOPE + MLA_V)
    wk = jnp.pad(wkv[:, :, :MLA_NOPE], ((0, 0), (0, 0), (0, LANES - MLA_NOPE))).reshape(lk, MLA_HEADS * LANES)
    wv = wkv[:, :, MLA_NOPE:].reshape(lk, MLA_HEADS * MLA_V)
    return wq.T.astype(BF16), wk.astype(BF16), wv.T.astype(BF16)


def kernel(x, positions, mix_norm_g, w_in, gate_b, mla_q_norm_g, mla_w_uq, mla_kv_norm_g, mla_w_ukv, mla_w_out,
           sc_conv_w, sc_w_out, diff_lambda, diff_norm_g, diff_w_out, conf_dw_w, conf_dw_b, conf_ln_g, conf_ln_b,
           conf_w_out, w_o, ffn_norm_g, peer_w_q, peer_sub_keys, peer_u, peer_v, final_norm_g):
    b, s, d = x.shape
    assert b == 1 and d == D_MODEL
    xs = x.reshape(s, d)
    pos = positions.reshape(s)
    mla_tabs = _rope_tables(pos, MLA_ROPE, (MLA_NOPE,))
    mla_tabs_t = tuple(t.T for t in mla_tabs)
    w_in_t = jnp.swapaxes(w_in, 1, 2)
    u_all = peer_u.astype(BF16)
    v_all = peer_v.astype(BF16)
    diff_tabs = _rope_tables(pos, DIFF_ROT, (0, DIFF_HEAD_DIM))

    for i in range(DEPTH):
        lam_init = 0.8 - 0.6 * math.exp(-0.3 * i)
        if i == 0:
            hn = _rmsnorm(xs, mix_norm_g[i], BF16)
        else:
            xs, hn = _add_norm(xs, delta_t, mix_norm_g[i], BF16, with_sum=True)
        z = _in_proj(hn, w_in_t, i)
        wqt, wk, wvt = _layout_mla_weights(mla_w_uq[i], mla_w_ukv[i])
        qt, k, vt = _mla_prep(z, mla_tabs, mla_tabs_t, mla_q_norm_g[i].reshape(1, -1),
                              mla_kv_norm_g[i].reshape(1, -1), wqt, wk, wvt)
        o_mla = _mla_attn(qt, k, vt)
        o_sc, o_conf = _conv_branches(z, sc_conv_w[i], conf_dw_w[i], conf_dw_b[i], conf_ln_g[i], conf_ln_b[i])
        qdt, kd, vdt = _diff_prep(z, diff_tabs)
        o_diff = _diff_attn(qdt, kd, vdt, diff_lambda[i], diff_norm_g[i], lam_init)
        merged = _merge((o_mla, o_sc, o_diff, o_conf),
                        tuple(w.astype(BF16) for w in (mla_w_out[i], sc_w_out[i], diff_w_out[i], conf_w_out[i])),
                        z, gate_b[i])
        xs = _matmul(merged, w_o[i].astype(BF16), F32, residual=xs, name="out_proj")
        hf, hf_t = _rmsnorm(xs, ffn_norm_g[i], BF16, with_transposed=True)
        pq = _matmul(hf, peer_w_q[i].astype(BF16), F32, name="peer_q")
        zk = jnp.zeros((PEER_N_KEYS, PEER_HALF), F32)
        k1 = jnp.concatenate([peer_sub_keys[i, 0], zk], axis=1)
        k2 = jnp.concatenate([zk, peer_sub_keys[i, 1]], axis=1)
        th, s2, e1, e2n = _peer_scores(pq, k1, k2)
        delta_t = _peer_dense(hf_t, u_all, v_all, i, th, e1, s2, e2n)
    (out,) = _add_norm(xs, delta_t, final_norm_g, F32, with_sum=False)
    return out.reshape(b, s, d)
```

```python
import functools
import math

import jax
import jax.numpy as jnp
import numpy as np
from jax import lax
from jax.experimental import pallas as pl
from jax.experimental.pallas import tpu as pltpu

F32 = jnp.float32
BF16 = jnp.bfloat16

D_MODEL = 2048
DEPTH = 2
CHUNK = 64
ROPE_THETA = 500000.0
NORM_EPS = 1e-6
LN_EPS = 1e-5
NEG_INF = -1e30
N_BRANCH = 4

MLA_HEADS = 8
MLA_Q_LORA = 512
MLA_KV_LORA = 256
MLA_NOPE = 64
MLA_ROPE = 32
MLA_V = 64
MLA_SCALE = (MLA_NOPE + MLA_ROPE) ** -0.5

SC_WIDTH = 512
SC_KERNEL = 3

DIFF_HEADS = 4
DIFF_HEAD_DIM = 64
DIFF_ROT = DIFF_HEAD_DIM // 4
DIFF_SCALE = DIFF_HEAD_DIM ** -0.5
DIFF_WIDTH = DIFF_HEADS * 2 * DIFF_HEAD_DIM

CONF_WIDTH = 512
CONF_KERNEL = 31

PEER_HEADS = 8
PEER_N_KEYS = 128
PEER_N_EXPERTS = PEER_N_KEYS * PEER_N_KEYS
PEER_KEY_DIM = 128
PEER_HALF = PEER_KEY_DIM // 2
PEER_TOPK = 16

IN_SIZES = (MLA_Q_LORA, MLA_KV_LORA, MLA_ROPE, 3 * SC_WIDTH, DIFF_WIDTH, DIFF_WIDTH, DIFF_WIDTH,
            2 * CONF_WIDTH, N_BRANCH * D_MODEL)
IN_OFFSETS = tuple(int(v) for v in np.cumsum((0,) + IN_SIZES)[:-1])

LOG2E = math.log2(math.e)

LANES = 128
SUBLANES = 8
HALO = 32
VMEM_LIMIT = 56 * 1024 * 1024

ROW_TILE = 512
MM_TILE = 1024
ATTN_TILE = 512
MLA_GROUP = 8
DIFF_GROUP = 4
SCORE_TILE = 1024
PEER_TOKENS = 512
PEER_EXPERTS = 1024
PEER_CHUNK = 256
PEER_TRIPS = 1

Z_SC = 0
Z_DQ = 1536
Z_DK = 2048
Z_DV = 2560
Z_CONF = 3072
Z_GATE = 4096
Z_CQ = 12288
Z_CKV = 12800
Z_KR = 13056
Z_COLS = 13312


def _cparams(sem):
    return pltpu.CompilerParams(dimension_semantics=sem, vmem_limit_bytes=VMEM_LIMIT)


def _rmsnorm_body(x_ref, g_ref, o_ref, *maybe_ot_ref):
    x = x_ref[...].astype(F32)
    y = x * lax.rsqrt(jnp.mean(x * x, axis=-1, keepdims=True) + NORM_EPS) * g_ref[...]
    o_ref[...] = y.astype(o_ref.dtype)
    for ot_ref in maybe_ot_ref:
        ot_ref[...] = y.T.astype(ot_ref.dtype)


def _rmsnorm(x, g, out_dtype, with_transposed=False):
    s, d = x.shape
    tm = min(ROW_TILE, s)
    out_specs = [pl.BlockSpec((tm, d), lambda i: (i, 0))]
    out_shape = [jax.ShapeDtypeStruct((s, d), out_dtype)]
    if with_transposed:
        out_specs.append(pl.BlockSpec((d, tm), lambda i: (0, i)))
        out_shape.append(jax.ShapeDtypeStruct((d, s), out_dtype))
    out = pl.pallas_call(
        _rmsnorm_body,
        grid=(s // tm,),
        in_specs=[pl.BlockSpec((tm, d), lambda i: (i, 0)), pl.BlockSpec((1, d), lambda i: (0, 0))],
        out_specs=out_specs,
        out_shape=out_shape,
        compiler_params=_cparams(("parallel",)),
        name="rmsnorm",
    )(x, g.reshape(1, d))
    return out if with_transposed else out[0]


def _matmul_body(a_ref, b_ref, o_ref):
    o_ref[...] = jnp.dot(a_ref[...], b_ref[...], preferred_element_type=F32).astype(o_ref.dtype)


def _matmul_res_body(a_ref, b_ref, r_ref, o_ref):
    o_ref[...] = (r_ref[...] + jnp.dot(a_ref[...], b_ref[...], preferred_element_type=F32)).astype(o_ref.dtype)


def _matmul(a, b, out_dtype, residual=None, name="matmul"):
    m, k = a.shape
    n = b.shape[1]
    tm, tn = min(MM_TILE, m), min(MM_TILE, n)
    in_specs = [pl.BlockSpec((tm, k), lambda i, j: (i, 0)), pl.BlockSpec((k, tn), lambda i, j: (0, j))]
    args = [a, b]
    body = _matmul_body
    if residual is not None:
        in_specs.append(pl.BlockSpec((tm, tn), lambda i, j: (i, j)))
        args.append(residual)
        body = _matmul_res_body
    return pl.pallas_call(
        body,
        grid=(m // tm, n // tn),
        in_specs=in_specs,
        out_specs=pl.BlockSpec((tm, tn), lambda i, j: (i, j)),
        out_shape=jax.ShapeDtypeStruct((m, n), out_dtype),
        compiler_params=_cparams(("parallel", "arbitrary")),
        name=name,
    )(*args)


def _sigmoid(x):
    return 0.5 + 0.5 * jnp.tanh(0.5 * x)


def _rope128(x, c, s_up, s_dn, half):
    return x * c + pltpu.roll(x, LANES - half, 1) * s_up + pltpu.roll(x, half, 1) * s_dn


def _mla_prep_body(cq_ref, ckv_ref, kr_ref, c_ref, su_ref, sd_ref, ct_ref, sut_ref, sdt_ref,
                   qg_ref, kvg_ref, wqt_ref, wk_ref, wvt_ref, qt_ref, k_ref, vt_ref):
    half = MLA_ROPE // 2
    nt = (((1,), (1,)), ((), ()))

    def norm(x_ref, g_ref):
        x = x_ref[...].astype(F32)
        y = x * lax.rsqrt(jnp.mean(x * x, axis=-1, keepdims=True) + NORM_EPS)
        return (y * g_ref[...]).astype(BF16)

    qf_t = lax.dot_general(wqt_ref[...], norm(cq_ref, qg_ref), nt, preferred_element_type=F32)
    ckvn = norm(ckv_ref, kvg_ref)
    kf = jnp.dot(ckvn, wk_ref[...], preferred_element_type=F32)
    vt_ref[...] = lax.dot_general(wvt_ref[...], ckvn, nt, preferred_element_type=F32).astype(vt_ref.dtype)
    kr = _rope128(kr_ref[...].astype(F32), c_ref[...], su_ref[...], sd_ref[...], half)
    ct, sut, sdt = ct_ref[...], sut_ref[...], sdt_ref[...]
    for h in range(MLA_HEADS):
        sl = slice(h * LANES, (h + 1) * LANES)
        x = qf_t[sl, :]
        q = x * ct + pltpu.roll(x, LANES - half, 0) * sut + pltpu.roll(x, half, 0) * sdt
        qt_ref[sl, :] = (q * (MLA_SCALE * LOG2E)).astype(qt_ref.dtype)
        k_ref[:, sl] = (kf[:, sl] + kr).astype(k_ref.dtype)


def _mla_prep(z, tabs, tabs_t, qg, kvg, wqt, wk, wvt):
    s = z.shape[0]
    tm = min(ROW_TILE, s)
    row = lambda w, idx: pl.BlockSpec((tm, w), lambda i: (i, idx))
    full = lambda a: pl.BlockSpec(a.shape, lambda i: (0,) * a.ndim)
    tab = pl.BlockSpec((tm, LANES), lambda i: (i, 0))
    tab_t = pl.BlockSpec((LANES, tm), lambda i: (0, i))
    hw = MLA_HEADS * LANES
    vw = MLA_HEADS * MLA_V
    return pl.pallas_call(
        _mla_prep_body,
        grid=(s // tm,),
        in_specs=[row(MLA_Q_LORA, Z_CQ // MLA_Q_LORA), row(MLA_KV_LORA, Z_CKV // MLA_KV_LORA),
                  row(LANES, Z_KR // LANES), tab, tab, tab, tab_t, tab_t, tab_t,
                  full(qg), full(kvg), full(wqt), full(wk), full(wvt)],
        out_specs=[pl.BlockSpec((hw, tm), lambda i: (0, i)), pl.BlockSpec((tm, hw), lambda i: (i, 0)),
                   pl.BlockSpec((vw, tm), lambda i: (0, i))],
        out_shape=[jax.ShapeDtypeStruct((hw, s), BF16), jax.ShapeDtypeStruct((s, hw), BF16),
                   jax.ShapeDtypeStruct((vw, s), BF16)],
        compiler_params=_cparams(("parallel",)),
        name="mla_prep",
    )(z, z, z, *tabs, *tabs_t, qg, kvg, wqt, wk, wvt)


def _flash_streams_t(q_ts, k_ats, vt_ats, m_ref, l_ref, acc_ref, s_a, s_b, n_full, t):
    streams = range(len(q_ts))
    for n in streams:
        m_ref[n] = jnp.full((1, t), NEG_INF, F32)
        l_ref[n] = jnp.zeros((1, t), F32)
        acc_ref[n] = jnp.zeros(acc_ref.shape[1:], F32)

    def scores_into(s_ref, j):
        for n in streams:
            s_ref[n] = jnp.dot(k_ats[n](j), q_ts[n], preferred_element_type=F32)

    def consume(s_ref, j, masked):
        probs, alphas = [], []
        for n in streams:
            s = s_ref[n]
            if masked:
                key_chunk = lax.broadcasted_iota(jnp.int32, (t, t), 0) // CHUNK
                qry_chunk = lax.broadcasted_iota(jnp.int32, (t, t), 1) // CHUNK
                s = jnp.where(key_chunk <= qry_chunk, s, NEG_INF)
            m_prev = m_ref[n]
            m_new = jnp.maximum(m_prev, jnp.max(s, axis=0, keepdims=True))
            alpha = jnp.exp2(m_prev - m_new)
            p = jnp.exp2(s - m_new)
            l_ref[n] = alpha * l_ref[n] + jnp.sum(p, axis=0, keepdims=True)
            m_ref[n] = m_new
            probs.append(p.astype(BF16))
            alphas.append(alpha)
        for n in streams:
            acc_ref[n] = alphas[n] * acc_ref[n] + jnp.dot(vt_ats[n](j), probs[n], preferred_element_type=F32)

    scores_into(s_a, 0)

    def pair(jj, carry):
        j = 2 * jj
        scores_into(s_b, j + 1)
        consume(s_a, j, False)
        scores_into(s_a, j + 2)
        consume(s_b, j + 1, False)
        return carry

    lax.fori_loop(0, n_full // 2, pair, 0)

    @pl.when(n_full % 2 == 0)
    def _():
        consume(s_a, n_full, True)

    @pl.when(n_full % 2 == 1)
    def _():
        scores_into(s_b, n_full)
        consume(s_a, n_full - 1, False)
        consume(s_b, n_full, True)


def _mla_attn_body(qt_ref, k_ref, vt_ref, o_ref, m_ref, l_ref, acc_ref, s_a, s_b, *, t):
    i = pl.program_id(1)
    rows = lambda j: pl.ds(pl.multiple_of(j * t, t), t)
    heads = range(MLA_GROUP)
    lanes = [slice(hh * LANES, (hh + 1) * LANES) for hh in heads]
    k_ats = [lambda j, sl=sl: k_ref[rows(j), sl] for sl in lanes]
    vt_ats = [lambda j, hh=hh: vt_ref[hh * MLA_V:(hh + 1) * MLA_V, rows(j)] for hh in heads]
    _flash_streams_t([qt_ref[sl, :] for sl in lanes], k_ats, vt_ats, m_ref, l_ref, acc_ref, s_a, s_b, i, t)
    o_t = jnp.concatenate([acc_ref[hh] / l_ref[hh] for hh in heads], axis=0)
    o_ref[...] = o_t.T.astype(o_ref.dtype)


def _mla_attn(qt, k, vt):
    s = k.shape[0]
    t = min(ATTN_TILE, s)
    g = MLA_GROUP
    return pl.pallas_call(
        functools.partial(_mla_attn_body, t=t),
        grid=(MLA_HEADS // g, s // t),
        in_specs=[pl.BlockSpec((g * LANES, t), lambda p, i: (p, i)),
                  pl.BlockSpec((s, g * LANES), lambda p, i: (0, p), pipeline_mode=pl.Buffered(1)),
                  pl.BlockSpec((g * MLA_V, s), lambda p, i: (p, 0), pipeline_mode=pl.Buffered(1))],
        out_specs=pl.BlockSpec((t, g * MLA_V), lambda p, i: (i, p)),
        out_shape=jax.ShapeDtypeStruct((s, MLA_HEADS * MLA_V), BF16),
        scratch_shapes=[pltpu.VMEM((g, 1, t), F32), pltpu.VMEM((g, 1, t), F32),
                        pltpu.VMEM((g, MLA_V, t), F32),
                        pltpu.VMEM((g, t, t), F32), pltpu.VMEM((g, t, t), F32)],
        compiler_params=_cparams(("parallel", "arbitrary")),
        name="mla_attn",
    )(qt, k, vt)


def _diff_prep_body(dq_ref, dk_ref, dv_ref, c_ref, su_ref, sd_ref, qt_ref, k_ref, vt_ref):
    half = DIFF_ROT // 2
    c, su, sd = c_ref[...], su_ref[...], sd_ref[...]
    tm = dq_ref.shape[0]
    lane = lax.broadcasted_iota(jnp.int32, (tm, LANES), 1)
    vt_ref[...] = dv_ref[...].astype(F32).T.astype(vt_ref.dtype)
    for h in range(DIFF_HEADS):
        sl = slice(h * LANES, (h + 1) * LANES)
        q = _rope128(dq_ref[:, sl].astype(F32), c, su, sd, half) * (DIFF_SCALE * LOG2E)
        qt_ref[(2 * h) * LANES:(2 * h + 1) * LANES, :] = jnp.where(lane < DIFF_HEAD_DIM, q, 0.0).T.astype(qt_ref.dtype)
        qt_ref[(2 * h + 1) * LANES:(2 * h + 2) * LANES, :] = jnp.where(lane >= DIFF_HEAD_DIM, q, 0.0).T.astype(qt_ref.dtype)
        k_ref[:, sl] = _rope128(dk_ref[:, sl].astype(F32), c, su, sd, half).astype(k_ref.dtype)


def _diff_prep(z, tabs):
    s = z.shape[0]
    tm = min(ROW_TILE, s)
    tab = pl.BlockSpec((tm, LANES), lambda i: (i, 0))
    col = lambda start: pl.BlockSpec((tm, DIFF_WIDTH), lambda i: (i, start // DIFF_WIDTH))
    return pl.pallas_call(
        _diff_prep_body,
        grid=(s // tm,),
        in_specs=[col(Z_DQ), col(Z_DK), col(Z_DV), tab, tab, tab],
        out_specs=[pl.BlockSpec((2 * DIFF_WIDTH, tm), lambda i: (0, i)),
                   pl.BlockSpec((tm, DIFF_WIDTH), lambda i: (i, 0)),
                   pl.BlockSpec((DIFF_WIDTH, tm), lambda i: (0, i))],
        out_shape=[jax.ShapeDtypeStruct((2 * DIFF_WIDTH, s), BF16), jax.ShapeDtypeStruct((s, DIFF_WIDTH), BF16),
                   jax.ShapeDtypeStruct((DIFF_WIDTH, s), BF16)],
        compiler_params=_cparams(("parallel",)),
        name="diff_prep",
    )(z, z, z, *tabs)


def _diff_attn_body(qt_ref, k_ref, vt_ref, lam_ref, g_ref, o_ref, m_ref, l_ref, acc_ref, s_a, s_b, *, t, lam_init):
    i = pl.program_id(1)
    rows = lambda j: pl.ds(pl.multiple_of(j * t, t), t)
    q_ts, k_ats, vt_ats = [], [], []
    for hh in range(DIFF_GROUP):
        sl = slice(hh * LANES, (hh + 1) * LANES)
        for c in range(2):
            q_ts.append(qt_ref[(2 * hh + c) * LANES:(2 * hh + c + 1) * LANES, :])
            k_ats.append(lambda j, sl=sl: k_ref[rows(j), sl])
            vt_ats.append(lambda j, sl=sl: vt_ref[sl, rows(j)])
    _flash_streams_t(q_ts, k_ats, vt_ats, m_ref, l_ref, acc_ref, s_a, s_b, i, t)
    lv = lam_ref[...]
    lam = (jnp.exp(jnp.sum(lv[0:1] * lv[1:2], axis=1, keepdims=True))
           - jnp.exp(jnp.sum(lv[2:3] * lv[3:4], axis=1, keepdims=True)) + lam_init)
    for hh in range(DIFF_GROUP):
        o = (acc_ref[2 * hh] / l_ref[2 * hh] - lam * (acc_ref[2 * hh + 1] / l_ref[2 * hh + 1])).T
        o = o * lax.rsqrt(jnp.mean(o * o, axis=-1, keepdims=True) + NORM_EPS)
        o_ref[:, hh * LANES:(hh + 1) * LANES] = (o * g_ref[...] * (1.0 - lam_init)).astype(o_ref.dtype)


def _diff_attn(qt, k, vt, lam_vecs, norm_g, lam_init):
    s = k.shape[0]
    t = min(ATTN_TILE, s)
    g = DIFF_GROUP
    return pl.pallas_call(
        functools.partial(_diff_attn_body, t=t, lam_init=lam_init),
        grid=(DIFF_HEADS // g, s // t),
        in_specs=[pl.BlockSpec((2 * g * LANES, t), lambda h, i: (h, i)),
                  pl.BlockSpec((s, g * LANES), lambda h, i: (0, h)),
                  pl.BlockSpec((g * LANES, s), lambda h, i: (h, 0)),
                  pl.BlockSpec(lam_vecs.shape, lambda h, i: (0, 0)),
                  pl.BlockSpec((1, LANES), lambda h, i: (0, 0))],
        out_specs=pl.BlockSpec((t, g * LANES), lambda h, i: (i, h)),
        out_shape=jax.ShapeDtypeStruct((s, DIFF_WIDTH), BF16),
        scratch_shapes=[pltpu.VMEM((2 * g, 1, t), F32), pltpu.VMEM((2 * g, 1, t), F32),
                        pltpu.VMEM((2 * g, LANES, t), F32),
                        pltpu.VMEM((2 * g, t, t), F32), pltpu.VMEM((2 * g, t, t), F32)],
        compiler_params=_cparams(("parallel", "arbitrary")),
        name="diff_attn",
    )(qt, k, vt, lam_vecs, norm_g.reshape(1, LANES))


def _conv_body(bg_ref, cg_ref, xv_ref, cgh_ref, xvh_ref, a_ref, gt_ref, ah_ref, gth_ref,
               scw_ref, dww_ref, dwb_ref, lng_ref, lnb_ref, sc_ref, cf_ref, ext_ref):
    tm = bg_ref.shape[0]
    first = pl.program_id(0) == 0

    shifted_rows = HALO + tm - SUBLANES

    def fill(cur, halo, taps):
        ext_ref[0, 0:HALO, :] = jnp.where(first, 0.0, halo)
        ext_ref[0, HALO:HALO + tm, :] = cur
        for r in sorted({(HALO - (taps - 1) + j) % SUBLANES for j in range(taps)} - {0}):
            ext_ref[r, 0:shifted_rows, :] = ext_ref[0, r:r + shifted_rows, :]

    def conv(w_ref, taps):
        acc = jnp.zeros((tm, ext_ref.shape[2]), F32)
        for j in range(taps):
            off = HALO - (taps - 1) + j
            r = off % SUBLANES
            acc = acc + w_ref[j:j + 1, :] * ext_ref[r, off - r:off - r + tm, :]
        return acc

    fill(cg_ref[...].astype(F32) * xv_ref[...].astype(F32), cgh_ref[...].astype(F32) * xvh_ref[...].astype(F32),
         SC_KERNEL)
    sc_ref[...] = (bg_ref[...].astype(F32) * conv(scw_ref, SC_KERNEL)).astype(sc_ref.dtype)

    fill(a_ref[...].astype(F32) * _sigmoid(gt_ref[...].astype(F32)),
         ah_ref[...].astype(F32) * _sigmoid(gth_ref[...].astype(F32)), CONF_KERNEL)
    u = conv(dww_ref, CONF_KERNEL) + dwb_ref[...]
    mu = jnp.mean(u, axis=-1, keepdims=True)
    var = jnp.mean(jnp.square(u - mu), axis=-1, keepdims=True)
    y = (u - mu) * lax.rsqrt(var + LN_EPS) * lng_ref[...] + lnb_ref[...]
    cf_ref[...] = (y * _sigmoid(y)).astype(cf_ref.dtype)


def _conv_branches(z, sc_w, dw_w, dw_b, ln_g, ln_b):
    s = z.shape[0]
    tm = min(ROW_TILE, s)
    w = SC_WIDTH
    cur = lambda col: pl.BlockSpec((tm, w), lambda i: (i, col // w))
    halo = lambda col: pl.BlockSpec((HALO, w), lambda i: (jnp.maximum(i * (tm // HALO) - 1, 0), col // w))
    full = lambda a: pl.BlockSpec(a.shape, lambda i: (0,) * a.ndim)
    vec = lambda a: a.reshape(1, -1)
    args = [sc_w, dw_w, vec(dw_b), vec(ln_g), vec(ln_b)]
    return pl.pallas_call(
        _conv_body,
        grid=(s // tm,),
        in_specs=[cur(Z_SC), cur(Z_SC + w), cur(Z_SC + 2 * w), halo(Z_SC + w), halo(Z_SC + 2 * w),
                  cur(Z_CONF), cur(Z_CONF + w), halo(Z_CONF), halo(Z_CONF + w)] + [full(a) for a in args],
        out_specs=[pl.BlockSpec((tm, w), lambda i: (i, 0)), pl.BlockSpec((tm, w), lambda i: (i, 0))],
        out_shape=[jax.ShapeDtypeStruct((s, w), BF16), jax.ShapeDtypeStruct((s, w), BF16)],
        scratch_shapes=[pltpu.VMEM((SUBLANES, HALO + tm, w), F32)],
        compiler_params=_cparams(("parallel",)),
        name="conv_branches",
    )(*([z] * 9), *args)


def _merge_body(b0, b1, b2, b3, w0, w1, w2, w3, g0, g1, g2, g3, gb_ref, o_ref):
    acc = None
    for n, (b_ref, w_ref, g_ref) in enumerate(((b0, w0, g0), (b1, w1, g1), (b2, w2, g2), (b3, w3, g3))):
        y = jnp.dot(b_ref[...], w_ref[...], preferred_element_type=F32)
        t = _sigmoid(g_ref[...].astype(F32) + gb_ref[n:n + 1, :]) * y
        acc = t if acc is None else acc + t
    o_ref[...] = acc.astype(o_ref.dtype)


def _merge(branches, w_outs, z, gate_b):
    s = z.shape[0]
    tm = min(MM_TILE, s)
    tn = MM_TILE
    kw = branches[0].shape[1]
    return pl.pallas_call(
        _merge_body,
        grid=(s // tm, D_MODEL // tn),
        in_specs=([pl.BlockSpec((tm, kw), lambda i, j: (i, 0))] * 4
                  + [pl.BlockSpec((kw, tn), lambda i, j: (0, j))] * 4
                  + [pl.BlockSpec((tm, tn), lambda i, j, b=b: (i, (Z_GATE + b * D_MODEL) // tn + j))
                     for b in range(N_BRANCH)]
                  + [pl.BlockSpec((N_BRANCH, tn), lambda i, j: (0, j))]),
        out_specs=pl.BlockSpec((tm, tn), lambda i, j: (i, j)),
        out_shape=jax.ShapeDtypeStruct((s, D_MODEL), BF16),
        compiler_params=_cparams(("parallel", "arbitrary")),
        name="merge",
    )(*branches, *w_outs, z, z, z, z, gate_b)


def _order(v, i, j):
    v[i], v[j] = jnp.maximum(v[i], v[j]), jnp.minimum(v[i], v[j])


def _bitonic_sort_desc(v):
    n = len(v)
    k = 2
    while k <= n:
        j = k // 2
        while j >= 1:
            for i in range(n):
                partner = i ^ j
                if partner > i:
                    if i & k == 0:
                        _order(v, i, partner)
                    else:
                        _order(v, partner, i)
            j //= 2
        k *= 2


def _bitonic_merge_desc(v):
    n = len(v)
    j = n // 2
    while j >= 1:
        for i in range(n):
            if i ^ j > i:
                _order(v, i, i ^ j)
        j //= 2


def _top16_desc(groups):
    v = list(groups)
    _bitonic_sort_desc(v)
    for shift in (SUBLANES // 2, SUBLANES // 4, SUBLANES // 8):
        other = [pltpu.roll(x, shift, 0) for x in v]
        v = [jnp.maximum(v[k], other[PEER_TOPK - 1 - k]) for k in range(PEER_TOPK)]
        _bitonic_merge_desc(v)
    return v


def _by_sublane(v, start):
    sub = lax.broadcasted_iota(jnp.int32, v[0].shape, 0)
    out = v[start]
    for r in range(1, SUBLANES):
        out = jnp.where(sub == r, v[start + r], out)
    return out


def _peer_candidates(a, a_hi, b_lo, b_hi, b0, op):
    groups = [op(a[0], b_lo), op(a[0], b_hi)]
    groups += [op(a[i], b_lo) for i in range(1, SUBLANES)]
    groups.append(op(a_hi, b0))
    return groups


def _peer_scores_body(q_ref, k1_ref, k2_ref, th_ref, s2_ref, e1_ref, e2n_ref):
    tt = q_ref.shape[0]
    n_groups = PEER_N_KEYS // SUBLANES

    def head(h, carry):
        qh = q_ref[:, pl.ds(pl.multiple_of(h * PEER_KEY_DIM, PEER_KEY_DIM), PEER_KEY_DIM)]
        dims = (((1,), (1,)), ((), ()))
        s1 = lax.dot_general(k1_ref[...], qh, dims, precision=lax.Precision.HIGHEST, preferred_element_type=F32)
        s2 = lax.dot_general(k2_ref[...], qh, dims, precision=lax.Precision.HIGHEST, preferred_element_type=F32)
        split = lambda s: [s[g * SUBLANES:(g + 1) * SUBLANES, :] for g in range(n_groups)]
        t1 = _top16_desc(split(s1))
        t2 = _top16_desc(split(s2))
        t1_hi, t2_lo, t2_hi = _by_sublane(t1, SUBLANES), _by_sublane(t2, 0), _by_sublane(t2, SUBLANES)
        cand = _peer_candidates(t1, t1_hi, t2_lo, t2_hi, t2[0], jnp.add)
        pad = [jnp.full((SUBLANES, tt), -jnp.inf, F32)] * (PEER_TOPK - len(cand))
        tau8 = _top16_desc(cand + pad)[PEER_TOPK - 1]

        m1, m2 = t1[0], t2[0]
        c1 = [jnp.exp(x - m1) for x in t1[:SUBLANES]]
        ec = _peer_candidates(c1, jnp.exp(t1_hi - m1), jnp.exp(t2_lo - m2), jnp.exp(t2_hi - m2),
                              jnp.ones((SUBLANES, tt), F32), jnp.multiply)
        zsum = jnp.zeros((SUBLANES, tt), F32)
        for cg, eg in zip(cand, ec):
            zsum = zsum + jnp.where(cg >= tau8, eg, 0.0)
        zden = jnp.sum(zsum, axis=0, keepdims=True)
        tau = tau8[0:1]
        th = jnp.full(s1.shape, jnp.inf, F32)
        for b in range(PEER_TOPK):
            t2b = t2[b][0:1]
            th = jnp.where(s1 + t2b >= tau, t2b, th)
        th_ref[h] = th
        s2_ref[h] = s2
        e1_ref[h] = jnp.exp(s1 - m1[0:1])
        e2n_ref[h] = jnp.exp(s2 - m2[0:1]) / zden
        return carry

    lax.fori_loop(0, PEER_HEADS, head, 0)


def _peer_scores(q, k1, k2):
    s = q.shape[0]
    tt = min(SCORE_TILE, s)
    tab = pl.BlockSpec((PEER_HEADS, PEER_N_KEYS, tt), lambda i: (0, 0, i))
    tab_shape = jax.ShapeDtypeStruct((PEER_HEADS, PEER_N_KEYS, s), F32)
    return pl.pallas_call(
        _peer_scores_body,
        grid=(s // tt,),
        in_specs=[pl.BlockSpec((tt, PEER_HEADS * PEER_KEY_DIM), lambda i: (i, 0)),
                  pl.BlockSpec(k1.shape, lambda i: (0, 0)), pl.BlockSpec(k2.shape, lambda i: (0, 0))],
        out_specs=[tab, tab, tab, tab],
        out_shape=[tab_shape, tab_shape, tab_shape, tab_shape],
        compiler_params=_cparams(("parallel",)),
        name="peer_scores",
    )(q, k1, k2)


def _peer_items(s, n_items, nb):
    split = lambda it: (it // nb, it % nb)
    return (split(jnp.minimum(s, n_items - 1)), split(jnp.clip(s - 1, 0, n_items - 1)),
            split(jnp.clip(s - 2, 0, n_items - 1)))


def _peer_dense_body(ht_ref, u_ref, v_ref, th_ref, e1_ref, s2_ref, e2n_ref, o_ref, a0, a1, w0, w1,
                     *, n_items, nb):
    s = pl.program_id(0)
    tt = ht_ref.shape[1]
    eb = u_ref.shape[0]
    d = ht_ref.shape[0]
    n_sub = eb // PEER_N_KEYS
    jrows = 2 * SUBLANES
    a_rows = eb // PEER_TRIPS

    @pl.when(s == 0)
    def _():
        for ref in (a0, a1, w0, w1):
            ref[...] = jnp.zeros_like(ref)

    _, _, (_, block_c) = _peer_items(s, n_items, nb)

    @pl.when(block_c == 0)
    def _():
        o_ref[...] = jnp.zeros_like(o_ref)

    def run(a_new, a_old, w_new, w_old):
        def stage_a(k):
            rows = pl.ds(pl.multiple_of(k * a_rows, a_rows), a_rows)
            a_t = jnp.dot(u_ref[rows, :], ht_ref[...], preferred_element_type=F32)
            a_new[rows, :] = 0.5 * a_t * (1.0 + lax.erf(a_t * (2.0 ** -0.5)))

        def stage_b(c, jc):
            cols = pl.ds(pl.multiple_of(c * LANES, LANES), LANES)
            jr = slice(jc * jrows, (jc + 1) * jrows)
            g = [jnp.zeros((jrows, LANES), F32) for _ in range(n_sub)]
            for h in range(PEER_HEADS):
                s2c = s2_ref[h, jr, cols]
                e2c = e2n_ref[h, jr, cols]
                for il in range(n_sub):
                    gate = e2c * e1_ref[h, il:il + 1, cols]
                    g[il] = g[il] + jnp.where(s2c >= th_ref[h, il:il + 1, cols], gate, 0.0)
            for il in range(n_sub):
                rows = slice(il * PEER_N_KEYS + jc * jrows, il * PEER_N_KEYS + (jc + 1) * jrows)
                w_new[rows, cols] = (a_old[rows, cols] * g[il]).astype(w_new.dtype)

        def stage_c(n):
            rows = pl.ds(pl.multiple_of(n * PEER_CHUNK, PEER_CHUNK), PEER_CHUNK)
            o_ref[rows, :] += lax.dot_general(v_ref[:, rows], w_old[...], (((0,), (0,)), ((), ())),
                                              preferred_element_type=F32)

        n_c = d // PEER_CHUNK // PEER_TRIPS
        n_col = tt // LANES // PEER_TRIPS
        assert n_c * PEER_TRIPS * PEER_CHUNK == d and n_col * PEER_TRIPS * LANES == tt and n_c % 2 == 0

        def trip(k, carry):
            stage_a(k)
            for n in range(n_c // 2):
                stage_c(n_c * k + n)
            for c in range(n_col):
                for jc in range(PEER_N_KEYS // jrows):
                    stage_b(n_col * k + c, jc)
            for n in range(n_c // 2, n_c):
                stage_c(n_c * k + n)
            return carry

        lax.fori_loop(0, PEER_TRIPS, trip, 0)

    parity = lax.rem(s, 2)

    @pl.when(parity == 0)
    def _():
        run(a0, a1, w1, w0)

    @pl.when(parity == 1)
    def _():
        run(a1, a0, w0, w1)


def _peer_dense(hn_t, u_all, v_all, layer, th, e1, s2, e2n):
    d, s = hn_t.shape
    tt = min(PEER_TOKENS, s)
    eb = PEER_EXPERTS
    nb = u_all.shape[1] // eb
    n_items = (s // tt) * nb
    items = functools.partial(_peer_items, n_items=n_items, nb=nb)
    sub = pl.BlockSpec((PEER_HEADS, eb // PEER_N_KEYS, tt), lambda i: (0, items(i)[1][1], items(i)[1][0]))
    tab = pl.BlockSpec((PEER_HEADS, PEER_N_KEYS, tt), lambda i: (0, 0, items(i)[1][0]))
    return pl.pallas_call(
        functools.partial(_peer_dense_body, n_items=n_items, nb=nb),
        grid=(n_items + 2,),
        in_specs=[pl.BlockSpec((d, tt), lambda i: (0, items(i)[0][0])),
                  pl.BlockSpec((None, eb, d), lambda i: (layer, items(i)[0][1], 0)),
                  pl.BlockSpec((None, eb, d), lambda i: (layer, items(i)[2][1], 0)),
                  sub, sub, tab, tab],
        out_specs=pl.BlockSpec((d, tt), lambda i: (0, items(i)[2][0])),
        out_shape=jax.ShapeDtypeStruct((d, s), F32),
        scratch_shapes=[pltpu.VMEM((eb, tt), F32), pltpu.VMEM((eb, tt), F32),
                        pltpu.VMEM((eb, tt), F32), pltpu.VMEM((eb, tt), F32)],
        compiler_params=_cparams(("arbitrary",)),
        name="peer_dense",
    )(hn_t, u_all, v_all, th, e1, s2, e2n)


def _add_norm_body(x_ref, dt_ref, g_ref, *out_refs):
    x = x_ref[...] + dt_ref[...].T
    y = x * lax.rsqrt(jnp.mean(x * x, axis=-1, keepdims=True) + NORM_EPS)
    out_refs[-1][...] = (y * g_ref[...]).astype(out_refs[-1].dtype)
    if len(out_refs) == 2:
        out_refs[0][...] = x


def _add_norm(x, delta_t, g, norm_dtype, with_sum):
    s, d = x.shape
    tm = min(ROW_TILE, s)
    row = pl.BlockSpec((tm, d), lambda i: (i, 0))
    out_specs = [row, row] if with_sum else [row]
    out_shape = [jax.ShapeDtypeStruct((s, d), norm_dtype)]
    if with_sum:
        out_shape.insert(0, jax.ShapeDtypeStruct((s, d), F32))
    return pl.pallas_call(
        _add_norm_body,
        grid=(s // tm,),
        in_specs=[row, pl.BlockSpec((d, tm), lambda i: (0, i)), pl.BlockSpec((1, d), lambda i: (0, 0))],
        out_specs=out_specs,
        out_shape=out_shape,
        compiler_params=_cparams(("parallel",)),
        name="add_norm",
    )(x, delta_t, g.reshape(1, d))


def _rope_tables(positions, dim, lanes_x1):
    half = dim // 2
    inv_freq = ROPE_THETA ** (-jnp.arange(0, dim, 2, dtype=F32) / dim)
    ang = positions.astype(F32)[:, None] * inv_freq
    cos, sin = jnp.cos(ang), jnp.sin(ang)
    s = positions.shape[0]
    c_parts, su_parts, sd_parts = [], [], []
    pos = 0
    for off in lanes_x1:
        gap = off - pos
        c_parts += [jnp.ones((s, gap), F32), cos, cos]
        su_parts += [jnp.zeros((s, gap), F32), -sin, jnp.zeros((s, half), F32)]
        sd_parts += [jnp.zeros((s, gap + half), F32), sin]
        pos = off + dim
    tail = LANES - pos
    c_parts.append(jnp.ones((s, tail), F32))
    su_parts.append(jnp.zeros((s, tail), F32))
    sd_parts.append(jnp.zeros((s, tail), F32))
    return tuple(jnp.concatenate(p, axis=1) for p in (c_parts, su_parts, sd_parts))


W_IN_HEAD = IN_OFFSETS[3]
W_IN_RUN = sum(IN_SIZES[3:])
LAYOUT_ROWS = 1024


def _in_proj_body(h_ref, a_ref, b_ref, o_ref, wt_ref):
    j = pl.program_id(0)
    rb = wt_ref.shape[0]
    lead = rb - W_IN_HEAD % rb
    new_block = pl.program_id(1) == 0

    @pl.when(new_block & (j < W_IN_RUN // rb))
    def _():
        wt_ref[0:lead, :] = a_ref[W_IN_HEAD % rb:rb, :].astype(wt_ref.dtype)
        wt_ref[lead:rb, :] = b_ref[0:rb - lead, :].astype(wt_ref.dtype)

    @pl.when(new_block & (j == W_IN_RUN // rb))
    def _():
        head = MLA_Q_LORA + MLA_KV_LORA
        wt_ref[0:head, :] = a_ref[0:head, :].astype(wt_ref.dtype)
        wt_ref[head:rb, :] = jnp.zeros((rb - head, wt_ref.shape[1]), wt_ref.dtype)
        wt_ref[head + MLA_NOPE:head + MLA_NOPE + MLA_ROPE, :] = a_ref[head:head + MLA_ROPE, :].astype(wt_ref.dtype)

    o_ref[...] = lax.dot_general(h_ref[...], wt_ref[...], (((1,), (1,)), ((), ())),
                                 preferred_element_type=F32).astype(o_ref.dtype)


def _in_proj(hn, w_in_t, layer):
    s, d = hn.shape
    n = w_in_t.shape[1]
    rb = LAYOUT_ROWS
    tm = min(MM_TILE, s)
    assert W_IN_RUN % rb == 0 and Z_CQ == W_IN_RUN and Z_COLS == W_IN_RUN + rb and W_IN_HEAD < rb
    run_blocks = W_IN_RUN // rb
    first = lambda j: jnp.where(j < run_blocks, j + W_IN_HEAD // rb, 0)
    second = lambda j: jnp.minimum(j + W_IN_HEAD // rb + 1, (n - 1) // rb)
    return pl.pallas_call(
        _in_proj_body,
        grid=(Z_COLS // rb, s // tm),
        in_specs=[pl.BlockSpec((tm, d), lambda j, i: (i, 0)),
                  pl.BlockSpec((None, rb, d), lambda j, i: (layer, first(j), 0)),
                  pl.BlockSpec((None, rb, d), lambda j, i: (layer, second(j), 0))],
        out_specs=pl.BlockSpec((tm, rb), lambda j, i: (i, j)),
        out_shape=jax.ShapeDtypeStruct((s, Z_COLS), BF16),
        scratch_shapes=[pltpu.VMEM((rb, d), BF16)],
        compiler_params=_cparams(("parallel", "arbitrary")),
        name="in_proj",
    )(hn, w_in_t, w_in_t)


def _layout_mla_weights(w_uq, w_ukv):
    lq = w_uq.shape[0]
    wq = w_uq.reshape(lq, MLA_HEADS, MLA_NOPE + MLA_ROPE)
    wq = jnp.pad(wq, ((0, 0), (0, 0), (0, LANES - MLA_NOPE - MLA_ROPE))).reshape(lq, MLA_HEADS * LANES)
    lk = w_ukv.shape[0]
    wkv = w_ukv.reshape(lk, MLA_HEADS, MLA_NOPE + MLA_V)
    wk = jnp.pad(wkv[:, :, :MLA_NOPE], ((0, 0), (0, 0), (0, LANES - MLA_NOPE))).reshape(lk, MLA_HEADS * LANES)
    wv = wkv[:, :, MLA_NOPE:].reshape(lk, MLA_HEADS * MLA_V)
    return wq.T.astype(BF16), wk.astype(BF16), wv.T.astype(BF16)


def kernel(x, positions, mix_norm_g, w_in, gate_b, mla_q_norm_g, mla_w_uq, mla_kv_norm_g, mla_w_ukv, mla_w_out,
           sc_conv_w, sc_w_out, diff_lambda, diff_norm_g, diff_w_out, conf_dw_w, conf_dw_b, conf_ln_g, conf_ln_b,
           conf_w_out, w_o, ffn_norm_g, peer_w_q, peer_sub_keys, peer_u, peer_v, final_norm_g):
    b, s, d = x.shape
    assert b == 1 and d == D_MODEL
    xs = x.reshape(s, d)
    pos = positions.reshape(s)
    mla_tabs = _rope_tables(pos, MLA_ROPE, (MLA_NOPE,))
    mla_tabs_t = tuple(t.T for t in mla_tabs)
    w_in_t = jnp.swapaxes(w_in, 1, 2)
    u_all = peer_u.astype(BF16)
    v_all = peer_v
    diff_tabs = _rope_tables(pos, DIFF_ROT, (0, DIFF_HEAD_DIM))

    for i in range(DEPTH):
        lam_init = 0.8 - 0.6 * math.exp(-0.3 * i)
        if i == 0:
            hn = _rmsnorm(xs, mix_norm_g[i], BF16)
        else:
            xs, hn = _add_norm(xs, delta_t, mix_norm_g[i], BF16, with_sum=True)
        z = _in_proj(hn, w_in_t, i)
        wqt, wk, wvt = _layout_mla_weights(mla_w_uq[i], mla_w_ukv[i])
        qt, k, vt = _mla_prep(z, mla_tabs, mla_tabs_t, mla_q_norm_g[i].reshape(1, -1),
                              mla_kv_norm_g[i].reshape(1, -1), wqt, wk, wvt)
        o_mla = _mla_attn(qt, k, vt)
        o_sc, o_conf = _conv_branches(z, sc_conv_w[i], conf_dw_w[i], conf_dw_b[i], conf_ln_g[i], conf_ln_b[i])
        qdt, kd, vdt = _diff_prep(z, diff_tabs)
        o_diff = _diff_attn(qdt, kd, vdt, diff_lambda[i], diff_norm_g[i], lam_init)
        merged = _merge((o_mla, o_sc, o_diff, o_conf),
                        tuple(w.astype(BF16) for w in (mla_w_out[i], sc_w_out[i], diff_w_out[i], conf_w_out[i])),
                        z, gate_b[i])
        xs = _matmul(merged, w_o[i].astype(BF16), F32, residual=xs, name="out_proj")
        hf, hf_t = _rmsnorm(xs, ffn_norm_g[i], BF16, with_transposed=True)
        pq = _matmul(hf, peer_w_q[i].astype(BF16), F32, name="peer_q")
        zk = jnp.zeros((PEER_N_KEYS, PEER_HALF), F32)
        k1 = jnp.concatenate([peer_sub_keys[i, 0], zk], axis=1)
        k2 = jnp.concatenate([zk, peer_sub_keys[i, 1]], axis=1)
        th, s2, e1, e2n = _peer_scores(pq, k1, k2)
        delta_t = _peer_dense(hf_t, u_all, v_all, i, th, e1, s2, e2n)
    (out,) = _add_norm(xs, delta_t, final_norm_g, F32, with_sum=False)
    return out.reshape(b, s, d)
```

```python
import functools
import math

import jax
import jax.numpy as jnp
import numpy as np
from jax import lax
from jax.experimental import pallas as pl
from jax.experimental.pallas import tpu as pltpu

F32 = jnp.float32
BF16 = jnp.bfloat16

D_MODEL = 2048
DEPTH = 2
CHUNK = 64
ROPE_THETA = 500000.0
NORM_EPS = 1e-6
LN_EPS = 1e-5
NEG_INF = -1e30
N_BRANCH = 4

MLA_HEADS = 8
MLA_Q_LORA = 512
MLA_KV_LORA = 256
MLA_NOPE = 64
MLA_ROPE = 32
MLA_V = 64
MLA_SCALE = (MLA_NOPE + MLA_ROPE) ** -0.5

SC_WIDTH = 512
SC_KERNEL = 3

DIFF_HEADS = 4
DIFF_HEAD_DIM = 64
DIFF_ROT = DIFF_HEAD_DIM // 4
DIFF_SCALE = DIFF_HEAD_DIM ** -0.5
DIFF_WIDTH = DIFF_HEADS * 2 * DIFF_HEAD_DIM

CONF_WIDTH = 512
CONF_KERNEL = 31

PEER_HEADS = 8
PEER_N_KEYS = 128
PEER_N_EXPERTS = PEER_N_KEYS * PEER_N_KEYS
PEER_KEY_DIM = 128
PEER_HALF = PEER_KEY_DIM // 2
PEER_TOPK = 16

IN_SIZES = (MLA_Q_LORA, MLA_KV_LORA, MLA_ROPE, 3 * SC_WIDTH, DIFF_WIDTH, DIFF_WIDTH, DIFF_WIDTH,
            2 * CONF_WIDTH, N_BRANCH * D_MODEL)
IN_OFFSETS = tuple(int(v) for v in np.cumsum((0,) + IN_SIZES)[:-1])

LOG2E = math.log2(math.e)

LANES = 128
SUBLANES = 8
HALO = 32
VMEM_LIMIT = 56 * 1024 * 1024

ROW_TILE = 512
MM_TILE = 1024
ATTN_TILE = 512
MLA_GROUP = 8
DIFF_GROUP = 4
SCORE_TILE = 1024
PEER_TOKENS = 512
PEER_EXPERTS = 1024
PEER_CHUNK = 256
PEER_TRIPS = 1

Z_SC = 0
Z_DQ = 1536
Z_DK = 2048
Z_DV = 2560
Z_CONF = 3072
Z_GATE = 4096
Z_CQ = 12288
Z_CKV = 12800
Z_KR = 13056
Z_COLS = 13312


def _cparams(sem):
    return pltpu.CompilerParams(dimension_semantics=sem, vmem_limit_bytes=VMEM_LIMIT)


def _rmsnorm_body(x_ref, g_ref, o_ref, *maybe_ot_ref):
    x = x_ref[...].astype(F32)
    y = x * lax.rsqrt(jnp.mean(x * x, axis=-1, keepdims=True) + NORM_EPS) * g_ref[...]
    o_ref[...] = y.astype(o_ref.dtype)
    for ot_ref in maybe_ot_ref:
        ot_ref[...] = y.T.astype(ot_ref.dtype)


def _rmsnorm(x, g, out_dtype, with_transposed=False):
    s, d = x.shape
    tm = min(ROW_TILE, s)
    out_specs = [pl.BlockSpec((tm, d), lambda i: (i, 0))]
    out_shape = [jax.ShapeDtypeStruct((s, d), out_dtype)]
    if with_transposed:
        out_specs.append(pl.BlockSpec((d, tm), lambda i: (0, i)))
        out_shape.append(jax.ShapeDtypeStruct((d, s), out_dtype))
    out = pl.pallas_call(
        _rmsnorm_body,
        grid=(s // tm,),
        in_specs=[pl.BlockSpec((tm, d), lambda i: (i, 0)), pl.BlockSpec((1, d), lambda i: (0, 0))],
        out_specs=out_specs,
        out_shape=out_shape,
        compiler_params=_cparams(("parallel",)),
        name="rmsnorm",
    )(x, g.reshape(1, d))
    return out if with_transposed else out[0]


def _matmul_body(a_ref, b_ref, o_ref):
    o_ref[...] = jnp.dot(a_ref[...], b_ref[...], preferred_element_type=F32).astype(o_ref.dtype)


def _matmul_res_body(a_ref, b_ref, r_ref, o_ref):
    o_ref[...] = (r_ref[...] + jnp.dot(a_ref[...], b_ref[...], preferred_element_type=F32)).astype(o_ref.dtype)


def _matmul(a, b, out_dtype, residual=None, name="matmul"):
    m, k = a.shape
    n = b.shape[1]
    tm, tn = min(MM_TILE, m), min(MM_TILE, n)
    in_specs = [pl.BlockSpec((tm, k), lambda i, j: (i, 0)), pl.BlockSpec((k, tn), lambda i, j: (0, j))]
    args = [a, b]
    body = _matmul_body
    if residual is not None:
        in_specs.append(pl.BlockSpec((tm, tn), lambda i, j: (i, j)))
        args.append(residual)
        body = _matmul_res_body
    return pl.pallas_call(
        body,
        grid=(m // tm, n // tn),
        in_specs=in_specs,
        out_specs=pl.BlockSpec((tm, tn), lambda i, j: (i, j)),
        out_shape=jax.ShapeDtypeStruct((m, n), out_dtype),
        compiler_params=_cparams(("parallel", "arbitrary")),
        name=name,
    )(*args)


def _sigmoid(x):
    return 0.5 + 0.5 * jnp.tanh(0.5 * x)


def _rope128(x, c, s_up, s_dn, half):
    return x * c + pltpu.roll(x, LANES - half, 1) * s_up + pltpu.roll(x, half, 1) * s_dn


def _mla_prep_body(cq_ref, ckv_ref, kr_ref, c_ref, su_ref, sd_ref, ct_ref, sut_ref, sdt_ref,
                   qg_ref, kvg_ref, wqt_ref, wk_ref, wvt_ref, qt_ref, k_ref, vt_ref):
    half = MLA_ROPE // 2
    nt = (((1,), (1,)), ((), ()))

    def norm(x_ref, g_ref):
        x = x_ref[...].astype(F32)
        y = x * lax.rsqrt(jnp.mean(x * x, axis=-1, keepdims=True) + NORM_EPS)
        return (y * g_ref[...]).astype(BF16)

    qf_t = lax.dot_general(wqt_ref[...], norm(cq_ref, qg_ref), nt, preferred_element_type=F32)
    ckvn = norm(ckv_ref, kvg_ref)
    kf = jnp.dot(ckvn, wk_ref[...], preferred_element_type=F32)
    vt_ref[...] = lax.dot_general(wvt_ref[...], ckvn, nt, preferred_element_type=F32).astype(vt_ref.dtype)
    kr = _rope128(kr_ref[...].astype(F32), c_ref[...], su_ref[...], sd_ref[...], half)
    ct, sut, sdt = ct_ref[...], sut_ref[...], sdt_ref[...]
    for h in range(MLA_HEADS):
        sl = slice(h * LANES, (h + 1) * LANES)
        x = qf_t[sl, :]
        q = x * ct + pltpu.roll(x, LANES - half, 0) * sut + pltpu.roll(x, half, 0) * sdt
        qt_ref[sl, :] = (q * (MLA_SCALE * LOG2E)).astype(qt_ref.dtype)
        k_ref[:, sl] = (kf[:, sl] + kr).astype(k_ref.dtype)


def _mla_prep(z, tabs, tabs_t, qg, kvg, wqt, wk, wvt):
    s = z.shape[0]
    tm = min(ROW_TILE, s)
    row = lambda w, idx: pl.BlockSpec((tm, w), lambda i: (i, idx))
    full = lambda a: pl.BlockSpec(a.shape, lambda i: (0,) * a.ndim)
    tab = pl.BlockSpec((tm, LANES), lambda i: (i, 0))
    tab_t = pl.BlockSpec((LANES, tm), lambda i: (0, i))
    hw = MLA_HEADS * LANES
    vw = MLA_HEADS * MLA_V
    return pl.pallas_call(
        _mla_prep_body,
        grid=(s // tm,),
        in_specs=[row(MLA_Q_LORA, Z_CQ // MLA_Q_LORA), row(MLA_KV_LORA, Z_CKV // MLA_KV_LORA),
                  row(LANES, Z_KR // LANES), tab, tab, tab, tab_t, tab_t, tab_t,
                  full(qg), full(kvg), full(wqt), full(wk), full(wvt)],
        out_specs=[pl.BlockSpec((hw, tm), lambda i: (0, i)), pl.BlockSpec((tm, hw), lambda i: (i, 0)),
                   pl.BlockSpec((vw, tm), lambda i: (0, i))],
        out_shape=[jax.ShapeDtypeStruct((hw, s), BF16), jax.ShapeDtypeStruct((s, hw), BF16),
                   jax.ShapeDtypeStruct((vw, s), BF16)],
        compiler_params=_cparams(("parallel",)),
        name="mla_prep",
    )(z, z, z, *tabs, *tabs_t, qg, kvg, wqt, wk, wvt)


def _flash_streams_t(q_ts, k_ats, vt_ats, m_ref, l_ref, acc_ref, s_a, s_b, n_full, t):
    streams = range(len(q_ts))
    for n in streams:
        m_ref[n] = jnp.full((1, t), NEG_INF, F32)
        l_ref[n] = jnp.zeros((1, t), F32)
        acc_ref[n] = jnp.zeros(acc_ref.shape[1:], F32)

    def scores_into(s_ref, j):
        for n in streams:
            s_ref[n] = jnp.dot(k_ats[n](j), q_ts[n], preferred_element_type=F32)

    def consume(s_ref, j, masked):
        probs, alphas = [], []
        for n in streams:
            s = s_ref[n]
            if masked:
                key_chunk = lax.broadcasted_iota(jnp.int32, (t, t), 0) // CHUNK
                qry_chunk = lax.broadcasted_iota(jnp.int32, (t, t), 1) // CHUNK
                s = jnp.where(key_chunk <= qry_chunk, s, NEG_INF)
            m_prev = m_ref[n]
            m_new = jnp.maximum(m_prev, jnp.max(s, axis=0, keepdims=True))
            alpha = jnp.exp2(m_prev - m_new)
            p = jnp.exp2(s - m_new)
            l_ref[n] = alpha * l_ref[n] + jnp.sum(p, axis=0, keepdims=True)
            m_ref[n] = m_new
            probs.append(p.astype(BF16))
            alphas.append(alpha)
        for n in streams:
            acc_ref[n] = alphas[n] * acc_ref[n] + jnp.dot(vt_ats[n](j), probs[n], preferred_element_type=F32)

    scores_into(s_a, 0)

    def pair(jj, carry):
        j = 2 * jj
        scores_into(s_b, j + 1)
        consume(s_a, j, False)
        scores_into(s_a, j + 2)
        consume(s_b, j + 1, False)
        return carry

    lax.fori_loop(0, n_full // 2, pair, 0)

    @pl.when(n_full % 2 == 0)
    def _():
        consume(s_a, n_full, True)

    @pl.when(n_full % 2 == 1)
    def _():
        scores_into(s_b, n_full)
        consume(s_a, n_full - 1, False)
        consume(s_b, n_full, True)


def _mla_attn_body(qt_ref, k_ref, vt_ref, o_ref, m_ref, l_ref, acc_ref, s_a, s_b, *, t):
    i = pl.program_id(1)
    rows = lambda j: pl.ds(pl.multiple_of(j * t, t), t)
    heads = range(MLA_GROUP)
    lanes = [slice(hh * LANES, (hh + 1) * LANES) for hh in heads]
    k_ats = [lambda j, sl=sl: k_ref[rows(j), sl] for sl in lanes]
    vt_ats = [lambda j, hh=hh: vt_ref[hh * MLA_V:(hh + 1) * MLA_V, rows(j)] for hh in heads]
    _flash_streams_t([qt_ref[sl, :] for sl in lanes], k_ats, vt_ats, m_ref, l_ref, acc_ref, s_a, s_b, i, t)
    o_t = jnp.concatenate([acc_ref[hh] / l_ref[hh] for hh in heads], axis=0)
    o_ref[...] = o_t.T.astype(o_ref.dtype)


def _mla_attn(qt, k, vt):
    s = k.shape[0]
    t = min(ATTN_TILE, s)
    g = MLA_GROUP
    return pl.pallas_call(
        functools.partial(_mla_attn_body, t=t),
        grid=(MLA_HEADS // g, s // t),
        in_specs=[pl.BlockSpec((g * LANES, t), lambda p, i: (p, i)),
                  pl.BlockSpec((s, g * LANES), lambda p, i: (0, p), pipeline_mode=pl.Buffered(1)),
                  pl.BlockSpec((g * MLA_V, s), lambda p, i: (p, 0), pipeline_mode=pl.Buffered(1))],
        out_specs=pl.BlockSpec((t, g * MLA_V), lambda p, i: (i, p)),
        out_shape=jax.ShapeDtypeStruct((s, MLA_HEADS * MLA_V), BF16),
        scratch_shapes=[pltpu.VMEM((g, 1, t), F32), pltpu.VMEM((g, 1, t), F32),
                        pltpu.VMEM((g, MLA_V, t), F32),
                        pltpu.VMEM((g, t, t), F32), pltpu.VMEM((g, t, t), F32)],
        compiler_params=_cparams(("parallel", "arbitrary")),
        name="mla_attn",
    )(qt, k, vt)


def _diff_prep_body(dq_ref, dk_ref, dv_ref, c_ref, su_ref, sd_ref, qt_ref, k_ref, vt_ref):
    half = DIFF_ROT // 2
    c, su, sd = c_ref[...], su_ref[...], sd_ref[...]
    tm = dq_ref.shape[0]
    lane = lax.broadcasted_iota(jnp.int32, (tm, LANES), 1)
    vt_ref[...] = dv_ref[...].astype(F32).T.astype(vt_ref.dtype)
    for h in range(DIFF_HEADS):
        sl = slice(h * LANES, (h + 1) * LANES)
        q = _rope128(dq_ref[:, sl].astype(F32), c, su, sd, half) * (DIFF_SCALE * LOG2E)
        qt_ref[(2 * h) * LANES:(2 * h + 1) * LANES, :] = jnp.where(lane < DIFF_HEAD_DIM, q, 0.0).T.astype(qt_ref.dtype)
        qt_ref[(2 * h + 1) * LANES:(2 * h + 2) * LANES, :] = jnp.where(lane >= DIFF_HEAD_DIM, q, 0.0).T.astype(qt_ref.dtype)
        k_ref[:, sl] = _rope128(dk_ref[:, sl].astype(F32), c, su, sd, half).astype(k_ref.dtype)


def _diff_prep(z, tabs):
    s = z.shape[0]
    tm = min(ROW_TILE, s)
    tab = pl.BlockSpec((tm, LANES), lambda i: (i, 0))
    col = lambda start: pl.BlockSpec((tm, DIFF_WIDTH), lambda i: (i, start // DIFF_WIDTH))
    return pl.pallas_call(
        _diff_prep_body,
        grid=(s // tm,),
        in_specs=[col(Z_DQ), col(Z_DK), col(Z_DV), tab, tab, tab],
        out_specs=[pl.BlockSpec((2 * DIFF_WIDTH, tm), lambda i: (0, i)),
                   pl.BlockSpec((tm, DIFF_WIDTH), lambda i: (i, 0)),
                   pl.BlockSpec((DIFF_WIDTH, tm), lambda i: (0, i))],
        out_shape=[jax.ShapeDtypeStruct((2 * DIFF_WIDTH, s), BF16), jax.ShapeDtypeStruct((s, DIFF_WIDTH), BF16),
                   jax.ShapeDtypeStruct((DIFF_WIDTH, s), BF16)],
        compiler_params=_cparams(("parallel",)),
        name="diff_prep",
    )(z, z, z, *tabs)


def _diff_attn_body(qt_ref, k_ref, vt_ref, lam_ref, g_ref, o_ref, m_ref, l_ref, acc_ref, s_a, s_b, *, t, lam_init):
    i = pl.program_id(1)
    rows = lambda j: pl.ds(pl.multiple_of(j * t, t), t)
    q_ts, k_ats, vt_ats = [], [], []
    for hh in range(DIFF_GROUP):
        sl = slice(hh * LANES, (hh + 1) * LANES)
        for c in range(2):
            q_ts.append(qt_ref[(2 * hh + c) * LANES:(2 * hh + c + 1) * LANES, :])
            k_ats.append(lambda j, sl=sl: k_ref[rows(j), sl])
            vt_ats.append(lambda j, sl=sl: vt_ref[sl, rows(j)])
    _flash_streams_t(q_ts, k_ats, vt_ats, m_ref, l_ref, acc_ref, s_a, s_b, i, t)
    lv = lam_ref[...]
    lam = (jnp.exp(jnp.sum(lv[0:1] * lv[1:2], axis=1, keepdims=True))
           - jnp.exp(jnp.sum(lv[2:3] * lv[3:4], axis=1, keepdims=True)) + lam_init)
    for hh in range(DIFF_GROUP):
        o = (acc_ref[2 * hh] / l_ref[2 * hh] - lam * (acc_ref[2 * hh + 1] / l_ref[2 * hh + 1])).T
        o = o * lax.rsqrt(jnp.mean(o * o, axis=-1, keepdims=True) + NORM_EPS)
        o_ref[:, hh * LANES:(hh + 1) * LANES] = (o * g_ref[...] * (1.0 - lam_init)).astype(o_ref.dtype)


def _diff_attn(qt, k, vt, lam_vecs, norm_g, lam_init):
    s = k.shape[0]
    t = min(ATTN_TILE, s)
    g = DIFF_GROUP
    return pl.pallas_call(
        functools.partial(_diff_attn_body, t=t, lam_init=lam_init),
        grid=(DIFF_HEADS // g, s // t),
        in_specs=[pl.BlockSpec((2 * g * LANES, t), lambda h, i: (h, i)),
                  pl.BlockSpec((s, g * LANES), lambda h, i: (0, h)),
                  pl.BlockSpec((g * LANES, s), lambda h, i: (h, 0)),
                  pl.BlockSpec(lam_vecs.shape, lambda h, i: (0, 0)),
                  pl.BlockSpec((1, LANES), lambda h, i: (0, 0))],
        out_specs=pl.BlockSpec((t, g * LANES), lambda h, i: (i, h)),
        out_shape=jax.ShapeDtypeStruct((s, DIFF_WIDTH), BF16),
        scratch_shapes=[pltpu.VMEM((2 * g, 1, t), F32), pltpu.VMEM((2 * g, 1, t), F32),
                        pltpu.VMEM((2 * g, LANES, t), F32),
                        pltpu.VMEM((2 * g, t, t), F32), pltpu.VMEM((2 * g, t, t), F32)],
        compiler_params=_cparams(("parallel", "arbitrary")),
        name="diff_attn",
    )(qt, k, vt, lam_vecs, norm_g.reshape(1, LANES))


def _conv_body(bg_ref, cg_ref, xv_ref, cgh_ref, xvh_ref, a_ref, gt_ref, ah_ref, gth_ref,
               scw_ref, dww_ref, dwb_ref, lng_ref, lnb_ref, sc_ref, cf_ref, ext_ref):
    tm = bg_ref.shape[0]
    first = pl.program_id(0) == 0

    shifted_rows = HALO + tm - SUBLANES

    def fill(cur, halo, taps):
        ext_ref[0, 0:HALO, :] = jnp.where(first, 0.0, halo)
        ext_ref[0, HALO:HALO + tm, :] = cur
        for r in sorted({(HALO - (taps - 1) + j) % SUBLANES for j in range(taps)} - {0}):
            ext_ref[r, 0:shifted_rows, :] = ext_ref[0, r:r + shifted_rows, :]

    def conv(w_ref, taps):
        acc = jnp.zeros((tm, ext_ref.shape[2]), F32)
        for j in range(taps):
            off = HALO - (taps - 1) + j
            r = off % SUBLANES
            acc = acc + w_ref[j:j + 1, :] * ext_ref[r, off - r:off - r + tm, :]
        return acc

    fill(cg_ref[...].astype(F32) * xv_ref[...].astype(F32), cgh_ref[...].astype(F32) * xvh_ref[...].astype(F32),
         SC_KERNEL)
    sc_ref[...] = (bg_ref[...].astype(F32) * conv(scw_ref, SC_KERNEL)).astype(sc_ref.dtype)

    fill(a_ref[...].astype(F32) * _sigmoid(gt_ref[...].astype(F32)),
         ah_ref[...].astype(F32) * _sigmoid(gth_ref[...].astype(F32)), CONF_KERNEL)
    u = conv(dww_ref, CONF_KERNEL) + dwb_ref[...]
    mu = jnp.mean(u, axis=-1, keepdims=True)
    var = jnp.mean(jnp.square(u - mu), axis=-1, keepdims=True)
    y = (u - mu) * lax.rsqrt(var + LN_EPS) * lng_ref[...] + lnb_ref[...]
    cf_ref[...] = (y * _sigmoid(y)).astype(cf_ref.dtype)


def _conv_branches(z, sc_w, dw_w, dw_b, ln_g, ln_b):
    s = z.shape[0]
    tm = min(ROW_TILE, s)
    w = SC_WIDTH
    cur = lambda col: pl.BlockSpec((tm, w), lambda i: (i, col // w))
    halo = lambda col: pl.BlockSpec((HALO, w), lambda i: (jnp.maximum(i * (tm // HALO) - 1, 0), col // w))
    full = lambda a: pl.BlockSpec(a.shape, lambda i: (0,) * a.ndim)
    vec = lambda a: a.reshape(1, -1)
    args = [sc_w, dw_w, vec(dw_b), vec(ln_g), vec(ln_b)]
    return pl.pallas_call(
        _conv_body,
        grid=(s // tm,),
        in_specs=[cur(Z_SC), cur(Z_SC + w), cur(Z_SC + 2 * w), halo(Z_SC + w), halo(Z_SC + 2 * w),
                  cur(Z_CONF), cur(Z_CONF + w), halo(Z_CONF), halo(Z_CONF + w)] + [full(a) for a in args],
        out_specs=[pl.BlockSpec((tm, w), lambda i: (i, 0)), pl.BlockSpec((tm, w), lambda i: (i, 0))],
        out_shape=[jax.ShapeDtypeStruct((s, w), BF16), jax.ShapeDtypeStruct((s, w), BF16)],
        scratch_shapes=[pltpu.VMEM((SUBLANES, HALO + tm, w), F32)],
        compiler_params=_cparams(("parallel",)),
        name="conv_branches",
    )(*([z] * 9), *args)


def _merge_body(b0, b1, b2, b3, w0, w1, w2, w3, g0, g1, g2, g3, gb_ref, o_ref):
    acc = None
    for n, (b_ref, w_ref, g_ref) in enumerate(((b0, w0, g0), (b1, w1, g1), (b2, w2, g2), (b3, w3, g3))):
        y = jnp.dot(b_ref[...], w_ref[...], preferred_element_type=F32)
        t = _sigmoid(g_ref[...].astype(F32) + gb_ref[n:n + 1, :]) * y
        acc = t if acc is None else acc + t
    o_ref[...] = acc.astype(o_ref.dtype)


def _merge(branches, w_outs, z, gate_b):
    s = z.shape[0]
    tm = min(MM_TILE, s)
    tn = MM_TILE
    kw = branches[0].shape[1]
    return pl.pallas_call(
        _merge_body,
        grid=(s // tm, D_MODEL // tn),
        in_specs=([pl.BlockSpec((tm, kw), lambda i, j: (i, 0))] * 4
                  + [pl.BlockSpec((kw, tn), lambda i, j: (0, j))] * 4
                  + [pl.BlockSpec((tm, tn), lambda i, j, b=b: (i, (Z_GATE + b * D_MODEL) // tn + j))
                     for b in range(N_BRANCH)]
                  + [pl.BlockSpec((N_BRANCH, tn), lambda i, j: (0, j))]),
        out_specs=pl.BlockSpec((tm, tn), lambda i, j: (i, j)),
        out_shape=jax.ShapeDtypeStruct((s, D_MODEL), BF16),
        compiler_params=_cparams(("parallel", "arbitrary")),
        name="merge",
    )(*branches, *w_outs, z, z, z, z, gate_b)


def _order(v, i, j):
    v[i], v[j] = jnp.maximum(v[i], v[j]), jnp.minimum(v[i], v[j])


def _bitonic_sort_desc(v):
    n = len(v)
    k = 2
    while k <= n:
        j = k // 2
        while j >= 1:
            for i in range(n):
                partner = i ^ j
                if partner > i:
                    if i & k == 0:
                        _order(v, i, partner)
                    else:
                        _order(v, partner, i)
            j //= 2
        k *= 2


def _bitonic_merge_desc(v):
    n = len(v)
    j = n // 2
    while j >= 1:
        for i in range(n):
            if i ^ j > i:
                _order(v, i, i ^ j)
        j //= 2


def _top16_desc(groups):
    v = list(groups)
    _bitonic_sort_desc(v)
    for shift in (SUBLANES // 2, SUBLANES // 4, SUBLANES // 8):
        other = [pltpu.roll(x, shift, 0) for x in v]
        v = [jnp.maximum(v[k], other[PEER_TOPK - 1 - k]) for k in range(PEER_TOPK)]
        _bitonic_merge_desc(v)
    return v


def _by_sublane(v, start):
    sub = lax.broadcasted_iota(jnp.int32, v[0].shape, 0)
    out = v[start]
    for r in range(1, SUBLANES):
        out = jnp.where(sub == r, v[start + r], out)
    return out


def _peer_candidates(a, a_hi, b_lo, b_hi, b0, op):
    groups = [op(a[0], b_lo), op(a[0], b_hi)]
    groups += [op(a[i], b_lo) for i in range(1, SUBLANES)]
    groups.append(op(a_hi, b0))
    return groups


def _peer_scores_body(q_ref, k1_ref, k2_ref, th_ref, s2_ref, e1_ref, e2n_ref):
    tt = q_ref.shape[0]
    n_groups = PEER_N_KEYS // SUBLANES

    def head(h, carry):
        qh = q_ref[:, pl.ds(pl.multiple_of(h * PEER_KEY_DIM, PEER_KEY_DIM), PEER_KEY_DIM)]
        dims = (((1,), (1,)), ((), ()))
        s1 = lax.dot_general(k1_ref[...], qh, dims, precision=lax.Precision.HIGHEST, preferred_element_type=F32)
        s2 = lax.dot_general(k2_ref[...], qh, dims, precision=lax.Precision.HIGHEST, preferred_element_type=F32)
        split = lambda s: [s[g * SUBLANES:(g + 1) * SUBLANES, :] for g in range(n_groups)]
        t1 = _top16_desc(split(s1))
        t2 = _top16_desc(split(s2))
        t1_hi, t2_lo, t2_hi = _by_sublane(t1, SUBLANES), _by_sublane(t2, 0), _by_sublane(t2, SUBLANES)
        cand = _peer_candidates(t1, t1_hi, t2_lo, t2_hi, t2[0], jnp.add)
        pad = [jnp.full((SUBLANES, tt), -jnp.inf, F32)] * (PEER_TOPK - len(cand))
        tau8 = _top16_desc(cand + pad)[PEER_TOPK - 1]

        m1, m2 = t1[0], t2[0]
        c1 = [jnp.exp(x - m1) for x in t1[:SUBLANES]]
        ec = _peer_candidates(c1, jnp.exp(t1_hi - m1), jnp.exp(t2_lo - m2), jnp.exp(t2_hi - m2),
                              jnp.ones((SUBLANES, tt), F32), jnp.multiply)
        zsum = jnp.zeros((SUBLANES, tt), F32)
        for cg, eg in zip(cand, ec):
            zsum = zsum + jnp.where(cg >= tau8, eg, 0.0)
        zden = jnp.sum(zsum, axis=0, keepdims=True)
        tau = tau8[0:1]
        th = jnp.full(s1.shape, jnp.inf, F32)
        for b in range(PEER_TOPK):
            t2b = t2[b][0:1]
            th = jnp.where(s1 + t2b >= tau, t2b, th)
        th_ref[h] = th
        s2_ref[h] = s2
        e1_ref[h] = jnp.exp(s1 - m1[0:1])
        e2n_ref[h] = jnp.exp(s2 - m2[0:1]) / zden
        return carry

    lax.fori_loop(0, PEER_HEADS, head, 0)


def _peer_scores(q, k1, k2):
    s = q.shape[0]
    tt = min(SCORE_TILE, s)
    tab = pl.BlockSpec((PEER_HEADS, PEER_N_KEYS, tt), lambda i: (0, 0, i))
    tab_shape = jax.ShapeDtypeStruct((PEER_HEADS, PEER_N_KEYS, s), F32)
    return pl.pallas_call(
        _peer_scores_body,
        grid=(s // tt,),
        in_specs=[pl.BlockSpec((tt, PEER_HEADS * PEER_KEY_DIM), lambda i: (i, 0)),
                  pl.BlockSpec(k1.shape, lambda i: (0, 0)), pl.BlockSpec(k2.shape, lambda i: (0, 0))],
        out_specs=[tab, tab, tab, tab],
        out_shape=[tab_shape, tab_shape, tab_shape, tab_shape],
        compiler_params=_cparams(("parallel",)),
        name="peer_scores",
    )(q, k1, k2)


def _peer_items(s, n_items, nb):
    split = lambda it: (it // nb, it % nb)
    return (split(jnp.minimum(s, n_items - 1)), split(jnp.clip(s - 1, 0, n_items - 1)),
            split(jnp.clip(s - 2, 0, n_items - 1)))


def _peer_dense_body(ht_ref, u_ref, v_ref, th_ref, e1_ref, s2_ref, e2n_ref, o_ref, a0, a1, w0, w1,
                     *, n_items, nb):
    s = pl.program_id(0)
    tt = ht_ref.shape[1]
    eb = u_ref.shape[0]
    d = ht_ref.shape[0]
    n_sub = eb // PEER_N_KEYS
    jrows = 2 * SUBLANES
    a_rows = eb // PEER_TRIPS

    @pl.when(s == 0)
    def _():
        for ref in (a0, a1, w0, w1):
            ref[...] = jnp.zeros_like(ref)

    _, _, (_, block_c) = _peer_items(s, n_items, nb)

    @pl.when(block_c == 0)
    def _():
        o_ref[...] = jnp.zeros_like(o_ref)

    def run(a_new, a_old, w_new, w_old):
        def stage_a(k):
            rows = pl.ds(pl.multiple_of(k * a_rows, a_rows), a_rows)
            a_t = jnp.dot(u_ref[rows, :], ht_ref[...], preferred_element_type=F32)
            a_new[rows, :] = 0.5 * a_t * (1.0 + lax.erf(a_t * (2.0 ** -0.5)))

        def stage_b(c, jc):
            cols = pl.ds(pl.multiple_of(c * LANES, LANES), LANES)
            jr = slice(jc * jrows, (jc + 1) * jrows)
            g = [jnp.zeros((jrows, LANES), F32) for _ in range(n_sub)]
            for h in range(PEER_HEADS):
                s2c = s2_ref[h, jr, cols]
                e2c = e2n_ref[h, jr, cols]
                for il in range(n_sub):
                    gate = e2c * e1_ref[h, il:il + 1, cols]
                    g[il] = g[il] + jnp.where(s2c >= th_ref[h, il:il + 1, cols], gate, 0.0)
            for il in range(n_sub):
                rows = slice(il * PEER_N_KEYS + jc * jrows, il * PEER_N_KEYS + (jc + 1) * jrows)
                w_new[rows, cols] = a_old[rows, cols] * g[il]

        def stage_c(n):
            rows = pl.ds(pl.multiple_of(n * PEER_CHUNK, PEER_CHUNK), PEER_CHUNK)
            o_ref[rows, :] += lax.dot_general(v_ref[:, rows], w_old[...], (((0,), (0,)), ((), ())),
                                              preferred_element_type=F32)

        n_c = d // PEER_CHUNK // PEER_TRIPS
        n_col = tt // LANES // PEER_TRIPS
        assert n_c * PEER_TRIPS * PEER_CHUNK == d and n_col * PEER_TRIPS * LANES == tt and n_c % 2 == 0

        def trip(k, carry):
            stage_a(k)
            for n in range(n_c // 2):
                stage_c(n_c * k + n)
            for c in range(n_col):
                for jc in range(PEER_N_KEYS // jrows):
                    stage_b(n_col * k + c, jc)
            for n in range(n_c // 2, n_c):
                stage_c(n_c * k + n)
            return carry

        lax.fori_loop(0, PEER_TRIPS, trip, 0)

    parity = lax.rem(s, 2)

    @pl.when(parity == 0)
    def _():
        run(a0, a1, w1, w0)

    @pl.when(parity == 1)
    def _():
        run(a1, a0, w0, w1)


def _peer_dense(hn_t, u_all, v_all, layer, th, e1, s2, e2n):
    d, s = hn_t.shape
    tt = min(PEER_TOKENS, s)
    eb = PEER_EXPERTS
    nb = u_all.shape[1] // eb
    n_items = (s // tt) * nb
    items = functools.partial(_peer_items, n_items=n_items, nb=nb)
    sub = pl.BlockSpec((PEER_HEADS, eb // PEER_N_KEYS, tt), lambda i: (0, items(i)[1][1], items(i)[1][0]))
    tab = pl.BlockSpec((PEER_HEADS, PEER_N_KEYS, tt), lambda i: (0, 0, items(i)[1][0]))
    return pl.pallas_call(
        functools.partial(_peer_dense_body, n_items=n_items, nb=nb),
        grid=(n_items + 2,),
        in_specs=[pl.BlockSpec((d, tt), lambda i: (0, items(i)[0][0])),
                  pl.BlockSpec((None, eb, d), lambda i: (layer, items(i)[0][1], 0)),
                  pl.BlockSpec((None, eb, d), lambda i: (layer, items(i)[2][1], 0)),
                  sub, sub, tab, tab],
        out_specs=pl.BlockSpec((d, tt), lambda i: (0, items(i)[2][0])),
        out_shape=jax.ShapeDtypeStruct((d, s), F32),
        scratch_shapes=[pltpu.VMEM((eb, tt), F32), pltpu.VMEM((eb, tt), F32),
                        pltpu.VMEM((eb, tt), F32), pltpu.VMEM((eb, tt), F32)],
        compiler_params=_cparams(("arbitrary",)),
        name="peer_dense",
    )(hn_t, u_all, v_all, th, e1, s2, e2n)


def _add_norm_body(x_ref, dt_ref, g_ref, *out_refs):
    x = x_ref[...] + dt_ref[...].T
    y = x * lax.rsqrt(jnp.mean(x * x, axis=-1, keepdims=True) + NORM_EPS)
    out_refs[-1][...] = (y * g_ref[...]).astype(out_refs[-1].dtype)
    if len(out_refs) == 2:
        out_refs[0][...] = x


def _add_norm(x, delta_t, g, norm_dtype, with_sum):
    s, d = x.shape
    tm = min(ROW_TILE, s)
    row = pl.BlockSpec((tm, d), lambda i: (i, 0))
    out_specs = [row, row] if with_sum else [row]
    out_shape = [jax.ShapeDtypeStruct((s, d), norm_dtype)]
    if with_sum:
        out_shape.insert(0, jax.ShapeDtypeStruct((s, d), F32))
    return pl.pallas_call(
        _add_norm_body,
        grid=(s // tm,),
        in_specs=[row, pl.BlockSpec((d, tm), lambda i: (0, i)), pl.BlockSpec((1, d), lambda i: (0, 0))],
        out_specs=out_specs,
        out_shape=out_shape,
        compiler_params=_cparams(("parallel",)),
        name="add_norm",
    )(x, delta_t, g.reshape(1, d))


def _rope_tables(positions, dim, lanes_x1):
    half = dim // 2
    inv_freq = ROPE_THETA ** (-jnp.arange(0, dim, 2, dtype=F32) / dim)
    ang = positions.astype(F32)[:, None] * inv_freq
    cos, sin = jnp.cos(ang), jnp.sin(ang)
    s = positions.shape[0]
    c_parts, su_parts, sd_parts = [], [], []
    pos = 0
    for off in lanes_x1:
        gap = off - pos
        c_parts += [jnp.ones((s, gap), F32), cos, cos]
        su_parts += [jnp.zeros((s, gap), F32), -sin, jnp.zeros((s, half), F32)]
        sd_parts += [jnp.zeros((s, gap + half), F32), sin]
        pos = off + dim
    tail = LANES - pos
    c_parts.append(jnp.ones((s, tail), F32))
    su_parts.append(jnp.zeros((s, tail), F32))
    sd_parts.append(jnp.zeros((s, tail), F32))
    return tuple(jnp.concatenate(p, axis=1) for p in (c_parts, su_parts, sd_parts))


W_IN_HEAD = IN_OFFSETS[3]
W_IN_RUN = sum(IN_SIZES[3:])
LAYOUT_ROWS = 1024


def _in_proj_body(h_ref, a_ref, b_ref, o_ref, wt_ref):
    j = pl.program_id(0)
    rb = wt_ref.shape[0]
    lead = rb - W_IN_HEAD % rb
    new_block = pl.program_id(1) == 0

    @pl.when(new_block & (j < W_IN_RUN // rb))
    def _():
        wt_ref[0:lead, :] = a_ref[W_IN_HEAD % rb:rb, :].astype(wt_ref.dtype)
        wt_ref[lead:rb, :] = b_ref[0:rb - lead, :].astype(wt_ref.dtype)

    @pl.when(new_block & (j == W_IN_RUN // rb))
    def _():
        head = MLA_Q_LORA + MLA_KV_LORA
        wt_ref[0:head, :] = a_ref[0:head, :].astype(wt_ref.dtype)
        wt_ref[head:rb, :] = jnp.zeros((rb - head, wt_ref.shape[1]), wt_ref.dtype)
        wt_ref[head + MLA_NOPE:head + MLA_NOPE + MLA_ROPE, :] = a_ref[head:head + MLA_ROPE, :].astype(wt_ref.dtype)

    o_ref[...] = lax.dot_general(h_ref[...], wt_ref[...], (((1,), (1,)), ((), ())),
                                 preferred_element_type=F32).astype(o_ref.dtype)


def _in_proj(hn, w_in_t, layer):
    s, d = hn.shape
    n = w_in_t.shape[1]
    rb = LAYOUT_ROWS
    tm = min(MM_TILE, s)
    assert W_IN_RUN % rb == 0 and Z_CQ == W_IN_RUN and Z_COLS == W_IN_RUN + rb and W_IN_HEAD < rb
    run_blocks = W_IN_RUN // rb
    first = lambda j: jnp.where(j < run_blocks, j + W_IN_HEAD // rb, 0)
    second = lambda j: jnp.minimum(j + W_IN_HEAD // rb + 1, (n - 1) // rb)
    return pl.pallas_call(
        _in_proj_body,
        grid=(Z_COLS // rb, s // tm),
        in_specs=[pl.BlockSpec((tm, d), lambda j, i: (i, 0)),
                  pl.BlockSpec((None, rb, d), lambda j, i: (layer, first(j), 0)),
                  pl.BlockSpec((None, rb, d), lambda j, i: (layer, second(j), 0))],
        out_specs=pl.BlockSpec((tm, rb), lambda j, i: (i, j)),
        out_shape=jax.ShapeDtypeStruct((s, Z_COLS), BF16),
        scratch_shapes=[pltpu.VMEM((rb, d), BF16)],
        compiler_params=_cparams(("parallel", "arbitrary")),
        name="in_proj",
    )(hn, w_in_t, w_in_t)


def _layout_mla_weights(w_uq, w_ukv):
    lq = w_uq.shape[0]
    wq = w_uq.reshape(lq, MLA_HEADS, MLA_NOPE + MLA_ROPE)
    wq = jnp.pad(wq, ((0, 0), (0, 0), (0, LANES - MLA_NOPE - MLA_ROPE))).reshape(lq, MLA_HEADS * LANES)
    lk = w_ukv.shape[0]
    wkv = w_ukv.reshape(lk, MLA_HEADS, MLA_NOPE + MLA_V)
    wk = jnp.pad(wkv[:, :, :MLA_NOPE], ((0, 0), (0, 0), (0, LANES - MLA_NOPE))).reshape(lk, MLA_HEADS * LANES)
    wv = wkv[:, :, MLA_NOPE:].reshape(lk, MLA_HEADS * MLA_V)
    return wq.T.astype(BF16), wk.astype(BF16), wv.T.astype(BF16)


def kernel(x, positions, mix_norm_g, w_in, gate_b, mla_q_norm_g, mla_w_uq, mla_kv_norm_g, mla_w_ukv, mla_w_out,
           sc_conv_w, sc_w_out, diff_lambda, diff_norm_g, diff_w_out, conf_dw_w, conf_dw_b, conf_ln_g, conf_ln_b,
           conf_w_out, w_o, ffn_norm_g, peer_w_q, peer_sub_keys, peer_u, peer_v, final_norm_g):
    b, s, d = x.shape
    assert b == 1 and d == D_MODEL
    xs = x.reshape(s, d)
    pos = positions.reshape(s)
    mla_tabs = _rope_tables(pos, MLA_ROPE, (MLA_NOPE,))
    mla_tabs_t = tuple(t.T for t in mla_tabs)
    w_in_t = jnp.swapaxes(w_in, 1, 2)
    u_all = peer_u.astype(BF16)
    v_all = peer_v
    diff_tabs = _rope_tables(pos, DIFF_ROT, (0, DIFF_HEAD_DIM))

    for i in range(DEPTH):
        lam_init = 0.8 - 0.6 * math.exp(-0.3 * i)
        if i == 0:
            hn = _rmsnorm(xs, mix_norm_g[i], BF16)
        else:
            xs, hn = _add_norm(xs, delta_t, mix_norm_g[i], BF16, with_sum=True)
        z = _in_proj(hn, w_in_t, i)
        wqt, wk, wvt = _layout_mla_weights(mla_w_uq[i], mla_w_ukv[i])
        qt, k, vt = _mla_prep(z, mla_tabs, mla_tabs_t, mla_q_norm_g[i].reshape(1, -1),
                              mla_kv_norm_g[i].reshape(1, -1), wqt, wk, wvt)
        o_mla = _mla_attn(qt, k, vt)
        o_sc, o_conf = _conv_branches(z, sc_conv_w[i], conf_dw_w[i], conf_dw_b[i], conf_ln_g[i], conf_ln_b[i])
        qdt, kd, vdt = _diff_prep(z, diff_tabs)
        o_diff = _diff_attn(qdt, kd, vdt, diff_lambda[i], diff_norm_g[i], lam_init)
        merged = _merge((o_mla, o_sc, o_diff, o_conf),
                        tuple(w.astype(BF16) for w in (mla_w_out[i], sc_w_out[i], diff_w_out[i], conf_w_out[i])),
                        z, gate_b[i])
        xs = _matmul(merged, w_o[i].astype(BF16), F32, residual=xs, name="out_proj")
        hf, hf_t = _rmsnorm(xs, ffn_norm_g[i], BF16, with_transposed=True)
        pq = _matmul(hf, peer_w_q[i].astype(BF16), F32, name="peer_q")
        zk = jnp.zeros((PEER_N_KEYS, PEER_HALF), F32)
        k1 = jnp.concatenate([peer_sub_keys[i, 0], zk], axis=1)
        k2 = jnp.concatenate([zk, peer_sub_keys[i, 1]], axis=1)
        th, s2, e1, e2n = _peer_scores(pq, k1, k2)
        delta_t = _peer_dense(hf_t, u_all, v_all, i, th, e1, s2, e2n)
    (out,) = _add_norm(xs, delta_t, final_norm_g, F32, with_sum=False)
    return out.reshape(b, s, d)
```

```python
import functools
import math

import jax
import jax.numpy as jnp
import numpy as np
from jax import lax
from jax.experimental import pallas as pl
from jax.experimental.pallas import tpu as pltpu

F32 = jnp.float32
BF16 = jnp.bfloat16

D_MODEL = 2048
DEPTH = 2
CHUNK = 64
ROPE_THETA = 500000.0
NORM_EPS = 1e-6
LN_EPS = 1e-5
NEG_INF = -1e30
N_BRANCH = 4

MLA_HEADS = 8
MLA_Q_LORA = 512
MLA_KV_LORA = 256
MLA_NOPE = 64
MLA_ROPE = 32
MLA_V = 64
MLA_SCALE = (MLA_NOPE + MLA_ROPE) ** -0.5

SC_WIDTH = 512
SC_KERNEL = 3

DIFF_HEADS = 4
DIFF_HEAD_DIM = 64
DIFF_ROT = DIFF_HEAD_DIM // 4
DIFF_SCALE = DIFF_HEAD_DIM ** -0.5
DIFF_WIDTH = DIFF_HEADS * 2 * DIFF_HEAD_DIM

CONF_WIDTH = 512
CONF_KERNEL = 31

PEER_HEADS = 8
PEER_N_KEYS = 128
PEER_N_EXPERTS = PEER_N_KEYS * PEER_N_KEYS
PEER_KEY_DIM = 128
PEER_HALF = PEER_KEY_DIM // 2
PEER_TOPK = 16

IN_SIZES = (MLA_Q_LORA, MLA_KV_LORA, MLA_ROPE, 3 * SC_WIDTH, DIFF_WIDTH, DIFF_WIDTH, DIFF_WIDTH,
            2 * CONF_WIDTH, N_BRANCH * D_MODEL)
IN_OFFSETS = tuple(int(v) for v in np.cumsum((0,) + IN_SIZES)[:-1])

LOG2E = math.log2(math.e)

LANES = 128
SUBLANES = 8
HALO = 32
VMEM_LIMIT = 56 * 1024 * 1024

ROW_TILE = 512
MM_TILE = 1024
ATTN_TILE = 512
MLA_GROUP = 8
DIFF_GROUP = 4
SCORE_TILE = 1024
PEER_TOKENS = 512
PEER_EXPERTS = 1024
PEER_CHUNK = 256
PEER_TRIPS = 1

Z_SC = 0
Z_DQ = 1536
Z_DK = 2048
Z_DV = 2560
Z_CONF = 3072
Z_GATE = 4096
Z_CQ = 12288
Z_CKV = 12800
Z_KR = 13056
Z_COLS = 13312


def _cparams(sem):
    return pltpu.CompilerParams(dimension_semantics=sem, vmem_limit_bytes=VMEM_LIMIT)


def _rmsnorm_body(x_ref, g_ref, o_ref):
    x = x_ref[...].astype(F32)
    y = x * lax.rsqrt(jnp.mean(x * x, axis=-1, keepdims=True) + NORM_EPS) * g_ref[...]
    o_ref[...] = y.astype(o_ref.dtype)


def _rmsnorm(x, g, out_dtype):
    s, d = x.shape
    tm = min(ROW_TILE, s)
    return pl.pallas_call(
        _rmsnorm_body,
        grid=(s // tm,),
        in_specs=[pl.BlockSpec((tm, d), lambda i: (i, 0)), pl.BlockSpec((1, d), lambda i: (0, 0))],
        out_specs=pl.BlockSpec((tm, d), lambda i: (i, 0)),
        out_shape=jax.ShapeDtypeStruct((s, d), out_dtype),
        compiler_params=_cparams(("parallel",)),
        name="rmsnorm",
    )(x, g.reshape(1, d))


def _sigmoid(x):
    return 0.5 + 0.5 * jnp.tanh(0.5 * x)


def _rope128(x, c, s_up, s_dn, half):
    return x * c + pltpu.roll(x, LANES - half, 1) * s_up + pltpu.roll(x, half, 1) * s_dn


def _mla_prep_body(cq_ref, ckv_ref, kr_ref, c_ref, su_ref, sd_ref, ct_ref, sut_ref, sdt_ref,
                   qg_ref, kvg_ref, wqt_ref, wk_ref, wvt_ref, qt_ref, k_ref, vt_ref):
    half = MLA_ROPE // 2
    nt = (((1,), (1,)), ((), ()))

    def norm(x_ref, g_ref):
        x = x_ref[...].astype(F32)
        y = x * lax.rsqrt(jnp.mean(x * x, axis=-1, keepdims=True) + NORM_EPS)
        return (y * g_ref[...]).astype(BF16)

    qf_t = lax.dot_general(wqt_ref[...], norm(cq_ref, qg_ref), nt, preferred_element_type=F32)
    ckvn = norm(ckv_ref, kvg_ref)
    kf = jnp.dot(ckvn, wk_ref[...], preferred_element_type=F32)
    vt_ref[...] = lax.dot_general(wvt_ref[...], ckvn, nt, preferred_element_type=F32).astype(vt_ref.dtype)
    kr = _rope128(kr_ref[...].astype(F32), c_ref[...], su_ref[...], sd_ref[...], half)
    ct, sut, sdt = ct_ref[...], sut_ref[...], sdt_ref[...]
    for h in range(MLA_HEADS):
        sl = slice(h * LANES, (h + 1) * LANES)
        x = qf_t[sl, :]
        q = x * ct + pltpu.roll(x, LANES - half, 0) * sut + pltpu.roll(x, half, 0) * sdt
        qt_ref[sl, :] = (q * (MLA_SCALE * LOG2E)).astype(qt_ref.dtype)
        k_ref[:, sl] = (kf[:, sl] + kr).astype(k_ref.dtype)


def _mla_prep(z, tabs, tabs_t, qg, kvg, wqt, wk, wvt):
    s = z.shape[0]
    tm = min(ROW_TILE, s)
    row = lambda w, idx: pl.BlockSpec((tm, w), lambda i: (i, idx))
    full = lambda a: pl.BlockSpec(a.shape, lambda i: (0,) * a.ndim)
    tab = pl.BlockSpec((tm, LANES), lambda i: (i, 0))
    tab_t = pl.BlockSpec((LANES, tm), lambda i: (0, i))
    hw = MLA_HEADS * LANES
    vw = MLA_HEADS * MLA_V
    return pl.pallas_call(
        _mla_prep_body,
        grid=(s // tm,),
        in_specs=[row(MLA_Q_LORA, Z_CQ // MLA_Q_LORA), row(MLA_KV_LORA, Z_CKV // MLA_KV_LORA),
                  row(LANES, Z_KR // LANES), tab, tab, tab, tab_t, tab_t, tab_t,
                  full(qg), full(kvg), full(wqt), full(wk), full(wvt)],
        out_specs=[pl.BlockSpec((hw, tm), lambda i: (0, i)), pl.BlockSpec((tm, hw), lambda i: (i, 0)),
                   pl.BlockSpec((vw, tm), lambda i: (0, i))],
        out_shape=[jax.ShapeDtypeStruct((hw, s), BF16), jax.ShapeDtypeStruct((s, hw), BF16),
                   jax.ShapeDtypeStruct((vw, s), BF16)],
        compiler_params=_cparams(("parallel",)),
        name="mla_prep",
    )(z, z, z, *tabs, *tabs_t, qg, kvg, wqt, wk, wvt)


def _flash_streams_t(q_ts, k_ats, vt_ats, m_ref, l_ref, acc_ref, s_a, s_b, n_full, t):
    streams = range(len(q_ts))
    for n in streams:
        m_ref[n] = jnp.full((1, t), NEG_INF, F32)
        l_ref[n] = jnp.zeros((1, t), F32)
        acc_ref[n] = jnp.zeros(acc_ref.shape[1:], F32)

    def scores_into(s_ref, j):
        for n in streams:
            s_ref[n] = jnp.dot(k_ats[n](j), q_ts[n], preferred_element_type=F32)

    def consume(s_ref, j, masked):
        probs, alphas = [], []
        for n in streams:
            s = s_ref[n]
            if masked:
                key_chunk = lax.broadcasted_iota(jnp.int32, (t, t), 0) // CHUNK
                qry_chunk = lax.broadcasted_iota(jnp.int32, (t, t), 1) // CHUNK
                s = jnp.where(key_chunk <= qry_chunk, s, NEG_INF)
            m_prev = m_ref[n]
            m_new = jnp.maximum(m_prev, jnp.max(s, axis=0, keepdims=True))
            alpha = jnp.exp2(m_prev - m_new)
            p = jnp.exp2(s - m_new)
            l_ref[n] = alpha * l_ref[n] + jnp.sum(p, axis=0, keepdims=True)
            m_ref[n] = m_new
            probs.append(p.astype(BF16))
            alphas.append(alpha)
        for n in streams:
            acc_ref[n] = alphas[n] * acc_ref[n] + jnp.dot(vt_ats[n](j), probs[n], preferred_element_type=F32)

    scores_into(s_a, 0)

    def pair(jj, carry):
        j = 2 * jj
        scores_into(s_b, j + 1)
        consume(s_a, j, False)
        scores_into(s_a, j + 2)
        consume(s_b, j + 1, False)
        return carry

    lax.fori_loop(0, n_full // 2, pair, 0)

    @pl.when(n_full % 2 == 0)
    def _():
        consume(s_a, n_full, True)

    @pl.when(n_full % 2 == 1)
    def _():
        scores_into(s_b, n_full)
        consume(s_a, n_full - 1, False)
        consume(s_b, n_full, True)


def _mla_attn_body(qt_ref, k_ref, vt_ref, o_ref, m_ref, l_ref, acc_ref, s_a, s_b, *, t):
    i = pl.program_id(1)
    rows = lambda j: pl.ds(pl.multiple_of(j * t, t), t)
    heads = range(MLA_GROUP)
    lanes = [slice(hh * LANES, (hh + 1) * LANES) for hh in heads]
    k_ats = [lambda j, sl=sl: k_ref[rows(j), sl] for sl in lanes]
    vt_ats = [lambda j, hh=hh: vt_ref[hh * MLA_V:(hh + 1) * MLA_V, rows(j)] for hh in heads]
    _flash_streams_t([qt_ref[sl, :] for sl in lanes], k_ats, vt_ats, m_ref, l_ref, acc_ref, s_a, s_b, i, t)
    o_t = jnp.concatenate([acc_ref[hh] / l_ref[hh] for hh in heads], axis=0)
    o_ref[...] = o_t.T.astype(o_ref.dtype)


def _mla_attn(qt, k, vt):
    s = k.shape[0]
    t = min(ATTN_TILE, s)
    g = MLA_GROUP
    return pl.pallas_call(
        functools.partial(_mla_attn_body, t=t),
        grid=(MLA_HEADS // g, s // t),
        in_specs=[pl.BlockSpec((g * LANES, t), lambda p, i: (p, i)),
                  pl.BlockSpec((s, g * LANES), lambda p, i: (0, p), pipeline_mode=pl.Buffered(1)),
                  pl.BlockSpec((g * MLA_V, s), lambda p, i: (p, 0), pipeline_mode=pl.Buffered(1))],
        out_specs=pl.BlockSpec((t, g * MLA_V), lambda p, i: (i, p)),
        out_shape=jax.ShapeDtypeStruct((s, MLA_HEADS * MLA_V), BF16),
        scratch_shapes=[pltpu.VMEM((g, 1, t), F32), pltpu.VMEM((g, 1, t), F32),
                        pltpu.VMEM((g, MLA_V, t), F32),
                        pltpu.VMEM((g, t, t), F32), pltpu.VMEM((g, t, t), F32)],
        compiler_params=_cparams(("parallel", "arbitrary")),
        name="mla_attn",
    )(qt, k, vt)


def _diff_prep_body(dq_ref, dk_ref, dv_ref, c_ref, su_ref, sd_ref, qt_ref, k_ref, vt_ref):
    half = DIFF_ROT // 2
    c, su, sd = c_ref[...], su_ref[...], sd_ref[...]
    tm = dq_ref.shape[0]
    lane = lax.broadcasted_iota(jnp.int32, (tm, LANES), 1)
    vt_ref[...] = dv_ref[...].astype(F32).T.astype(vt_ref.dtype)
    for h in range(DIFF_HEADS):
        sl = slice(h * LANES, (h + 1) * LANES)
        q = _rope128(dq_ref[:, sl].astype(F32), c, su, sd, half) * (DIFF_SCALE * LOG2E)
        qt_ref[(2 * h) * LANES:(2 * h + 1) * LANES, :] = jnp.where(lane < DIFF_HEAD_DIM, q, 0.0).T.astype(qt_ref.dtype)
        qt_ref[(2 * h + 1) * LANES:(2 * h + 2) * LANES, :] = jnp.where(lane >= DIFF_HEAD_DIM, q, 0.0).T.astype(qt_ref.dtype)
        k_ref[:, sl] = _rope128(dk_ref[:, sl].astype(F32), c, su, sd, half).astype(k_ref.dtype)


def _diff_prep(z, tabs):
    s = z.shape[0]
    tm = min(ROW_TILE, s)
    tab = pl.BlockSpec((tm, LANES), lambda i: (i, 0))
    col = lambda start: pl.BlockSpec((tm, DIFF_WIDTH), lambda i: (i, start // DIFF_WIDTH))
    return pl.pallas_call(
        _diff_prep_body,
        grid=(s // tm,),
        in_specs=[col(Z_DQ), col(Z_DK), col(Z_DV), tab, tab, tab],
        out_specs=[pl.BlockSpec((2 * DIFF_WIDTH, tm), lambda i: (0, i)),
                   pl.BlockSpec((tm, DIFF_WIDTH), lambda i: (i, 0)),
                   pl.BlockSpec((DIFF_WIDTH, tm), lambda i: (0, i))],
        out_shape=[jax.ShapeDtypeStruct((2 * DIFF_WIDTH, s), BF16), jax.ShapeDtypeStruct((s, DIFF_WIDTH), BF16),
                   jax.ShapeDtypeStruct((DIFF_WIDTH, s), BF16)],
        compiler_params=_cparams(("parallel",)),
        name="diff_prep",
    )(z, z, z, *tabs)


def _diff_attn_body(qt_ref, k_ref, vt_ref, lam_ref, g_ref, o_ref, m_ref, l_ref, acc_ref, s_a, s_b, *, t, lam_init):
    i = pl.program_id(1)
    rows = lambda j: pl.ds(pl.multiple_of(j * t, t), t)
    q_ts, k_ats, vt_ats = [], [], []
    for hh in range(DIFF_GROUP):
        sl = slice(hh * LANES, (hh + 1) * LANES)
        for c in range(2):
            q_ts.append(qt_ref[(2 * hh + c) * LANES:(2 * hh + c + 1) * LANES, :])
            k_ats.append(lambda j, sl=sl: k_ref[rows(j), sl])
            vt_ats.append(lambda j, sl=sl: vt_ref[sl, rows(j)])
    _flash_streams_t(q_ts, k_ats, vt_ats, m_ref, l_ref, acc_ref, s_a, s_b, i, t)
    lv = lam_ref[...]
    lam = (jnp.exp(jnp.sum(lv[0:1] * lv[1:2], axis=1, keepdims=True))
           - jnp.exp(jnp.sum(lv[2:3] * lv[3:4], axis=1, keepdims=True)) + lam_init)
    for hh in range(DIFF_GROUP):
        o = (acc_ref[2 * hh] / l_ref[2 * hh] - lam * (acc_ref[2 * hh + 1] / l_ref[2 * hh + 1])).T
        o = o * lax.rsqrt(jnp.mean(o * o, axis=-1, keepdims=True) + NORM_EPS)
        o_ref[:, hh * LANES:(hh + 1) * LANES] = (o * g_ref[...] * (1.0 - lam_init)).astype(o_ref.dtype)


def _diff_attn(qt, k, vt, lam_vecs, norm_g, lam_init):
    s = k.shape[0]
    t = min(ATTN_TILE, s)
    g = DIFF_GROUP
    return pl.pallas_call(
        functools.partial(_diff_attn_body, t=t, lam_init=lam_init),
        grid=(DIFF_HEADS // g, s // t),
        in_specs=[pl.BlockSpec((2 * g * LANES, t), lambda h, i: (h, i)),
                  pl.BlockSpec((s, g * LANES), lambda h, i: (0, h)),
                  pl.BlockSpec((g * LANES, s), lambda h, i: (h, 0)),
                  pl.BlockSpec(lam_vecs.shape, lambda h, i: (0, 0)),
                  pl.BlockSpec((1, LANES), lambda h, i: (0, 0))],
        out_specs=pl.BlockSpec((t, g * LANES), lambda h, i: (i, h)),
        out_shape=jax.ShapeDtypeStruct((s, DIFF_WIDTH), BF16),
        scratch_shapes=[pltpu.VMEM((2 * g, 1, t), F32), pltpu.VMEM((2 * g, 1, t), F32),
                        pltpu.VMEM((2 * g, LANES, t), F32),
                        pltpu.VMEM((2 * g, t, t), F32), pltpu.VMEM((2 * g, t, t), F32)],
        compiler_params=_cparams(("parallel", "arbitrary")),
        name="diff_attn",
    )(qt, k, vt, lam_vecs, norm_g.reshape(1, LANES))


def _conv_body(bg_ref, cg_ref, xv_ref, cgh_ref, xvh_ref, a_ref, gt_ref, ah_ref, gth_ref,
               scw_ref, dww_ref, dwb_ref, lng_ref, lnb_ref, sc_ref, cf_ref, ext_ref):
    tm = bg_ref.shape[0]
    first = pl.program_id(0) == 0

    shifted_rows = HALO + tm - SUBLANES

    def fill(cur, halo, taps):
        ext_ref[0, 0:HALO, :] = jnp.where(first, 0.0, halo)
        ext_ref[0, HALO:HALO + tm, :] = cur
        for r in sorted({(HALO - (taps - 1) + j) % SUBLANES for j in range(taps)} - {0}):
            ext_ref[r, 0:shifted_rows, :] = ext_ref[0, r:r + shifted_rows, :]

    def conv(w_ref, taps):
        acc = jnp.zeros((tm, ext_ref.shape[2]), F32)
        for j in range(taps):
            off = HALO - (taps - 1) + j
            r = off % SUBLANES
            acc = acc + w_ref[j:j + 1, :] * ext_ref[r, off - r:off - r + tm, :]
        return acc

    fill(cg_ref[...].astype(F32) * xv_ref[...].astype(F32), cgh_ref[...].astype(F32) * xvh_ref[...].astype(F32),
         SC_KERNEL)
    sc_ref[...] = (bg_ref[...].astype(F32) * conv(scw_ref, SC_KERNEL)).astype(sc_ref.dtype)

    fill(a_ref[...].astype(F32) * _sigmoid(gt_ref[...].astype(F32)),
         ah_ref[...].astype(F32) * _sigmoid(gth_ref[...].astype(F32)), CONF_KERNEL)
    u = conv(dww_ref, CONF_KERNEL) + dwb_ref[...]
    mu = jnp.mean(u, axis=-1, keepdims=True)
    var = jnp.mean(jnp.square(u - mu), axis=-1, keepdims=True)
    y = (u - mu) * lax.rsqrt(var + LN_EPS) * lng_ref[...] + lnb_ref[...]
    cf_ref[...] = (y * _sigmoid(y)).astype(cf_ref.dtype)


def _conv_branches(z, sc_w, dw_w, dw_b, ln_g, ln_b):
    s = z.shape[0]
    tm = min(ROW_TILE, s)
    w = SC_WIDTH
    cur = lambda col: pl.BlockSpec((tm, w), lambda i: (i, col // w))
    halo = lambda col: pl.BlockSpec((HALO, w), lambda i: (jnp.maximum(i * (tm // HALO) - 1, 0), col // w))
    full = lambda a: pl.BlockSpec(a.shape, lambda i: (0,) * a.ndim)
    vec = lambda a: a.reshape(1, -1)
    args = [sc_w, dw_w, vec(dw_b), vec(ln_g), vec(ln_b)]
    return pl.pallas_call(
        _conv_body,
        grid=(s // tm,),
        in_specs=[cur(Z_SC), cur(Z_SC + w), cur(Z_SC + 2 * w), halo(Z_SC + w), halo(Z_SC + 2 * w),
                  cur(Z_CONF), cur(Z_CONF + w), halo(Z_CONF), halo(Z_CONF + w)] + [full(a) for a in args],
        out_specs=[pl.BlockSpec((tm, w), lambda i: (i, 0)), pl.BlockSpec((tm, w), lambda i: (i, 0))],
        out_shape=[jax.ShapeDtypeStruct((s, w), BF16), jax.ShapeDtypeStruct((s, w), BF16)],
        scratch_shapes=[pltpu.VMEM((SUBLANES, HALO + tm, w), F32)],
        compiler_params=_cparams(("parallel",)),
        name="conv_branches",
    )(*([z] * 9), *args)


def _merge_body(b0, b1, b2, b3, w0, w1, w2, w3, g0, g1, g2, g3, gb_ref, o_ref):
    acc = None
    for n, (b_ref, w_ref, g_ref) in enumerate(((b0, w0, g0), (b1, w1, g1), (b2, w2, g2), (b3, w3, g3))):
        y = jnp.dot(b_ref[...], w_ref[...], preferred_element_type=F32)
        t = _sigmoid(g_ref[...].astype(F32) + gb_ref[n:n + 1, :]) * y
        acc = t if acc is None else acc + t
    o_ref[...] = acc.astype(o_ref.dtype)


def _merge(branches, w_outs, z, gate_b):
    s = z.shape[0]
    tm = min(MM_TILE, s)
    tn = MM_TILE
    kw = branches[0].shape[1]
    return pl.pallas_call(
        _merge_body,
        grid=(s // tm, D_MODEL // tn),
        in_specs=([pl.BlockSpec((tm, kw), lambda i, j: (i, 0))] * 4
                  + [pl.BlockSpec((kw, tn), lambda i, j: (0, j))] * 4
                  + [pl.BlockSpec((tm, tn), lambda i, j, b=b: (i, (Z_GATE + b * D_MODEL) // tn + j))
                     for b in range(N_BRANCH)]
                  + [pl.BlockSpec((N_BRANCH, tn), lambda i, j: (0, j))]),
        out_specs=pl.BlockSpec((tm, tn), lambda i, j: (i, j)),
        out_shape=jax.ShapeDtypeStruct((s, D_MODEL), BF16),
        compiler_params=_cparams(("parallel", "arbitrary")),
        name="merge",
    )(*branches, *w_outs, z, z, z, z, gate_b)


def _order(v, i, j):
    v[i], v[j] = jnp.maximum(v[i], v[j]), jnp.minimum(v[i], v[j])


def _bitonic_sort_desc(v):
    n = len(v)
    k = 2
    while k <= n:
        j = k // 2
        while j >= 1:
            for i in range(n):
                partner = i ^ j
                if partner > i:
                    if i & k == 0:
                        _order(v, i, partner)
                    else:
                        _order(v, partner, i)
            j //= 2
        k *= 2


def _bitonic_merge_desc(v):
    n = len(v)
    j = n // 2
    while j >= 1:
        for i in range(n):
            if i ^ j > i:
                _order(v, i, i ^ j)
        j //= 2


def _top16_desc(groups):
    v = list(groups)
    _bitonic_sort_desc(v)
    for shift in (SUBLANES // 2, SUBLANES // 4, SUBLANES // 8):
        other = [pltpu.roll(x, shift, 0) for x in v]
        v = [jnp.maximum(v[k], other[PEER_TOPK - 1 - k]) for k in range(PEER_TOPK)]
        _bitonic_merge_desc(v)
    return v


def _by_sublane(v, start):
    sub = lax.broadcasted_iota(jnp.int32, v[0].shape, 0)
    out = v[start]
    for r in range(1, SUBLANES):
        out = jnp.where(sub == r, v[start + r], out)
    return out


def _peer_candidates(a, a_hi, b_lo, b_hi, b0, op):
    groups = [op(a[0], b_lo), op(a[0], b_hi)]
    groups += [op(a[i], b_lo) for i in range(1, SUBLANES)]
    groups.append(op(a_hi, b0))
    return groups


def _peer_scores_body(q_ref, k1_ref, k2_ref, th_ref, s2_ref, e1_ref, e2n_ref):
    tt = q_ref.shape[0]
    n_groups = PEER_N_KEYS // SUBLANES

    def head(h, carry):
        qh = q_ref[:, pl.ds(pl.multiple_of(h * PEER_KEY_DIM, PEER_KEY_DIM), PEER_KEY_DIM)]
        dims = (((1,), (1,)), ((), ()))
        s1 = lax.dot_general(k1_ref[...], qh, dims, precision=lax.Precision.HIGHEST, preferred_element_type=F32)
        s2 = lax.dot_general(k2_ref[...], qh, dims, precision=lax.Precision.HIGHEST, preferred_element_type=F32)
        split = lambda s: [s[g * SUBLANES:(g + 1) * SUBLANES, :] for g in range(n_groups)]
        t1 = _top16_desc(split(s1))
        t2 = _top16_desc(split(s2))
        t1_hi, t2_lo, t2_hi = _by_sublane(t1, SUBLANES), _by_sublane(t2, 0), _by_sublane(t2, SUBLANES)
        cand = _peer_candidates(t1, t1_hi, t2_lo, t2_hi, t2[0], jnp.add)
        pad = [jnp.full((SUBLANES, tt), -jnp.inf, F32)] * (PEER_TOPK - len(cand))
        tau8 = _top16_desc(cand + pad)[PEER_TOPK - 1]

        m1, m2 = t1[0], t2[0]
        c1 = [jnp.exp(x - m1) for x in t1[:SUBLANES]]
        ec = _peer_candidates(c1, jnp.exp(t1_hi - m1), jnp.exp(t2_lo - m2), jnp.exp(t2_hi - m2),
                              jnp.ones((SUBLANES, tt), F32), jnp.multiply)
        zsum = jnp.zeros((SUBLANES, tt), F32)
        for cg, eg in zip(cand, ec):
            zsum = zsum + jnp.where(cg >= tau8, eg, 0.0)
        zden = jnp.sum(zsum, axis=0, keepdims=True)
        tau = tau8[0:1]
        th = jnp.full(s1.shape, jnp.inf, F32)
        for b in range(PEER_TOPK):
            t2b = t2[b][0:1]
            th = jnp.where(s1 + t2b >= tau, t2b, th)
        th_ref[h] = th
        s2_ref[h] = s2
        e1_ref[h] = jnp.exp(s1 - m1[0:1])
        e2n_ref[h] = jnp.exp(s2 - m2[0:1]) / zden
        return carry

    lax.fori_loop(0, PEER_HEADS, head, 0)


def _peer_scores(q, k1, k2):
    s = q.shape[0]
    tt = min(SCORE_TILE, s)
    tab = pl.BlockSpec((PEER_HEADS, PEER_N_KEYS, tt), lambda i: (0, 0, i))
    tab_shape = jax.ShapeDtypeStruct((PEER_HEADS, PEER_N_KEYS, s), F32)
    return pl.pallas_call(
        _peer_scores_body,
        grid=(s // tt,),
        in_specs=[pl.BlockSpec((tt, PEER_HEADS * PEER_KEY_DIM), lambda i: (i, 0)),
                  pl.BlockSpec(k1.shape, lambda i: (0, 0)), pl.BlockSpec(k2.shape, lambda i: (0, 0))],
        out_specs=[tab, tab, tab, tab],
        out_shape=[tab_shape, tab_shape, tab_shape, tab_shape],
        compiler_params=_cparams(("parallel",)),
        name="peer_scores",
    )(q, k1, k2)


def _peer_items(s, n_items, nb):
    split = lambda it: (it // nb, it % nb)
    return (split(jnp.minimum(s, n_items - 1)), split(jnp.clip(s - 1, 0, n_items - 1)),
            split(jnp.clip(s - 2, 0, n_items - 1)))


def _peer_dense_body(ht_ref, u_ref, v_ref, th_ref, e1_ref, s2_ref, e2n_ref, o_ref, a0, a1, w0, w1,
                     *, n_items, nb):
    s = pl.program_id(0)
    tt = ht_ref.shape[1]
    eb = u_ref.shape[0]
    d = ht_ref.shape[0]
    n_sub = eb // PEER_N_KEYS
    jrows = 2 * SUBLANES
    a_rows = eb // PEER_TRIPS

    @pl.when(s == 0)
    def _():
        for ref in (a0, a1, w0, w1):
            ref[...] = jnp.zeros_like(ref)

    _, _, (_, block_c) = _peer_items(s, n_items, nb)

    @pl.when(block_c == 0)
    def _():
        o_ref[...] = jnp.zeros_like(o_ref)

    def run(a_new, a_old, w_new, w_old):
        def stage_a(k):
            rows = pl.ds(pl.multiple_of(k * a_rows, a_rows), a_rows)
            a_t = jnp.dot(u_ref[rows, :], ht_ref[...], preferred_element_type=F32)
            a_new[rows, :] = 0.5 * a_t * (1.0 + lax.erf(a_t * (2.0 ** -0.5)))

        def stage_b(c, jc):
            cols = pl.ds(pl.multiple_of(c * LANES, LANES), LANES)
            jr = slice(jc * jrows, (jc + 1) * jrows)
            g = [jnp.zeros((jrows, LANES), F32) for _ in range(n_sub)]
            for h in range(PEER_HEADS):
                s2c = s2_ref[h, jr, cols]
                e2c = e2n_ref[h, jr, cols]
                for il in range(n_sub):
                    gate = e2c * e1_ref[h, il:il + 1, cols]
                    g[il] = g[il] + jnp.where(s2c >= th_ref[h, il:il + 1, cols], gate, 0.0)
            for il in range(n_sub):
                rows = slice(il * PEER_N_KEYS + jc * jrows, il * PEER_N_KEYS + (jc + 1) * jrows)
                w_new[rows, cols] = a_old[rows, cols] * g[il]

        def stage_c(n):
            rows = pl.ds(pl.multiple_of(n * PEER_CHUNK, PEER_CHUNK), PEER_CHUNK)
            o_ref[rows, :] += lax.dot_general(v_ref[:, rows], w_old[...], (((0,), (0,)), ((), ())),
                                              preferred_element_type=F32)

        n_c = d // PEER_CHUNK // PEER_TRIPS
        n_col = tt // LANES // PEER_TRIPS
        assert n_c * PEER_TRIPS * PEER_CHUNK == d and n_col * PEER_TRIPS * LANES == tt and n_c % 2 == 0

        def trip(k, carry):
            stage_a(k)
            for n in range(n_c // 2):
                stage_c(n_c * k + n)
            for c in range(n_col):
                for jc in range(PEER_N_KEYS // jrows):
                    stage_b(n_col * k + c, jc)
            for n in range(n_c // 2, n_c):
                stage_c(n_c * k + n)
            return carry

        lax.fori_loop(0, PEER_TRIPS, trip, 0)

    parity = lax.rem(s, 2)

    @pl.when(parity == 0)
    def _():
        run(a0, a1, w1, w0)

    @pl.when(parity == 1)
    def _():
        run(a1, a0, w0, w1)


def _peer_dense(hn_t, u_all, v_all, layer, th, e1, s2, e2n):
    d, s = hn_t.shape
    tt = min(PEER_TOKENS, s)
    eb = PEER_EXPERTS
    nb = u_all.shape[1] // eb
    n_items = (s // tt) * nb
    items = functools.partial(_peer_items, n_items=n_items, nb=nb)
    sub = pl.BlockSpec((PEER_HEADS, eb // PEER_N_KEYS, tt), lambda i: (0, items(i)[1][1], items(i)[1][0]))
    tab = pl.BlockSpec((PEER_HEADS, PEER_N_KEYS, tt), lambda i: (0, 0, items(i)[1][0]))
    return pl.pallas_call(
        functools.partial(_peer_dense_body, n_items=n_items, nb=nb),
        grid=(n_items + 2,),
        in_specs=[pl.BlockSpec((d, tt), lambda i: (0, items(i)[0][0])),
                  pl.BlockSpec((None, eb, d), lambda i: (layer, items(i)[0][1], 0)),
                  pl.BlockSpec((None, eb, d), lambda i: (layer, items(i)[2][1], 0)),
                  sub, sub, tab, tab],
        out_specs=pl.BlockSpec((d, tt), lambda i: (0, items(i)[2][0])),
        out_shape=jax.ShapeDtypeStruct((d, s), F32),
        scratch_shapes=[pltpu.VMEM((eb, tt), F32), pltpu.VMEM((eb, tt), F32),
                        pltpu.VMEM((eb, tt), F32), pltpu.VMEM((eb, tt), F32)],
        compiler_params=_cparams(("arbitrary",)),
        name="peer_dense",
    )(hn_t, u_all, v_all, th, e1, s2, e2n)


def _out_proj_peer_in_body(m_ref, wo_ref, r_ref, g_ref, wq_ref, x_ref, ht_ref, pq_ref):
    x = r_ref[...] + jnp.dot(m_ref[...], wo_ref[...], preferred_element_type=F32)
    x_ref[...] = x
    y = x * lax.rsqrt(jnp.mean(x * x, axis=-1, keepdims=True) + NORM_EPS) * g_ref[...]
    ht_ref[...] = y.T.astype(ht_ref.dtype)
    pq_ref[...] = jnp.dot(y.astype(BF16), wq_ref[...], preferred_element_type=F32)


def _out_proj_peer_in(merged, w_o, x, g, w_q):
    s, d = x.shape
    nq = w_q.shape[1]
    tm = min(ROW_TILE, s)
    row = lambda n: pl.BlockSpec((tm, n), lambda i: (i, 0))
    resident = lambda a: pl.BlockSpec(a.shape, lambda i: (0, 0), pipeline_mode=pl.Buffered(1))
    return pl.pallas_call(
        _out_proj_peer_in_body,
        grid=(s // tm,),
        in_specs=[row(d), resident(w_o), row(d), pl.BlockSpec((1, d), lambda i: (0, 0)), resident(w_q)],
        out_specs=[row(d), pl.BlockSpec((d, tm), lambda i: (0, i)), row(nq)],
        out_shape=[jax.ShapeDtypeStruct((s, d), F32), jax.ShapeDtypeStruct((d, s), BF16),
                   jax.ShapeDtypeStruct((s, nq), F32)],
        compiler_params=_cparams(("parallel",)),
        name="out_proj_peer_in",
    )(merged, w_o, x, g.reshape(1, d), w_q)


def _add_norm_body(x_ref, dt_ref, g_ref, *out_refs):
    x = x_ref[...] + dt_ref[...].T
    y = x * lax.rsqrt(jnp.mean(x * x, axis=-1, keepdims=True) + NORM_EPS)
    out_refs[-1][...] = (y * g_ref[...]).astype(out_refs[-1].dtype)
    if len(out_refs) == 2:
        out_refs[0][...] = x


def _add_norm(x, delta_t, g, norm_dtype, with_sum):
    s, d = x.shape
    tm = min(ROW_TILE, s)
    row = pl.BlockSpec((tm, d), lambda i: (i, 0))
    out_specs = [row, row] if with_sum else [row]
    out_shape = [jax.ShapeDtypeStruct((s, d), norm_dtype)]
    if with_sum:
        out_shape.insert(0, jax.ShapeDtypeStruct((s, d), F32))
    return pl.pallas_call(
        _add_norm_body,
        grid=(s // tm,),
        in_specs=[row, pl.BlockSpec((d, tm), lambda i: (0, i)), pl.BlockSpec((1, d), lambda i: (0, 0))],
        out_specs=out_specs,
        out_shape=out_shape,
        compiler_params=_cparams(("parallel",)),
        name="add_norm",
    )(x, delta_t, g.reshape(1, d))


def _rope_tables(positions, dim, lanes_x1):
    half = dim // 2
    inv_freq = ROPE_THETA ** (-jnp.arange(0, dim, 2, dtype=F32) / dim)
    ang = positions.astype(F32)[:, None] * inv_freq
    cos, sin = jnp.cos(ang), jnp.sin(ang)
    s = positions.shape[0]
    c_parts, su_parts, sd_parts = [], [], []
    pos = 0
    for off in lanes_x1:
        gap = off - pos
        c_parts += [jnp.ones((s, gap), F32), cos, cos]
        su_parts += [jnp.zeros((s, gap), F32), -sin, jnp.zeros((s, half), F32)]
        sd_parts += [jnp.zeros((s, gap + half), F32), sin]
        pos = off + dim
    tail = LANES - pos
    c_parts.append(jnp.ones((s, tail), F32))
    su_parts.append(jnp.zeros((s, tail), F32))
    sd_parts.append(jnp.zeros((s, tail), F32))
    return tuple(jnp.concatenate(p, axis=1) for p in (c_parts, su_parts, sd_parts))


W_IN_HEAD = IN_OFFSETS[3]
W_IN_RUN = sum(IN_SIZES[3:])
LAYOUT_ROWS = 1024


def _in_proj_body(h_ref, a_ref, b_ref, o_ref, wt_ref):
    j = pl.program_id(0)
    rb = wt_ref.shape[0]
    lead = rb - W_IN_HEAD % rb
    new_block = pl.program_id(1) == 0

    @pl.when(new_block & (j < W_IN_RUN // rb))
    def _():
        wt_ref[0:lead, :] = a_ref[W_IN_HEAD % rb:rb, :].astype(wt_ref.dtype)
        wt_ref[lead:rb, :] = b_ref[0:rb - lead, :].astype(wt_ref.dtype)

    @pl.when(new_block & (j == W_IN_RUN // rb))
    def _():
        head = MLA_Q_LORA + MLA_KV_LORA
        wt_ref[0:head, :] = a_ref[0:head, :].astype(wt_ref.dtype)
        wt_ref[head:rb, :] = jnp.zeros((rb - head, wt_ref.shape[1]), wt_ref.dtype)
        wt_ref[head + MLA_NOPE:head + MLA_NOPE + MLA_ROPE, :] = a_ref[head:head + MLA_ROPE, :].astype(wt_ref.dtype)

    o_ref[...] = lax.dot_general(h_ref[...], wt_ref[...], (((1,), (1,)), ((), ())),
                                 preferred_element_type=F32).astype(o_ref.dtype)


def _in_proj(hn, w_in_t, layer):
    s, d = hn.shape
    n = w_in_t.shape[1]
    rb = LAYOUT_ROWS
    tm = min(MM_TILE, s)
    assert W_IN_RUN % rb == 0 and Z_CQ == W_IN_RUN and Z_COLS == W_IN_RUN + rb and W_IN_HEAD < rb
    run_blocks = W_IN_RUN // rb
    first = lambda j: jnp.where(j < run_blocks, j + W_IN_HEAD // rb, 0)
    second = lambda j: jnp.minimum(j + W_IN_HEAD // rb + 1, (n - 1) // rb)
    return pl.pallas_call(
        _in_proj_body,
        grid=(Z_COLS // rb, s // tm),
        in_specs=[pl.BlockSpec((tm, d), lambda j, i: (i, 0)),
                  pl.BlockSpec((None, rb, d), lambda j, i: (layer, first(j), 0)),
                  pl.BlockSpec((None, rb, d), lambda j, i: (layer, second(j), 0))],
        out_specs=pl.BlockSpec((tm, rb), lambda j, i: (i, j)),
        out_shape=jax.ShapeDtypeStruct((s, Z_COLS), BF16),
        scratch_shapes=[pltpu.VMEM((rb, d), BF16)],
        compiler_params=_cparams(("parallel", "arbitrary")),
        name="in_proj",
    )(hn, w_in_t, w_in_t)


def _layout_mla_weights(w_uq, w_ukv):
    lq = w_uq.shape[0]
    wq = w_uq.reshape(lq, MLA_HEADS, MLA_NOPE + MLA_ROPE)
    wq = jnp.pad(wq, ((0, 0), (0, 0), (0, LANES - MLA_NOPE - MLA_ROPE))).reshape(lq, MLA_HEADS * LANES)
    lk = w_ukv.shape[0]
    wkv = w_ukv.reshape(lk, MLA_HEADS, MLA_NOPE + MLA_V)
    wk = jnp.pad(wkv[:, :, :MLA_NOPE], ((0, 0), (0, 0), (0, LANES - MLA_NOPE))).reshape(lk, MLA_HEADS * LANES)
    wv = wkv[:, :, MLA_NOPE:].reshape(lk, MLA_HEADS * MLA_V)
    return wq.T.astype(BF16), wk.astype(BF16), wv.T.astype(BF16)


def kernel(x, positions, mix_norm_g, w_in, gate_b, mla_q_norm_g, mla_w_uq, mla_kv_norm_g, mla_w_ukv, mla_w_out,
           sc_conv_w, sc_w_out, diff_lambda, diff_norm_g, diff_w_out, conf_dw_w, conf_dw_b, conf_ln_g, conf_ln_b,
           conf_w_out, w_o, ffn_norm_g, peer_w_q, peer_sub_keys, peer_u, peer_v, final_norm_g):
    b, s, d = x.shape
    assert b == 1 and d == D_MODEL
    xs = x.reshape(s, d)
    pos = positions.reshape(s)
    mla_tabs = _rope_tables(pos, MLA_ROPE, (MLA_NOPE,))
    mla_tabs_t = tuple(t.T for t in mla_tabs)
    w_in_t = jnp.swapaxes(w_in, 1, 2)
    u_all = peer_u.astype(BF16)
    v_all = peer_v
    diff_tabs = _rope_tables(pos, DIFF_ROT, (0, DIFF_HEAD_DIM))

    for i in range(DEPTH):
        lam_init = 0.8 - 0.6 * math.exp(-0.3 * i)
        if i == 0:
            hn = _rmsnorm(xs, mix_norm_g[i], BF16)
        else:
            xs, hn = _add_norm(xs, delta_t, mix_norm_g[i], BF16, with_sum=True)
        z = _in_proj(hn, w_in_t, i)
        wqt, wk, wvt = _layout_mla_weights(mla_w_uq[i], mla_w_ukv[i])
        qt, k, vt = _mla_prep(z, mla_tabs, mla_tabs_t, mla_q_norm_g[i].reshape(1, -1),
                              mla_kv_norm_g[i].reshape(1, -1), wqt, wk, wvt)
        o_mla = _mla_attn(qt, k, vt)
        o_sc, o_conf = _conv_branches(z, sc_conv_w[i], conf_dw_w[i], conf_dw_b[i], conf_ln_g[i], conf_ln_b[i])
        qdt, kd, vdt = _diff_prep(z, diff_tabs)
        o_diff = _diff_attn(qdt, kd, vdt, diff_lambda[i], diff_norm_g[i], lam_init)
        merged = _merge((o_mla, o_sc, o_diff, o_conf),
                        tuple(w.astype(BF16) for w in (mla_w_out[i], sc_w_out[i], diff_w_out[i], conf_w_out[i])),
                        z, gate_b[i])
        xs, hf_t, pq = _out_proj_peer_in(merged, w_o[i].astype(BF16), xs, ffn_norm_g[i],
                                         peer_w_q[i].astype(BF16))
        zk = jnp.zeros((PEER_N_KEYS, PEER_HALF), F32)
        k1 = jnp.concatenate([peer_sub_keys[i, 0], zk], axis=1)
        k2 = jnp.concatenate([zk, peer_sub_keys[i, 1]], axis=1)
        th, s2, e1, e2n = _peer_scores(pq, k1, k2)
        delta_t = _peer_dense(hf_t, u_all, v_all, i, th, e1, s2, e2n)
    (out,) = _add_norm(xs, delta_t, final_norm_g, F32, with_sum=False)
    return out.reshape(b, s, d)
```

```python
import functools
import math

import jax
import jax.numpy as jnp
import numpy as np
from jax import lax
from jax.experimental import pallas as pl
from jax.experimental.pallas import tpu as pltpu

F32 = jnp.float32
BF16 = jnp.bfloat16

D_MODEL = 2048
DEPTH = 2
CHUNK = 64
ROPE_THETA = 500000.0
NORM_EPS = 1e-6
LN_EPS = 1e-5
NEG_INF = -1e30
N_BRANCH = 4

MLA_HEADS = 8
MLA_Q_LORA = 512
MLA_KV_LORA = 256
MLA_NOPE = 64
MLA_ROPE = 32
MLA_V = 64
MLA_SCALE = (MLA_NOPE + MLA_ROPE) ** -0.5

SC_WIDTH = 512
SC_KERNEL = 3

DIFF_HEADS = 4
DIFF_HEAD_DIM = 64
DIFF_ROT = DIFF_HEAD_DIM // 4
DIFF_SCALE = DIFF_HEAD_DIM ** -0.5
DIFF_WIDTH = DIFF_HEADS * 2 * DIFF_HEAD_DIM

CONF_WIDTH = 512
CONF_KERNEL = 31

PEER_HEADS = 8
PEER_N_KEYS = 128
PEER_N_EXPERTS = PEER_N_KEYS * PEER_N_KEYS
PEER_KEY_DIM = 128
PEER_HALF = PEER_KEY_DIM // 2
PEER_TOPK = 16

IN_SIZES = (MLA_Q_LORA, MLA_KV_LORA, MLA_ROPE, 3 * SC_WIDTH, DIFF_WIDTH, DIFF_WIDTH, DIFF_WIDTH,
            2 * CONF_WIDTH, N_BRANCH * D_MODEL)
IN_OFFSETS = tuple(int(v) for v in np.cumsum((0,) + IN_SIZES)[:-1])

LOG2E = math.log2(math.e)

LANES = 128
SUBLANES = 8
HALO = 32
VMEM_LIMIT = 56 * 1024 * 1024

ROW_TILE = 512
MM_TILE = 1024
MERGE_ROWS = 256
ATTN_TILE = 512
MLA_GROUP = 8
DIFF_GROUP = 4
SCORE_TILE = 1024
PEER_TOKENS = 512
PEER_EXPERTS = 1024
PEER_CHUNK = 256
PEER_TRIPS = 1

Z_SC = 0
Z_DQ = 1536
Z_DK = 2048
Z_DV = 2560
Z_CONF = 3072
Z_GATE = 4096
Z_CQ = 12288
Z_CKV = 12800
Z_KR = 13056
Z_COLS = 13312


def _cparams(sem):
    return pltpu.CompilerParams(dimension_semantics=sem, vmem_limit_bytes=VMEM_LIMIT)


def _rmsnorm_body(x_ref, g_ref, o_ref):
    x = x_ref[...].astype(F32)
    y = x * lax.rsqrt(jnp.mean(x * x, axis=-1, keepdims=True) + NORM_EPS) * g_ref[...]
    o_ref[...] = y.astype(o_ref.dtype)


def _rmsnorm(x, g, out_dtype):
    s, d = x.shape
    tm = min(ROW_TILE, s)
    return pl.pallas_call(
        _rmsnorm_body,
        grid=(s // tm,),
        in_specs=[pl.BlockSpec((tm, d), lambda i: (i, 0)), pl.BlockSpec((1, d), lambda i: (0, 0))],
        out_specs=pl.BlockSpec((tm, d), lambda i: (i, 0)),
        out_shape=jax.ShapeDtypeStruct((s, d), out_dtype),
        compiler_params=_cparams(("parallel",)),
        name="rmsnorm",
    )(x, g.reshape(1, d))


def _sigmoid(x):
    return 0.5 + 0.5 * jnp.tanh(0.5 * x)


def _rope128(x, c, s_up, s_dn, half):
    return x * c + pltpu.roll(x, LANES - half, 1) * s_up + pltpu.roll(x, half, 1) * s_dn


def _mla_prep_body(cq_ref, ckv_ref, kr_ref, c_ref, su_ref, sd_ref, ct_ref, sut_ref, sdt_ref,
                   qg_ref, kvg_ref, wqt_ref, wk_ref, wvt_ref, qt_ref, k_ref, vt_ref):
    half = MLA_ROPE // 2
    nt = (((1,), (1,)), ((), ()))

    def norm(x_ref, g_ref):
        x = x_ref[...].astype(F32)
        y = x * lax.rsqrt(jnp.mean(x * x, axis=-1, keepdims=True) + NORM_EPS)
        return (y * g_ref[...]).astype(BF16)

    qf_t = lax.dot_general(wqt_ref[...], norm(cq_ref, qg_ref), nt, preferred_element_type=F32)
    ckvn = norm(ckv_ref, kvg_ref)
    kf = jnp.dot(ckvn, wk_ref[...], preferred_element_type=F32)
    vt_ref[...] = lax.dot_general(wvt_ref[...], ckvn, nt, preferred_element_type=F32).astype(vt_ref.dtype)
    kr = _rope128(kr_ref[...].astype(F32), c_ref[...], su_ref[...], sd_ref[...], half)
    ct, sut, sdt = ct_ref[...], sut_ref[...], sdt_ref[...]
    for h in range(MLA_HEADS):
        sl = slice(h * LANES, (h + 1) * LANES)
        x = qf_t[sl, :]
        q = x * ct + pltpu.roll(x, LANES - half, 0) * sut + pltpu.roll(x, half, 0) * sdt
        qt_ref[sl, :] = (q * (MLA_SCALE * LOG2E)).astype(qt_ref.dtype)
        k_ref[:, sl] = (kf[:, sl] + kr).astype(k_ref.dtype)


def _mla_prep(z, tabs, tabs_t, qg, kvg, wqt, wk, wvt):
    s = z.shape[0]
    tm = min(ROW_TILE, s)
    row = lambda w, idx: pl.BlockSpec((tm, w), lambda i: (i, idx))
    full = lambda a: pl.BlockSpec(a.shape, lambda i: (0,) * a.ndim)
    tab = pl.BlockSpec((tm, LANES), lambda i: (i, 0))
    tab_t = pl.BlockSpec((LANES, tm), lambda i: (0, i))
    hw = MLA_HEADS * LANES
    vw = MLA_HEADS * MLA_V
    return pl.pallas_call(
        _mla_prep_body,
        grid=(s // tm,),
        in_specs=[row(MLA_Q_LORA, Z_CQ // MLA_Q_LORA), row(MLA_KV_LORA, Z_CKV // MLA_KV_LORA),
                  row(LANES, Z_KR // LANES), tab, tab, tab, tab_t, tab_t, tab_t,
                  full(qg), full(kvg), full(wqt), full(wk), full(wvt)],
        out_specs=[pl.BlockSpec((hw, tm), lambda i: (0, i)), pl.BlockSpec((tm, hw), lambda i: (i, 0)),
                   pl.BlockSpec((vw, tm), lambda i: (0, i))],
        out_shape=[jax.ShapeDtypeStruct((hw, s), BF16), jax.ShapeDtypeStruct((s, hw), BF16),
                   jax.ShapeDtypeStruct((vw, s), BF16)],
        compiler_params=_cparams(("parallel",)),
        name="mla_prep",
    )(z, z, z, *tabs, *tabs_t, qg, kvg, wqt, wk, wvt)


def _flash_streams_t(q_ts, k_ats, vt_ats, m_ref, l_ref, acc_ref, s_a, s_b, n_full, t):
    streams = range(len(q_ts))
    for n in streams:
        m_ref[n] = jnp.full((1, t), NEG_INF, F32)
        l_ref[n] = jnp.zeros((1, t), F32)
        acc_ref[n] = jnp.zeros(acc_ref.shape[1:], F32)

    def scores_into(s_ref, j):
        for n in streams:
            s_ref[n] = jnp.dot(k_ats[n](j), q_ts[n], preferred_element_type=F32)

    def consume(s_ref, j, masked):
        probs, alphas = [], []
        for n in streams:
            s = s_ref[n]
            if masked:
                key_chunk = lax.broadcasted_iota(jnp.int32, (t, t), 0) // CHUNK
                qry_chunk = lax.broadcasted_iota(jnp.int32, (t, t), 1) // CHUNK
                s = jnp.where(key_chunk <= qry_chunk, s, NEG_INF)
            m_prev = m_ref[n]
            m_new = jnp.maximum(m_prev, jnp.max(s, axis=0, keepdims=True))
            alpha = jnp.exp2(m_prev - m_new)
            p = jnp.exp2(s - m_new)
            l_ref[n] = alpha * l_ref[n] + jnp.sum(p, axis=0, keepdims=True)
            m_ref[n] = m_new
            probs.append(p.astype(BF16))
            alphas.append(alpha)
        for n in streams:
            acc_ref[n] = alphas[n] * acc_ref[n] + jnp.dot(vt_ats[n](j), probs[n], preferred_element_type=F32)

    scores_into(s_a, 0)

    def pair(jj, carry):
        j = 2 * jj
        scores_into(s_b, j + 1)
        consume(s_a, j, False)
        scores_into(s_a, j + 2)
        consume(s_b, j + 1, False)
        return carry

    lax.fori_loop(0, n_full // 2, pair, 0)

    @pl.when(n_full % 2 == 0)
    def _():
        consume(s_a, n_full, True)

    @pl.when(n_full % 2 == 1)
    def _():
        scores_into(s_b, n_full)
        consume(s_a, n_full - 1, False)
        consume(s_b, n_full, True)


def _mla_attn_body(qt_ref, k_ref, vt_ref, o_ref, m_ref, l_ref, acc_ref, s_a, s_b, *, t):
    i = pl.program_id(1)
    rows = lambda j: pl.ds(pl.multiple_of(j * t, t), t)
    heads = range(MLA_GROUP)
    lanes = [slice(hh * LANES, (hh + 1) * LANES) for hh in heads]
    k_ats = [lambda j, sl=sl: k_ref[rows(j), sl] for sl in lanes]
    vt_ats = [lambda j, hh=hh: vt_ref[hh * MLA_V:(hh + 1) * MLA_V, rows(j)] for hh in heads]
    _flash_streams_t([qt_ref[sl, :] for sl in lanes], k_ats, vt_ats, m_ref, l_ref, acc_ref, s_a, s_b, i, t)
    o_t = jnp.concatenate([acc_ref[hh] / l_ref[hh] for hh in heads], axis=0)
    o_ref[...] = o_t.T.astype(o_ref.dtype)


def _mla_attn(qt, k, vt):
    s = k.shape[0]
    t = min(ATTN_TILE, s)
    g = MLA_GROUP
    return pl.pallas_call(
        functools.partial(_mla_attn_body, t=t),
        grid=(MLA_HEADS // g, s // t),
        in_specs=[pl.BlockSpec((g * LANES, t), lambda p, i: (p, i)),
                  pl.BlockSpec((s, g * LANES), lambda p, i: (0, p), pipeline_mode=pl.Buffered(1)),
                  pl.BlockSpec((g * MLA_V, s), lambda p, i: (p, 0), pipeline_mode=pl.Buffered(1))],
        out_specs=pl.BlockSpec((t, g * MLA_V), lambda p, i: (i, p)),
        out_shape=jax.ShapeDtypeStruct((s, MLA_HEADS * MLA_V), BF16),
        scratch_shapes=[pltpu.VMEM((g, 1, t), F32), pltpu.VMEM((g, 1, t), F32),
                        pltpu.VMEM((g, MLA_V, t), F32),
                        pltpu.VMEM((g, t, t), F32), pltpu.VMEM((g, t, t), F32)],
        compiler_params=_cparams(("parallel", "arbitrary")),
        name="mla_attn",
    )(qt, k, vt)


def _diff_prep_body(dq_ref, dk_ref, dv_ref, c_ref, su_ref, sd_ref, qt_ref, k_ref, vt_ref):
    half = DIFF_ROT // 2
    c, su, sd = c_ref[...], su_ref[...], sd_ref[...]
    tm = dq_ref.shape[0]
    lane = lax.broadcasted_iota(jnp.int32, (tm, LANES), 1)
    vt_ref[...] = dv_ref[...].astype(F32).T.astype(vt_ref.dtype)
    for h in range(DIFF_HEADS):
        sl = slice(h * LANES, (h + 1) * LANES)
        q = _rope128(dq_ref[:, sl].astype(F32), c, su, sd, half) * (DIFF_SCALE * LOG2E)
        qt_ref[(2 * h) * LANES:(2 * h + 1) * LANES, :] = jnp.where(lane < DIFF_HEAD_DIM, q, 0.0).T.astype(qt_ref.dtype)
        qt_ref[(2 * h + 1) * LANES:(2 * h + 2) * LANES, :] = jnp.where(lane >= DIFF_HEAD_DIM, q, 0.0).T.astype(qt_ref.dtype)
        k_ref[:, sl] = _rope128(dk_ref[:, sl].astype(F32), c, su, sd, half).astype(k_ref.dtype)


def _diff_prep(z, tabs):
    s = z.shape[0]
    tm = min(ROW_TILE, s)
    tab = pl.BlockSpec((tm, LANES), lambda i: (i, 0))
    col = lambda start: pl.BlockSpec((tm, DIFF_WIDTH), lambda i: (i, start // DIFF_WIDTH))
    return pl.pallas_call(
        _diff_prep_body,
        grid=(s // tm,),
        in_specs=[col(Z_DQ), col(Z_DK), col(Z_DV), tab, tab, tab],
        out_specs=[pl.BlockSpec((2 * DIFF_WIDTH, tm), lambda i: (0, i)),
                   pl.BlockSpec((tm, DIFF_WIDTH), lambda i: (i, 0)),
                   pl.BlockSpec((DIFF_WIDTH, tm), lambda i: (0, i))],
        out_shape=[jax.ShapeDtypeStruct((2 * DIFF_WIDTH, s), BF16), jax.ShapeDtypeStruct((s, DIFF_WIDTH), BF16),
                   jax.ShapeDtypeStruct((DIFF_WIDTH, s), BF16)],
        compiler_params=_cparams(("parallel",)),
        name="diff_prep",
    )(z, z, z, *tabs)


def _diff_attn_body(qt_ref, k_ref, vt_ref, lam_ref, g_ref, o_ref, m_ref, l_ref, acc_ref, s_a, s_b, *, t, lam_init):
    i = pl.program_id(1)
    rows = lambda j: pl.ds(pl.multiple_of(j * t, t), t)
    q_ts, k_ats, vt_ats = [], [], []
    for hh in range(DIFF_GROUP):
        sl = slice(hh * LANES, (hh + 1) * LANES)
        for c in range(2):
            q_ts.append(qt_ref[(2 * hh + c) * LANES:(2 * hh + c + 1) * LANES, :])
            k_ats.append(lambda j, sl=sl: k_ref[rows(j), sl])
            vt_ats.append(lambda j, sl=sl: vt_ref[sl, rows(j)])
    _flash_streams_t(q_ts, k_ats, vt_ats, m_ref, l_ref, acc_ref, s_a, s_b, i, t)
    lv = lam_ref[...]
    lam = (jnp.exp(jnp.sum(lv[0:1] * lv[1:2], axis=1, keepdims=True))
           - jnp.exp(jnp.sum(lv[2:3] * lv[3:4], axis=1, keepdims=True)) + lam_init)
    for hh in range(DIFF_GROUP):
        o = (acc_ref[2 * hh] / l_ref[2 * hh] - lam * (acc_ref[2 * hh + 1] / l_ref[2 * hh + 1])).T
        o = o * lax.rsqrt(jnp.mean(o * o, axis=-1, keepdims=True) + NORM_EPS)
        o_ref[:, hh * LANES:(hh + 1) * LANES] = (o * g_ref[...] * (1.0 - lam_init)).astype(o_ref.dtype)


def _diff_attn(qt, k, vt, lam_vecs, norm_g, lam_init):
    s = k.shape[0]
    t = min(ATTN_TILE, s)
    g = DIFF_GROUP
    return pl.pallas_call(
        functools.partial(_diff_attn_body, t=t, lam_init=lam_init),
        grid=(DIFF_HEADS // g, s // t),
        in_specs=[pl.BlockSpec((2 * g * LANES, t), lambda h, i: (h, i)),
                  pl.BlockSpec((s, g * LANES), lambda h, i: (0, h)),
                  pl.BlockSpec((g * LANES, s), lambda h, i: (h, 0)),
                  pl.BlockSpec(lam_vecs.shape, lambda h, i: (0, 0)),
                  pl.BlockSpec((1, LANES), lambda h, i: (0, 0))],
        out_specs=pl.BlockSpec((t, g * LANES), lambda h, i: (i, h)),
        out_shape=jax.ShapeDtypeStruct((s, DIFF_WIDTH), BF16),
        scratch_shapes=[pltpu.VMEM((2 * g, 1, t), F32), pltpu.VMEM((2 * g, 1, t), F32),
                        pltpu.VMEM((2 * g, LANES, t), F32),
                        pltpu.VMEM((2 * g, t, t), F32), pltpu.VMEM((2 * g, t, t), F32)],
        compiler_params=_cparams(("parallel", "arbitrary")),
        name="diff_attn",
    )(qt, k, vt, lam_vecs, norm_g.reshape(1, LANES))


def _conv_body(bg_ref, cg_ref, xv_ref, cgh_ref, xvh_ref, a_ref, gt_ref, ah_ref, gth_ref,
               scw_ref, dww_ref, dwb_ref, lng_ref, lnb_ref, sc_ref, cf_ref, ext_ref):
    tm = bg_ref.shape[0]
    first = pl.program_id(0) == 0

    shifted_rows = HALO + tm - SUBLANES

    def fill(cur, halo, taps):
        ext_ref[0, 0:HALO, :] = jnp.where(first, 0.0, halo)
        ext_ref[0, HALO:HALO + tm, :] = cur
        for r in sorted({(HALO - (taps - 1) + j) % SUBLANES for j in range(taps)} - {0}):
            ext_ref[r, 0:shifted_rows, :] = ext_ref[0, r:r + shifted_rows, :]

    def conv(w_ref, taps):
        acc = jnp.zeros((tm, ext_ref.shape[2]), F32)
        for j in range(taps):
            off = HALO - (taps - 1) + j
            r = off % SUBLANES
            acc = acc + w_ref[j:j + 1, :] * ext_ref[r, off - r:off - r + tm, :]
        return acc

    fill(cg_ref[...].astype(F32) * xv_ref[...].astype(F32), cgh_ref[...].astype(F32) * xvh_ref[...].astype(F32),
         SC_KERNEL)
    sc_ref[...] = (bg_ref[...].astype(F32) * conv(scw_ref, SC_KERNEL)).astype(sc_ref.dtype)

    fill(a_ref[...].astype(F32) * _sigmoid(gt_ref[...].astype(F32)),
         ah_ref[...].astype(F32) * _sigmoid(gth_ref[...].astype(F32)), CONF_KERNEL)
    u = conv(dww_ref, CONF_KERNEL) + dwb_ref[...]
    mu = jnp.mean(u, axis=-1, keepdims=True)
    var = jnp.mean(jnp.square(u - mu), axis=-1, keepdims=True)
    y = (u - mu) * lax.rsqrt(var + LN_EPS) * lng_ref[...] + lnb_ref[...]
    cf_ref[...] = (y * _sigmoid(y)).astype(cf_ref.dtype)


def _conv_branches(z, sc_w, dw_w, dw_b, ln_g, ln_b):
    s = z.shape[0]
    tm = min(ROW_TILE, s)
    w = SC_WIDTH
    cur = lambda col: pl.BlockSpec((tm, w), lambda i: (i, col // w))
    halo = lambda col: pl.BlockSpec((HALO, w), lambda i: (jnp.maximum(i * (tm // HALO) - 1, 0), col // w))
    full = lambda a: pl.BlockSpec(a.shape, lambda i: (0,) * a.ndim)
    vec = lambda a: a.reshape(1, -1)
    args = [sc_w, dw_w, vec(dw_b), vec(ln_g), vec(ln_b)]
    return pl.pallas_call(
        _conv_body,
        grid=(s // tm,),
        in_specs=[cur(Z_SC), cur(Z_SC + w), cur(Z_SC + 2 * w), halo(Z_SC + w), halo(Z_SC + 2 * w),
                  cur(Z_CONF), cur(Z_CONF + w), halo(Z_CONF), halo(Z_CONF + w)] + [full(a) for a in args],
        out_specs=[pl.BlockSpec((tm, w), lambda i: (i, 0)), pl.BlockSpec((tm, w), lambda i: (i, 0))],
        out_shape=[jax.ShapeDtypeStruct((s, w), BF16), jax.ShapeDtypeStruct((s, w), BF16)],
        scratch_shapes=[pltpu.VMEM((SUBLANES, HALO + tm, w), F32)],
        compiler_params=_cparams(("parallel",)),
        name="conv_branches",
    )(*([z] * 9), *args)


def _merged_branches(branch_refs, w_refs, gate_refs, gb_ref):
    acc = None
    for n, (b_ref, w_ref, g_ref) in enumerate(zip(branch_refs, w_refs, gate_refs)):
        y = jnp.dot(b_ref[...], w_ref[...], preferred_element_type=F32)
        t = _sigmoid(g_ref[...].astype(F32) + gb_ref[n:n + 1, :]) * y
        acc = t if acc is None else acc + t
    return acc.astype(BF16)


def _order(v, i, j):
    v[i], v[j] = jnp.maximum(v[i], v[j]), jnp.minimum(v[i], v[j])


def _bitonic_sort_desc(v):
    n = len(v)
    k = 2
    while k <= n:
        j = k // 2
        while j >= 1:
            for i in range(n):
                partner = i ^ j
                if partner > i:
                    if i & k == 0:
                        _order(v, i, partner)
                    else:
                        _order(v, partner, i)
            j //= 2
        k *= 2


def _bitonic_merge_desc(v):
    n = len(v)
    j = n // 2
    while j >= 1:
        for i in range(n):
            if i ^ j > i:
                _order(v, i, i ^ j)
        j //= 2


def _top16_desc(groups):
    v = list(groups)
    _bitonic_sort_desc(v)
    for shift in (SUBLANES // 2, SUBLANES // 4, SUBLANES // 8):
        other = [pltpu.roll(x, shift, 0) for x in v]
        v = [jnp.maximum(v[k], other[PEER_TOPK - 1 - k]) for k in range(PEER_TOPK)]
        _bitonic_merge_desc(v)
    return v


def _by_sublane(v, start):
    sub = lax.broadcasted_iota(jnp.int32, v[0].shape, 0)
    out = v[start]
    for r in range(1, SUBLANES):
        out = jnp.where(sub == r, v[start + r], out)
    return out


def _peer_candidates(a, a_hi, b_lo, b_hi, b0, op):
    groups = [op(a[0], b_lo), op(a[0], b_hi)]
    groups += [op(a[i], b_lo) for i in range(1, SUBLANES)]
    groups.append(op(a_hi, b0))
    return groups


def _peer_scores_body(q_ref, k1_ref, k2_ref, th_ref, s2_ref, e1_ref, e2n_ref):
    tt = q_ref.shape[0]
    n_groups = PEER_N_KEYS // SUBLANES

    def head(h, carry):
        qh = q_ref[:, pl.ds(pl.multiple_of(h * PEER_KEY_DIM, PEER_KEY_DIM), PEER_KEY_DIM)]
        dims = (((1,), (1,)), ((), ()))
        s1 = lax.dot_general(k1_ref[...], qh, dims, precision=lax.Precision.HIGHEST, preferred_element_type=F32)
        s2 = lax.dot_general(k2_ref[...], qh, dims, precision=lax.Precision.HIGHEST, preferred_element_type=F32)
        split = lambda s: [s[g * SUBLANES:(g + 1) * SUBLANES, :] for g in range(n_groups)]
        t1 = _top16_desc(split(s1))
        t2 = _top16_desc(split(s2))
        t1_hi, t2_lo, t2_hi = _by_sublane(t1, SUBLANES), _by_sublane(t2, 0), _by_sublane(t2, SUBLANES)
        cand = _peer_candidates(t1, t1_hi, t2_lo, t2_hi, t2[0], jnp.add)
        pad = [jnp.full((SUBLANES, tt), -jnp.inf, F32)] * (PEER_TOPK - len(cand))
        tau8 = _top16_desc(cand + pad)[PEER_TOPK - 1]

        m1, m2 = t1[0], t2[0]
        c1 = [jnp.exp(x - m1) for x in t1[:SUBLANES]]
        ec = _peer_candidates(c1, jnp.exp(t1_hi - m1), jnp.exp(t2_lo - m2), jnp.exp(t2_hi - m2),
                              jnp.ones((SUBLANES, tt), F32), jnp.multiply)
        zsum = jnp.zeros((SUBLANES, tt), F32)
        for cg, eg in zip(cand, ec):
            zsum = zsum + jnp.where(cg >= tau8, eg, 0.0)
        zden = jnp.sum(zsum, axis=0, keepdims=True)
        tau = tau8[0:1]
        th = jnp.full(s1.shape, jnp.inf, F32)
        for b in range(PEER_TOPK):
            t2b = t2[b][0:1]
            th = jnp.where(s1 + t2b >= tau, t2b, th)
        th_ref[h] = th
        s2_ref[h] = s2
        e1_ref[h] = jnp.exp(s1 - m1[0:1])
        e2n_ref[h] = jnp.exp(s2 - m2[0:1]) / zden
        return carry

    lax.fori_loop(0, PEER_HEADS, head, 0)


def _peer_scores(q, k1, k2):
    s = q.shape[0]
    tt = min(SCORE_TILE, s)
    tab = pl.BlockSpec((PEER_HEADS, PEER_N_KEYS, tt), lambda i: (0, 0, i))
    tab_shape = jax.ShapeDtypeStruct((PEER_HEADS, PEER_N_KEYS, s), F32)
    return pl.pallas_call(
        _peer_scores_body,
        grid=(s // tt,),
        in_specs=[pl.BlockSpec((tt, PEER_HEADS * PEER_KEY_DIM), lambda i: (i, 0)),
                  pl.BlockSpec(k1.shape, lambda i: (0, 0)), pl.BlockSpec(k2.shape, lambda i: (0, 0))],
        out_specs=[tab, tab, tab, tab],
        out_shape=[tab_shape, tab_shape, tab_shape, tab_shape],
        compiler_params=_cparams(("parallel",)),
        name="peer_scores",
    )(q, k1, k2)


def _peer_items(s, n_items, nb):
    split = lambda it: (it // nb, it % nb)
    return (split(jnp.minimum(s, n_items - 1)), split(jnp.clip(s - 1, 0, n_items - 1)),
            split(jnp.clip(s - 2, 0, n_items - 1)))


def _peer_dense_body(ht_ref, u_ref, v_ref, th_ref, e1_ref, s2_ref, e2n_ref, o_ref, a0, a1, w0, w1,
                     *, n_items, nb):
    s = pl.program_id(0)
    tt = ht_ref.shape[1]
    eb = u_ref.shape[0]
    d = ht_ref.shape[0]
    n_sub = eb // PEER_N_KEYS
    jrows = 2 * SUBLANES
    a_rows = eb // PEER_TRIPS

    @pl.when(s == 0)
    def _():
        for ref in (a0, a1, w0, w1):
            ref[...] = jnp.zeros_like(ref)

    _, _, (_, block_c) = _peer_items(s, n_items, nb)

    @pl.when(block_c == 0)
    def _():
        o_ref[...] = jnp.zeros_like(o_ref)

    def run(a_new, a_old, w_new, w_old):
        def stage_a(k):
            rows = pl.ds(pl.multiple_of(k * a_rows, a_rows), a_rows)
            a_t = jnp.dot(u_ref[rows, :], ht_ref[...], preferred_element_type=F32)
            a_new[rows, :] = 0.5 * a_t * (1.0 + lax.erf(a_t * (2.0 ** -0.5)))

        def stage_b(c, jc):
            cols = pl.ds(pl.multiple_of(c * LANES, LANES), LANES)
            jr = slice(jc * jrows, (jc + 1) * jrows)
            g = [jnp.zeros((jrows, LANES), F32) for _ in range(n_sub)]
            for h in range(PEER_HEADS):
                s2c = s2_ref[h, jr, cols]
                e2c = e2n_ref[h, jr, cols]
                for il in range(n_sub):
                    gate = e2c * e1_ref[h, il:il + 1, cols]
                    g[il] = g[il] + jnp.where(s2c >= th_ref[h, il:il + 1, cols], gate, 0.0)
            for il in range(n_sub):
                rows = slice(il * PEER_N_KEYS + jc * jrows, il * PEER_N_KEYS + (jc + 1) * jrows)
                w_new[rows, cols] = a_old[rows, cols] * g[il]

        def stage_c(n):
            rows = pl.ds(pl.multiple_of(n * PEER_CHUNK, PEER_CHUNK), PEER_CHUNK)
            o_ref[rows, :] += lax.dot_general(v_ref[:, rows], w_old[...], (((0,), (0,)), ((), ())),
                                              preferred_element_type=F32)

        n_c = d // PEER_CHUNK // PEER_TRIPS
        n_col = tt // LANES // PEER_TRIPS
        assert n_c * PEER_TRIPS * PEER_CHUNK == d and n_col * PEER_TRIPS * LANES == tt and n_c % 2 == 0

        def trip(k, carry):
            stage_a(k)
            for n in range(n_c // 2):
                stage_c(n_c * k + n)
            for c in range(n_col):
                for jc in range(PEER_N_KEYS // jrows):
                    stage_b(n_col * k + c, jc)
            for n in range(n_c // 2, n_c):
                stage_c(n_c * k + n)
            return carry

        lax.fori_loop(0, PEER_TRIPS, trip, 0)

    parity = lax.rem(s, 2)

    @pl.when(parity == 0)
    def _():
        run(a0, a1, w1, w0)

    @pl.when(parity == 1)
    def _():
        run(a1, a0, w0, w1)


def _peer_dense(hn_t, u_all, v_all, layer, th, e1, s2, e2n):
    d, s = hn_t.shape
    tt = min(PEER_TOKENS, s)
    eb = PEER_EXPERTS
    nb = u_all.shape[1] // eb
    n_items = (s // tt) * nb
    items = functools.partial(_peer_items, n_items=n_items, nb=nb)
    sub = pl.BlockSpec((PEER_HEADS, eb // PEER_N_KEYS, tt), lambda i: (0, items(i)[1][1], items(i)[1][0]))
    tab = pl.BlockSpec((PEER_HEADS, PEER_N_KEYS, tt), lambda i: (0, 0, items(i)[1][0]))
    return pl.pallas_call(
        functools.partial(_peer_dense_body, n_items=n_items, nb=nb),
        grid=(n_items + 2,),
        in_specs=[pl.BlockSpec((d, tt), lambda i: (0, items(i)[0][0])),
                  pl.BlockSpec((None, eb, d), lambda i: (layer, items(i)[0][1], 0)),
                  pl.BlockSpec((None, eb, d), lambda i: (layer, items(i)[2][1], 0)),
                  sub, sub, tab, tab],
        out_specs=pl.BlockSpec((d, tt), lambda i: (0, items(i)[2][0])),
        out_shape=jax.ShapeDtypeStruct((d, s), F32),
        scratch_shapes=[pltpu.VMEM((eb, tt), F32), pltpu.VMEM((eb, tt), F32),
                        pltpu.VMEM((eb, tt), F32), pltpu.VMEM((eb, tt), F32)],
        compiler_params=_cparams(("arbitrary",)),
        name="peer_dense",
    )(hn_t, u_all, v_all, th, e1, s2, e2n)


def _out_proj_peer_in_body(b0, b1, b2, b3, w0, w1, w2, w3, g0, g1, g2, g3, gb_ref,
                           wo_ref, r_ref, g_ref, wq_ref, x_ref, ht_ref, pq_ref):
    merged = _merged_branches((b0, b1, b2, b3), (w0, w1, w2, w3), (g0, g1, g2, g3), gb_ref)
    x = r_ref[...] + jnp.dot(merged, wo_ref[...], preferred_element_type=F32)
    x_ref[...] = x
    y = x * lax.rsqrt(jnp.mean(x * x, axis=-1, keepdims=True) + NORM_EPS) * g_ref[...]
    ht_ref[...] = y.T.astype(ht_ref.dtype)
    pq_ref[...] = jnp.dot(y.astype(BF16), wq_ref[...], preferred_element_type=F32)


def _out_proj_peer_in(branches, w_outs, z, gate_b, w_o, x, g, w_q):
    s, d = x.shape
    nq = w_q.shape[1]
    tm = min(MERGE_ROWS, s)
    row = lambda n: pl.BlockSpec((tm, n), lambda i: (i, 0))
    resident = lambda a: pl.BlockSpec(a.shape, lambda i: (0, 0), pipeline_mode=pl.Buffered(1))
    gates = [pl.BlockSpec((tm, d), lambda i, b=b: (i, Z_GATE // d + b)) for b in range(N_BRANCH)]
    assert Z_GATE % d == 0
    return pl.pallas_call(
        _out_proj_peer_in_body,
        grid=(s // tm,),
        in_specs=([row(b.shape[1]) for b in branches] + [resident(w) for w in w_outs] + gates
                  + [resident(gate_b), resident(w_o), row(d), pl.BlockSpec((1, d), lambda i: (0, 0)),
                     resident(w_q)]),
        out_specs=[row(d), pl.BlockSpec((d, tm), lambda i: (0, i)), row(nq)],
        out_shape=[jax.ShapeDtypeStruct((s, d), F32), jax.ShapeDtypeStruct((d, s), BF16),
                   jax.ShapeDtypeStruct((s, nq), F32)],
        compiler_params=_cparams(("parallel",)),
        name="out_proj_peer_in",
    )(*branches, *w_outs, z, z, z, z, gate_b, w_o, x, g.reshape(1, d), w_q)


def _add_norm_body(x_ref, dt_ref, g_ref, *out_refs):
    x = x_ref[...] + dt_ref[...].T
    y = x * lax.rsqrt(jnp.mean(x * x, axis=-1, keepdims=True) + NORM_EPS)
    out_refs[-1][...] = (y * g_ref[...]).astype(out_refs[-1].dtype)
    if len(out_refs) == 2:
        out_refs[0][...] = x


def _add_norm(x, delta_t, g, norm_dtype, with_sum):
    s, d = x.shape
    tm = min(ROW_TILE, s)
    row = pl.BlockSpec((tm, d), lambda i: (i, 0))
    out_specs = [row, row] if with_sum else [row]
    out_shape = [jax.ShapeDtypeStruct((s, d), norm_dtype)]
    if with_sum:
        out_shape.insert(0, jax.ShapeDtypeStruct((s, d), F32))
    return pl.pallas_call(
        _add_norm_body,
        grid=(s // tm,),
        in_specs=[row, pl.BlockSpec((d, tm), lambda i: (0, i)), pl.BlockSpec((1, d), lambda i: (0, 0))],
        out_specs=out_specs,
        out_shape=out_shape,
        compiler_params=_cparams(("parallel",)),
        name="add_norm",
    )(x, delta_t, g.reshape(1, d))


def _rope_tables(positions, dim, lanes_x1):
    half = dim // 2
    inv_freq = ROPE_THETA ** (-jnp.arange(0, dim, 2, dtype=F32) / dim)
    ang = positions.astype(F32)[:, None] * inv_freq
    cos, sin = jnp.cos(ang), jnp.sin(ang)
    s = positions.shape[0]
    c_parts, su_parts, sd_parts = [], [], []
    pos = 0
    for off in lanes_x1:
        gap = off - pos
        c_parts += [jnp.ones((s, gap), F32), cos, cos]
        su_parts += [jnp.zeros((s, gap), F32), -sin, jnp.zeros((s, half), F32)]
        sd_parts += [jnp.zeros((s, gap + half), F32), sin]
        pos = off + dim
    tail = LANES - pos
    c_parts.append(jnp.ones((s, tail), F32))
    su_parts.append(jnp.zeros((s, tail), F32))
    sd_parts.append(jnp.zeros((s, tail), F32))
    return tuple(jnp.concatenate(p, axis=1) for p in (c_parts, su_parts, sd_parts))


W_IN_HEAD = IN_OFFSETS[3]
W_IN_RUN = sum(IN_SIZES[3:])
LAYOUT_ROWS = 1024


def _in_proj_body(h_ref, a_ref, b_ref, o_ref, wt_ref):
    j = pl.program_id(0)
    rb = wt_ref.shape[0]
    lead = rb - W_IN_HEAD % rb
    new_block = pl.program_id(1) == 0

    @pl.when(new_block & (j < W_IN_RUN // rb))
    def _():
        wt_ref[0:lead, :] = a_ref[W_IN_HEAD % rb:rb, :].astype(wt_ref.dtype)
        wt_ref[lead:rb, :] = b_ref[0:rb - lead, :].astype(wt_ref.dtype)

    @pl.when(new_block & (j == W_IN_RUN // rb))
    def _():
        head = MLA_Q_LORA + MLA_KV_LORA
        wt_ref[0:head, :] = a_ref[0:head, :].astype(wt_ref.dtype)
        wt_ref[head:rb, :] = jnp.zeros((rb - head, wt_ref.shape[1]), wt_ref.dtype)
        wt_ref[head + MLA_NOPE:head + MLA_NOPE + MLA_ROPE, :] = a_ref[head:head + MLA_ROPE, :].astype(wt_ref.dtype)

    o_ref[...] = lax.dot_general(h_ref[...], wt_ref[...], (((1,), (1,)), ((), ())),
                                 preferred_element_type=F32).astype(o_ref.dtype)


def _in_proj(hn, w_in_t, layer):
    s, d = hn.shape
    n = w_in_t.shape[1]
    rb = LAYOUT_ROWS
    tm = min(MM_TILE, s)
    assert W_IN_RUN % rb == 0 and Z_CQ == W_IN_RUN and Z_COLS == W_IN_RUN + rb and W_IN_HEAD < rb
    run_blocks = W_IN_RUN // rb
    first = lambda j: jnp.where(j < run_blocks, j + W_IN_HEAD // rb, 0)
    second = lambda j: jnp.minimum(j + W_IN_HEAD // rb + 1, (n - 1) // rb)
    return pl.pallas_call(
        _in_proj_body,
        grid=(Z_COLS // rb, s // tm),
        in_specs=[pl.BlockSpec((tm, d), lambda j, i: (i, 0)),
                  pl.BlockSpec((None, rb, d), lambda j, i: (layer, first(j), 0)),
                  pl.BlockSpec((None, rb, d), lambda j, i: (layer, second(j), 0))],
        out_specs=pl.BlockSpec((tm, rb), lambda j, i: (i, j)),
        out_shape=jax.ShapeDtypeStruct((s, Z_COLS), BF16),
        scratch_shapes=[pltpu.VMEM((rb, d), BF16)],
        compiler_params=_cparams(("parallel", "arbitrary")),
        name="in_proj",
    )(hn, w_in_t, w_in_t)


def _layout_mla_weights(w_uq, w_ukv):
    lq = w_uq.shape[0]
    wq = w_uq.reshape(lq, MLA_HEADS, MLA_NOPE + MLA_ROPE)
    wq = jnp.pad(wq, ((0, 0), (0, 0), (0, LANES - MLA_NOPE - MLA_ROPE))).reshape(lq, MLA_HEADS * LANES)
    lk = w_ukv.shape[0]
    wkv = w_ukv.reshape(lk, MLA_HEADS, MLA_NOPE + MLA_V)
    wk = jnp.pad(wkv[:, :, :MLA_NOPE], ((0, 0), (0, 0), (0, LANES - MLA_NOPE))).reshape(lk, MLA_HEADS * LANES)
    wv = wkv[:, :, MLA_NOPE:].reshape(lk, MLA_HEADS * MLA_V)
    return wq.T.astype(BF16), wk.astype(BF16), wv.T.astype(BF16)


def kernel(x, positions, mix_norm_g, w_in, gate_b, mla_q_norm_g, mla_w_uq, mla_kv_norm_g, mla_w_ukv, mla_w_out,
           sc_conv_w, sc_w_out, diff_lambda, diff_norm_g, diff_w_out, conf_dw_w, conf_dw_b, conf_ln_g, conf_ln_b,
           conf_w_out, w_o, ffn_norm_g, peer_w_q, peer_sub_keys, peer_u, peer_v, final_norm_g):
    b, s, d = x.shape
    assert b == 1 and d == D_MODEL
    xs = x.reshape(s, d)
    pos = positions.reshape(s)
    mla_tabs = _rope_tables(pos, MLA_ROPE, (MLA_NOPE,))
    mla_tabs_t = tuple(t.T for t in mla_tabs)
    w_in_t = jnp.swapaxes(w_in, 1, 2)
    u_all = peer_u.astype(BF16)
    v_all = peer_v
    diff_tabs = _rope_tables(pos, DIFF_ROT, (0, DIFF_HEAD_DIM))

    for i in range(DEPTH):
        lam_init = 0.8 - 0.6 * math.exp(-0.3 * i)
        if i == 0:
            hn = _rmsnorm(xs, mix_norm_g[i], BF16)
        else:
            xs, hn = _add_norm(xs, delta_t, mix_norm_g[i], BF16, with_sum=True)
        z = _in_proj(hn, w_in_t, i)
        wqt, wk, wvt = _layout_mla_weights(mla_w_uq[i], mla_w_ukv[i])
        qt, k, vt = _mla_prep(z, mla_tabs, mla_tabs_t, mla_q_norm_g[i].reshape(1, -1),
                              mla_kv_norm_g[i].reshape(1, -1), wqt, wk, wvt)
        o_mla = _mla_attn(qt, k, vt)
        o_sc, o_conf = _conv_branches(z, sc_conv_w[i], conf_dw_w[i], conf_dw_b[i], conf_ln_g[i], conf_ln_b[i])
        qdt, kd, vdt = _diff_prep(z, diff_tabs)
        o_diff = _diff_attn(qdt, kd, vdt, diff_lambda[i], diff_norm_g[i], lam_init)
        w_outs = tuple(w.astype(BF16) for w in (mla_w_out[i], sc_w_out[i], diff_w_out[i], conf_w_out[i]))
        xs, hf_t, pq = _out_proj_peer_in((o_mla, o_sc, o_diff, o_conf), w_outs, z, gate_b[i],
                                         w_o[i].astype(BF16), xs, ffn_norm_g[i], peer_w_q[i].astype(BF16))
        zk = jnp.zeros((PEER_N_KEYS, PEER_HALF), F32)
        k1 = jnp.concatenate([peer_sub_keys[i, 0], zk], axis=1)
        k2 = jnp.concatenate([zk, peer_sub_keys[i, 1]], axis=1)
        th, s2, e1, e2n = _peer_scores(pq, k1, k2)
        delta_t = _peer_dense(hf_t, u_all, v_all, i, th, e1, s2, e2n)
    (out,) = _add_norm(xs, delta_t, final_norm_g, F32, with_sum=False)
    return out.reshape(b, s, d)
```
